```python
import math
import jax
import jax.numpy as jnp
from jax import lax
import numpy as np

D_MODEL = 2048
BATCH = 4
SEQ = 4096
DEPTH = 2

HEAD_DIM = 128
N_A = DEPTH // 2
N_B = DEPTH - N_A
N_HEADS_A = 12
N_KV_A = 2
HPG_A = N_HEADS_A // N_KV_A
CMP_LEN = 32
CMP_STRIDE = 16
CMP_HIDDEN = 256
SLC_BLK = 64
N_SEL = 16
WIN_A = 512
Q_BLK_A = 64
DIL_CONFIGS = ((128, 1), (512, 4), (2048, 16))
N_DIL_GROUPS = len(DIL_CONFIGS)
DIL_HEADS = 4
N_KV_B = DIL_HEADS
Q_BLK_B = 128
N_MEM = 256
N_MEM_HEADS = 4
D_FF = -(-(8 * D_MODEL) // (3 * 256)) * 256
ROPE_THETA = 10000.0
EPS = 1e-6
NEG_INF = -1e30
TINY = 1e-30

A_Q = N_HEADS_A * HEAD_DIM
A_KV = 6 * N_KV_A * HEAD_DIM
A_GATE = 3 * N_HEADS_A
MEM_Q = N_MEM_HEADS * HEAD_DIM
A_IN = A_Q + A_KV + A_GATE + MEM_Q
A_OUT_IN = A_Q + MEM_Q
B_Q = N_DIL_GROUPS * DIL_HEADS * HEAD_DIM
B_IN = B_Q + MEM_Q
B_OUT_IN = DIL_HEADS * HEAD_DIM + MEM_Q

kernel_name = "yoco_nsa_dilated_mem_swiglu"


def rms_norm(x, g):
    x32 = x.astype(jnp.float32)
    y = x32 * lax.rsqrt(jnp.mean(x32 * x32, axis=-1, keepdims=True) + EPS)
    return (y * g.astype(jnp.float32)).astype(x.dtype)


def rope_tables(seq):
    inv = 1.0 / (ROPE_THETA ** (jnp.arange(0, HEAD_DIM, 2, dtype=jnp.float32) / HEAD_DIM))
    ang = jnp.arange(seq, dtype=jnp.float32)[:, None] * inv[None, :]
    return jnp.cos(ang), jnp.sin(ang)


def apply_rope(x, cos, sin):
    x32 = x.astype(jnp.float32)
    x1, x2 = jnp.split(x32, 2, axis=-1)
    c = cos[None, :, None, :]
    s = sin[None, :, None, :]
    return jnp.concatenate([x1 * c - x2 * s, x2 * c + x1 * s], axis=-1).astype(x.dtype)


def masked_probs(s, mask):
    s = jnp.where(mask, s, NEG_INF)
    m = jnp.max(s, axis=-1, keepdims=True)
    e = jnp.where(mask, jnp.exp(s - m), 0.0)
    den = jnp.sum(e, axis=-1, keepdims=True)
    return e / jnp.maximum(den, TINY), m + jnp.log(jnp.maximum(den, TINY))


def swiglu(h, w_gate, w_up, w_down):
    return (jax.nn.silu(h @ w_gate) * (h @ w_up)) @ w_down


def compress(x_raw, pe, w1, w2):
    b, s, g, d = x_raw.shape
    n_cmp = (s - CMP_LEN) // CMP_STRIDE + 1
    idx = np.arange(n_cmp)[:, None] * CMP_STRIDE + np.arange(CMP_LEN)[None, :]
    blocks = x_raw[:, idx] + pe[None, None, :, None, :].astype(x_raw.dtype)
    blocks = jnp.transpose(blocks, (0, 1, 3, 2, 4)).reshape(b, n_cmp, g, CMP_LEN * d)
    return jax.nn.silu(blocks @ w1) @ w2


def cmp_to_slc_matrix(seq):
    n_cmp = (seq - CMP_LEN) // CMP_STRIDE + 1
    n_slc = seq // SLC_BLK
    c0 = np.arange(n_cmp) * CMP_STRIDE
    s0 = np.arange(n_slc) * SLC_BLK
    ov = (c0[None, :] < s0[:, None] + SLC_BLK) & (c0[None, :] + CMP_LEN > s0[:, None])
    return jnp.asarray(ov.astype(np.float32))


def nsa_attention(q, k_cmp_raw, v_cmp_raw, k_slc, v_slc, k_win, v_win, gates,
                  pe_k, w1_k, w2_k, pe_v, w1_v, w2_v):
    b, s, h, d = q.shape
    g = N_KV_A
    scale = HEAD_DIM ** -0.5
    kc = compress(k_cmp_raw, pe_k, w1_k, w2_k)
    vc = compress(v_cmp_raw, pe_v, w1_v, w2_v)
    n_cmp = kc.shape[1]
    cmp_end = jnp.arange(n_cmp) * CMP_STRIDE + CMP_LEN - 1
    n_slc = s // SLC_BLK
    n_top = min(N_SEL, n_slc)
    m_map = cmp_to_slc_matrix(s)
    kb = k_slc.reshape(b, n_slc, SLC_BLK, g, d).transpose(0, 3, 1, 2, 4)
    vb = v_slc.reshape(b, n_slc, SLC_BLK, g, d).transpose(0, 3, 1, 2, 4)
    kw = jnp.pad(k_win, ((0, 0), (WIN_A, 0), (0, 0), (0, 0)))
    vw = jnp.pad(v_win, ((0, 0), (WIN_A, 0), (0, 0), (0, 0)))
    qg = q.reshape(b, s, g, HPG_A, d)
    gg = gates.reshape(b, s, g, HPG_A, 3)
    bi = jnp.arange(b)[:, None, None, None]
    gi = jnp.arange(g)[None, :, None, None]
    j_blk = jnp.arange(n_slc)
    lb = jnp.arange(SLC_BLK)

    def block(i):
        s0 = i * Q_BLK_A
        t = s0 + jnp.arange(Q_BLK_A)
        qb = lax.dynamic_slice_in_dim(qg, s0, Q_BLK_A, axis=1)
        gb = lax.dynamic_slice_in_dim(gg, s0, Q_BLK_A, axis=1)
        sc = jnp.einsum('bqgpd,bcgd->bgpqc', qb, kc).astype(jnp.float32) * scale
        p_cmp, _ = masked_probs(sc, cmp_end[None, :] <= t[:, None])
        o_cmp = jnp.einsum('bgpqc,bcgd->bqgpd', p_cmp.astype(vc.dtype), vc)
        imp = jnp.einsum('bgpqc,sc->bgqs', p_cmp, m_map)
        cur = t // SLC_BLK
        forced = (j_blk[None] == 0) | (j_blk[None] == cur[:, None]) | (j_blk[None] == cur[:, None] - 1)
        eligible = j_blk[None] * SLC_BLK <= t[:, None]
        score = jnp.where(forced, 1e9, jnp.where(eligible, imp, -1e9))
        _, sel = lax.top_k(score, n_top)
        ks = kb[bi, gi, sel]
        vs = vb[bi, gi, sel]
        ss = jnp.einsum('bqgpd,bgqnkd->bgpqnk', qb, ks).astype(jnp.float32) * scale
        ss = ss.reshape(b, g, HPG_A, Q_BLK_A, n_top * SLC_BLK)
        tok = sel[..., None] * SLC_BLK + lb
        smask = (tok <= t[None, None, :, None, None]).reshape(b, g, 1, Q_BLK_A, n_top * SLC_BLK)
        p_slc, _ = masked_probs(ss, smask)
        o_slc = jnp.einsum('bgpqm,bgqmd->bqgpd', p_slc.astype(vs.dtype),
                           vs.reshape(b, g, Q_BLK_A, n_top * SLC_BLK, d))
        kwb = lax.dynamic_slice_in_dim(kw, s0, Q_BLK_A + WIN_A, axis=1)
        vwb = lax.dynamic_slice_in_dim(vw, s0, Q_BLK_A + WIN_A, axis=1)
        pos = s0 - WIN_A + jnp.arange(Q_BLK_A + WIN_A)
        dist = t[:, None] - pos[None, :]
        wmask = (dist >= 0) & (dist < WIN_A) & (pos[None, :] >= 0)
        sw = jnp.einsum('bqgpd,bkgd->bgpqk', qb, kwb).astype(jnp.float32) * scale
        p_win, _ = masked_probs(sw, wmask)
        o_win = jnp.einsum('bgpqk,bkgd->bqgpd', p_win.astype(vwb.dtype), vwb)
        o = gb[..., 0:1] * o_cmp + gb[..., 1:2] * o_slc + gb[..., 2:3] * o_win
        return o.reshape(b, Q_BLK_A, h, d)

    out = lax.map(block, jnp.arange(s // Q_BLK_A))
    return jnp.transpose(out, (1, 0, 2, 3, 4)).reshape(b, s, h, d)


def dilated_attention(q, k, v):
    b, s, _, hg, d = q.shape
    scale = HEAD_DIM ** -0.5

    def block(i):
        s0 = i * Q_BLK_B
        t = s0 + jnp.arange(Q_BLK_B)
        qb = lax.dynamic_slice_in_dim(q, s0, Q_BLK_B, axis=1)
        outs, lses = [], []
        for gidx, (w, r) in enumerate(DIL_CONFIGS):
            n_k = w // r + 1
            pos = t[:, None] - r * jnp.arange(n_k)[None, :]
            valid = pos >= 0
            posc = jnp.maximum(pos, 0)
            kg = k[:, posc]
            vg = v[:, posc]
            sg = jnp.einsum('bqhd,bqkhd->bhqk', qb[:, :, gidx], kg).astype(jnp.float32) * scale
            p, lse = masked_probs(sg, valid[None, None])
            og = jnp.einsum('bhqk,bqkhd->bhqd', p.astype(vg.dtype), vg)
            outs.append(og.astype(jnp.float32))
            lses.append(lse[..., 0])
        alpha = jax.nn.softmax(jnp.stack(lses, 0), axis=0)
        o = jnp.sum(alpha[..., None] * jnp.stack(outs, 0), axis=0)
        return jnp.transpose(o, (0, 2, 1, 3)).astype(q.dtype)

    out = lax.map(block, jnp.arange(s // Q_BLK_B))
    return jnp.transpose(out, (1, 0, 2, 3, 4)).reshape(b, s, hg, d)


def memory_attention(qm, mem, norm_mem, w_mem_kv):
    b, m, _ = mem.shape
    mkv = (rms_norm(mem, norm_mem) @ w_mem_kv).reshape(b, m, 2, N_MEM_HEADS, HEAD_DIM)
    mk, mv = mkv[:, :, 0], mkv[:, :, 1]
    sm = jnp.einsum('bshd,bmhd->bhsm', qm, mk).astype(jnp.float32) * (HEAD_DIM ** -0.5)
    p = jax.nn.softmax(sm, axis=-1)
    return jnp.einsum('bhsm,bmhd->bshd', p.astype(mv.dtype), mv)


def layer_a(h, mem, cos, sin, norm_attn, w_in, gate_bias, pe_k, w1_k, w2_k, pe_v, w1_v, w2_v,
            norm_mem, w_mem_kv, w_out, norm_ffn, w_gate, w_up, w_down):
    b, s, _ = h.shape
    z = rms_norm(h, norm_attn) @ w_in
    zq, zkv, zg, zm = jnp.split(z, [A_Q, A_Q + A_KV, A_Q + A_KV + A_GATE], axis=-1)
    q = apply_rope(zq.reshape(b, s, N_HEADS_A, HEAD_DIM), cos, sin)
    kv = zkv.reshape(b, s, 6, N_KV_A, HEAD_DIM)
    k_cmp = apply_rope(kv[:, :, 0], cos, sin)
    k_slc = apply_rope(kv[:, :, 2], cos, sin)
    k_win = apply_rope(kv[:, :, 4], cos, sin)
    gates = jax.nn.sigmoid(zg + gate_bias).reshape(b, s, N_HEADS_A, 3)
    o_nsa = nsa_attention(q, k_cmp, kv[:, :, 1], k_slc, kv[:, :, 3], k_win, kv[:, :, 5], gates,
                          pe_k, w1_k, w2_k, pe_v, w1_v, w2_v)
    o_mem = memory_attention(zm.reshape(b, s, N_MEM_HEADS, HEAD_DIM), mem, norm_mem, w_mem_kv)
    o = jnp.concatenate([o_nsa.reshape(b, s, A_Q), o_mem.reshape(b, s, MEM_Q)], axis=-1) @ w_out
    h = h + o
    return h + swiglu(rms_norm(h, norm_ffn), w_gate, w_up, w_down)


def layer_b(h, mem, cos, sin, k_sh, v_sh, norm_attn, w_in, norm_mem, w_mem_kv, w_out,
            norm_ffn, w_gate, w_up, w_down):
    b, s, _ = h.shape
    z = rms_norm(h, norm_attn) @ w_in
    zq, zm = jnp.split(z, [B_Q], axis=-1)
    q = apply_rope(zq.reshape(b, s, N_DIL_GROUPS * DIL_HEADS, HEAD_DIM), cos, sin)
    q = q.reshape(b, s, N_DIL_GROUPS, DIL_HEADS, HEAD_DIM)
    o_dil = dilated_attention(q, k_sh, v_sh)
    o_mem = memory_attention(zm.reshape(b, s, N_MEM_HEADS, HEAD_DIM), mem, norm_mem, w_mem_kv)
    o = jnp.concatenate([o_dil.reshape(b, s, DIL_HEADS * HEAD_DIM),
                         o_mem.reshape(b, s, MEM_Q)], axis=-1) @ w_out
    h = h + o
    return h + swiglu(rms_norm(h, norm_ffn), w_gate, w_up, w_down)


def setup_inputs(seed: int = 0) -> dict:
    key = jax.random.key(seed)
    ks = jax.random.split(key, 40)
    f32 = jnp.float32

    def w(k, shape, fan_in):
        return jax.random.normal(k, shape, f32) * (fan_in ** -0.5)

    def gain(k, shape):
        return 1.0 + 0.02 * jax.random.normal(k, shape, f32)

    return {
        "x": jax.random.normal(ks[0], (BATCH, SEQ, D_MODEL), f32),
        "mem": jax.random.normal(ks[1], (BATCH, N_MEM, D_MODEL), f32),
        "a_norm_attn": gain(ks[2], (N_A, D_MODEL)),
        "a_w_in": w(ks[3], (N_A, D_MODEL, A_IN), D_MODEL),
        "a_gate_bias": 0.01 * jax.random.normal(ks[4], (N_A, A_GATE), f32),
        "a_cmp_pe_k": 0.1 * jax.random.normal(ks[5], (N_A, CMP_LEN, HEAD_DIM), f32),
        "a_cmp_w1_k": w(ks[6], (N_A, CMP_LEN * HEAD_DIM, CMP_HIDDEN), CMP_LEN * HEAD_DIM),
        "a_cmp_w2_k": w(ks[7], (N_A, CMP_HIDDEN, HEAD_DIM), CMP_HIDDEN),
        "a_cmp_pe_v": 0.1 * jax.random.normal(ks[8], (N_A, CMP_LEN, HEAD_DIM), f32),
        "a_cmp_w1_v": w(ks[9], (N_A, CMP_LEN * HEAD_DIM, CMP_HIDDEN), CMP_LEN * HEAD_DIM),
        "a_cmp_w2_v": w(ks[10], (N_A, CMP_HIDDEN, HEAD_DIM), CMP_HIDDEN),
        "a_norm_mem": gain(ks[11], (N_A, D_MODEL)),
        "a_w_mem_kv": w(ks[12], (N_A, D_MODEL, 2 * MEM_Q), D_MODEL),
        "a_w_out": w(ks[13], (N_A, A_OUT_IN, D_MODEL), A_OUT_IN),
        "a_norm_ffn": gain(ks[14], (N_A, D_MODEL)),
        "a_w_gate": w(ks[15], (N_A, D_MODEL, D_FF), D_MODEL),
        "a_w_up": w(ks[16], (N_A, D_MODEL, D_FF), D_MODEL),
        "a_w_down": w(ks[17], (N_A, D_FF, D_MODEL), D_FF),
        "kv_norm": gain(ks[18], (D_MODEL,)),
        "w_kv_shared": w(ks[19], (D_MODEL, 2 * N_KV_B * HEAD_DIM), D_MODEL),
        "b_norm_attn": gain(ks[20], (N_B, D_MODEL)),
        "b_w_in": w(ks[21], (N_B, D_MODEL, B_IN), D_MODEL),
        "b_norm_mem": gain(ks[22], (N_B, D_MODEL)),
        "b_w_mem_kv": w(ks[23], (N_B, D_MODEL, 2 * MEM_Q), D_MODEL),
        "b_w_out": w(ks[24], (N_B, B_OUT_IN, D_MODEL), B_OUT_IN),
        "b_norm_ffn": gain(ks[25], (N_B, D_MODEL)),
        "b_w_gate": w(ks[26], (N_B, D_MODEL, D_FF), D_MODEL),
        "b_w_up": w(ks[27], (N_B, D_MODEL, D_FF), D_MODEL),
        "b_w_down": w(ks[28], (N_B, D_FF, D_MODEL), D_FF),
        "final_norm": gain(ks[29], (D_MODEL,)),
    }


def reference(x, mem, a_norm_attn, a_w_in, a_gate_bias, a_cmp_pe_k, a_cmp_w1_k, a_cmp_w2_k,
              a_cmp_pe_v, a_cmp_w1_v, a_cmp_w2_v, a_norm_mem, a_w_mem_kv, a_w_out, a_norm_ffn,
              a_w_gate, a_w_up, a_w_down, kv_norm, w_kv_shared, b_norm_attn, b_w_in,
              b_norm_mem, b_w_mem_kv, b_w_out, b_norm_ffn, b_w_gate, b_w_up, b_w_down,
              final_norm):
    b, s, _ = x.shape
    cos, sin = rope_tables(s)
    h = x
    k_sh = v_sh = None
    for layer in range(DEPTH):
        if layer < N_A:
            l = layer
            h = layer_a(h, mem, cos, sin, a_norm_attn[l], a_w_in[l], a_gate_bias[l],
                        a_cmp_pe_k[l], a_cmp_w1_k[l], a_cmp_w2_k[l],
                        a_cmp_pe_v[l], a_cmp_w1_v[l], a_cmp_w2_v[l],
                        a_norm_mem[l], a_w_mem_kv[l], a_w_out[l], a_norm_ffn[l],
                        a_w_gate[l], a_w_up[l], a_w_down[l])
        else:
            if layer == N_A:
                kv = (rms_norm(h, kv_norm) @ w_kv_shared).reshape(b, s, 2, N_KV_B, HEAD_DIM)
                k_sh = apply_rope(kv[:, :, 0], cos, sin)
                v_sh = kv[:, :, 1]
            l = layer - N_A
            h = layer_b(h, mem, cos, sin, k_sh, v_sh, b_norm_attn[l], b_w_in[l], b_norm_mem[l],
                        b_w_mem_kv[l], b_w_out[l], b_norm_ffn[l], b_w_gate[l], b_w_up[l],
                        b_w_down[l])
    return rms_norm(h, final_norm)
```

```python
import functools

import numpy as np
import jax
import jax.numpy as jnp
from jax import lax
from jax.experimental import pallas as pl
from jax.experimental.pallas import tpu as pltpu

F32 = jnp.float32
BF16 = jnp.bfloat16

HEAD_DIM = 128
N_HEADS_A = 12
N_KV_A = 2
HPG_A = N_HEADS_A // N_KV_A
CMP_LEN = 32
CMP_STRIDE = 16
CMP_HIDDEN = 256
SLC_BLK = 64
SLC_SHIFT = SLC_BLK.bit_length() - 1
N_SEL = 16
WIN_A = 512
DIL_CONFIGS = ((128, 1), (512, 4), (2048, 16))
N_DIL_GROUPS = len(DIL_CONFIGS)
DIL_HEADS = 4
N_MEM_HEADS = 4
ROPE_THETA = 10000.0
EPS = 1e-6
NEG_INF = -1e30
TINY = 1e-30
SCALE = HEAD_DIM ** -0.5

A_Q = N_HEADS_A * HEAD_DIM
A_KV = 6 * N_KV_A * HEAD_DIM
A_GATE = 3 * N_HEADS_A
MEM_Q = N_MEM_HEADS * HEAD_DIM
B_Q = N_DIL_GROUPS * DIL_HEADS * HEAD_DIM

LANES = 128
VMEM_LIMIT_BYTES = 56 * 1024 * 1024

PROJ_TM = 1024
FFN_TM = 512
FFN_TF = 512
OUT_TM = 1024
OUT_TN = 1024
NSA_TQ = 128
NSA_KC = 512
DIL_TQ = 256
MEM_TQ = 512

NT_DIMS = (((1,), (1,)), ((), ()))


def _compiler_params(semantics):
    return pltpu.CompilerParams(dimension_semantics=semantics,
                                vmem_limit_bytes=VMEM_LIMIT_BYTES)


def _rms_rows(x, g):
    ms = jnp.mean(x * x, axis=-1, keepdims=True)
    return x * lax.rsqrt(ms + EPS) * g


def _norm_proj_kernel(x_ref, g_ref, w_ref, cs_ref, cos_ref, sin_ref, o_ref, xn_ref, *,
                      n_rope_blocks, tn):
    j = pl.program_id(1)

    @pl.when(j == 0)
    def _():
        xn_ref[...] = _rms_rows(x_ref[...], g_ref[...]).astype(BF16)

    acc = jnp.dot(xn_ref[...], w_ref[...], preferred_element_type=F32) * cs_ref[...]

    if n_rope_blocks > 0:
        @pl.when(j < n_rope_blocks)
        def _():
            c = cos_ref[...]
            s = sin_ref[...]
            for h in range(tn // HEAD_DIM):
                y = acc[:, h * HEAD_DIM:(h + 1) * HEAD_DIM]
                rot = pltpu.roll(y, HEAD_DIM // 2, 1)
                o_ref[:, h * HEAD_DIM:(h + 1) * HEAD_DIM] = (y * c + rot * s).astype(o_ref.dtype)

        @pl.when(j >= n_rope_blocks)
        def _():
            o_ref[...] = acc.astype(o_ref.dtype)
    else:
        o_ref[...] = acc.astype(o_ref.dtype)


def _norm_proj(x, g, w_bf, col_scale, cosf, sinf, *, tn, n_rope_blocks, seq, tm=PROJ_TM):
    m, d = x.shape
    n = w_bf.shape[1]
    tm = min(tm, m)
    assert m % tm == 0 and n % tn == 0 and seq % tm == 0
    pos_blocks = seq // tm
    kern = functools.partial(_norm_proj_kernel, n_rope_blocks=n_rope_blocks, tn=tn)
    return pl.pallas_call(
        kern,
        grid=(m // tm, n // tn),
        in_specs=[
            pl.BlockSpec((tm, d), lambda i, j: (i, 0)),
            pl.BlockSpec((1, d), lambda i, j: (0, 0)),
            pl.BlockSpec((d, tn), lambda i, j: (0, j)),
            pl.BlockSpec((1, tn), lambda i, j: (0, j)),
            pl.BlockSpec((tm, HEAD_DIM), lambda i, j: (i % pos_blocks, 0)),
            pl.BlockSpec((tm, HEAD_DIM), lambda i, j: (i % pos_blocks, 0)),
        ],
        out_specs=pl.BlockSpec((tm, tn), lambda i, j: (i, j)),
        out_shape=jax.ShapeDtypeStruct((m, n), BF16),
        scratch_shapes=[pltpu.VMEM((tm, d), BF16)],
        compiler_params=_compiler_params(("parallel", "arbitrary")),
        name="norm_proj",
    )(x, g.reshape(1, d), w_bf, col_scale, cosf, sinf)


def _compress_kernel(x_ref, pe_ref, w1_ref, w2_ref, o_ref):
    half = (CMP_LEN // 2) * HEAD_DIM
    x = x_ref[...].astype(F32)
    xlo = (x + pe_ref[0:1, :]).astype(BF16)
    xhi = (x + pe_ref[1:2, :]).astype(BF16)
    ylo = jnp.dot(xlo, w1_ref[:half, :], preferred_element_type=F32)
    yhi = jnp.dot(xhi, w1_ref[half:, :], preferred_element_type=F32)
    n_rows = x.shape[0]
    hid = ylo + pltpu.roll(yhi, n_rows - 1, 0)
    act = (hid * jax.nn.sigmoid(hid)).astype(BF16)
    o_ref[...] = jnp.dot(act, w2_ref[...], preferred_element_type=F32).astype(o_ref.dtype)


def _compress(x2, pe2, w1_bf, w2_bf):
    _, bg, nrow, wide = x2.shape
    return pl.pallas_call(
        _compress_kernel,
        grid=(2, bg),
        in_specs=[
            pl.BlockSpec((None, None, nrow, wide), lambda t, i: (t, i, 0, 0)),
            pl.BlockSpec((None, 8, wide), lambda t, i: (t, 0, 0)),
            pl.BlockSpec((None, 2 * wide, CMP_HIDDEN), lambda t, i: (t, 0, 0)),
            pl.BlockSpec((None, CMP_HIDDEN, HEAD_DIM), lambda t, i: (t, 0, 0)),
        ],
        out_specs=pl.BlockSpec((None, None, nrow, HEAD_DIM), lambda t, i: (t, i, 0, 0)),
        out_shape=jax.ShapeDtypeStruct((2, bg, nrow, HEAD_DIM), BF16),
        compiler_params=_compiler_params(("parallel", "arbitrary")),
        name="nsa_compress",
    )(x2, pe2, w1_bf, w2_bf)


def _tile_lanes(x, reps):
    return jnp.concatenate([x] * reps, axis=1)


def _nsa_kernel(q_ref, kc_ref, vct_ref, ks_ref, vst_ref, kw_ref, vwt_ref, gz_ref, gb_ref,
                o_ref, m_ref, l_ref, acc_ref, *, tq, seq):
    hq = HPG_A
    n_cmp_rows = kc_ref.shape[0]
    n_slc = seq // SLC_BLK
    qi = pl.program_id(2)
    s0 = qi * tq
    q6 = jnp.concatenate([q_ref[:, h * HEAD_DIM:(h + 1) * HEAD_DIM] for h in range(hq)], axis=0)
    t_row = s0 + lax.broadcasted_iota(jnp.int32, (1, tq), 1)

    sc = lax.dot_general(kc_ref[...], q6, NT_DIMS, preferred_element_type=F32)
    c_end = lax.broadcasted_iota(jnp.int32, (n_cmp_rows, 1), 0) * CMP_STRIDE + (CMP_LEN - 1)
    cmask = _tile_lanes(c_end <= t_row, hq)
    sc = jnp.where(cmask, sc, NEG_INF)
    mc = jnp.max(sc, axis=0, keepdims=True)
    ec = jnp.where(cmask, jnp.exp(sc - mc), 0.0)
    pc = ec / jnp.maximum(jnp.sum(ec, axis=0, keepdims=True), TINY)
    o_cmp = jnp.dot(vct_ref[...], pc.astype(BF16), preferred_element_type=F32)

    psum = pc[:, 0:tq]
    for h in range(1, hq):
        psum = psum + pc[:, h * tq:(h + 1) * tq]
    jrow = lax.broadcasted_iota(jnp.int32, (n_slc, 1), 0)
    ccol = lax.broadcasted_iota(jnp.int32, (1, n_cmp_rows), 1)
    lo = (SLC_BLK // CMP_STRIDE) * jrow - (CMP_LEN // CMP_STRIDE - 1)
    hi = (SLC_BLK // CMP_STRIDE) * jrow + (SLC_BLK // CMP_STRIDE - 1)
    mmap = jnp.where((ccol >= lo) & (ccol <= hi), 1.0, 0.0).astype(BF16)
    p1 = psum.astype(BF16)
    r1 = psum - p1.astype(F32)
    p2 = r1.astype(BF16)
    p3 = (r1 - p2.astype(F32)).astype(BF16)
    imp = (jnp.dot(mmap, p1, preferred_element_type=F32)
           + jnp.dot(mmap, p2, preferred_element_type=F32)
           + jnp.dot(mmap, p3, preferred_element_type=F32))
    cur = t_row >> SLC_SHIFT
    forced = (jrow == 0) | (jrow == cur) | (jrow == cur - 1)
    score = jnp.where(forced, 1e9, jnp.where(jrow <= cur, imp, -1e9))
    n_top = min(N_SEL, n_slc)
    rank = jnp.zeros((n_slc, tq), F32)
    for j in range(n_slc):
        rj = score[j:j + 1, :]
        ahead = (rj > score) | ((rj == score) & (jrow > j))
        rank = rank + jnp.where(ahead, 1.0, 0.0)
    sel_bf = jnp.where(rank < n_top, 1.0, 0.0).astype(BF16)

    kc_sz = NSA_KC
    blocks_per_chunk = kc_sz // SLC_BLK
    tiles_per_chunk = kc_sz // LANES
    m_ref[...] = jnp.full(m_ref.shape, NEG_INF, F32)
    l_ref[...] = jnp.zeros(l_ref.shape, F32)
    acc_ref[...] = jnp.zeros(acc_ref.shape, F32)
    krow = lax.broadcasted_iota(jnp.int32, (kc_sz, 1), 0)
    jcol = lax.broadcasted_iota(jnp.int32, (1, n_slc), 1)

    def chunk(c, carry):
        k0 = pl.multiple_of(c * kc_sz, kc_sz)
        s = lax.dot_general(ks_ref[pl.ds(k0, kc_sz), :], q6, NT_DIMS,
                            preferred_element_type=F32)
        expand = jnp.where((krow >> SLC_SHIFT) + c * blocks_per_chunk == jcol, 1.0, 0.0).astype(BF16)
        selc = jnp.dot(expand, sel_bf, preferred_element_type=F32)
        keep = (selc > 0.5) & (k0 + krow <= t_row)
        s = s + _tile_lanes(jnp.where(keep, 0.0, NEG_INF), hq)
        m_old = m_ref[...]
        m_new = jnp.maximum(m_old, jnp.max(s, axis=0, keepdims=True))
        alpha = jnp.exp(m_old - m_new)
        p = jnp.exp(s - m_new)
        l_ref[...] = alpha * l_ref[...] + jnp.sum(p, axis=0, keepdims=True)
        p_bf = p.astype(BF16)
        vt = jnp.concatenate([vst_ref[c * tiles_per_chunk + i] for i in range(tiles_per_chunk)],
                             axis=1)
        acc_ref[...] = alpha * acc_ref[...] + jnp.dot(vt, p_bf, preferred_element_type=F32)
        m_ref[...] = m_new
        return carry

    n_chunks = (s0 + tq + kc_sz - 1) // kc_sz
    lax.fori_loop(0, n_chunks, chunk, 0)

    wtiles = (WIN_A + tq) // LANES
    wlen = wtiles * LANES
    ws = pl.multiple_of(jnp.maximum(s0 - WIN_A, 0), LANES)
    sw = lax.dot_general(kw_ref[pl.ds(ws, wlen), :], q6, NT_DIMS, preferred_element_type=F32)
    dist = t_row - (ws + lax.broadcasted_iota(jnp.int32, (wlen, 1), 0))
    wkeep = (dist >= 0) & (dist < WIN_A)
    sw = sw + _tile_lanes(jnp.where(wkeep, 0.0, NEG_INF), hq)
    mw = jnp.max(sw, axis=0, keepdims=True)
    ew = jnp.exp(sw - mw)
    lw = jnp.sum(ew, axis=0, keepdims=True)
    ew_bf = ew.astype(BF16)
    wt0 = ws // LANES
    o_win = jnp.dot(vwt_ref[wt0], ew_bf[0:LANES, :], preferred_element_type=F32)
    for i in range(1, wtiles):
        o_win = o_win + jnp.dot(vwt_ref[wt0 + i], ew_bf[i * LANES:(i + 1) * LANES, :],
                                preferred_element_type=F32)

    gates_t = jnp.transpose(jax.nn.sigmoid(gz_ref[...].astype(F32) + gb_ref[...]))
    o_slc = acc_ref[...] / l_ref[...]
    o_win = o_win / lw
    for h in range(hq):
        cols = slice(h * tq, (h + 1) * tq)
        o_h = (gates_t[3 * h:3 * h + 1, :] * o_cmp[:, cols]
               + gates_t[3 * h + 1:3 * h + 2, :] * o_slc[:, cols]
               + gates_t[3 * h + 2:3 * h + 3, :] * o_win[:, cols])
        o_ref[:, h * HEAD_DIM:(h + 1) * HEAD_DIM] = jnp.transpose(o_h).astype(o_ref.dtype)


def _nsa_attention(z, kc, vct, vst, vwt, gate_bias, *, batch, seq, units):
    tq = NSA_TQ
    nq = seq // tq
    n_cmp_rows = kc.shape[2]
    qb = HPG_A * HEAD_DIM
    kern = functools.partial(_nsa_kernel, tq=tq, seq=seq)
    n_tiles = seq // LANES
    return pl.pallas_call(
        kern,
        grid=(batch, N_KV_A, nq),
        in_specs=[
            pl.BlockSpec((tq, qb), lambda b, g, i: (b * nq + i, g)),
            pl.BlockSpec((None, None, n_cmp_rows, HEAD_DIM), lambda b, g, i: (b, g, 0, 0)),
            pl.BlockSpec((None, None, HEAD_DIM, n_cmp_rows), lambda b, g, i: (b, g, 0, 0)),
            pl.BlockSpec((seq, HEAD_DIM), lambda b, g, i: (b, units["k_slc"] + g)),
            pl.BlockSpec((None, None, n_tiles, HEAD_DIM, LANES), lambda b, g, i: (b, g, 0, 0, 0)),
            pl.BlockSpec((seq, HEAD_DIM), lambda b, g, i: (b, units["k_win"] + g)),
            pl.BlockSpec((None, None, n_tiles, HEAD_DIM, LANES), lambda b, g, i: (b, g, 0, 0, 0)),
            pl.BlockSpec((tq, LANES), lambda b, g, i: (b * nq + i, units["gates"] + g)),
            pl.BlockSpec((None, 1, LANES), lambda b, g, i: (g, 0, 0)),
        ],
        out_specs=pl.BlockSpec((tq, qb), lambda b, g, i: (b * nq + i, g)),
        out_shape=jax.ShapeDtypeStruct((batch * seq, A_Q), BF16),
        scratch_shapes=[
            pltpu.VMEM((1, HPG_A * tq), F32),
            pltpu.VMEM((1, HPG_A * tq), F32),
            pltpu.VMEM((HEAD_DIM, HPG_A * tq), F32),
        ],
        compiler_params=_compiler_params(("parallel", "parallel", "arbitrary")),
        name="nsa_attention",
    )(z, kc, vct, z, vst, z, vwt, z, gate_bias)


def _dil_kernel(q0_ref, q1_ref, q2_ref, k_ref, vt_ref, o_ref, *, tq):
    qi = pl.program_id(2)
    s0 = qi * tq
    t_row = s0 + lax.broadcasted_iota(jnp.int32, (1, tq), 1)
    m_run = jnp.full((1, tq), NEG_INF, F32)
    l_run = jnp.zeros((1, tq), F32)
    acc = jnp.zeros((HEAD_DIM, tq), F32)
    for q_ref, (w, r) in zip((q0_ref, q1_ref, q2_ref), DIL_CONFIGS):
        q = q_ref[...]
        span = w + tq
        ws = pl.multiple_of(jnp.maximum(s0 - w, 0), LANES)
        n_parts = -(-span // 768)
        part = span // n_parts
        assert part * n_parts == span and part % LANES == 0
        for pi in range(n_parts):
            k0 = pl.multiple_of(ws + pi * part, LANES)
            s = lax.dot_general(k_ref[pl.ds(k0, part), :], q, NT_DIMS,
                                preferred_element_type=F32)
            dist = t_row - (k0 + lax.broadcasted_iota(jnp.int32, (part, 1), 0))
            keep = (dist >= 0) & (dist <= w) & ((dist & (r - 1)) == 0)
            s = jnp.where(keep, s, NEG_INF)
            m_new = jnp.maximum(m_run, jnp.max(s, axis=0, keepdims=True))
            alpha = jnp.exp(m_run - m_new)
            p = jnp.where(keep, jnp.exp(s - m_new), 0.0)
            l_run = alpha * l_run + jnp.sum(p, axis=0, keepdims=True)
            p_bf = p.astype(BF16)
            t0 = k0 // LANES
            pv = jnp.dot(vt_ref[t0], p_bf[0:LANES, :], preferred_element_type=F32)
            for i in range(1, part // LANES):
                pv = pv + jnp.dot(vt_ref[t0 + i], p_bf[i * LANES:(i + 1) * LANES, :],
                                  preferred_element_type=F32)
            acc = alpha * acc + pv
            m_run = m_new
    o_ref[...] = jnp.transpose(acc / l_run).astype(o_ref.dtype)


def _dilated_attention(zb, k_sh, vt_sh, *, batch, seq):
    tq = DIL_TQ
    nq = seq // tq
    n_tiles = seq // LANES
    kern = functools.partial(_dil_kernel, tq=tq)

    def q_spec(gi):
        return pl.BlockSpec((tq, HEAD_DIM), lambda b, h, i: (b * nq + i, gi * DIL_HEADS + h))

    return pl.pallas_call(
        kern,
        grid=(batch, DIL_HEADS, nq),
        in_specs=[
            q_spec(0), q_spec(1), q_spec(2),
            pl.BlockSpec((seq, HEAD_DIM), lambda b, h, i: (b, h)),
            pl.BlockSpec((None, None, n_tiles, HEAD_DIM, LANES), lambda b, h, i: (b, h, 0, 0, 0)),
        ],
        out_specs=pl.BlockSpec((tq, HEAD_DIM), lambda b, h, i: (b * nq + i, h)),
        out_shape=jax.ShapeDtypeStruct((batch * seq, DIL_HEADS * HEAD_DIM), BF16),
        compiler_params=_compiler_params(("parallel", "parallel", "arbitrary")),
        name="dilated_attention",
    )(zb, zb, zb, k_sh, vt_sh)


def _mem_attn_kernel(q_ref, kv_ref, o_ref):
    for h in range(N_MEM_HEADS):
        cols = slice(h * HEAD_DIM, (h + 1) * HEAD_DIM)
        k = kv_ref[:, cols]
        v = kv_ref[:, MEM_Q + h * HEAD_DIM:MEM_Q + (h + 1) * HEAD_DIM]
        s = lax.dot_general(q_ref[:, cols], k, NT_DIMS, preferred_element_type=F32)
        e = jnp.exp(s - jnp.max(s, axis=-1, keepdims=True))
        p = e / jnp.sum(e, axis=-1, keepdims=True)
        o_ref[:, cols] = jnp.dot(p.astype(BF16), v, preferred_element_type=F32).astype(o_ref.dtype)


def _memory_attention(z, mkv, *, batch, seq, q_block):
    tq = MEM_TQ
    nq = seq // tq
    n_mem = mkv.shape[0] // batch
    return pl.pallas_call(
        _mem_attn_kernel,
        grid=(batch, nq),
        in_specs=[
            pl.BlockSpec((tq, MEM_Q), lambda b, i: (b * nq + i, q_block)),
            pl.BlockSpec((n_mem, 2 * MEM_Q), lambda b, i: (b, 0)),
        ],
        out_specs=pl.BlockSpec((tq, MEM_Q), lambda b, i: (b * nq + i, 0)),
        out_shape=jax.ShapeDtypeStruct((batch * seq, MEM_Q), BF16),
        compiler_params=_compiler_params(("parallel", "arbitrary")),
        name="memory_attention",
    )(z, mkv)


def _out_proj_kernel(a1_ref, a2_ref, w1_ref, w2_ref, h_ref, o_ref):
    o_ref[...] = (h_ref[...]
                  + jnp.dot(a1_ref[...], w1_ref[...], preferred_element_type=F32)
                  + jnp.dot(a2_ref[...], w2_ref[...], preferred_element_type=F32))


def _out_proj(a1, a2, w1_bf, w2_bf, h):
    m, d = h.shape
    tm, tn = min(OUT_TM, m), OUT_TN
    k1, k2 = a1.shape[1], a2.shape[1]
    return pl.pallas_call(
        _out_proj_kernel,
        grid=(m // tm, d // tn),
        in_specs=[
            pl.BlockSpec((tm, k1), lambda i, j: (i, 0)),
            pl.BlockSpec((tm, k2), lambda i, j: (i, 0)),
            pl.BlockSpec((k1, tn), lambda i, j: (0, j)),
            pl.BlockSpec((k2, tn), lambda i, j: (0, j)),
            pl.BlockSpec((tm, tn), lambda i, j: (i, j)),
        ],
        out_specs=pl.BlockSpec((tm, tn), lambda i, j: (i, j)),
        out_shape=jax.ShapeDtypeStruct((m, d), F32),
        compiler_params=_compiler_params(("parallel", "arbitrary")),
        name="out_proj",
    )(a1, a2, w1_bf, w2_bf, h)


def _ffn_kernel(x_ref, g_ref, wg_ref, wu_ref, wd_ref, fg_ref, o_ref, xn_ref, acc_ref, *,
                final_norm):
    f = pl.program_id(1)

    @pl.when(f == 0)
    def _():
        xn_ref[...] = _rms_rows(x_ref[...], g_ref[...]).astype(BF16)
        acc_ref[...] = jnp.zeros(acc_ref.shape, F32)

    xn = xn_ref[...]
    gate = jnp.dot(xn, wg_ref[...], preferred_element_type=F32)
    up = jnp.dot(xn, wu_ref[...], preferred_element_type=F32)
    act = (gate * jax.nn.sigmoid(gate) * up).astype(BF16)
    acc_ref[...] += jnp.dot(act, wd_ref[...], preferred_element_type=F32)

    @pl.when(f == pl.num_programs(1) - 1)
    def _():
        y = x_ref[...] + acc_ref[...]
        if final_norm:
            y = _rms_rows(y, fg_ref[...])
        o_ref[...] = y


def _ffn(h, g, wg_bf, wu_bf, wd_bf, final_gain, *, final_norm):
    m, d = h.shape
    dff = wg_bf.shape[1]
    tm, tf = min(FFN_TM, m), FFN_TF
    assert m % tm == 0 and dff % tf == 0
    kern = functools.partial(_ffn_kernel, final_norm=final_norm)
    return pl.pallas_call(
        kern,
        grid=(m // tm, dff // tf),
        in_specs=[
            pl.BlockSpec((tm, d), lambda i, f: (i, 0)),
            pl.BlockSpec((1, d), lambda i, f: (0, 0)),
            pl.BlockSpec((d, tf), lambda i, f: (0, f)),
            pl.BlockSpec((d, tf), lambda i, f: (0, f)),
            pl.BlockSpec((tf, d), lambda i, f: (f, 0)),
            pl.BlockSpec((1, d), lambda i, f: (0, 0)),
        ],
        out_specs=pl.BlockSpec((tm, d), lambda i, f: (i, 0)),
        out_shape=jax.ShapeDtypeStruct((m, d), F32),
        scratch_shapes=[pltpu.VMEM((tm, d), BF16), pltpu.VMEM((tm, d), F32)],
        compiler_params=_compiler_params(("parallel", "arbitrary")),
        name="ffn",
    )(h, g.reshape(1, d), wg_bf, wu_bf, wd_bf, final_gain.reshape(1, d))


def _rope_tables(seq):
    inv = 1.0 / (ROPE_THETA ** (jnp.arange(0, HEAD_DIM, 2, dtype=F32) / HEAD_DIM))
    ang = jnp.arange(seq, dtype=F32)[:, None] * inv[None, :]
    cos, sin = jnp.cos(ang), jnp.sin(ang)
    return jnp.concatenate([cos, cos], axis=1), jnp.concatenate([-sin, sin], axis=1)


def _tiles_transposed(v, batch, seq, heads):
    v = v.reshape(batch, seq // LANES, LANES, heads, HEAD_DIM)
    return jnp.transpose(v, (0, 3, 1, 4, 2))


A_UNITS = {"q": 0, "k_cmp": 12, "k_slc": 14, "k_win": 16, "v_cmp": 18, "v_slc": 20,
           "v_win": 22, "mem_q": 24, "gates": 28}
A_TN = 6 * HEAD_DIM
A_NPAD = 30 * HEAD_DIM
A_ROPE_BLOCKS = 3


def _layer_a_weight(w_in):
    kv0 = A_Q
    def kv_cols(branch):
        return np.arange(kv0 + branch * N_KV_A * HEAD_DIM, kv0 + (branch + 1) * N_KV_A * HEAD_DIM)
    gate0 = A_Q + A_KV
    mem0 = gate0 + A_GATE
    perm = np.concatenate([np.arange(A_Q), kv_cols(0), kv_cols(2), kv_cols(4),
                           kv_cols(1), kv_cols(3), kv_cols(5), np.arange(mem0, mem0 + MEM_Q)])
    per_group = 3 * HPG_A
    gate_units = [jnp.pad(w_in[:, gate0 + g * per_group:gate0 + (g + 1) * per_group],
                          ((0, 0), (0, LANES - per_group))) for g in range(N_KV_A)]
    w = jnp.concatenate([w_in[:, perm]] + gate_units, axis=1)
    assert w.shape[1] == A_NPAD
    scale = np.ones((1, A_NPAD), np.float32)
    scale[0, :A_Q] = SCALE
    scale[0, A_UNITS["mem_q"] * HEAD_DIM:A_UNITS["mem_q"] * HEAD_DIM + MEM_Q] = SCALE
    return w.astype(BF16), jnp.asarray(scale)


def _layer_a(h, mem, cosf, sinf, p, *, batch, seq):
    w_in_bf, col_scale = _layer_a_weight(p["w_in"])
    z = _norm_proj(h, p["norm_attn"], w_in_bf, col_scale, cosf, sinf,
                   tn=A_TN, n_rope_blocks=A_ROPE_BLOCKS, seq=seq)

    def unit_cols(name, n_units):
        return z[:, A_UNITS[name] * HEAD_DIM:(A_UNITS[name] + n_units) * HEAD_DIM]

    per_row = CMP_LEN // 2
    def cmp_rows(name):
        x = unit_cols(name, N_KV_A).reshape(batch, seq // per_row, per_row, N_KV_A, HEAD_DIM)
        return jnp.transpose(x, (0, 3, 1, 2, 4)).reshape(batch * N_KV_A, seq // per_row,
                                                          per_row * HEAD_DIM)
    x2 = jnp.stack([cmp_rows("k_cmp"), cmp_rows("v_cmp")])
    def pe_rows(pe):
        return jnp.pad(pe.reshape(2, per_row * HEAD_DIM), ((0, 6), (0, 0)))
    pe2 = jnp.stack([pe_rows(p["cmp_pe_k"]), pe_rows(p["cmp_pe_v"])])
    w1 = jnp.stack([p["cmp_w1_k"], p["cmp_w1_v"]]).astype(BF16)
    w2 = jnp.stack([p["cmp_w2_k"], p["cmp_w2_v"]]).astype(BF16)
    cmp_out = _compress(x2, pe2, w1, w2)
    n_rows = seq // per_row
    kc = cmp_out[0].reshape(batch, N_KV_A, n_rows, HEAD_DIM)
    vct = jnp.transpose(cmp_out[1].reshape(batch, N_KV_A, n_rows, HEAD_DIM), (0, 1, 3, 2))

    vst = _tiles_transposed(unit_cols("v_slc", N_KV_A), batch, seq, N_KV_A)
    vwt = _tiles_transposed(unit_cols("v_win", N_KV_A), batch, seq, N_KV_A)
    gb = jnp.pad(p["gate_bias"].reshape(N_KV_A, 1, 3 * HPG_A),
                 ((0, 0), (0, 0), (0, LANES - 3 * HPG_A)))
    o_nsa = _nsa_attention(z, kc, vct, vst, vwt, gb, batch=batch, seq=seq, units=A_UNITS)

    mkv = _mem_kv(mem, p["norm_mem"], p["w_mem_kv"])
    o_mem = _memory_attention(z, mkv, batch=batch, seq=seq,
                              q_block=A_UNITS["mem_q"] * HEAD_DIM // MEM_Q)
    w_out = p["w_out"].astype(BF16)
    h = _out_proj(o_nsa, o_mem, w_out[:A_Q], w_out[A_Q:], h)
    return h


def _mem_kv(mem, norm_mem, w_mem_kv):
    b, m, d = mem.shape
    ones = jnp.ones((1, w_mem_kv.shape[1]), F32)
    dummy = jnp.zeros((m, HEAD_DIM), F32)
    return _norm_proj(mem.reshape(b * m, d), norm_mem, w_mem_kv.astype(BF16), ones, dummy, dummy,
                      tn=MEM_Q, n_rope_blocks=0, seq=m, tm=m)


def kernel(x, mem, a_norm_attn, a_w_in, a_gate_bias, a_cmp_pe_k, a_cmp_w1_k, a_cmp_w2_k, a_cmp_pe_v, a_cmp_w1_v, a_cmp_w2_v, a_norm_mem, a_w_mem_kv, a_w_out, a_norm_ffn, a_w_gate, a_w_up, a_w_down, kv_norm, w_kv_shared, b_norm_attn, b_w_in, b_norm_mem, b_w_mem_kv, b_w_out, b_norm_ffn, b_w_gate, b_w_up, b_w_down, final_norm):
    batch, seq, d = x.shape
    n_a = a_w_in.shape[0]
    n_b = b_w_in.shape[0]
    cosf, sinf = _rope_tables(seq)
    h = x.reshape(batch * seq, d)
    unit_gain = jnp.ones((d,), F32)

    for l in range(n_a):
        p = {"norm_attn": a_norm_attn[l], "w_in": a_w_in[l], "gate_bias": a_gate_bias[l],
             "cmp_pe_k": a_cmp_pe_k[l], "cmp_w1_k": a_cmp_w1_k[l], "cmp_w2_k": a_cmp_w2_k[l],
             "cmp_pe_v": a_cmp_pe_v[l], "cmp_w1_v": a_cmp_w1_v[l], "cmp_w2_v": a_cmp_w2_v[l],
             "norm_mem": a_norm_mem[l], "w_mem_kv": a_w_mem_kv[l], "w_out": a_w_out[l]}
        h = _layer_a(h, mem, cosf, sinf, p, batch=batch, seq=seq)
        last = (l == n_a - 1) and n_b == 0
        h = _ffn(h, a_norm_ffn[l], a_w_gate[l].astype(BF16), a_w_up[l].astype(BF16),
                 a_w_down[l].astype(BF16), final_norm if last else unit_gain, final_norm=last)

    if n_b > 0:
        kv_w = w_kv_shared.astype(BF16)
        n_kv = kv_w.shape[1]
        kv = _norm_proj(h, kv_norm, kv_w, jnp.ones((1, n_kv), F32), cosf, sinf,
                        tn=n_kv // 2, n_rope_blocks=1, seq=seq)
        k_sh = kv[:, :n_kv // 2]
        vt_sh = _tiles_transposed(kv[:, n_kv // 2:], batch, seq, DIL_HEADS)
        b_scale = jnp.full((1, B_Q + MEM_Q), SCALE, F32)
        for l in range(n_b):
            zb = _norm_proj(h, b_norm_attn[l], b_w_in[l].astype(BF16), b_scale, cosf, sinf,
                            tn=MEM_Q, n_rope_blocks=B_Q // MEM_Q, seq=seq)
            o_dil = _dilated_attention(zb, k_sh, vt_sh, batch=batch, seq=seq)
            mkv = _mem_kv(mem, b_norm_mem[l], b_w_mem_kv[l])
            o_mem = _memory_attention(zb, mkv, batch=batch, seq=seq, q_block=B_Q // MEM_Q)
            w_out = b_w_out[l].astype(BF16)
            n_dil = DIL_HEADS * HEAD_DIM
            h = _out_proj(o_dil, o_mem, w_out[:n_dil], w_out[n_dil:], h)
            last = l == n_b - 1
            h = _ffn(h, b_norm_ffn[l], b_w_gate[l].astype(BF16), b_w_up[l].astype(BF16),
                     b_w_down[l].astype(BF16), final_norm if last else unit_gain, final_norm=last)

    return h.reshape(batch, seq, d)
```

```python
import functools
import math

import numpy as np
import jax
import jax.numpy as jnp
from jax import lax
from jax.experimental import pallas as pl
from jax.experimental.pallas import tpu as pltpu

F32 = jnp.float32
BF16 = jnp.bfloat16

HEAD_DIM = 128
N_HEADS_A = 12
N_KV_A = 2
HPG_A = N_HEADS_A // N_KV_A
CMP_LEN = 32
CMP_STRIDE = 16
CMP_HIDDEN = 256
SLC_BLK = 64
SLC_SHIFT = SLC_BLK.bit_length() - 1
N_SEL = 16
WIN_A = 512
DIL_CONFIGS = ((128, 1), (512, 4), (2048, 16))
N_DIL_GROUPS = len(DIL_CONFIGS)
DIL_HEADS = 4
N_MEM_HEADS = 4
ROPE_THETA = 10000.0
EPS = 1e-6
NEG_INF = -1e30
TINY = 1e-30
SCALE = HEAD_DIM ** -0.5
LOG2E = math.log2(math.e)

A_Q = N_HEADS_A * HEAD_DIM
A_KV = 6 * N_KV_A * HEAD_DIM
A_GATE = 3 * N_HEADS_A
MEM_Q = N_MEM_HEADS * HEAD_DIM
B_Q = N_DIL_GROUPS * DIL_HEADS * HEAD_DIM

LANES = 128
SUBLANES = 8
VMEM_LIMIT_BYTES = 56 * 1024 * 1024

PROJ_TM = 1024
FFN_TM = 512
FFN_TF = 512
OUT_TM = 1024
OUT_TN = 1024
NSA_TQ = 128
NSA_KC = 512
DIL_TQ = 256
MEM_TQ = 512

NT_DIMS = (((1,), (1,)), ((), ()))


def _compiler_params(semantics):
    return pltpu.CompilerParams(dimension_semantics=semantics,
                                vmem_limit_bytes=VMEM_LIMIT_BYTES)


def _rms_rows(x, g):
    ms = jnp.mean(x * x, axis=-1, keepdims=True)
    return x * lax.rsqrt(ms + EPS) * g


def _dot(a, b):
    return jnp.dot(a, b, preferred_element_type=F32)


def _dot_nt(a, b):
    return lax.dot_general(a, b, NT_DIMS, preferred_element_type=F32)


def _norm_proj_kernel(x_ref, g_ref, w_ref, cs_ref, cos_ref, sin_ref, o_ref, xn_ref, *,
                      n_rope_blocks, tn):
    j = pl.program_id(1)

    @pl.when(j == 0)
    def _():
        xn_ref[...] = _rms_rows(x_ref[...], g_ref[...]).astype(BF16)

    acc = _dot(xn_ref[...], w_ref[...]) * cs_ref[...]

    if n_rope_blocks > 0:
        @pl.when(j < n_rope_blocks)
        def _():
            c = cos_ref[...]
            s = sin_ref[...]
            for h in range(tn // HEAD_DIM):
                y = acc[:, h * HEAD_DIM:(h + 1) * HEAD_DIM]
                rot = pltpu.roll(y, HEAD_DIM // 2, 1)
                o_ref[:, h * HEAD_DIM:(h + 1) * HEAD_DIM] = (y * c + rot * s).astype(o_ref.dtype)

        @pl.when(j >= n_rope_blocks)
        def _():
            o_ref[...] = acc.astype(o_ref.dtype)
    else:
        o_ref[...] = acc.astype(o_ref.dtype)


def _norm_proj(x, g, w_bf, col_scale, cosf, sinf, *, tn, n_rope_blocks, seq, tm=PROJ_TM):
    m, d = x.shape
    n = w_bf.shape[1]
    tm = min(tm, m)
    assert m % tm == 0 and n % tn == 0 and seq % tm == 0
    pos_blocks = seq // tm
    kern = functools.partial(_norm_proj_kernel, n_rope_blocks=n_rope_blocks, tn=tn)
    return pl.pallas_call(
        kern,
        grid=(m // tm, n // tn),
        in_specs=[
            pl.BlockSpec((tm, d), lambda i, j: (i, 0)),
            pl.BlockSpec((1, d), lambda i, j: (0, 0)),
            pl.BlockSpec((d, tn), lambda i, j: (0, j)),
            pl.BlockSpec((1, tn), lambda i, j: (0, j)),
            pl.BlockSpec((tm, HEAD_DIM), lambda i, j: (i % pos_blocks, 0)),
            pl.BlockSpec((tm, HEAD_DIM), lambda i, j: (i % pos_blocks, 0)),
        ],
        out_specs=pl.BlockSpec((tm, tn), lambda i, j: (i, j)),
        out_shape=jax.ShapeDtypeStruct((m, n), BF16),
        scratch_shapes=[pltpu.VMEM((tm, d), BF16)],
        compiler_params=_compiler_params(("parallel", "arbitrary")),
        name="norm_proj",
    )(x, g.reshape(1, d), w_bf, col_scale, cosf, sinf)


def _compress_kernel(x_ref, pe_ref, w1_ref, w2_ref, o_ref):
    half = (CMP_LEN // 2) * HEAD_DIM
    x = x_ref[...].astype(F32)
    xlo = (x + pe_ref[0:1, :]).astype(BF16)
    xhi = (x + pe_ref[1:2, :]).astype(BF16)
    ylo = _dot(xlo, w1_ref[:half, :])
    yhi = _dot(xhi, w1_ref[half:, :])
    n_rows = x.shape[0]
    hid = ylo + pltpu.roll(yhi, n_rows - 1, 0)
    act = (hid * jax.nn.sigmoid(hid)).astype(BF16)
    o_ref[...] = _dot(act, w2_ref[...]).astype(o_ref.dtype)


def _compress(x2, pe2, w1_bf, w2_bf):
    _, bg, nrow, wide = x2.shape
    return pl.pallas_call(
        _compress_kernel,
        grid=(2, bg),
        in_specs=[
            pl.BlockSpec((None, None, nrow, wide), lambda t, i: (t, i, 0, 0)),
            pl.BlockSpec((None, 8, wide), lambda t, i: (t, 0, 0)),
            pl.BlockSpec((None, 2 * wide, CMP_HIDDEN), lambda t, i: (t, 0, 0)),
            pl.BlockSpec((None, CMP_HIDDEN, HEAD_DIM), lambda t, i: (t, 0, 0)),
        ],
        out_specs=pl.BlockSpec((None, None, nrow, HEAD_DIM), lambda t, i: (t, i, 0, 0)),
        out_shape=jax.ShapeDtypeStruct((2, bg, nrow, HEAD_DIM), BF16),
        compiler_params=_compiler_params(("parallel", "arbitrary")),
        name="nsa_compress",
    )(x2, pe2, w1_bf, w2_bf)


def _block_ranks(score, jrow):
    n_blk = score.shape[0]
    groups = n_blk // SUBLANES
    blocks = [score[SUBLANES * r:SUBLANES * (r + 1), :] for r in range(groups)]
    rows = [jrow[SUBLANES * r:SUBLANES * (r + 1), :] for r in range(groups)]
    ranks = [jnp.zeros(blocks[0].shape, F32) for _ in range(groups)]
    for j in range(n_blk):
        rj = score[j:j + 1, :]
        for r in range(groups):
            if r > j // SUBLANES:
                ahead = rj >= blocks[r]
            elif r < j // SUBLANES:
                ahead = rj > blocks[r]
            else:
                ahead = (rj > blocks[r]) | ((rj == blocks[r]) & (rows[r] > j))
            ranks[r] = ranks[r] + jnp.where(ahead, 1.0, 0.0)
    return jnp.concatenate(ranks, axis=0)


def _nsa_kernel(q_ref, kc_ref, vct_ref, ksa_ref, vst_ref, kw_ref, vwt_ref, gz_ref, gb_ref,
                o_ref, qa_ref, sa_ref, sb_ref, m_ref, l_ref, acc_ref, *, tq, seq):
    hq = HPG_A
    n_cmp_rows = kc_ref.shape[0]
    n_slc = seq // SLC_BLK
    qi = pl.program_id(2)
    s0 = qi * tq
    t_row = s0 + lax.broadcasted_iota(jnp.int32, (1, tq), 1)

    def lanes(h):
        return slice(h * tq, (h + 1) * tq)

    def tile_heads(x):
        return jnp.concatenate([x] * hq, axis=1)

    for h in range(hq):
        qa_ref[lanes(h), 0:HEAD_DIM] = q_ref[:, h * HEAD_DIM:(h + 1) * HEAD_DIM]
    q6 = qa_ref[:, 0:HEAD_DIM]

    c_end = lax.broadcasted_iota(jnp.int32, (n_cmp_rows, 1), 0) * CMP_STRIDE + (CMP_LEN - 1)
    cvalid = tile_heads(c_end <= t_row)
    sc = jnp.where(cvalid, _dot_nt(kc_ref[...], q6), NEG_INF)
    ec = jnp.where(cvalid, jnp.exp2(sc - jnp.max(sc, axis=0, keepdims=True)), 0.0)
    pc = ec * (1.0 / jnp.maximum(jnp.sum(ec, axis=0, keepdims=True), TINY))
    o_cmp = _dot(vct_ref[...], pc.astype(BF16))
    psum = pc[:, lanes(0)]
    for h in range(1, hq):
        psum = psum + pc[:, lanes(h)]

    jrow = lax.broadcasted_iota(jnp.int32, (n_slc, 1), 0)
    ccol = lax.broadcasted_iota(jnp.int32, (1, n_cmp_rows), 1)
    lo = (SLC_BLK // CMP_STRIDE) * jrow - (CMP_LEN // CMP_STRIDE - 1)
    hi = (SLC_BLK // CMP_STRIDE) * jrow + (SLC_BLK // CMP_STRIDE - 1)
    mmap = jnp.where((ccol >= lo) & (ccol <= hi), 1.0, 0.0).astype(BF16)
    p1 = psum.astype(BF16)
    r1 = psum - p1.astype(F32)
    p2 = r1.astype(BF16)
    p3 = (r1 - p2.astype(F32)).astype(BF16)
    imp = _dot(mmap, p1) + _dot(mmap, p2) + _dot(mmap, p3)
    cur = t_row >> SLC_SHIFT
    forced = (jrow == 0) | (jrow == cur) | (jrow == cur - 1)
    score = jnp.where(forced, 1e9, jnp.where(jrow <= cur, imp, -1e9))
    rank = _block_ranks(score, jrow)
    sel_bias = jnp.where((rank < min(N_SEL, n_slc)) & (jrow <= cur), 0.0, NEG_INF)

    assert n_slc <= LANES and tq == LANES
    bias_q = jnp.transpose(jnp.concatenate(
        [sel_bias, jnp.zeros((LANES - n_slc, tq), F32)], axis=0)).astype(BF16)
    for h in range(hq):
        qa_ref[lanes(h), HEAD_DIM:HEAD_DIM + LANES] = bias_q

    m_ref[...] = jnp.full(m_ref.shape, NEG_INF, F32)
    l_ref[...] = jnp.zeros(l_ref.shape, F32)
    acc_ref[...] = jnp.zeros(acc_ref.shape, F32)
    tiles_per_chunk = NSA_KC // LANES

    def scores(c, s_ref):
        k0 = pl.multiple_of(c * NSA_KC, NSA_KC)
        s_ref[...] = _dot_nt(ksa_ref[pl.ds(k0, NSA_KC), :], qa_ref[...])

    def softmax_pv(c, s_ref):
        s = s_ref[...]
        m_old = m_ref[...]
        m_new = jnp.maximum(m_old, jnp.max(s, axis=0, keepdims=True))
        alpha = jnp.exp2(m_old - m_new)
        p = jnp.exp2(s - m_new)
        l_ref[...] = alpha * l_ref[...] + jnp.sum(p, axis=0, keepdims=True)
        vt = jnp.concatenate([vst_ref[c * tiles_per_chunk + i] for i in range(tiles_per_chunk)],
                             axis=1)
        acc_ref[...] = alpha * acc_ref[...] + _dot(vt, p.astype(BF16))
        m_ref[...] = m_new

    last = s0 // NSA_KC
    scores(0, sa_ref)

    def chunk_pair(i, carry):
        scores(2 * i + 1, sb_ref)
        softmax_pv(2 * i, sa_ref)
        scores(2 * i + 2, sa_ref)
        softmax_pv(2 * i + 1, sb_ref)
        return carry

    lax.fori_loop(0, last // 2, chunk_pair, 0)

    def last_chunk(s_ref):
        diag = pl.multiple_of(s0 - last * NSA_KC, LANES)
        krow = lax.broadcasted_iota(jnp.int32, (tq, 1), 0)
        lane = lax.broadcasted_iota(jnp.int32, (1, tq), 1)
        causal = tile_heads(jnp.where(krow <= lane, 0.0, NEG_INF))
        s_ref[pl.ds(diag, tq), :] = s_ref[pl.ds(diag, tq), :] + causal
        softmax_pv(last, s_ref)

    @pl.when(last % 2 == 0)
    def _():
        last_chunk(sa_ref)

    @pl.when(last % 2 == 1)
    def _():
        scores(last, sb_ref)
        softmax_pv(last - 1, sa_ref)
        last_chunk(sb_ref)

    wtiles = (WIN_A + tq) // LANES
    wlen = wtiles * LANES
    ws = pl.multiple_of(jnp.maximum(s0 - WIN_A, 0), LANES)
    dist = t_row - (ws + lax.broadcasted_iota(jnp.int32, (wlen, 1), 0))
    wbias = tile_heads(jnp.where((dist >= 0) & (dist < WIN_A), 0.0, NEG_INF))
    sw = _dot_nt(kw_ref[pl.ds(ws, wlen), :], q6) + wbias
    ew = jnp.exp2(sw - jnp.max(sw, axis=0, keepdims=True))
    wt0 = ws // LANES
    vwt = jnp.concatenate([vwt_ref[wt0 + i] for i in range(wtiles)], axis=1)
    o_win = _dot(vwt, ew.astype(BF16)) * (1.0 / jnp.sum(ew, axis=0, keepdims=True))

    gates_t = jnp.transpose(jax.nn.sigmoid(gz_ref[...].astype(F32) + gb_ref[...]))
    o_slc = acc_ref[...] * (1.0 / l_ref[...])
    for h in range(hq):
        o_h = (gates_t[3 * h:3 * h + 1, :] * o_cmp[:, lanes(h)]
               + gates_t[3 * h + 1:3 * h + 2, :] * o_slc[:, lanes(h)]
               + gates_t[3 * h + 2:3 * h + 3, :] * o_win[:, lanes(h)])
        o_ref[:, h * HEAD_DIM:(h + 1) * HEAD_DIM] = jnp.transpose(o_h).astype(o_ref.dtype)


def _nsa_attention(z, kc, vct, ksa, vst, vwt, gate_bias, *, batch, seq, units):
    tq = NSA_TQ
    nq = seq // tq
    n_cmp_rows = kc.shape[2]
    qb = HPG_A * HEAD_DIM
    kern = functools.partial(_nsa_kernel, tq=tq, seq=seq)
    n_tiles = seq // LANES
    return pl.pallas_call(
        kern,
        grid=(batch, N_KV_A, nq),
        in_specs=[
            pl.BlockSpec((tq, qb), lambda b, g, i: (b * nq + i, g)),
            pl.BlockSpec((None, None, n_cmp_rows, HEAD_DIM), lambda b, g, i: (b, g, 0, 0)),
            pl.BlockSpec((None, None, HEAD_DIM, n_cmp_rows), lambda b, g, i: (b, g, 0, 0)),
            pl.BlockSpec((None, None, seq, HEAD_DIM + LANES), lambda b, g, i: (b, g, 0, 0)),
            pl.BlockSpec((None, None, n_tiles, HEAD_DIM, LANES), lambda b, g, i: (b, g, 0, 0, 0)),
            pl.BlockSpec((seq, HEAD_DIM), lambda b, g, i: (b, units["k_win"] + g)),
            pl.BlockSpec((None, None, n_tiles, HEAD_DIM, LANES), lambda b, g, i: (b, g, 0, 0, 0)),
            pl.BlockSpec((tq, LANES), lambda b, g, i: (b * nq + i, units["gates"] + g)),
            pl.BlockSpec((None, 1, LANES), lambda b, g, i: (g, 0, 0)),
        ],
        out_specs=pl.BlockSpec((tq, qb), lambda b, g, i: (b * nq + i, g)),
        out_shape=jax.ShapeDtypeStruct((batch * seq, A_Q), BF16),
        scratch_shapes=[
            pltpu.VMEM((HPG_A * tq, HEAD_DIM + LANES), BF16),
            pltpu.VMEM((NSA_KC, HPG_A * tq), F32),
            pltpu.VMEM((NSA_KC, HPG_A * tq), F32),
            pltpu.VMEM((1, HPG_A * tq), F32),
            pltpu.VMEM((1, HPG_A * tq), F32),
            pltpu.VMEM((HEAD_DIM, HPG_A * tq), F32),
        ],
        compiler_params=_compiler_params(("parallel", "parallel", "arbitrary")),
        name="nsa_attention",
    )(z, kc, vct, ksa, vst, z, vwt, z, gate_bias)


def _dil_kernel(q0_ref, q1_ref, q2_ref, k_ref, vt_ref, o_ref, *, tq):
    qi = pl.program_id(2)
    s0 = qi * tq
    t_row = s0 + lax.broadcasted_iota(jnp.int32, (1, tq), 1)
    m_run = jnp.full((1, tq), NEG_INF, F32)
    l_run = jnp.zeros((1, tq), F32)
    acc = jnp.zeros((HEAD_DIM, tq), F32)
    for q_ref, (w, r) in zip((q0_ref, q1_ref, q2_ref), DIL_CONFIGS):
        q = q_ref[...]
        span = w + tq
        ws = pl.multiple_of(jnp.maximum(s0 - w, 0), LANES)
        n_parts = -(-span // 768)
        part = span // n_parts
        assert part * n_parts == span and part % LANES == 0
        for pi in range(n_parts):
            k0 = pl.multiple_of(ws + pi * part, LANES)
            s = _dot_nt(k_ref[pl.ds(k0, part), :], q)
            dist = t_row - (k0 + lax.broadcasted_iota(jnp.int32, (part, 1), 0))
            keep = (dist >= 0) & (dist <= w) & ((dist & (r - 1)) == 0)
            s = jnp.where(keep, s, NEG_INF)
            m_new = jnp.maximum(m_run, jnp.max(s, axis=0, keepdims=True))
            alpha = jnp.exp(m_run - m_new)
            p = jnp.where(keep, jnp.exp(s - m_new), 0.0)
            l_run = alpha * l_run + jnp.sum(p, axis=0, keepdims=True)
            p_bf = p.astype(BF16)
            t0 = k0 // LANES
            pv = _dot(vt_ref[t0], p_bf[0:LANES, :])
            for i in range(1, part // LANES):
                pv = pv + _dot(vt_ref[t0 + i], p_bf[i * LANES:(i + 1) * LANES, :])
            acc = alpha * acc + pv
            m_run = m_new
    o_ref[...] = jnp.transpose(acc / l_run).astype(o_ref.dtype)


def _dilated_attention(zb, k_sh, vt_sh, *, batch, seq):
    tq = DIL_TQ
    nq = seq // tq
    n_tiles = seq // LANES
    kern = functools.partial(_dil_kernel, tq=tq)

    def q_spec(gi):
        return pl.BlockSpec((tq, HEAD_DIM), lambda b, h, i: (b * nq + i, gi * DIL_HEADS + h))

    return pl.pallas_call(
        kern,
        grid=(batch, DIL_HEADS, nq),
        in_specs=[
            q_spec(0), q_spec(1), q_spec(2),
            pl.BlockSpec((seq, HEAD_DIM), lambda b, h, i: (b, h)),
            pl.BlockSpec((None, None, n_tiles, HEAD_DIM, LANES), lambda b, h, i: (b, h, 0, 0, 0)),
        ],
        out_specs=pl.BlockSpec((tq, HEAD_DIM), lambda b, h, i: (b * nq + i, h)),
        out_shape=jax.ShapeDtypeStruct((batch * seq, DIL_HEADS * HEAD_DIM), BF16),
        compiler_params=_compiler_params(("parallel", "parallel", "arbitrary")),
        name="dilated_attention",
    )(zb, zb, zb, k_sh, vt_sh)


def _mem_attn_kernel(q_ref, kv_ref, o_ref):
    for h in range(N_MEM_HEADS):
        cols = slice(h * HEAD_DIM, (h + 1) * HEAD_DIM)
        k = kv_ref[:, cols]
        v = kv_ref[:, MEM_Q + h * HEAD_DIM:MEM_Q + (h + 1) * HEAD_DIM]
        s = _dot_nt(q_ref[:, cols], k)
        e = jnp.exp(s - jnp.max(s, axis=-1, keepdims=True))
        p = e / jnp.sum(e, axis=-1, keepdims=True)
        o_ref[:, cols] = _dot(p.astype(BF16), v).astype(o_ref.dtype)


def _memory_attention(z, mkv, *, batch, seq, q_block):
    tq = MEM_TQ
    nq = seq // tq
    n_mem = mkv.shape[0] // batch
    return pl.pallas_call(
        _mem_attn_kernel,
        grid=(batch, nq),
        in_specs=[
            pl.BlockSpec((tq, MEM_Q), lambda b, i: (b * nq + i, q_block)),
            pl.BlockSpec((n_mem, 2 * MEM_Q), lambda b, i: (b, 0)),
        ],
        out_specs=pl.BlockSpec((tq, MEM_Q), lambda b, i: (b * nq + i, 0)),
        out_shape=jax.ShapeDtypeStruct((batch * seq, MEM_Q), BF16),
        compiler_params=_compiler_params(("parallel", "arbitrary")),
        name="memory_attention",
    )(z, mkv)


def _out_proj_kernel(a1_ref, a2_ref, w1_ref, w2_ref, h_ref, o_ref):
    o_ref[...] = h_ref[...] + _dot(a1_ref[...], w1_ref[...]) + _dot(a2_ref[...], w2_ref[...])


def _out_proj(a1, a2, w1_bf, w2_bf, h):
    m, d = h.shape
    tm, tn = min(OUT_TM, m), OUT_TN
    k1, k2 = a1.shape[1], a2.shape[1]
    return pl.pallas_call(
        _out_proj_kernel,
        grid=(m // tm, d // tn),
        in_specs=[
            pl.BlockSpec((tm, k1), lambda i, j: (i, 0)),
            pl.BlockSpec((tm, k2), lambda i, j: (i, 0)),
            pl.BlockSpec((k1, tn), lambda i, j: (0, j)),
            pl.BlockSpec((k2, tn), lambda i, j: (0, j)),
            pl.BlockSpec((tm, tn), lambda i, j: (i, j)),
        ],
        out_specs=pl.BlockSpec((tm, tn), lambda i, j: (i, j)),
        out_shape=jax.ShapeDtypeStruct((m, d), F32),
        compiler_params=_compiler_params(("parallel", "arbitrary")),
        name="out_proj",
    )(a1, a2, w1_bf, w2_bf, h)


def _ffn_kernel(x_ref, g_ref, wg_ref, wu_ref, wd_ref, fg_ref, o_ref, xn_ref, acc_ref, *,
                final_norm):
    f = pl.program_id(1)

    @pl.when(f == 0)
    def _():
        xn_ref[...] = _rms_rows(x_ref[...], g_ref[...]).astype(BF16)
        acc_ref[...] = jnp.zeros(acc_ref.shape, F32)

    xn = xn_ref[...]
    gate = _dot(xn, wg_ref[...])
    up = _dot(xn, wu_ref[...])
    act = (gate * jax.nn.sigmoid(gate) * up).astype(BF16)
    acc_ref[...] += _dot(act, wd_ref[...])

    @pl.when(f == pl.num_programs(1) - 1)
    def _():
        y = x_ref[...] + acc_ref[...]
        if final_norm:
            y = _rms_rows(y, fg_ref[...])
        o_ref[...] = y


def _ffn(h, g, wg_bf, wu_bf, wd_bf, final_gain, *, final_norm):
    m, d = h.shape
    dff = wg_bf.shape[1]
    tm, tf = min(FFN_TM, m), FFN_TF
    assert m % tm == 0 and dff % tf == 0
    kern = functools.partial(_ffn_kernel, final_norm=final_norm)
    return pl.pallas_call(
        kern,
        grid=(m // tm, dff // tf),
        in_specs=[
            pl.BlockSpec((tm, d), lambda i, f: (i, 0)),
            pl.BlockSpec((1, d), lambda i, f: (0, 0)),
            pl.BlockSpec((d, tf), lambda i, f: (0, f)),
            pl.BlockSpec((d, tf), lambda i, f: (0, f)),
            pl.BlockSpec((tf, d), lambda i, f: (f, 0)),
            pl.BlockSpec((1, d), lambda i, f: (0, 0)),
        ],
        out_specs=pl.BlockSpec((tm, d), lambda i, f: (i, 0)),
        out_shape=jax.ShapeDtypeStruct((m, d), F32),
        scratch_shapes=[pltpu.VMEM((tm, d), BF16), pltpu.VMEM((tm, d), F32)],
        compiler_params=_compiler_params(("parallel", "arbitrary")),
        name="ffn",
    )(h, g.reshape(1, d), wg_bf, wu_bf, wd_bf, final_gain.reshape(1, d))


def _rope_tables(seq):
    inv = 1.0 / (ROPE_THETA ** (jnp.arange(0, HEAD_DIM, 2, dtype=F32) / HEAD_DIM))
    ang = jnp.arange(seq, dtype=F32)[:, None] * inv[None, :]
    cos, sin = jnp.cos(ang), jnp.sin(ang)
    return jnp.concatenate([cos, cos], axis=1), jnp.concatenate([-sin, sin], axis=1)


def _tiles_transposed(v, batch, seq, heads):
    v = v.reshape(batch, seq // LANES, LANES, heads, HEAD_DIM)
    return jnp.transpose(v, (0, 3, 1, 4, 2))


A_UNITS = {"q": 0, "k_cmp": 12, "k_slc": 14, "k_win": 16, "v_cmp": 18, "v_slc": 20,
           "v_win": 22, "mem_q": 24, "gates": 28}
A_TN = 6 * HEAD_DIM
A_NPAD = 30 * HEAD_DIM
A_ROPE_BLOCKS = 3


def _layer_a_weight(w_in):
    kv0 = A_Q

    def kv_cols(branch):
        return np.arange(kv0 + branch * N_KV_A * HEAD_DIM, kv0 + (branch + 1) * N_KV_A * HEAD_DIM)

    gate0 = A_Q + A_KV
    mem0 = gate0 + A_GATE
    perm = np.concatenate([np.arange(A_Q), kv_cols(0), kv_cols(2), kv_cols(4),
                           kv_cols(1), kv_cols(3), kv_cols(5), np.arange(mem0, mem0 + MEM_Q)])
    per_group = 3 * HPG_A
    gate_units = [jnp.pad(w_in[:, gate0 + g * per_group:gate0 + (g + 1) * per_group],
                          ((0, 0), (0, LANES - per_group))) for g in range(N_KV_A)]
    w = jnp.concatenate([w_in[:, perm]] + gate_units, axis=1)
    assert w.shape[1] == A_NPAD
    scale = np.ones((1, A_NPAD), np.float32)
    scale[0, :A_Q] = SCALE * LOG2E
    scale[0, A_UNITS["mem_q"] * HEAD_DIM:A_UNITS["mem_q"] * HEAD_DIM + MEM_Q] = SCALE
    return w.astype(BF16), jnp.asarray(scale)


def _layer_a(h, mem, cosf, sinf, p, *, batch, seq):
    w_in_bf, col_scale = _layer_a_weight(p["w_in"])
    z = _norm_proj(h, p["norm_attn"], w_in_bf, col_scale, cosf, sinf,
                   tn=A_TN, n_rope_blocks=A_ROPE_BLOCKS, seq=seq)

    def unit_cols(name, n_units):
        return z[:, A_UNITS[name] * HEAD_DIM:(A_UNITS[name] + n_units) * HEAD_DIM]

    per_row = CMP_LEN // 2

    def cmp_rows(name):
        x = unit_cols(name, N_KV_A).reshape(batch, seq // per_row, per_row, N_KV_A, HEAD_DIM)
        return jnp.transpose(x, (0, 3, 1, 2, 4)).reshape(batch * N_KV_A, seq // per_row,
                                                          per_row * HEAD_DIM)

    x2 = jnp.stack([cmp_rows("k_cmp"), cmp_rows("v_cmp")])

    def pe_rows(pe):
        return jnp.pad(pe.reshape(2, per_row * HEAD_DIM), ((0, 6), (0, 0)))

    pe2 = jnp.stack([pe_rows(p["cmp_pe_k"]), pe_rows(p["cmp_pe_v"])])
    w1 = jnp.stack([p["cmp_w1_k"], p["cmp_w1_v"]]).astype(BF16)
    w2 = jnp.stack([p["cmp_w2_k"], p["cmp_w2_v"]]).astype(BF16)
    cmp_out = _compress(x2, pe2, w1, w2)
    n_rows = seq // per_row
    kc = cmp_out[0].reshape(batch, N_KV_A, n_rows, HEAD_DIM)
    vct = jnp.transpose(cmp_out[1].reshape(batch, N_KV_A, n_rows, HEAD_DIM), (0, 1, 3, 2))

    ks = jnp.transpose(unit_cols("k_slc", N_KV_A).reshape(batch, seq, N_KV_A, HEAD_DIM),
                       (0, 2, 1, 3))
    onehot = (np.arange(seq)[:, None] // SLC_BLK == np.arange(LANES)[None, :])
    onehot = jnp.broadcast_to(jnp.asarray(onehot, BF16), (batch, N_KV_A, seq, LANES))
    ksa = jnp.concatenate([ks, onehot], axis=-1)
    vst = _tiles_transposed(unit_cols("v_slc", N_KV_A), batch, seq, N_KV_A)
    vwt = _tiles_transposed(unit_cols("v_win", N_KV_A), batch, seq, N_KV_A)
    gb = jnp.pad(p["gate_bias"].reshape(N_KV_A, 1, 3 * HPG_A),
                 ((0, 0), (0, 0), (0, LANES - 3 * HPG_A)))
    o_nsa = _nsa_attention(z, kc, vct, ksa, vst, vwt, gb, batch=batch, seq=seq, units=A_UNITS)

    mkv = _mem_kv(mem, p["norm_mem"], p["w_mem_kv"])
    o_mem = _memory_attention(z, mkv, batch=batch, seq=seq,
                              q_block=A_UNITS["mem_q"] * HEAD_DIM // MEM_Q)
    w_out = p["w_out"].astype(BF16)
    h = _out_proj(o_nsa, o_mem, w_out[:A_Q], w_out[A_Q:], h)
    return h


def _mem_kv(mem, norm_mem, w_mem_kv):
    b, m, d = mem.shape
    ones = jnp.ones((1, w_mem_kv.shape[1]), F32)
    dummy = jnp.zeros((m, HEAD_DIM), F32)
    return _norm_proj(mem.reshape(b * m, d), norm_mem, w_mem_kv.astype(BF16), ones, dummy, dummy,
                      tn=MEM_Q, n_rope_blocks=0, seq=m, tm=m)


def kernel(x, mem, a_norm_attn, a_w_in, a_gate_bias, a_cmp_pe_k, a_cmp_w1_k, a_cmp_w2_k, a_cmp_pe_v, a_cmp_w1_v, a_cmp_w2_v, a_norm_mem, a_w_mem_kv, a_w_out, a_norm_ffn, a_w_gate, a_w_up, a_w_down, kv_norm, w_kv_shared, b_norm_attn, b_w_in, b_norm_mem, b_w_mem_kv, b_w_out, b_norm_ffn, b_w_gate, b_w_up, b_w_down, final_norm):
    batch, seq, d = x.shape
    n_a = a_w_in.shape[0]
    n_b = b_w_in.shape[0]
    cosf, sinf = _rope_tables(seq)
    h = x.reshape(batch * seq, d)
    unit_gain = jnp.ones((d,), F32)

    for l in range(n_a):
        p = {"norm_attn": a_norm_attn[l], "w_in": a_w_in[l], "gate_bias": a_gate_bias[l],
             "cmp_pe_k": a_cmp_pe_k[l], "cmp_w1_k": a_cmp_w1_k[l], "cmp_w2_k": a_cmp_w2_k[l],
             "cmp_pe_v": a_cmp_pe_v[l], "cmp_w1_v": a_cmp_w1_v[l], "cmp_w2_v": a_cmp_w2_v[l],
             "norm_mem": a_norm_mem[l], "w_mem_kv": a_w_mem_kv[l], "w_out": a_w_out[l]}
        h = _layer_a(h, mem, cosf, sinf, p, batch=batch, seq=seq)
        last = (l == n_a - 1) and n_b == 0
        h = _ffn(h, a_norm_ffn[l], a_w_gate[l].astype(BF16), a_w_up[l].astype(BF16),
                 a_w_down[l].astype(BF16), final_norm if last else unit_gain, final_norm=last)

    if n_b > 0:
        kv_w = w_kv_shared.astype(BF16)
        n_kv = kv_w.shape[1]
        kv = _norm_proj(h, kv_norm, kv_w, jnp.ones((1, n_kv), F32), cosf, sinf,
                        tn=n_kv // 2, n_rope_blocks=1, seq=seq)
        k_sh = kv[:, :n_kv // 2]
        vt_sh = _tiles_transposed(kv[:, n_kv // 2:], batch, seq, DIL_HEADS)
        b_scale = jnp.full((1, B_Q + MEM_Q), SCALE, F32)
        for l in range(n_b):
            zb = _norm_proj(h, b_norm_attn[l], b_w_in[l].astype(BF16), b_scale, cosf, sinf,
                            tn=MEM_Q, n_rope_blocks=B_Q // MEM_Q, seq=seq)
            o_dil = _dilated_attention(zb, k_sh, vt_sh, batch=batch, seq=seq)
            mkv = _mem_kv(mem, b_norm_mem[l], b_w_mem_kv[l])
            o_mem = _memory_attention(zb, mkv, batch=batch, seq=seq, q_block=B_Q // MEM_Q)
            w_out = b_w_out[l].astype(BF16)
            n_dil = DIL_HEADS * HEAD_DIM
            h = _out_proj(o_dil, o_mem, w_out[:n_dil], w_out[n_dil:], h)
            last = l == n_b - 1
            h = _ffn(h, b_norm_ffn[l], b_w_gate[l].astype(BF16), b_w_up[l].astype(BF16),
                     b_w_down[l].astype(BF16), final_norm if last else unit_gain, final_norm=last)

    return h.reshape(batch, seq, d)
```

```python
import functools
import math

import numpy as np
import jax
import jax.numpy as jnp
from jax import lax
from jax.experimental import pallas as pl
from jax.experimental.pallas import tpu as pltpu

F32 = jnp.float32
BF16 = jnp.bfloat16

HEAD_DIM = 128
N_HEADS_A = 12
N_KV_A = 2
HPG_A = N_HEADS_A // N_KV_A
CMP_LEN = 32
CMP_STRIDE = 16
CMP_HIDDEN = 256
SLC_BLK = 64
SLC_SHIFT = SLC_BLK.bit_length() - 1
N_SEL = 16
WIN_A = 512
DIL_CONFIGS = ((128, 1), (512, 4), (2048, 16))
N_DIL_GROUPS = len(DIL_CONFIGS)
DIL_HEADS = 4
N_MEM_HEADS = 4
ROPE_THETA = 10000.0
EPS = 1e-6
NEG_INF = -1e30
TINY = 1e-30
SCALE = HEAD_DIM ** -0.5
LOG2E = math.log2(math.e)

A_Q = N_HEADS_A * HEAD_DIM
A_KV = 6 * N_KV_A * HEAD_DIM
A_GATE = 3 * N_HEADS_A
MEM_Q = N_MEM_HEADS * HEAD_DIM
B_Q = N_DIL_GROUPS * DIL_HEADS * HEAD_DIM

LANES = 128
SUBLANES = 8
VMEM_LIMIT_BYTES = 56 * 1024 * 1024

PROJ_TM = 1024
FFN_TM = 512
FFN_TF = 512
OUT_TM = 1024
OUT_TN = 1024
NSA_TQ = 128
NSA_KC = 512
DIL_TQ = 256
MEM_TQ = 512

NT_DIMS = (((1,), (1,)), ((), ()))


def _compiler_params(semantics):
    return pltpu.CompilerParams(dimension_semantics=semantics,
                                vmem_limit_bytes=VMEM_LIMIT_BYTES)


def _rms_rows(x, g):
    ms = jnp.mean(x * x, axis=-1, keepdims=True)
    return x * lax.rsqrt(ms + EPS) * g


def _dot(a, b):
    return jnp.dot(a, b, preferred_element_type=F32)


def _dot_nt(a, b):
    return lax.dot_general(a, b, NT_DIMS, preferred_element_type=F32)


def _norm_proj_kernel(x_ref, g_ref, w_ref, cs_ref, cos_ref, sin_ref, o_ref, xn_ref, *,
                      n_rope_blocks, tn):
    j = pl.program_id(1)

    @pl.when(j == 0)
    def _():
        xn_ref[...] = _rms_rows(x_ref[...], g_ref[...]).astype(BF16)

    acc = _dot(xn_ref[...], w_ref[...]) * cs_ref[...]

    if n_rope_blocks > 0:
        roped = j < n_rope_blocks
        c = jnp.where(roped, cos_ref[...], 1.0)
        s = jnp.where(roped, sin_ref[...], 0.0)
        for h in range(tn // HEAD_DIM):
            y = acc[:, h * HEAD_DIM:(h + 1) * HEAD_DIM]
            rot = pltpu.roll(y, HEAD_DIM // 2, 1)
            o_ref[:, h * HEAD_DIM:(h + 1) * HEAD_DIM] = (y * c + rot * s).astype(o_ref.dtype)
    else:
        o_ref[...] = acc.astype(o_ref.dtype)


def _norm_proj(x, g, w_bf, col_scale, cosf, sinf, *, tn, n_rope_blocks, seq, tm=PROJ_TM):
    m, d = x.shape
    n = w_bf.shape[1]
    tm = min(tm, m)
    assert m % tm == 0 and n % tn == 0 and seq % tm == 0
    pos_blocks = seq // tm
    kern = functools.partial(_norm_proj_kernel, n_rope_blocks=n_rope_blocks, tn=tn)
    return pl.pallas_call(
        kern,
        grid=(m // tm, n // tn),
        in_specs=[
            pl.BlockSpec((tm, d), lambda i, j: (i, 0)),
            pl.BlockSpec((1, d), lambda i, j: (0, 0)),
            pl.BlockSpec((d, tn), lambda i, j: (0, j)),
            pl.BlockSpec((1, tn), lambda i, j: (0, j)),
            pl.BlockSpec((tm, HEAD_DIM), lambda i, j: (i % pos_blocks, 0)),
            pl.BlockSpec((tm, HEAD_DIM), lambda i, j: (i % pos_blocks, 0)),
        ],
        out_specs=pl.BlockSpec((tm, tn), lambda i, j: (i, j)),
        out_shape=jax.ShapeDtypeStruct((m, n), BF16),
        scratch_shapes=[pltpu.VMEM((tm, d), BF16)],
        compiler_params=_compiler_params(("parallel", "arbitrary")),
        name="norm_proj",
    )(x, g.reshape(1, d), w_bf, col_scale, cosf, sinf)


def _compress_kernel(x_ref, pe_ref, w1_ref, w2_ref, o_ref):
    half = (CMP_LEN // 2) * HEAD_DIM
    x = x_ref[...].astype(F32)
    xlo = (x + pe_ref[0:1, :]).astype(BF16)
    xhi = (x + pe_ref[1:2, :]).astype(BF16)
    ylo = _dot(xlo, w1_ref[:half, :])
    yhi = _dot(xhi, w1_ref[half:, :])
    n_rows = x.shape[0]
    hid = ylo + pltpu.roll(yhi, n_rows - 1, 0)
    act = (hid * jax.nn.sigmoid(hid)).astype(BF16)
    o_ref[...] = _dot(act, w2_ref[...]).astype(o_ref.dtype)


def _compress(x2, pe2, w1_bf, w2_bf):
    _, bg, nrow, wide = x2.shape
    return pl.pallas_call(
        _compress_kernel,
        grid=(2, bg),
        in_specs=[
            pl.BlockSpec((None, None, nrow, wide), lambda t, i: (t, i, 0, 0)),
            pl.BlockSpec((None, 8, wide), lambda t, i: (t, 0, 0)),
            pl.BlockSpec((None, 2 * wide, CMP_HIDDEN), lambda t, i: (t, 0, 0)),
            pl.BlockSpec((None, CMP_HIDDEN, HEAD_DIM), lambda t, i: (t, 0, 0)),
        ],
        out_specs=pl.BlockSpec((None, None, nrow, HEAD_DIM), lambda t, i: (t, i, 0, 0)),
        out_shape=jax.ShapeDtypeStruct((2, bg, nrow, HEAD_DIM), BF16),
        compiler_params=_compiler_params(("parallel", "arbitrary")),
        name="nsa_compress",
    )(x2, pe2, w1_bf, w2_bf)


def _block_ranks(score, jrow):
    n_blk = score.shape[0]
    groups = n_blk // SUBLANES
    blocks = [score[SUBLANES * r:SUBLANES * (r + 1), :] for r in range(groups)]
    rows = [jrow[SUBLANES * r:SUBLANES * (r + 1), :] for r in range(groups)]
    ranks = [jnp.zeros(blocks[0].shape, F32) for _ in range(groups)]
    for j in range(n_blk):
        rj = score[j:j + 1, :]
        for r in range(groups):
            if r > j // SUBLANES:
                ahead = rj >= blocks[r]
            elif r < j // SUBLANES:
                ahead = rj > blocks[r]
            else:
                ahead = (rj > blocks[r]) | ((rj == blocks[r]) & (rows[r] > j))
            ranks[r] = ranks[r] + jnp.where(ahead, 1.0, 0.0)
    return jnp.concatenate(ranks, axis=0)


def _nsa_kernel(q_ref, kc_ref, vct_ref, ks_ref, e_ref, vst_ref, kw_ref, vwt_ref, gz_ref, gb_ref,
                o_ref, qa_ref, sa_ref, sb_ref, sw_ref, m_ref, l_ref, acc_ref, *, tq, seq):
    hq = HPG_A
    n_cmp_rows = kc_ref.shape[0]
    n_slc = seq // SLC_BLK
    qi = pl.program_id(2)
    s0 = qi * tq
    t_row = s0 + lax.broadcasted_iota(jnp.int32, (1, tq), 1)

    def lanes(h):
        return slice(h * tq, (h + 1) * tq)

    def tile_heads(x):
        return jnp.concatenate([x] * hq, axis=1)

    for h in range(hq):
        qa_ref[lanes(h), 0:HEAD_DIM] = q_ref[:, h * HEAD_DIM:(h + 1) * HEAD_DIM]
    q6 = qa_ref[:, 0:HEAD_DIM]

    wtiles = (WIN_A + tq) // LANES
    wlen = wtiles * LANES
    ws = pl.multiple_of(jnp.maximum(s0 - WIN_A, 0), LANES)
    sw_ref[...] = _dot_nt(kw_ref[pl.ds(ws, wlen), :], q6)

    c_end = lax.broadcasted_iota(jnp.int32, (n_cmp_rows, 1), 0) * CMP_STRIDE + (CMP_LEN - 1)
    cvalid = tile_heads(c_end <= t_row)
    sc = jnp.where(cvalid, _dot_nt(kc_ref[...], q6), NEG_INF)
    ec = jnp.where(cvalid, jnp.exp2(sc - jnp.max(sc, axis=0, keepdims=True)), 0.0)
    pc = ec * (1.0 / jnp.maximum(jnp.sum(ec, axis=0, keepdims=True), TINY))
    o_cmp = _dot(vct_ref[...], pc.astype(BF16))
    psum = pc[:, lanes(0)]
    for h in range(1, hq):
        psum = psum + pc[:, lanes(h)]

    jrow = lax.broadcasted_iota(jnp.int32, (n_slc, 1), 0)
    ccol = lax.broadcasted_iota(jnp.int32, (1, n_cmp_rows), 1)
    lo = (SLC_BLK // CMP_STRIDE) * jrow - (CMP_LEN // CMP_STRIDE - 1)
    hi = (SLC_BLK // CMP_STRIDE) * jrow + (SLC_BLK // CMP_STRIDE - 1)
    mmap = jnp.where((ccol >= lo) & (ccol <= hi), 1.0, 0.0).astype(BF16)
    p1 = psum.astype(BF16)
    r1 = psum - p1.astype(F32)
    p2 = r1.astype(BF16)
    p3 = (r1 - p2.astype(F32)).astype(BF16)
    imp = _dot(mmap, p1) + _dot(mmap, p2) + _dot(mmap, p3)
    cur = t_row >> SLC_SHIFT
    forced = (jrow == 0) | (jrow == cur) | (jrow == cur - 1)
    score = jnp.where(forced, 1e9, jnp.where(jrow <= cur, imp, -1e9))
    rank = _block_ranks(score, jrow)
    sel_bias = jnp.where((rank < min(N_SEL, n_slc)) & (jrow <= cur), 0.0, NEG_INF)

    assert n_slc <= LANES and tq == LANES
    bias_q = jnp.transpose(jnp.concatenate(
        [sel_bias, jnp.zeros((LANES - n_slc, tq), F32)], axis=0)).astype(BF16)
    for h in range(hq):
        qa_ref[lanes(h), HEAD_DIM:HEAD_DIM + LANES] = bias_q

    m_ref[...] = jnp.full(m_ref.shape, NEG_INF, F32)
    l_ref[...] = jnp.zeros(l_ref.shape, F32)
    acc_ref[...] = jnp.zeros(acc_ref.shape, F32)
    tiles_per_chunk = NSA_KC // LANES

    def scores(c, s_ref):
        k0 = pl.multiple_of(c * NSA_KC, NSA_KC)
        k_aug = jnp.concatenate([ks_ref[pl.ds(k0, NSA_KC), :], e_ref[pl.ds(k0, NSA_KC), :]],
                                axis=1)
        s_ref[...] = _dot_nt(k_aug, qa_ref[...])

    def softmax_pv(c, s_ref):
        s = s_ref[...]
        m_old = m_ref[...]
        m_new = jnp.maximum(m_old, jnp.max(s, axis=0, keepdims=True))
        alpha = jnp.exp2(m_old - m_new)
        p = jnp.exp2(s - m_new)
        l_ref[...] = alpha * l_ref[...] + jnp.sum(p, axis=0, keepdims=True)
        vt = jnp.concatenate([vst_ref[c * tiles_per_chunk + i] for i in range(tiles_per_chunk)],
                             axis=1)
        acc_ref[...] = alpha * acc_ref[...] + _dot(vt, p.astype(BF16))
        m_ref[...] = m_new

    last = s0 // NSA_KC
    scores(0, sa_ref)

    def chunk_pair(i, carry):
        scores(2 * i + 1, sb_ref)
        softmax_pv(2 * i, sa_ref)
        scores(2 * i + 2, sa_ref)
        softmax_pv(2 * i + 1, sb_ref)
        return carry

    lax.fori_loop(0, last // 2, chunk_pair, 0)

    def last_chunk(s_ref):
        diag = pl.multiple_of(s0 - last * NSA_KC, LANES)
        krow = lax.broadcasted_iota(jnp.int32, (tq, 1), 0)
        lane = lax.broadcasted_iota(jnp.int32, (1, tq), 1)
        causal = tile_heads(jnp.where(krow <= lane, 0.0, NEG_INF))
        s_ref[pl.ds(diag, tq), :] = s_ref[pl.ds(diag, tq), :] + causal
        softmax_pv(last, s_ref)

    @pl.when(last % 2 == 0)
    def _():
        last_chunk(sa_ref)

    @pl.when(last % 2 == 1)
    def _():
        scores(last, sb_ref)
        softmax_pv(last - 1, sa_ref)
        last_chunk(sb_ref)

    dist = t_row - (ws + lax.broadcasted_iota(jnp.int32, (wlen, 1), 0))
    wbias = tile_heads(jnp.where((dist >= 0) & (dist < WIN_A), 0.0, NEG_INF))
    sw = sw_ref[...] + wbias
    ew = jnp.exp2(sw - jnp.max(sw, axis=0, keepdims=True))
    wt0 = ws // LANES
    vwt = jnp.concatenate([vwt_ref[wt0 + i] for i in range(wtiles)], axis=1)
    o_win = _dot(vwt, ew.astype(BF16)) * (1.0 / jnp.sum(ew, axis=0, keepdims=True))

    gates_t = jnp.transpose(jax.nn.sigmoid(gz_ref[...].astype(F32) + gb_ref[...]))
    o_slc = acc_ref[...] * (1.0 / l_ref[...])
    for h in range(hq):
        o_h = (gates_t[3 * h:3 * h + 1, :] * o_cmp[:, lanes(h)]
               + gates_t[3 * h + 1:3 * h + 2, :] * o_slc[:, lanes(h)]
               + gates_t[3 * h + 2:3 * h + 3, :] * o_win[:, lanes(h)])
        o_ref[:, h * HEAD_DIM:(h + 1) * HEAD_DIM] = jnp.transpose(o_h).astype(o_ref.dtype)


def _nsa_attention(z, kc, vct, vst, vwt, gate_bias, *, batch, seq, units):
    tq = NSA_TQ
    nq = seq // tq
    n_cmp_rows = kc.shape[2]
    qb = HPG_A * HEAD_DIM
    kern = functools.partial(_nsa_kernel, tq=tq, seq=seq)
    n_tiles = seq // LANES
    onehot = jnp.asarray(np.arange(seq)[:, None] // SLC_BLK == np.arange(LANES)[None, :], BF16)
    return pl.pallas_call(
        kern,
        grid=(batch, N_KV_A, nq),
        in_specs=[
            pl.BlockSpec((tq, qb), lambda b, g, i: (b * nq + i, g)),
            pl.BlockSpec((None, None, n_cmp_rows, HEAD_DIM), lambda b, g, i: (b, g, 0, 0)),
            pl.BlockSpec((None, None, HEAD_DIM, n_cmp_rows), lambda b, g, i: (b, g, 0, 0)),
            pl.BlockSpec((seq, HEAD_DIM), lambda b, g, i: (b, units["k_slc"] + g)),
            pl.BlockSpec((seq, LANES), lambda b, g, i: (0, 0)),
            pl.BlockSpec((None, None, n_tiles, HEAD_DIM, LANES), lambda b, g, i: (b, g, 0, 0, 0)),
            pl.BlockSpec((seq, HEAD_DIM), lambda b, g, i: (b, units["k_win"] + g)),
            pl.BlockSpec((None, None, n_tiles, HEAD_DIM, LANES), lambda b, g, i: (b, g, 0, 0, 0)),
            pl.BlockSpec((tq, LANES), lambda b, g, i: (b * nq + i, units["gates"] + g)),
            pl.BlockSpec((None, 1, LANES), lambda b, g, i: (g, 0, 0)),
        ],
        out_specs=pl.BlockSpec((tq, qb), lambda b, g, i: (b * nq + i, g)),
        out_shape=jax.ShapeDtypeStruct((batch * seq, A_Q), BF16),
        scratch_shapes=[
            pltpu.VMEM((HPG_A * tq, HEAD_DIM + LANES), BF16),
            pltpu.VMEM((NSA_KC, HPG_A * tq), F32),
            pltpu.VMEM((NSA_KC, HPG_A * tq), F32),
            pltpu.VMEM((WIN_A + tq, HPG_A * tq), F32),
            pltpu.VMEM((1, HPG_A * tq), F32),
            pltpu.VMEM((1, HPG_A * tq), F32),
            pltpu.VMEM((HEAD_DIM, HPG_A * tq), F32),
        ],
        compiler_params=_compiler_params(("parallel", "parallel", "arbitrary")),
        name="nsa_attention",
    )(z, kc, vct, z, onehot, vst, z, vwt, z, gate_bias)


def _band_attn_kernel(*refs, tu, lk, span, first, final):
    if first:
        q_ref, k_ref, v_ref, o_ref, lse_ref = refs
    elif final:
        q_ref, k_ref, v_ref, op_ref, lp_ref, o_ref = refs
    else:
        q_ref, k_ref, v_ref, op_ref, lp_ref, o_ref, lse_ref = refs
    u0 = pl.program_id(2) * tu
    n_seq = k_ref.shape[0]
    ks = pl.multiple_of(jnp.clip(u0 - span, 0, n_seq - lk), LANES)
    dist = (u0 + lax.broadcasted_iota(jnp.int32, (tu, 1), 0)
            - (ks + lax.broadcasted_iota(jnp.int32, (1, lk), 1)))
    bias = jnp.where((dist >= 0) & (dist <= span), 0.0, NEG_INF)
    for h in range(DIL_HEADS):
        cols = slice(h * HEAD_DIM, (h + 1) * HEAD_DIM)
        s = _dot_nt(q_ref[:, cols], k_ref[pl.ds(ks, lk), cols]) + bias
        m = jnp.max(s, axis=-1, keepdims=True)
        p = jnp.exp2(s - m)
        l = jnp.sum(p, axis=-1, keepdims=True)
        o = _dot(p.astype(BF16), v_ref[pl.ds(ks, lk), cols]) * (1.0 / l)
        lse = jnp.broadcast_to(m + jnp.log2(l), o.shape)
        if not first:
            lp = lp_ref[:, cols]
            top = jnp.maximum(lp, lse)
            wp = jnp.exp2(lp - top)
            wc = jnp.exp2(lse - top)
            o = (op_ref[:, cols] * wp + o * wc) * (1.0 / (wp + wc))
            lse = top + jnp.log2(wp + wc)
        o_ref[:, cols] = o.astype(o_ref.dtype)
        if not final:
            lse_ref[:, cols] = lse


def _dilated_attention(zb, *, batch, seq, units):
    n = zb.shape[1]
    width = DIL_HEADS * HEAD_DIM
    per_row = n // width
    o_prev = lse_prev = None
    for gi, (w, r) in enumerate(DIL_CONFIGS):
        first, final = gi == 0, gi == N_DIL_GROUPS - 1
        n_seq = seq // r
        span = w // r
        tu = min(DIL_TQ, n_seq)
        lk = min(tu + span, n_seq)
        assert n_seq % tu == 0 and span % LANES == 0 and tu % LANES == 0
        kern = functools.partial(_band_attn_kernel, tu=tu, lk=lk, span=span, first=first,
                                 final=final)
        zv = zb.reshape(batch, n_seq, r * n)

        def z_spec(rows, unit, whole):
            return pl.BlockSpec((None, rows, width),
                                lambda b, c, i: (b, 0 if whole else i, c * per_row + unit))

        def tok_spec():
            return pl.BlockSpec((None, tu, width), lambda b, c, i: (b, i, c))

        in_specs = [z_spec(tu, units["q"] + gi, False), z_spec(n_seq, units["k"], True),
                    z_spec(n_seq, units["v"], True)]
        operands = [zv, zv, zv]
        if not first:
            in_specs += [tok_spec(), tok_spec()]
            operands += [o_prev.reshape(batch, n_seq, r * width),
                         lse_prev.reshape(batch, n_seq, r * width)]
        o_shape = jax.ShapeDtypeStruct((batch, n_seq, r * width), BF16 if final else F32)
        if final:
            out_specs, out_shape = tok_spec(), o_shape
        else:
            out_specs = [tok_spec(), tok_spec()]
            out_shape = [o_shape, jax.ShapeDtypeStruct((batch, n_seq, r * width), F32)]
        res = pl.pallas_call(
            kern,
            grid=(batch, r, n_seq // tu),
            in_specs=in_specs,
            out_specs=out_specs,
            out_shape=out_shape,
            compiler_params=_compiler_params(("parallel", "parallel", "arbitrary")),
            name="dilated_attention",
        )(*operands)
        if final:
            return res.reshape(batch * seq, width)
        o_prev = res[0].reshape(batch * seq, width)
        lse_prev = res[1].reshape(batch * seq, width)


def _mem_attn_kernel(q_ref, kv_ref, o_ref):
    for h in range(N_MEM_HEADS):
        cols = slice(h * HEAD_DIM, (h + 1) * HEAD_DIM)
        k = kv_ref[:, cols]
        v = kv_ref[:, MEM_Q + h * HEAD_DIM:MEM_Q + (h + 1) * HEAD_DIM]
        s = _dot_nt(q_ref[:, cols], k)
        e = jnp.exp(s - jnp.max(s, axis=-1, keepdims=True))
        p = e / jnp.sum(e, axis=-1, keepdims=True)
        o_ref[:, cols] = _dot(p.astype(BF16), v).astype(o_ref.dtype)


def _memory_attention(z, mkv, *, batch, seq, q_block):
    tq = MEM_TQ
    nq = seq // tq
    n_mem = mkv.shape[0] // batch
    return pl.pallas_call(
        _mem_attn_kernel,
        grid=(batch, nq),
        in_specs=[
            pl.BlockSpec((tq, MEM_Q), lambda b, i: (b * nq + i, q_block)),
            pl.BlockSpec((n_mem, 2 * MEM_Q), lambda b, i: (b, 0)),
        ],
        out_specs=pl.BlockSpec((tq, MEM_Q), lambda b, i: (b * nq + i, 0)),
        out_shape=jax.ShapeDtypeStruct((batch * seq, MEM_Q), BF16),
        compiler_params=_compiler_params(("parallel", "arbitrary")),
        name="memory_attention",
    )(z, mkv)


def _out_proj_kernel(a1_ref, a2_ref, w1_ref, w2_ref, h_ref, o_ref):
    o_ref[...] = h_ref[...] + _dot(a1_ref[...], w1_ref[...]) + _dot(a2_ref[...], w2_ref[...])


def _out_proj(a1, a2, w1_bf, w2_bf, h):
    m, d = h.shape
    tm, tn = min(OUT_TM, m), OUT_TN
    k1, k2 = a1.shape[1], a2.shape[1]
    return pl.pallas_call(
        _out_proj_kernel,
        grid=(m // tm, d // tn),
        in_specs=[
            pl.BlockSpec((tm, k1), lambda i, j: (i, 0)),
            pl.BlockSpec((tm, k2), lambda i, j: (i, 0)),
            pl.BlockSpec((k1, tn), lambda i, j: (0, j)),
            pl.BlockSpec((k2, tn), lambda i, j: (0, j)),
            pl.BlockSpec((tm, tn), lambda i, j: (i, j)),
        ],
        out_specs=pl.BlockSpec((tm, tn), lambda i, j: (i, j)),
        out_shape=jax.ShapeDtypeStruct((m, d), F32),
        compiler_params=_compiler_params(("parallel", "arbitrary")),
        name="out_proj",
    )(a1, a2, w1_bf, w2_bf, h)


def _ffn_kernel(x_ref, g_ref, wg_ref, wu_ref, wd_ref, fg_ref, o_ref, xn_ref, acc_ref, *,
                final_norm):
    f = pl.program_id(1)

    @pl.when(f == 0)
    def _():
        xn_ref[...] = _rms_rows(x_ref[...], g_ref[...]).astype(BF16)
        acc_ref[...] = jnp.zeros(acc_ref.shape, F32)

    xn = xn_ref[...]
    gate = _dot(xn, wg_ref[...])
    up = _dot(xn, wu_ref[...])
    act = (gate * jax.nn.sigmoid(gate) * up).astype(BF16)
    acc_ref[...] += _dot(act, wd_ref[...])

    @pl.when(f == pl.num_programs(1) - 1)
    def _():
        y = x_ref[...] + acc_ref[...]
        if final_norm:
            y = _rms_rows(y, fg_ref[...])
        o_ref[...] = y


def _ffn(h, g, wg_bf, wu_bf, wd_bf, final_gain, *, final_norm):
    m, d = h.shape
    dff = wg_bf.shape[1]
    tm, tf = min(FFN_TM, m), FFN_TF
    assert m % tm == 0 and dff % tf == 0
    kern = functools.partial(_ffn_kernel, final_norm=final_norm)
    return pl.pallas_call(
        kern,
        grid=(m // tm, dff // tf),
        in_specs=[
            pl.BlockSpec((tm, d), lambda i, f: (i, 0)),
            pl.BlockSpec((1, d), lambda i, f: (0, 0)),
            pl.BlockSpec((d, tf), lambda i, f: (0, f)),
            pl.BlockSpec((d, tf), lambda i, f: (0, f)),
            pl.BlockSpec((tf, d), lambda i, f: (f, 0)),
            pl.BlockSpec((1, d), lambda i, f: (0, 0)),
        ],
        out_specs=pl.BlockSpec((tm, d), lambda i, f: (i, 0)),
        out_shape=jax.ShapeDtypeStruct((m, d), F32),
        scratch_shapes=[pltpu.VMEM((tm, d), BF16), pltpu.VMEM((tm, d), F32)],
        compiler_params=_compiler_params(("parallel", "arbitrary")),
        name="ffn",
    )(h, g.reshape(1, d), wg_bf, wu_bf, wd_bf, final_gain.reshape(1, d))


def _rope_tables(seq):
    inv = 1.0 / (ROPE_THETA ** (jnp.arange(0, HEAD_DIM, 2, dtype=F32) / HEAD_DIM))
    ang = jnp.arange(seq, dtype=F32)[:, None] * inv[None, :]
    cos, sin = jnp.cos(ang), jnp.sin(ang)
    return jnp.concatenate([cos, cos], axis=1), jnp.concatenate([-sin, sin], axis=1)


def _tiles_transposed(v, batch, seq, heads):
    v = v.reshape(batch, seq // LANES, LANES, heads, HEAD_DIM)
    return jnp.transpose(v, (0, 3, 1, 4, 2))


A_UNITS = {"q": 0, "k_cmp": 12, "k_slc": 14, "k_win": 16, "v_cmp": 18, "v_slc": 20,
           "v_win": 22, "mem_q": 24, "gates": 28}
B_UNITS = {"q": 0, "k": 3, "mem_q": 4, "v": 5}
A_TN = 6 * HEAD_DIM
A_NPAD = 30 * HEAD_DIM
A_ROPE_BLOCKS = 3


def _layer_a_weight(w_in):
    kv0 = A_Q

    def kv_cols(branch):
        return np.arange(kv0 + branch * N_KV_A * HEAD_DIM, kv0 + (branch + 1) * N_KV_A * HEAD_DIM)

    gate0 = A_Q + A_KV
    mem0 = gate0 + A_GATE
    perm = np.concatenate([np.arange(A_Q), kv_cols(0), kv_cols(2), kv_cols(4),
                           kv_cols(1), kv_cols(3), kv_cols(5), np.arange(mem0, mem0 + MEM_Q)])
    per_group = 3 * HPG_A
    gate_units = [jnp.pad(w_in[:, gate0 + g * per_group:gate0 + (g + 1) * per_group],
                          ((0, 0), (0, LANES - per_group))) for g in range(N_KV_A)]
    w = jnp.concatenate([w_in[:, perm]] + gate_units, axis=1)
    assert w.shape[1] == A_NPAD
    scale = np.ones((1, A_NPAD), np.float32)
    scale[0, :A_Q] = SCALE * LOG2E
    scale[0, A_UNITS["mem_q"] * HEAD_DIM:A_UNITS["mem_q"] * HEAD_DIM + MEM_Q] = SCALE
    return w.astype(BF16), jnp.asarray(scale)


def _layer_a(h, mem, cosf, sinf, p, *, batch, seq):
    w_in_bf, col_scale = _layer_a_weight(p["w_in"])
    z = _norm_proj(h, p["norm_attn"], w_in_bf, col_scale, cosf, sinf,
                   tn=A_TN, n_rope_blocks=A_ROPE_BLOCKS, seq=seq)

    def unit_cols(name, n_units):
        return z[:, A_UNITS[name] * HEAD_DIM:(A_UNITS[name] + n_units) * HEAD_DIM]

    per_row = CMP_LEN // 2

    def cmp_rows(name):
        x = unit_cols(name, N_KV_A).reshape(batch, seq // per_row, per_row, N_KV_A, HEAD_DIM)
        return jnp.transpose(x, (0, 3, 1, 2, 4)).reshape(batch * N_KV_A, seq // per_row,
                                                          per_row * HEAD_DIM)

    x2 = jnp.stack([cmp_rows("k_cmp"), cmp_rows("v_cmp")])

    def pe_rows(pe):
        return jnp.pad(pe.reshape(2, per_row * HEAD_DIM), ((0, 6), (0, 0)))

    pe2 = jnp.stack([pe_rows(p["cmp_pe_k"]), pe_rows(p["cmp_pe_v"])])
    w1 = jnp.stack([p["cmp_w1_k"], p["cmp_w1_v"]]).astype(BF16)
    w2 = jnp.stack([p["cmp_w2_k"], p["cmp_w2_v"]]).astype(BF16)
    cmp_out = _compress(x2, pe2, w1, w2)
    n_rows = seq // per_row
    kc = cmp_out[0].reshape(batch, N_KV_A, n_rows, HEAD_DIM)
    vct = jnp.transpose(cmp_out[1].reshape(batch, N_KV_A, n_rows, HEAD_DIM), (0, 1, 3, 2))

    vst = _tiles_transposed(unit_cols("v_slc", N_KV_A), batch, seq, N_KV_A)
    vwt = _tiles_transposed(unit_cols("v_win", N_KV_A), batch, seq, N_KV_A)
    gb = jnp.pad(p["gate_bias"].reshape(N_KV_A, 1, 3 * HPG_A),
                 ((0, 0), (0, 0), (0, LANES - 3 * HPG_A)))
    o_nsa = _nsa_attention(z, kc, vct, vst, vwt, gb, batch=batch, seq=seq, units=A_UNITS)

    mkv = _mem_kv(mem, p["norm_mem"], p["w_mem_kv"])
    o_mem = _memory_attention(z, mkv, batch=batch, seq=seq,
                              q_block=A_UNITS["mem_q"] * HEAD_DIM // MEM_Q)
    w_out = p["w_out"].astype(BF16)
    h = _out_proj(o_nsa, o_mem, w_out[:A_Q], w_out[A_Q:], h)
    return h


def _mem_kv(mem, norm_mem, w_mem_kv):
    b, m, d = mem.shape
    ones = jnp.ones((1, w_mem_kv.shape[1]), F32)
    dummy = jnp.zeros((m, HEAD_DIM), F32)
    return _norm_proj(mem.reshape(b * m, d), norm_mem, w_mem_kv.astype(BF16), ones, dummy, dummy,
                      tn=MEM_Q, n_rope_blocks=0, seq=m, tm=m)


def kernel(x, mem, a_norm_attn, a_w_in, a_gate_bias, a_cmp_pe_k, a_cmp_w1_k, a_cmp_w2_k, a_cmp_pe_v, a_cmp_w1_v, a_cmp_w2_v, a_norm_mem, a_w_mem_kv, a_w_out, a_norm_ffn, a_w_gate, a_w_up, a_w_down, kv_norm, w_kv_shared, b_norm_attn, b_w_in, b_norm_mem, b_w_mem_kv, b_w_out, b_norm_ffn, b_w_gate, b_w_up, b_w_down, final_norm):
    batch, seq, d = x.shape
    n_a = a_w_in.shape[0]
    n_b = b_w_in.shape[0]
    cosf, sinf = _rope_tables(seq)
    h = x.reshape(batch * seq, d)
    unit_gain = jnp.ones((d,), F32)

    for l in range(n_a):
        p = {"norm_attn": a_norm_attn[l], "w_in": a_w_in[l], "gate_bias": a_gate_bias[l],
             "cmp_pe_k": a_cmp_pe_k[l], "cmp_w1_k": a_cmp_w1_k[l], "cmp_w2_k": a_cmp_w2_k[l],
             "cmp_pe_v": a_cmp_pe_v[l], "cmp_w1_v": a_cmp_w1_v[l], "cmp_w2_v": a_cmp_w2_v[l],
             "norm_mem": a_norm_mem[l], "w_mem_kv": a_w_mem_kv[l], "w_out": a_w_out[l]}
        h = _layer_a(h, mem, cosf, sinf, p, batch=batch, seq=seq)
        last = (l == n_a - 1) and n_b == 0
        h = _ffn(h, a_norm_ffn[l], a_w_gate[l].astype(BF16), a_w_up[l].astype(BF16),
                 a_w_down[l].astype(BF16), final_norm if last else unit_gain, final_norm=last)

    if n_b > 0:
        assert n_b == 1, "the shared K/V projection is fused into the single mixer-B layer"
        n_kv_half = w_kv_shared.shape[1] // 2
        for l in range(n_b):
            w_q = b_norm_attn[l][:, None] * b_w_in[l]
            w_kv = kv_norm[:, None] * w_kv_shared
            w_cat = jnp.concatenate([w_q[:, :B_Q], w_kv[:, :n_kv_half], w_q[:, B_Q:],
                                     w_kv[:, n_kv_half:]], axis=1).astype(BF16)
            b_scale = np.ones((1, w_cat.shape[1]), np.float32)
            b_scale[0, :B_Q] = SCALE * LOG2E
            b_scale[0, B_Q + n_kv_half:B_Q + n_kv_half + MEM_Q] = SCALE
            zb = _norm_proj(h, unit_gain, w_cat, jnp.asarray(b_scale), cosf, sinf,
                            tn=MEM_Q, n_rope_blocks=(B_Q + n_kv_half) // MEM_Q, seq=seq)
            o_dil = _dilated_attention(zb, batch=batch, seq=seq, units=B_UNITS)
            mkv = _mem_kv(mem, b_norm_mem[l], b_w_mem_kv[l])
            o_mem = _memory_attention(zb, mkv, batch=batch, seq=seq, q_block=B_UNITS["mem_q"])
            w_out = b_w_out[l].astype(BF16)
            n_dil = DIL_HEADS * HEAD_DIM
            h = _out_proj(o_dil, o_mem, w_out[:n_dil], w_out[n_dil:], h)
            last = l == n_b - 1
            h = _ffn(h, b_norm_ffn[l], b_w_gate[l].astype(BF16), b_w_up[l].astype(BF16),
                     b_w_down[l].astype(BF16), final_norm if last else unit_gain, final_norm=last)

    return h.reshape(batch, seq, d)
```

```python
import functools
import math

import numpy as np
import jax
import jax.numpy as jnp
from jax import lax
from jax.experimental import pallas as pl
from jax.experimental.pallas import tpu as pltpu

F32 = jnp.float32
BF16 = jnp.bfloat16

HEAD_DIM = 128
N_HEADS_A = 12
N_KV_A = 2
HPG_A = N_HEADS_A // N_KV_A
CMP_LEN = 32
CMP_STRIDE = 16
CMP_HIDDEN = 256
SLC_BLK = 64
SLC_SHIFT = SLC_BLK.bit_length() - 1
N_SEL = 16
WIN_A = 512
DIL_CONFIGS = ((128, 1), (512, 4), (2048, 16))
N_DIL_GROUPS = len(DIL_CONFIGS)
DIL_HEADS = 4
N_MEM_HEADS = 4
ROPE_THETA = 10000.0
EPS = 1e-6
NEG_INF = -1e30
TINY = 1e-30
SCALE = HEAD_DIM ** -0.5
LOG2E = math.log2(math.e)

A_Q = N_HEADS_A * HEAD_DIM
A_KV = 6 * N_KV_A * HEAD_DIM
A_GATE = 3 * N_HEADS_A
MEM_Q = N_MEM_HEADS * HEAD_DIM
B_Q = N_DIL_GROUPS * DIL_HEADS * HEAD_DIM

LANES = 128
SUBLANES = 8
VMEM_LIMIT_BYTES = 56 * 1024 * 1024

PROJ_TM = 1024
FFN_TM = 512
FFN_TF = 512
OUT_TM = 1024
OUT_TN = 1024
NSA_TQ = 128
NSA_KC = 512
DIL_TQ = 256
MEM_TQ = 512

NT_DIMS = (((1,), (1,)), ((), ()))


def _compiler_params(semantics):
    return pltpu.CompilerParams(dimension_semantics=semantics,
                                vmem_limit_bytes=VMEM_LIMIT_BYTES)


def _rms_rows(x, g):
    ms = jnp.mean(x * x, axis=-1, keepdims=True)
    return x * lax.rsqrt(ms + EPS) * g


def _dot(a, b):
    return jnp.dot(a, b, preferred_element_type=F32)


def _dot_nt(a, b):
    return lax.dot_general(a, b, NT_DIMS, preferred_element_type=F32)


def _norm_proj_kernel(x_ref, g_ref, w_ref, cs_ref, cos_ref, sin_ref, o_ref, xn_ref, *,
                      n_rope_blocks, tn):
    j = pl.program_id(1)

    @pl.when(j == 0)
    def _():
        xn_ref[...] = _rms_rows(x_ref[...], g_ref[...]).astype(BF16)

    acc = _dot(xn_ref[...], w_ref[...]) * cs_ref[...]

    if n_rope_blocks > 0:
        roped = j < n_rope_blocks
        c = jnp.where(roped, cos_ref[...], 1.0)
        s = jnp.where(roped, sin_ref[...], 0.0)
        for h in range(tn // HEAD_DIM):
            y = acc[:, h * HEAD_DIM:(h + 1) * HEAD_DIM]
            rot = pltpu.roll(y, HEAD_DIM // 2, 1)
            o_ref[:, h * HEAD_DIM:(h + 1) * HEAD_DIM] = (y * c + rot * s).astype(o_ref.dtype)
    else:
        o_ref[...] = acc.astype(o_ref.dtype)


def _norm_proj(x, g, w_bf, col_scale, cosf, sinf, *, tn, n_rope_blocks, seq, tm=PROJ_TM):
    m, d = x.shape
    n = w_bf.shape[1]
    tm = min(tm, m)
    assert m % tm == 0 and n % tn == 0 and seq % tm == 0
    pos_blocks = seq // tm
    kern = functools.partial(_norm_proj_kernel, n_rope_blocks=n_rope_blocks, tn=tn)
    return pl.pallas_call(
        kern,
        grid=(m // tm, n // tn),
        in_specs=[
            pl.BlockSpec((tm, d), lambda i, j: (i, 0)),
            pl.BlockSpec((1, d), lambda i, j: (0, 0)),
            pl.BlockSpec((d, tn), lambda i, j: (0, j)),
            pl.BlockSpec((1, tn), lambda i, j: (0, j)),
            pl.BlockSpec((tm, HEAD_DIM), lambda i, j: (i % pos_blocks, 0)),
            pl.BlockSpec((tm, HEAD_DIM), lambda i, j: (i % pos_blocks, 0)),
        ],
        out_specs=pl.BlockSpec((tm, tn), lambda i, j: (i, j)),
        out_shape=jax.ShapeDtypeStruct((m, n), BF16),
        scratch_shapes=[pltpu.VMEM((tm, d), BF16)],
        compiler_params=_compiler_params(("parallel", "arbitrary")),
        name="norm_proj",
    )(x, g.reshape(1, d), w_bf, col_scale, cosf, sinf)


def _compress_kernel(x_ref, pe_ref, w1_ref, w2_ref, o_ref):
    half = (CMP_LEN // 2) * HEAD_DIM
    x = x_ref[...].astype(F32)
    xlo = (x + pe_ref[0:1, :]).astype(BF16)
    xhi = (x + pe_ref[1:2, :]).astype(BF16)
    ylo = _dot(xlo, w1_ref[:half, :])
    yhi = _dot(xhi, w1_ref[half:, :])
    n_rows = x.shape[0]
    hid = ylo + pltpu.roll(yhi, n_rows - 1, 0)
    act = (hid * jax.nn.sigmoid(hid)).astype(BF16)
    o_ref[...] = _dot(act, w2_ref[...]).astype(o_ref.dtype)


def _compress(x2, pe2, w1_bf, w2_bf):
    _, bg, nrow, wide = x2.shape
    return pl.pallas_call(
        _compress_kernel,
        grid=(2, bg),
        in_specs=[
            pl.BlockSpec((None, None, nrow, wide), lambda t, i: (t, i, 0, 0)),
            pl.BlockSpec((None, 8, wide), lambda t, i: (t, 0, 0)),
            pl.BlockSpec((None, 2 * wide, CMP_HIDDEN), lambda t, i: (t, 0, 0)),
            pl.BlockSpec((None, CMP_HIDDEN, HEAD_DIM), lambda t, i: (t, 0, 0)),
        ],
        out_specs=pl.BlockSpec((None, None, nrow, HEAD_DIM), lambda t, i: (t, i, 0, 0)),
        out_shape=jax.ShapeDtypeStruct((2, bg, nrow, HEAD_DIM), BF16),
        compiler_params=_compiler_params(("parallel", "arbitrary")),
        name="nsa_compress",
    )(x2, pe2, w1_bf, w2_bf)


def _block_ranks(score, jrow):
    n_blk = score.shape[0]
    groups = n_blk // SUBLANES
    blocks = [score[SUBLANES * r:SUBLANES * (r + 1), :] for r in range(groups)]
    rows = [jrow[SUBLANES * r:SUBLANES * (r + 1), :] for r in range(groups)]
    ranks = [jnp.zeros(blocks[0].shape, F32) for _ in range(groups)]
    for j in range(n_blk):
        rj = score[j:j + 1, :]
        for r in range(groups):
            if r > j // SUBLANES:
                ahead = rj >= blocks[r]
            elif r < j // SUBLANES:
                ahead = rj > blocks[r]
            else:
                ahead = (rj > blocks[r]) | ((rj == blocks[r]) & (rows[r] > j))
            ranks[r] = ranks[r] + jnp.where(ahead, 1.0, 0.0)
    return jnp.concatenate(ranks, axis=0)


def _nsa_kernel(q_ref, kc_ref, vct_ref, ks_ref, e_ref, vst_ref, kw_ref, vwt_ref, gz_ref, gb_ref,
                o_ref, qa_ref, sa_ref, sb_ref, sw_ref, ow_ref, m_ref, l_ref, acc_ref, *, tq, seq):
    hq = HPG_A
    n_cmp_rows = kc_ref.shape[0]
    n_slc = seq // SLC_BLK
    qi = pl.program_id(2)
    s0 = qi * tq
    t_row = s0 + lax.broadcasted_iota(jnp.int32, (1, tq), 1)

    def lanes(h):
        return slice(h * tq, (h + 1) * tq)

    def tile_heads(x):
        return jnp.concatenate([x] * hq, axis=1)

    for h in range(hq):
        qa_ref[lanes(h), 0:HEAD_DIM] = q_ref[:, h * HEAD_DIM:(h + 1) * HEAD_DIM]
    q6 = qa_ref[:, 0:HEAD_DIM]

    wtiles = (WIN_A + tq) // LANES
    wlen = wtiles * LANES
    ws = pl.multiple_of(jnp.maximum(s0 - WIN_A, 0), LANES)
    sw_ref[...] = _dot_nt(kw_ref[pl.ds(ws, wlen), :], q6)

    c_end = lax.broadcasted_iota(jnp.int32, (n_cmp_rows, 1), 0) * CMP_STRIDE + (CMP_LEN - 1)
    cvalid = tile_heads(c_end <= t_row)
    sc = jnp.where(cvalid, _dot_nt(kc_ref[...], q6), NEG_INF)
    ec = jnp.where(cvalid, jnp.exp2(sc - jnp.max(sc, axis=0, keepdims=True)), 0.0)
    pc = ec * (1.0 / jnp.maximum(jnp.sum(ec, axis=0, keepdims=True), TINY))
    o_cmp = _dot(vct_ref[...], pc.astype(BF16))
    psum = pc[:, lanes(0)]
    for h in range(1, hq):
        psum = psum + pc[:, lanes(h)]

    jrow = lax.broadcasted_iota(jnp.int32, (n_slc, 1), 0)
    ccol = lax.broadcasted_iota(jnp.int32, (1, n_cmp_rows), 1)
    lo = (SLC_BLK // CMP_STRIDE) * jrow - (CMP_LEN // CMP_STRIDE - 1)
    hi = (SLC_BLK // CMP_STRIDE) * jrow + (SLC_BLK // CMP_STRIDE - 1)
    mmap = jnp.where((ccol >= lo) & (ccol <= hi), 1.0, 0.0).astype(BF16)
    p1 = psum.astype(BF16)
    r1 = psum - p1.astype(F32)
    p2 = r1.astype(BF16)
    p3 = (r1 - p2.astype(F32)).astype(BF16)
    imp = _dot(mmap, p1) + _dot(mmap, p2) + _dot(mmap, p3)
    cur = t_row >> SLC_SHIFT
    forced = (jrow == 0) | (jrow == cur) | (jrow == cur - 1)
    score = jnp.where(forced, 1e9, jnp.where(jrow <= cur, imp, -1e9))
    rank = _block_ranks(score, jrow)
    sel_bias = jnp.where((rank < min(N_SEL, n_slc)) & (jrow <= cur), 0.0, NEG_INF)

    assert n_slc <= LANES and tq == LANES
    bias_q = jnp.transpose(jnp.concatenate(
        [sel_bias, jnp.zeros((LANES - n_slc, tq), F32)], axis=0)).astype(BF16)
    for h in range(hq):
        qa_ref[lanes(h), HEAD_DIM:HEAD_DIM + LANES] = bias_q

    m_ref[...] = jnp.full(m_ref.shape, NEG_INF, F32)
    l_ref[...] = jnp.zeros(l_ref.shape, F32)
    acc_ref[...] = jnp.zeros(acc_ref.shape, F32)
    tiles_per_chunk = NSA_KC // LANES

    def scores(c, s_ref):
        k0 = pl.multiple_of(c * NSA_KC, NSA_KC)
        k_aug = jnp.concatenate([ks_ref[pl.ds(k0, NSA_KC), :], e_ref[pl.ds(k0, NSA_KC), :]],
                                axis=1)
        s_ref[...] = _dot_nt(k_aug, qa_ref[...])

    def softmax_pv(c, s_ref):
        s = s_ref[...]
        m_old = m_ref[...]
        m_new = jnp.maximum(m_old, jnp.max(s, axis=0, keepdims=True))
        alpha = jnp.exp2(m_old - m_new)
        p = jnp.exp2(s - m_new)
        l_ref[...] = alpha * l_ref[...] + jnp.sum(p, axis=0, keepdims=True)
        vt = jnp.concatenate([vst_ref[c * tiles_per_chunk + i] for i in range(tiles_per_chunk)],
                             axis=1)
        acc_ref[...] = alpha * acc_ref[...] + _dot(vt, p.astype(BF16))
        m_ref[...] = m_new

    last = s0 // NSA_KC
    scores(0, sa_ref)

    dist = t_row - (ws + lax.broadcasted_iota(jnp.int32, (wlen, 1), 0))
    wbias = tile_heads(jnp.where((dist >= 0) & (dist < WIN_A), 0.0, NEG_INF))
    sw = sw_ref[...] + wbias
    ew = jnp.exp2(sw - jnp.max(sw, axis=0, keepdims=True))
    wt0 = ws // LANES
    vwt = jnp.concatenate([vwt_ref[wt0 + i] for i in range(wtiles)], axis=1)
    ow_ref[...] = _dot(vwt, ew.astype(BF16)) * (1.0 / jnp.sum(ew, axis=0, keepdims=True))

    def chunk_pair(i, carry):
        scores(2 * i + 1, sb_ref)
        softmax_pv(2 * i, sa_ref)
        scores(2 * i + 2, sa_ref)
        softmax_pv(2 * i + 1, sb_ref)
        return carry

    lax.fori_loop(0, last // 2, chunk_pair, 0)

    def last_chunk(s_ref):
        diag = pl.multiple_of(s0 - last * NSA_KC, LANES)
        krow = lax.broadcasted_iota(jnp.int32, (tq, 1), 0)
        lane = lax.broadcasted_iota(jnp.int32, (1, tq), 1)
        causal = tile_heads(jnp.where(krow <= lane, 0.0, NEG_INF))
        s_ref[pl.ds(diag, tq), :] = s_ref[pl.ds(diag, tq), :] + causal
        softmax_pv(last, s_ref)

    @pl.when(last % 2 == 0)
    def _():
        last_chunk(sa_ref)

    @pl.when(last % 2 == 1)
    def _():
        scores(last, sb_ref)
        softmax_pv(last - 1, sa_ref)
        last_chunk(sb_ref)

    gates_t = jnp.transpose(jax.nn.sigmoid(gz_ref[...].astype(F32) + gb_ref[...]))
    o_slc = acc_ref[...] * (1.0 / l_ref[...])
    for h in range(hq):
        o_h = (gates_t[3 * h:3 * h + 1, :] * o_cmp[:, lanes(h)]
               + gates_t[3 * h + 1:3 * h + 2, :] * o_slc[:, lanes(h)]
               + gates_t[3 * h + 2:3 * h + 3, :] * ow_ref[:, lanes(h)])
        o_ref[:, h * HEAD_DIM:(h + 1) * HEAD_DIM] = jnp.transpose(o_h).astype(o_ref.dtype)


def _nsa_attention(z, kc, vct, vst, vwt, gate_bias, *, batch, seq, units):
    tq = NSA_TQ
    nq = seq // tq
    n_cmp_rows = kc.shape[2]
    qb = HPG_A * HEAD_DIM
    kern = functools.partial(_nsa_kernel, tq=tq, seq=seq)
    n_tiles = seq // LANES
    onehot = jnp.asarray(np.arange(seq)[:, None] // SLC_BLK == np.arange(LANES)[None, :], BF16)
    return pl.pallas_call(
        kern,
        grid=(batch, N_KV_A, nq),
        in_specs=[
            pl.BlockSpec((tq, qb), lambda b, g, i: (b * nq + i, g)),
            pl.BlockSpec((None, None, n_cmp_rows, HEAD_DIM), lambda b, g, i: (b, g, 0, 0)),
            pl.BlockSpec((None, None, HEAD_DIM, n_cmp_rows), lambda b, g, i: (b, g, 0, 0)),
            pl.BlockSpec((seq, HEAD_DIM), lambda b, g, i: (b, units["k_slc"] + g)),
            pl.BlockSpec((seq, LANES), lambda b, g, i: (0, 0)),
            pl.BlockSpec((None, None, n_tiles, HEAD_DIM, LANES), lambda b, g, i: (b, g, 0, 0, 0)),
            pl.BlockSpec((seq, HEAD_DIM), lambda b, g, i: (b, units["k_win"] + g)),
            pl.BlockSpec((None, None, n_tiles, HEAD_DIM, LANES), lambda b, g, i: (b, g, 0, 0, 0)),
            pl.BlockSpec((tq, LANES), lambda b, g, i: (b * nq + i, units["gates"] + g)),
            pl.BlockSpec((None, 1, LANES), lambda b, g, i: (g, 0, 0)),
        ],
        out_specs=pl.BlockSpec((tq, qb), lambda b, g, i: (b * nq + i, g)),
        out_shape=jax.ShapeDtypeStruct((batch * seq, A_Q), BF16),
        scratch_shapes=[
            pltpu.VMEM((HPG_A * tq, HEAD_DIM + LANES), BF16),
            pltpu.VMEM((NSA_KC, HPG_A * tq), F32),
            pltpu.VMEM((NSA_KC, HPG_A * tq), F32),
            pltpu.VMEM((WIN_A + tq, HPG_A * tq), F32),
            pltpu.VMEM((HEAD_DIM, HPG_A * tq), F32),
            pltpu.VMEM((1, HPG_A * tq), F32),
            pltpu.VMEM((1, HPG_A * tq), F32),
            pltpu.VMEM((HEAD_DIM, HPG_A * tq), F32),
        ],
        compiler_params=_compiler_params(("parallel", "parallel", "arbitrary")),
        name="nsa_attention",
    )(z, kc, vct, z, onehot, vst, z, vwt, z, gate_bias)


def _band_attn_kernel(q_ref, k_ref, v_ref, o_ref, lse_ref, *, tu, lk, span):
    u0 = pl.program_id(2) * tu
    n_seq = k_ref.shape[0]
    ks = pl.multiple_of(jnp.clip(u0 - span, 0, n_seq - lk), LANES)
    dist = (u0 + lax.broadcasted_iota(jnp.int32, (tu, 1), 0)
            - (ks + lax.broadcasted_iota(jnp.int32, (1, lk), 1)))
    bias = jnp.where((dist >= 0) & (dist <= span), 0.0, NEG_INF)
    lane = lax.broadcasted_iota(jnp.int32, (1, LANES), 1)
    lse_tile = jnp.zeros((tu, LANES), F32)
    for h in range(DIL_HEADS):
        cols = slice(h * HEAD_DIM, (h + 1) * HEAD_DIM)
        s = _dot_nt(q_ref[:, cols], k_ref[pl.ds(ks, lk), cols]) + bias
        m = jnp.max(s, axis=-1, keepdims=True)
        p = jnp.exp2(s - m)
        l = jnp.sum(p, axis=-1, keepdims=True)
        o = _dot(p.astype(BF16), v_ref[pl.ds(ks, lk), cols]) * (1.0 / l)
        o_ref[:, cols] = o.astype(o_ref.dtype)
        lse_tile = jnp.where(lane == h, m + jnp.log2(l), lse_tile)
    lse_ref[...] = lse_tile


def _dil_merge_kernel(o0_ref, l0_ref, o1_ref, l1_ref, o2_ref, l2_ref, out_ref):
    groups = ((o0_ref, l0_ref), (o1_ref, l1_ref), (o2_ref, l2_ref))
    for h in range(DIL_HEADS):
        cols = slice(h * HEAD_DIM, (h + 1) * HEAD_DIM)
        shape = (out_ref.shape[0], HEAD_DIM)
        lses = [jnp.broadcast_to(l_ref[:, h:h + 1], shape) for _, l_ref in groups]
        top = jnp.maximum(jnp.maximum(lses[0], lses[1]), lses[2])
        ws = [jnp.exp2(lse - top) for lse in lses]
        num = ws[0] * o0_ref[:, cols].astype(F32)
        for w, (o_ref, _) in zip(ws[1:], groups[1:]):
            num = num + w * o_ref[:, cols].astype(F32)
        out_ref[:, cols] = (num * (1.0 / (ws[0] + ws[1] + ws[2]))).astype(out_ref.dtype)


def _dilated_attention(zb, *, batch, seq, units):
    n = zb.shape[1]
    width = DIL_HEADS * HEAD_DIM
    results = []
    for gi, (w, r) in enumerate(DIL_CONFIGS):
        n_seq = seq // r
        span = w // r
        tu = min(DIL_TQ, n_seq)
        lk = min(tu + span, n_seq)
        assert n_seq % tu == 0 and span % LANES == 0 and tu % LANES == 0
        kern = functools.partial(_band_attn_kernel, tu=tu, lk=lk, span=span)
        if r == 1:
            per_row = n // width
            q_unit, k_unit, v_unit = units["q"] + gi, units["k"], units["v"]
            zv = zb.reshape(batch, seq, n)
            operands = [zv, zv, zv]
        else:
            per_row, q_unit, k_unit, v_unit = 1, 0, 0, 0

            def class_view(unit):
                return zb[:, unit * width:(unit + 1) * width].reshape(batch, n_seq, r * width)

            operands = [class_view(units["q"] + gi), class_view(units["k"]),
                        class_view(units["v"])]

        def z_spec(rows, unit, whole):
            return pl.BlockSpec((None, rows, width),
                                lambda b, c, i: (b, 0 if whole else i, c * per_row + unit))

        o_g, lse_g = pl.pallas_call(
            kern,
            grid=(batch, r, n_seq // tu),
            in_specs=[z_spec(tu, q_unit, False), z_spec(n_seq, k_unit, True),
                      z_spec(n_seq, v_unit, True)],
            out_specs=[pl.BlockSpec((None, tu, width), lambda b, c, i: (b, i, c)),
                       pl.BlockSpec((None, tu, LANES), lambda b, c, i: (b, i, c))],
            out_shape=[jax.ShapeDtypeStruct((batch, n_seq, r * width), BF16),
                       jax.ShapeDtypeStruct((batch, n_seq, r * LANES), F32)],
            compiler_params=_compiler_params(("parallel", "parallel", "arbitrary")),
            name="dilated_attention",
        )(*operands)
        results += [o_g.reshape(batch * seq, width), lse_g.reshape(batch * seq, LANES)]

    m = batch * seq
    tm = min(OUT_TM, m)
    o_spec = pl.BlockSpec((tm, width), lambda i: (i, 0))
    l_spec = pl.BlockSpec((tm, LANES), lambda i: (i, 0))
    return pl.pallas_call(
        _dil_merge_kernel,
        grid=(m // tm,),
        in_specs=[o_spec, l_spec] * N_DIL_GROUPS,
        out_specs=o_spec,
        out_shape=jax.ShapeDtypeStruct((m, width), BF16),
        compiler_params=_compiler_params(("parallel",)),
        name="dilated_merge",
    )(*results)


def _mem_attn_kernel(q_ref, kv_ref, o_ref):
    for h in range(N_MEM_HEADS):
        cols = slice(h * HEAD_DIM, (h + 1) * HEAD_DIM)
        k = kv_ref[:, cols]
        v = kv_ref[:, MEM_Q + h * HEAD_DIM:MEM_Q + (h + 1) * HEAD_DIM]
        s = _dot_nt(q_ref[:, cols], k)
        e = jnp.exp(s - jnp.max(s, axis=-1, keepdims=True))
        p = e / jnp.sum(e, axis=-1, keepdims=True)
        o_ref[:, cols] = _dot(p.astype(BF16), v).astype(o_ref.dtype)


def _memory_attention(z, mkv, *, batch, seq, q_block):
    tq = MEM_TQ
    nq = seq // tq
    n_mem = mkv.shape[0] // batch
    return pl.pallas_call(
        _mem_attn_kernel,
        grid=(batch, nq),
        in_specs=[
            pl.BlockSpec((tq, MEM_Q), lambda b, i: (b * nq + i, q_block)),
            pl.BlockSpec((n_mem, 2 * MEM_Q), lambda b, i: (b, 0)),
        ],
        out_specs=pl.BlockSpec((tq, MEM_Q), lambda b, i: (b * nq + i, 0)),
        out_shape=jax.ShapeDtypeStruct((batch * seq, MEM_Q), BF16),
        compiler_params=_compiler_params(("parallel", "arbitrary")),
        name="memory_attention",
    )(z, mkv)


def _out_proj_kernel(a1_ref, a2_ref, w_ref, h_ref, o_ref):
    a = jnp.concatenate([a1_ref[...], a2_ref[...]], axis=1)
    o_ref[...] = h_ref[...] + _dot(a, w_ref[...])


def _out_proj(a1, a2, w_bf, h):
    m, d = h.shape
    tm, tn = min(OUT_TM, m), OUT_TN
    k1, k2 = a1.shape[1], a2.shape[1]
    assert w_bf.shape[0] == k1 + k2
    return pl.pallas_call(
        _out_proj_kernel,
        grid=(m // tm, d // tn),
        in_specs=[
            pl.BlockSpec((tm, k1), lambda i, j: (i, 0)),
            pl.BlockSpec((tm, k2), lambda i, j: (i, 0)),
            pl.BlockSpec((k1 + k2, tn), lambda i, j: (0, j)),
            pl.BlockSpec((tm, tn), lambda i, j: (i, j)),
        ],
        out_specs=pl.BlockSpec((tm, tn), lambda i, j: (i, j)),
        out_shape=jax.ShapeDtypeStruct((m, d), F32),
        compiler_params=_compiler_params(("parallel", "arbitrary")),
        name="out_proj",
    )(a1, a2, w_bf, h)


def _ffn_kernel(x_ref, g_ref, wg_ref, wu_ref, wd_ref, fg_ref, o_ref, xn_ref, acc_ref, *,
                final_norm):
    f = pl.program_id(1)

    @pl.when(f == 0)
    def _():
        xn_ref[...] = _rms_rows(x_ref[...], g_ref[...]).astype(BF16)
        acc_ref[...] = jnp.zeros(acc_ref.shape, F32)

    xn = xn_ref[...]
    gate = _dot(xn, wg_ref[...])
    up = _dot(xn, wu_ref[...])
    act = (gate * jax.nn.sigmoid(gate) * up).astype(BF16)
    acc_ref[...] += _dot(act, wd_ref[...])

    @pl.when(f == pl.num_programs(1) - 1)
    def _():
        y = x_ref[...] + acc_ref[...]
        if final_norm:
            y = _rms_rows(y, fg_ref[...])
        o_ref[...] = y


def _ffn(h, g, wg_bf, wu_bf, wd_bf, final_gain, *, final_norm):
    m, d = h.shape
    dff = wg_bf.shape[1]
    tm, tf = min(FFN_TM, m), FFN_TF
    assert m % tm == 0 and dff % tf == 0
    kern = functools.partial(_ffn_kernel, final_norm=final_norm)
    return pl.pallas_call(
        kern,
        grid=(m // tm, dff // tf),
        in_specs=[
            pl.BlockSpec((tm, d), lambda i, f: (i, 0)),
            pl.BlockSpec((1, d), lambda i, f: (0, 0)),
            pl.BlockSpec((d, tf), lambda i, f: (0, f)),
            pl.BlockSpec((d, tf), lambda i, f: (0, f)),
            pl.BlockSpec((tf, d), lambda i, f: (f, 0)),
            pl.BlockSpec((1, d), lambda i, f: (0, 0)),
        ],
        out_specs=pl.BlockSpec((tm, d), lambda i, f: (i, 0)),
        out_shape=jax.ShapeDtypeStruct((m, d), F32),
        scratch_shapes=[pltpu.VMEM((tm, d), BF16), pltpu.VMEM((tm, d), F32)],
        compiler_params=_compiler_params(("parallel", "arbitrary")),
        name="ffn",
    )(h, g.reshape(1, d), wg_bf, wu_bf, wd_bf, final_gain.reshape(1, d))


def _rope_tables(seq):
    inv = 1.0 / (ROPE_THETA ** (jnp.arange(0, HEAD_DIM, 2, dtype=F32) / HEAD_DIM))
    ang = jnp.arange(seq, dtype=F32)[:, None] * inv[None, :]
    cos, sin = jnp.cos(ang), jnp.sin(ang)
    return jnp.concatenate([cos, cos], axis=1), jnp.concatenate([-sin, sin], axis=1)


def _tiles_transposed(v, batch, seq, heads):
    v = v.reshape(batch, seq // LANES, LANES, heads, HEAD_DIM)
    return jnp.transpose(v, (0, 3, 1, 4, 2))


A_UNITS = {"q": 0, "k_cmp": 12, "k_slc": 14, "k_win": 16, "v_cmp": 18, "v_slc": 20,
           "v_win": 22, "mem_q": 24, "gates": 28}
B_UNITS = {"q": 0, "k": 3, "mem_q": 4, "v": 5}
A_TN = 6 * HEAD_DIM
A_NPAD = 30 * HEAD_DIM
A_ROPE_BLOCKS = 3


def _layer_a_weight(w_in):
    kv0 = A_Q

    def kv_cols(branch):
        return w_in[:, kv0 + branch * N_KV_A * HEAD_DIM:kv0 + (branch + 1) * N_KV_A * HEAD_DIM]

    gate0 = A_Q + A_KV
    mem0 = gate0 + A_GATE
    per_group = 3 * HPG_A
    gate_units = [jnp.pad(w_in[:, gate0 + g * per_group:gate0 + (g + 1) * per_group],
                          ((0, 0), (0, LANES - per_group))) for g in range(N_KV_A)]
    w = jnp.concatenate([w_in[:, :A_Q], kv_cols(0), kv_cols(2), kv_cols(4), kv_cols(1),
                         kv_cols(3), kv_cols(5), w_in[:, mem0:mem0 + MEM_Q]] + gate_units, axis=1)
    assert w.shape[1] == A_NPAD
    scale = np.ones((1, A_NPAD), np.float32)
    scale[0, :A_Q] = SCALE * LOG2E
    scale[0, A_UNITS["mem_q"] * HEAD_DIM:A_UNITS["mem_q"] * HEAD_DIM + MEM_Q] = SCALE
    return w.astype(BF16), jnp.asarray(scale)


def _layer_a(h, mem, cosf, sinf, p, *, batch, seq):
    w_in_bf, col_scale = _layer_a_weight(p["w_in"])
    z = _norm_proj(h, p["norm_attn"], w_in_bf, col_scale, cosf, sinf,
                   tn=A_TN, n_rope_blocks=A_ROPE_BLOCKS, seq=seq)

    def unit_cols(name, n_units):
        return z[:, A_UNITS[name] * HEAD_DIM:(A_UNITS[name] + n_units) * HEAD_DIM]

    per_row = CMP_LEN // 2

    def cmp_rows(name):
        x = unit_cols(name, N_KV_A).reshape(batch, seq // per_row, per_row, N_KV_A, HEAD_DIM)
        return jnp.transpose(x, (0, 3, 1, 2, 4)).reshape(batch * N_KV_A, seq // per_row,
                                                          per_row * HEAD_DIM)

    x2 = jnp.stack([cmp_rows("k_cmp"), cmp_rows("v_cmp")])

    def pe_rows(pe):
        return jnp.pad(pe.reshape(2, per_row * HEAD_DIM), ((0, 6), (0, 0)))

    pe2 = jnp.stack([pe_rows(p["cmp_pe_k"]), pe_rows(p["cmp_pe_v"])])
    w1 = jnp.stack([p["cmp_w1_k"], p["cmp_w1_v"]]).astype(BF16)
    w2 = jnp.stack([p["cmp_w2_k"], p["cmp_w2_v"]]).astype(BF16)
    cmp_out = _compress(x2, pe2, w1, w2)
    n_rows = seq // per_row
    kc = cmp_out[0].reshape(batch, N_KV_A, n_rows, HEAD_DIM)
    vct = jnp.transpose(cmp_out[1].reshape(batch, N_KV_A, n_rows, HEAD_DIM), (0, 1, 3, 2))

    vst = _tiles_transposed(unit_cols("v_slc", N_KV_A), batch, seq, N_KV_A)
    vwt = _tiles_transposed(unit_cols("v_win", N_KV_A), batch, seq, N_KV_A)
    gb = jnp.pad(p["gate_bias"].reshape(N_KV_A, 1, 3 * HPG_A),
                 ((0, 0), (0, 0), (0, LANES - 3 * HPG_A)))
    o_nsa = _nsa_attention(z, kc, vct, vst, vwt, gb, batch=batch, seq=seq, units=A_UNITS)

    mkv = _mem_kv(mem, p["norm_mem"], p["w_mem_kv"])
    o_mem = _memory_attention(z, mkv, batch=batch, seq=seq,
                              q_block=A_UNITS["mem_q"] * HEAD_DIM // MEM_Q)
    return _out_proj(o_nsa, o_mem, p["w_out"].astype(BF16), h)


def _mem_kv(mem, norm_mem, w_mem_kv):
    b, m, d = mem.shape
    ones = jnp.ones((1, w_mem_kv.shape[1]), F32)
    dummy = jnp.zeros((m, HEAD_DIM), F32)
    return _norm_proj(mem.reshape(b * m, d), norm_mem, w_mem_kv.astype(BF16), ones, dummy, dummy,
                      tn=MEM_Q, n_rope_blocks=0, seq=m, tm=m)


def kernel(x, mem, a_norm_attn, a_w_in, a_gate_bias, a_cmp_pe_k, a_cmp_w1_k, a_cmp_w2_k, a_cmp_pe_v, a_cmp_w1_v, a_cmp_w2_v, a_norm_mem, a_w_mem_kv, a_w_out, a_norm_ffn, a_w_gate, a_w_up, a_w_down, kv_norm, w_kv_shared, b_norm_attn, b_w_in, b_norm_mem, b_w_mem_kv, b_w_out, b_norm_ffn, b_w_gate, b_w_up, b_w_down, final_norm):
    batch, seq, d = x.shape
    n_a = a_w_in.shape[0]
    n_b = b_w_in.shape[0]
    cosf, sinf = _rope_tables(seq)
    h = x.reshape(batch * seq, d)
    unit_gain = jnp.ones((d,), F32)

    for l in range(n_a):
        p = {"norm_attn": a_norm_attn[l], "w_in": a_w_in[l], "gate_bias": a_gate_bias[l],
             "cmp_pe_k": a_cmp_pe_k[l], "cmp_w1_k": a_cmp_w1_k[l], "cmp_w2_k": a_cmp_w2_k[l],
             "cmp_pe_v": a_cmp_pe_v[l], "cmp_w1_v": a_cmp_w1_v[l], "cmp_w2_v": a_cmp_w2_v[l],
             "norm_mem": a_norm_mem[l], "w_mem_kv": a_w_mem_kv[l], "w_out": a_w_out[l]}
        h = _layer_a(h, mem, cosf, sinf, p, batch=batch, seq=seq)
        last = (l == n_a - 1) and n_b == 0
        h = _ffn(h, a_norm_ffn[l], a_w_gate[l].astype(BF16), a_w_up[l].astype(BF16),
                 a_w_down[l].astype(BF16), final_norm if last else unit_gain, final_norm=last)

    if n_b > 0:
        assert n_b == 1, "the shared K/V projection is fused into the single mixer-B layer"
        n_kv_half = w_kv_shared.shape[1] // 2
        for l in range(n_b):
            w_q = b_norm_attn[l][:, None] * b_w_in[l]
            w_kv = kv_norm[:, None] * w_kv_shared
            w_cat = jnp.concatenate([w_q[:, :B_Q], w_kv[:, :n_kv_half], w_q[:, B_Q:],
                                     w_kv[:, n_kv_half:]], axis=1).astype(BF16)
            b_scale = np.ones((1, w_cat.shape[1]), np.float32)
            b_scale[0, :B_Q] = SCALE * LOG2E
            b_scale[0, B_Q + n_kv_half:B_Q + n_kv_half + MEM_Q] = SCALE
            zb = _norm_proj(h, unit_gain, w_cat, jnp.asarray(b_scale), cosf, sinf,
                            tn=MEM_Q, n_rope_blocks=(B_Q + n_kv_half) // MEM_Q, seq=seq)
            o_dil = _dilated_attention(zb, batch=batch, seq=seq, units=B_UNITS)
            mkv = _mem_kv(mem, b_norm_mem[l], b_w_mem_kv[l])
            o_mem = _memory_attention(zb, mkv, batch=batch, seq=seq, q_block=B_UNITS["mem_q"])
            h = _out_proj(o_dil, o_mem, b_w_out[l].astype(BF16), h)
            last = l == n_b - 1
            h = _ffn(h, b_norm_ffn[l], b_w_gate[l].astype(BF16), b_w_up[l].astype(BF16),
                     b_w_down[l].astype(BF16), final_norm if last else unit_gain, final_norm=last)

    return h.reshape(batch, seq, d)
```

```python
import functools
import math

import numpy as np
import jax
import jax.numpy as jnp
from jax import lax
from jax.experimental import pallas as pl
from jax.experimental.pallas import tpu as pltpu

F32 = jnp.float32
BF16 = jnp.bfloat16

HEAD_DIM = 128
N_HEADS_A = 12
N_KV_A = 2
HPG_A = N_HEADS_A // N_KV_A
CMP_LEN = 32
CMP_STRIDE = 16
CMP_HIDDEN = 256
SLC_BLK = 64
SLC_SHIFT = SLC_BLK.bit_length() - 1
N_SEL = 16
WIN_A = 512
DIL_CONFIGS = ((128, 1), (512, 4), (2048, 16))
N_DIL_GROUPS = len(DIL_CONFIGS)
DIL_HEADS = 4
N_MEM_HEADS = 4
ROPE_THETA = 10000.0
EPS = 1e-6
NEG_INF = -1e30
TINY = 1e-30
SCALE = HEAD_DIM ** -0.5
LOG2E = math.log2(math.e)

A_Q = N_HEADS_A * HEAD_DIM
A_KV = 6 * N_KV_A * HEAD_DIM
A_GATE = 3 * N_HEADS_A
MEM_Q = N_MEM_HEADS * HEAD_DIM
B_Q = N_DIL_GROUPS * DIL_HEADS * HEAD_DIM

LANES = 128
SUBLANES = 8
VMEM_LIMIT_BYTES = 56 * 1024 * 1024

PROJ_TM = 1024
FFN_UP_TM = 1024
FFN_TM = 512
FFN_TF = 512
OUT_TM = 1024
OUT_TN = 1024
NSA_TQ = 128
NSA_KC = 512
DIL_TQ = 256
MEM_TQ = 512

NT_DIMS = (((1,), (1,)), ((), ()))


def _compiler_params(semantics):
    return pltpu.CompilerParams(dimension_semantics=semantics,
                                vmem_limit_bytes=VMEM_LIMIT_BYTES)


def _rms_rows(x, g):
    ms = jnp.mean(x * x, axis=-1, keepdims=True)
    return x * lax.rsqrt(ms + EPS) * g


def _dot(a, b):
    return jnp.dot(a, b, preferred_element_type=F32)


def _dot_nt(a, b):
    return lax.dot_general(a, b, NT_DIMS, preferred_element_type=F32)


def _norm_proj_kernel(x_ref, g_ref, w_ref, cs_ref, cos_ref, sin_ref, o_ref, xn_ref, *,
                      n_rope_blocks, tn):
    j = pl.program_id(1)

    @pl.when(j == 0)
    def _():
        xn_ref[...] = _rms_rows(x_ref[...], g_ref[...]).astype(BF16)

    acc = _dot(xn_ref[...], w_ref[...]) * cs_ref[...]

    if n_rope_blocks > 0:
        roped = j < n_rope_blocks
        c = jnp.where(roped, cos_ref[...], 1.0)
        s = jnp.where(roped, sin_ref[...], 0.0)
        for h in range(tn // HEAD_DIM):
            y = acc[:, h * HEAD_DIM:(h + 1) * HEAD_DIM]
            rot = pltpu.roll(y, HEAD_DIM // 2, 1)
            o_ref[:, h * HEAD_DIM:(h + 1) * HEAD_DIM] = (y * c + rot * s).astype(o_ref.dtype)
    else:
        o_ref[...] = acc.astype(o_ref.dtype)


def _norm_proj(x, g, w_bf, col_scale, cosf, sinf, *, tn, n_rope_blocks, seq, tm=PROJ_TM):
    m, d = x.shape
    n = w_bf.shape[1]
    tm = min(tm, m)
    assert m % tm == 0 and n % tn == 0 and seq % tm == 0
    pos_blocks = seq // tm
    kern = functools.partial(_norm_proj_kernel, n_rope_blocks=n_rope_blocks, tn=tn)
    return pl.pallas_call(
        kern,
        grid=(m // tm, n // tn),
        in_specs=[
            pl.BlockSpec((tm, d), lambda i, j: (i, 0)),
            pl.BlockSpec((1, d), lambda i, j: (0, 0)),
            pl.BlockSpec((d, tn), lambda i, j: (0, j)),
            pl.BlockSpec((1, tn), lambda i, j: (0, j)),
            pl.BlockSpec((tm, HEAD_DIM), lambda i, j: (i % pos_blocks, 0)),
            pl.BlockSpec((tm, HEAD_DIM), lambda i, j: (i % pos_blocks, 0)),
        ],
        out_specs=pl.BlockSpec((tm, tn), lambda i, j: (i, j)),
        out_shape=jax.ShapeDtypeStruct((m, n), BF16),
        scratch_shapes=[pltpu.VMEM((tm, d), BF16)],
        compiler_params=_compiler_params(("parallel", "arbitrary")),
        name="norm_proj",
    )(x, g.reshape(1, d), w_bf, col_scale, cosf, sinf)


def _compress_kernel(x_ref, pe_ref, w1_ref, w2_ref, o_ref):
    half = (CMP_LEN // 2) * HEAD_DIM
    x = x_ref[...].astype(F32)
    xlo = (x + pe_ref[0:1, :]).astype(BF16)
    xhi = (x + pe_ref[1:2, :]).astype(BF16)
    ylo = _dot(xlo, w1_ref[:half, :])
    yhi = _dot(xhi, w1_ref[half:, :])
    n_rows = x.shape[0]
    hid = ylo + pltpu.roll(yhi, n_rows - 1, 0)
    act = (hid * jax.nn.sigmoid(hid)).astype(BF16)
    o_ref[...] = _dot(act, w2_ref[...]).astype(o_ref.dtype)


def _compress(x2, pe2, w1_bf, w2_bf):
    _, bg, nrow, wide = x2.shape
    return pl.pallas_call(
        _compress_kernel,
        grid=(2, bg),
        in_specs=[
            pl.BlockSpec((None, None, nrow, wide), lambda t, i: (t, i, 0, 0)),
            pl.BlockSpec((None, 8, wide), lambda t, i: (t, 0, 0)),
            pl.BlockSpec((None, 2 * wide, CMP_HIDDEN), lambda t, i: (t, 0, 0)),
            pl.BlockSpec((None, CMP_HIDDEN, HEAD_DIM), lambda t, i: (t, 0, 0)),
        ],
        out_specs=pl.BlockSpec((None, None, nrow, HEAD_DIM), lambda t, i: (t, i, 0, 0)),
        out_shape=jax.ShapeDtypeStruct((2, bg, nrow, HEAD_DIM), BF16),
        compiler_params=_compiler_params(("parallel", "arbitrary")),
        name="nsa_compress",
    )(x2, pe2, w1_bf, w2_bf)


def _block_ranks(score, jrow):
    n_blk = score.shape[0]
    groups = n_blk // SUBLANES
    blocks = [score[SUBLANES * r:SUBLANES * (r + 1), :] for r in range(groups)]
    rows = [jrow[SUBLANES * r:SUBLANES * (r + 1), :] for r in range(groups)]
    ranks = [jnp.zeros(blocks[0].shape, F32) for _ in range(groups)]
    for j in range(n_blk):
        rj = score[j:j + 1, :]
        for r in range(groups):
            if r > j // SUBLANES:
                ahead = rj >= blocks[r]
            elif r < j // SUBLANES:
                ahead = rj > blocks[r]
            else:
                ahead = (rj > blocks[r]) | ((rj == blocks[r]) & (rows[r] > j))
            ranks[r] = ranks[r] + jnp.where(ahead, 1.0, 0.0)
    return jnp.concatenate(ranks, axis=0)


def _nsa_kernel(q_ref, kc_ref, vct_ref, ks_ref, e_ref, vst_ref, kw_ref, vwt_ref, gz_ref, gb_ref,
                o_ref, qa_ref, sa_ref, sb_ref, sw_ref, ow_ref, m_ref, l_ref, acc_ref, *, tq, seq):
    hq = HPG_A
    groups = range(N_KV_A)
    n_cmp_rows = kc_ref.shape[1]
    n_slc = seq // SLC_BLK
    qi = pl.program_id(1)
    s0 = qi * tq
    t_row = s0 + lax.broadcasted_iota(jnp.int32, (1, tq), 1)

    def lanes(h):
        return slice(h * tq, (h + 1) * tq)

    def gcols(g):
        return slice(g * HEAD_DIM, (g + 1) * HEAD_DIM)

    def tile_heads(x):
        return jnp.concatenate([x] * hq, axis=1)

    wtiles = (WIN_A + tq) // LANES
    wlen = wtiles * LANES
    ws = pl.multiple_of(jnp.maximum(s0 - WIN_A, 0), LANES)
    c_end = lax.broadcasted_iota(jnp.int32, (n_cmp_rows, 1), 0) * CMP_STRIDE + (CMP_LEN - 1)
    cvalid = tile_heads(c_end <= t_row)
    jrow = lax.broadcasted_iota(jnp.int32, (n_slc, 1), 0)
    ccol = lax.broadcasted_iota(jnp.int32, (1, n_cmp_rows), 1)
    lo = (SLC_BLK // CMP_STRIDE) * jrow - (CMP_LEN // CMP_STRIDE - 1)
    hi = (SLC_BLK // CMP_STRIDE) * jrow + (SLC_BLK // CMP_STRIDE - 1)
    mmap = jnp.where((ccol >= lo) & (ccol <= hi), 1.0, 0.0).astype(BF16)
    cur = t_row >> SLC_SHIFT
    forced = (jrow == 0) | (jrow == cur) | (jrow == cur - 1)
    assert n_slc <= LANES and tq == LANES

    o_cmp = []
    for g in groups:
        for h in range(hq):
            head = g * hq + h
            qa_ref[g, lanes(h), 0:HEAD_DIM] = q_ref[:, head * HEAD_DIM:(head + 1) * HEAD_DIM]
        q6 = qa_ref[g, :, 0:HEAD_DIM]

        sw_ref[g] = _dot_nt(kw_ref[pl.ds(ws, wlen), gcols(g)], q6)

        sc = jnp.where(cvalid, _dot_nt(kc_ref[g], q6), NEG_INF)
        ec = jnp.where(cvalid, jnp.exp2(sc - jnp.max(sc, axis=0, keepdims=True)), 0.0)
        pc = ec * (1.0 / jnp.maximum(jnp.sum(ec, axis=0, keepdims=True), TINY))
        o_cmp.append(_dot(vct_ref[g], pc.astype(BF16)))
        psum = pc[:, lanes(0)]
        for h in range(1, hq):
            psum = psum + pc[:, lanes(h)]

        p1 = psum.astype(BF16)
        r1 = psum - p1.astype(F32)
        p2 = r1.astype(BF16)
        p3 = (r1 - p2.astype(F32)).astype(BF16)
        imp = _dot(mmap, p1) + _dot(mmap, p2) + _dot(mmap, p3)
        score = jnp.where(forced, 1e9, jnp.where(jrow <= cur, imp, -1e9))
        rank = _block_ranks(score, jrow)
        sel_bias = jnp.where((rank < min(N_SEL, n_slc)) & (jrow <= cur), 0.0, NEG_INF)

        bias_q = jnp.transpose(jnp.concatenate(
            [sel_bias, jnp.zeros((LANES - n_slc, tq), F32)], axis=0)).astype(BF16)
        for h in range(hq):
            qa_ref[g, lanes(h), HEAD_DIM:HEAD_DIM + LANES] = bias_q

    m_ref[...] = jnp.full(m_ref.shape, NEG_INF, F32)
    l_ref[...] = jnp.zeros(l_ref.shape, F32)
    acc_ref[...] = jnp.zeros(acc_ref.shape, F32)
    tiles_per_chunk = NSA_KC // LANES

    def scores(g, c, s_ref):
        k0 = pl.multiple_of(c * NSA_KC, NSA_KC)
        k_aug = jnp.concatenate([ks_ref[pl.ds(k0, NSA_KC), gcols(g)],
                                 e_ref[pl.ds(k0, NSA_KC), :]], axis=1)
        s_ref[g] = _dot_nt(k_aug, qa_ref[g])

    def softmax_pv(g, c, s_ref):
        s = s_ref[g]
        m_old = m_ref[g]
        m_new = jnp.maximum(m_old, jnp.max(s, axis=0, keepdims=True))
        alpha = jnp.exp2(m_old - m_new)
        p = jnp.exp2(s - m_new)
        l_ref[g] = alpha * l_ref[g] + jnp.sum(p, axis=0, keepdims=True)
        vt = jnp.concatenate([vst_ref[g, c * tiles_per_chunk + i]
                              for i in range(tiles_per_chunk)], axis=1)
        acc_ref[g] = alpha * acc_ref[g] + _dot(vt, p.astype(BF16))
        m_ref[g] = m_new

    last = s0 // NSA_KC
    for g in groups:
        scores(g, 0, sa_ref)

    dist = t_row - (ws + lax.broadcasted_iota(jnp.int32, (wlen, 1), 0))
    wbias = tile_heads(jnp.where((dist >= 0) & (dist < WIN_A), 0.0, NEG_INF))
    wt0 = ws // LANES
    for g in groups:
        sw = sw_ref[g] + wbias
        ew = jnp.exp2(sw - jnp.max(sw, axis=0, keepdims=True))
        vwt = jnp.concatenate([vwt_ref[g, wt0 + i] for i in range(wtiles)], axis=1)
        ow_ref[g] = _dot(vwt, ew.astype(BF16)) * (1.0 / jnp.sum(ew, axis=0, keepdims=True))

    def chunk_pair(i, carry):
        for g in groups:
            scores(g, 2 * i + 1, sb_ref)
            softmax_pv(g, 2 * i, sa_ref)
        for g in groups:
            scores(g, 2 * i + 2, sa_ref)
            softmax_pv(g, 2 * i + 1, sb_ref)
        return carry

    lax.fori_loop(0, last // 2, chunk_pair, 0)

    def last_chunk(g, s_ref):
        diag = pl.multiple_of(s0 - last * NSA_KC, LANES)
        krow = lax.broadcasted_iota(jnp.int32, (tq, 1), 0)
        lane = lax.broadcasted_iota(jnp.int32, (1, tq), 1)
        causal = tile_heads(jnp.where(krow <= lane, 0.0, NEG_INF))
        s_ref[g, pl.ds(diag, tq), :] = s_ref[g, pl.ds(diag, tq), :] + causal
        softmax_pv(g, last, s_ref)

    @pl.when(last % 2 == 0)
    def _():
        for g in groups:
            last_chunk(g, sa_ref)

    @pl.when(last % 2 == 1)
    def _():
        for g in groups:
            scores(g, last, sb_ref)
            softmax_pv(g, last - 1, sa_ref)
        for g in groups:
            last_chunk(g, sb_ref)

    for g in groups:
        gates_t = jnp.transpose(jax.nn.sigmoid(gz_ref[:, g * LANES:(g + 1) * LANES].astype(F32)
                                               + gb_ref[g]))
        o_slc = acc_ref[g] * (1.0 / l_ref[g])
        for h in range(hq):
            o_h = (gates_t[3 * h:3 * h + 1, :] * o_cmp[g][:, lanes(h)]
                   + gates_t[3 * h + 1:3 * h + 2, :] * o_slc[:, lanes(h)]
                   + gates_t[3 * h + 2:3 * h + 3, :] * ow_ref[g, :, lanes(h)])
            head = g * hq + h
            o_ref[:, head * HEAD_DIM:(head + 1) * HEAD_DIM] = jnp.transpose(o_h).astype(o_ref.dtype)


def _nsa_attention(z, kc, vct, vst, vwt, gate_bias, *, batch, seq, units):
    tq = NSA_TQ
    nq = seq // tq
    n_cmp_rows = kc.shape[2]
    ng = N_KV_A
    hl = HPG_A * tq
    gw = ng * HEAD_DIM
    for name in ("k_slc", "k_win", "gates"):
        assert units[name] % ng == 0
    kern = functools.partial(_nsa_kernel, tq=tq, seq=seq)
    n_tiles = seq // LANES
    onehot = jnp.asarray(np.arange(seq)[:, None] // SLC_BLK == np.arange(LANES)[None, :], BF16)
    return pl.pallas_call(
        kern,
        grid=(batch, nq),
        in_specs=[
            pl.BlockSpec((tq, A_Q), lambda b, i: (b * nq + i, 0)),
            pl.BlockSpec((None, ng, n_cmp_rows, HEAD_DIM), lambda b, i: (b, 0, 0, 0)),
            pl.BlockSpec((None, ng, HEAD_DIM, n_cmp_rows), lambda b, i: (b, 0, 0, 0)),
            pl.BlockSpec((seq, gw), lambda b, i: (b, units["k_slc"] // ng)),
            pl.BlockSpec((seq, LANES), lambda b, i: (0, 0)),
            pl.BlockSpec((None, ng, n_tiles, HEAD_DIM, LANES), lambda b, i: (b, 0, 0, 0, 0)),
            pl.BlockSpec((seq, gw), lambda b, i: (b, units["k_win"] // ng)),
            pl.BlockSpec((None, ng, n_tiles, HEAD_DIM, LANES), lambda b, i: (b, 0, 0, 0, 0)),
            pl.BlockSpec((tq, ng * LANES), lambda b, i: (b * nq + i, units["gates"] // ng)),
            pl.BlockSpec((ng, 1, LANES), lambda b, i: (0, 0, 0)),
        ],
        out_specs=pl.BlockSpec((tq, A_Q), lambda b, i: (b * nq + i, 0)),
        out_shape=jax.ShapeDtypeStruct((batch * seq, A_Q), BF16),
        scratch_shapes=[
            pltpu.VMEM((ng, hl, HEAD_DIM + LANES), BF16),
            pltpu.VMEM((ng, NSA_KC, hl), F32),
            pltpu.VMEM((ng, NSA_KC, hl), F32),
            pltpu.VMEM((ng, WIN_A + tq, hl), F32),
            pltpu.VMEM((ng, HEAD_DIM, hl), F32),
            pltpu.VMEM((ng, 1, hl), F32),
            pltpu.VMEM((ng, 1, hl), F32),
            pltpu.VMEM((ng, HEAD_DIM, hl), F32),
        ],
        compiler_params=_compiler_params(("parallel", "arbitrary")),
        name="nsa_attention",
    )(z, kc, vct, z, onehot, vst, z, vwt, z, gate_bias)


def _band_attn_kernel(q_ref, k_ref, v_ref, o_ref, lse_ref, *, tu, lk, span):
    u0 = pl.program_id(2) * tu
    n_seq = k_ref.shape[0]
    ks = pl.multiple_of(jnp.clip(u0 - span, 0, n_seq - lk), LANES)
    dist = (u0 + lax.broadcasted_iota(jnp.int32, (tu, 1), 0)
            - (ks + lax.broadcasted_iota(jnp.int32, (1, lk), 1)))
    bias = jnp.where((dist >= 0) & (dist <= span), 0.0, NEG_INF)
    lane = lax.broadcasted_iota(jnp.int32, (1, LANES), 1)
    lse_tile = jnp.zeros((tu, LANES), F32)
    for h in range(DIL_HEADS):
        cols = slice(h * HEAD_DIM, (h + 1) * HEAD_DIM)
        s = _dot_nt(q_ref[:, cols], k_ref[pl.ds(ks, lk), cols]) + bias
        m = jnp.max(s, axis=-1, keepdims=True)
        p = jnp.exp2(s - m)
        l = jnp.sum(p, axis=-1, keepdims=True)
        o = _dot(p.astype(BF16), v_ref[pl.ds(ks, lk), cols]) * (1.0 / l)
        o_ref[:, cols] = o.astype(o_ref.dtype)
        lse_tile = jnp.where(lane == h, m + jnp.log2(l), lse_tile)
    lse_ref[...] = lse_tile


def _dil_merge_kernel(o0_ref, l0_ref, o1_ref, l1_ref, o2_ref, l2_ref, out_ref):
    groups = ((o0_ref, l0_ref), (o1_ref, l1_ref), (o2_ref, l2_ref))
    for h in range(DIL_HEADS):
        cols = slice(h * HEAD_DIM, (h + 1) * HEAD_DIM)
        shape = (out_ref.shape[0], HEAD_DIM)
        lses = [jnp.broadcast_to(l_ref[:, h:h + 1], shape) for _, l_ref in groups]
        top = jnp.maximum(jnp.maximum(lses[0], lses[1]), lses[2])
        ws = [jnp.exp2(lse - top) for lse in lses]
        num = ws[0] * o0_ref[:, cols].astype(F32)
        for w, (o_ref, _) in zip(ws[1:], groups[1:]):
            num = num + w * o_ref[:, cols].astype(F32)
        out_ref[:, cols] = (num * (1.0 / (ws[0] + ws[1] + ws[2]))).astype(out_ref.dtype)


def _dilated_attention(zb, *, batch, seq, units):
    n = zb.shape[1]
    width = DIL_HEADS * HEAD_DIM
    results = []
    for gi, (w, r) in enumerate(DIL_CONFIGS):
        n_seq = seq // r
        span = w // r
        tu = min(DIL_TQ, n_seq)
        lk = min(tu + span, n_seq)
        assert n_seq % tu == 0 and span % LANES == 0 and tu % LANES == 0
        kern = functools.partial(_band_attn_kernel, tu=tu, lk=lk, span=span)
        if r == 1:
            per_row = n // width
            q_unit, k_unit, v_unit = units["q"] + gi, units["k"], units["v"]
            zv = zb.reshape(batch, seq, n)
            operands = [zv, zv, zv]
        else:
            per_row, q_unit, k_unit, v_unit = 1, 0, 0, 0

            def class_view(unit):
                return zb[:, unit * width:(unit + 1) * width].reshape(batch, n_seq, r * width)

            operands = [class_view(units["q"] + gi), class_view(units["k"]),
                        class_view(units["v"])]

        def z_spec(rows, unit, whole):
            return pl.BlockSpec((None, rows, width),
                                lambda b, c, i: (b, 0 if whole else i, c * per_row + unit))

        o_g, lse_g = pl.pallas_call(
            kern,
            grid=(batch, r, n_seq // tu),
            in_specs=[z_spec(tu, q_unit, False), z_spec(n_seq, k_unit, True),
                      z_spec(n_seq, v_unit, True)],
            out_specs=[pl.BlockSpec((None, tu, width), lambda b, c, i: (b, i, c)),
                       pl.BlockSpec((None, tu, LANES), lambda b, c, i: (b, i, c))],
            out_shape=[jax.ShapeDtypeStruct((batch, n_seq, r * width), BF16),
                       jax.ShapeDtypeStruct((batch, n_seq, r * LANES), F32)],
            compiler_params=_compiler_params(("parallel", "parallel", "arbitrary")),
            name="dilated_attention",
        )(*operands)
        results += [o_g.reshape(batch * seq, width), lse_g.reshape(batch * seq, LANES)]

    m = batch * seq
    tm = min(OUT_TM, m)
    o_spec = pl.BlockSpec((tm, width), lambda i: (i, 0))
    l_spec = pl.BlockSpec((tm, LANES), lambda i: (i, 0))
    return pl.pallas_call(
        _dil_merge_kernel,
        grid=(m // tm,),
        in_specs=[o_spec, l_spec] * N_DIL_GROUPS,
        out_specs=o_spec,
        out_shape=jax.ShapeDtypeStruct((m, width), BF16),
        compiler_params=_compiler_params(("parallel",)),
        name="dilated_merge",
    )(*results)


def _mem_attn_kernel(q_ref, kv_ref, o_ref):
    for h in range(N_MEM_HEADS):
        cols = slice(h * HEAD_DIM, (h + 1) * HEAD_DIM)
        k = kv_ref[:, cols]
        v = kv_ref[:, MEM_Q + h * HEAD_DIM:MEM_Q + (h + 1) * HEAD_DIM]
        s = _dot_nt(q_ref[:, cols], k)
        e = jnp.exp(s - jnp.max(s, axis=-1, keepdims=True))
        p = e / jnp.sum(e, axis=-1, keepdims=True)
        o_ref[:, cols] = _dot(p.astype(BF16), v).astype(o_ref.dtype)


def _memory_attention(z, mkv, *, batch, seq, q_block):
    tq = MEM_TQ
    nq = seq // tq
    n_mem = mkv.shape[0] // batch
    return pl.pallas_call(
        _mem_attn_kernel,
        grid=(batch, nq),
        in_specs=[
            pl.BlockSpec((tq, MEM_Q), lambda b, i: (b * nq + i, q_block)),
            pl.BlockSpec((n_mem, 2 * MEM_Q), lambda b, i: (b, 0)),
        ],
        out_specs=pl.BlockSpec((tq, MEM_Q), lambda b, i: (b * nq + i, 0)),
        out_shape=jax.ShapeDtypeStruct((batch * seq, MEM_Q), BF16),
        compiler_params=_compiler_params(("parallel", "arbitrary")),
        name="memory_attention",
    )(z, mkv)


def _out_proj_kernel(a1_ref, a2_ref, w_ref, h_ref, o_ref):
    a = jnp.concatenate([a1_ref[...], a2_ref[...]], axis=1)
    o_ref[...] = h_ref[...] + _dot(a, w_ref[...])


def _out_proj(a1, a2, w_bf, h):
    m, d = h.shape
    tm, tn = min(OUT_TM, m), OUT_TN
    k1, k2 = a1.shape[1], a2.shape[1]
    assert w_bf.shape[0] == k1 + k2
    return pl.pallas_call(
        _out_proj_kernel,
        grid=(m // tm, d // tn),
        in_specs=[
            pl.BlockSpec((tm, k1), lambda i, j: (i, 0)),
            pl.BlockSpec((tm, k2), lambda i, j: (i, 0)),
            pl.BlockSpec((k1 + k2, tn), lambda i, j: (0, j)),
            pl.BlockSpec((tm, tn), lambda i, j: (i, j)),
        ],
        out_specs=pl.BlockSpec((tm, tn), lambda i, j: (i, j)),
        out_shape=jax.ShapeDtypeStruct((m, d), F32),
        compiler_params=_compiler_params(("parallel", "arbitrary")),
        name="out_proj",
    )(a1, a2, w_bf, h)


def _ffn_up_kernel(x_ref, g_ref, wg_ref, wu_ref, o_ref, xn_ref):
    @pl.when(pl.program_id(1) == 0)
    def _():
        xn_ref[...] = _rms_rows(x_ref[...], g_ref[...]).astype(BF16)

    xn = xn_ref[...]
    gate = _dot(xn, wg_ref[...])
    up = _dot(xn, wu_ref[...])
    o_ref[...] = (gate * jax.nn.sigmoid(gate) * up).astype(o_ref.dtype)


def _ffn_down_kernel(a_ref, w_ref, h_ref, fg_ref, o_ref, *, final_norm):
    y = h_ref[...] + _dot(a_ref[...], w_ref[...])
    if final_norm:
        y = _rms_rows(y, fg_ref[...])
    o_ref[...] = y


def _ffn(h, g, wg_bf, wu_bf, wd_bf, final_gain, *, final_norm):
    m, d = h.shape
    dff = wg_bf.shape[1]
    tm, tf = min(FFN_UP_TM, m), FFN_TF
    assert m % tm == 0 and dff % tf == 0
    act = pl.pallas_call(
        _ffn_up_kernel,
        grid=(m // tm, dff // tf),
        in_specs=[
            pl.BlockSpec((tm, d), lambda i, f: (i, 0)),
            pl.BlockSpec((1, d), lambda i, f: (0, 0)),
            pl.BlockSpec((d, tf), lambda i, f: (0, f)),
            pl.BlockSpec((d, tf), lambda i, f: (0, f)),
        ],
        out_specs=pl.BlockSpec((tm, tf), lambda i, f: (i, f)),
        out_shape=jax.ShapeDtypeStruct((m, dff), BF16),
        scratch_shapes=[pltpu.VMEM((tm, d), BF16)],
        compiler_params=_compiler_params(("parallel", "arbitrary")),
        name="ffn_up",
    )(h, g.reshape(1, d), wg_bf, wu_bf)

    tm = min(FFN_TM, m)
    kern = functools.partial(_ffn_down_kernel, final_norm=final_norm)
    return pl.pallas_call(
        kern,
        grid=(m // tm,),
        in_specs=[
            pl.BlockSpec((tm, dff), lambda i: (i, 0)),
            pl.BlockSpec((dff, d), lambda i: (0, 0), pipeline_mode=pl.Buffered(1)),
            pl.BlockSpec((tm, d), lambda i: (i, 0)),
            pl.BlockSpec((1, d), lambda i: (0, 0)),
        ],
        out_specs=pl.BlockSpec((tm, d), lambda i: (i, 0)),
        out_shape=jax.ShapeDtypeStruct((m, d), F32),
        compiler_params=_compiler_params(("parallel",)),
        name="ffn_down",
    )(act, wd_bf, h, final_gain.reshape(1, d))


def _rope_tables(seq):
    inv = 1.0 / (ROPE_THETA ** (jnp.arange(0, HEAD_DIM, 2, dtype=F32) / HEAD_DIM))
    ang = jnp.arange(seq, dtype=F32)[:, None] * inv[None, :]
    cos, sin = jnp.cos(ang), jnp.sin(ang)
    return jnp.concatenate([cos, cos], axis=1), jnp.concatenate([-sin, sin], axis=1)


def _tiles_transposed(v, batch, seq, heads):
    v = v.reshape(batch, seq // LANES, LANES, heads, HEAD_DIM)
    return jnp.transpose(v, (0, 3, 1, 4, 2))


A_UNITS = {"q": 0, "k_cmp": 12, "k_slc": 14, "k_win": 16, "v_cmp": 18, "v_slc": 20,
           "v_win": 22, "mem_q": 24, "gates": 28}
B_UNITS = {"q": 0, "k": 3, "mem_q": 4, "v": 5}
B_TN = 2 * DIL_HEADS * HEAD_DIM
A_TN = 6 * HEAD_DIM
A_NPAD = 30 * HEAD_DIM
A_ROPE_BLOCKS = 3


def _layer_a_weight(w_in):
    kv0 = A_Q

    def kv_cols(branch):
        return w_in[:, kv0 + branch * N_KV_A * HEAD_DIM:kv0 + (branch + 1) * N_KV_A * HEAD_DIM]

    gate0 = A_Q + A_KV
    mem0 = gate0 + A_GATE
    per_group = 3 * HPG_A
    gate_units = [jnp.pad(w_in[:, gate0 + g * per_group:gate0 + (g + 1) * per_group],
                          ((0, 0), (0, LANES - per_group))) for g in range(N_KV_A)]
    w = jnp.concatenate([w_in[:, :A_Q], kv_cols(0), kv_cols(2), kv_cols(4), kv_cols(1),
                         kv_cols(3), kv_cols(5), w_in[:, mem0:mem0 + MEM_Q]] + gate_units, axis=1)
    assert w.shape[1] == A_NPAD
    scale = np.ones((1, A_NPAD), np.float32)
    scale[0, :A_Q] = SCALE * LOG2E
    scale[0, A_UNITS["mem_q"] * HEAD_DIM:A_UNITS["mem_q"] * HEAD_DIM + MEM_Q] = SCALE
    return w.astype(BF16), jnp.asarray(scale)


def _layer_a(h, mem, cosf, sinf, p, *, batch, seq):
    w_in_bf, col_scale = _layer_a_weight(p["w_in"])
    z = _norm_proj(h, p["norm_attn"], w_in_bf, col_scale, cosf, sinf,
                   tn=A_TN, n_rope_blocks=A_ROPE_BLOCKS, seq=seq)

    def unit_cols(name, n_units):
        return z[:, A_UNITS[name] * HEAD_DIM:(A_UNITS[name] + n_units) * HEAD_DIM]

    per_row = CMP_LEN // 2

    def cmp_rows(name):
        x = unit_cols(name, N_KV_A).reshape(batch, seq // per_row, per_row, N_KV_A, HEAD_DIM)
        return jnp.transpose(x, (0, 3, 1, 2, 4)).reshape(batch * N_KV_A, seq // per_row,
                                                          per_row * HEAD_DIM)

    x2 = jnp.stack([cmp_rows("k_cmp"), cmp_rows("v_cmp")])

    def pe_rows(pe):
        return jnp.pad(pe.reshape(2, per_row * HEAD_DIM), ((0, 6), (0, 0)))

    pe2 = jnp.stack([pe_rows(p["cmp_pe_k"]), pe_rows(p["cmp_pe_v"])])
    w1 = jnp.stack([p["cmp_w1_k"], p["cmp_w1_v"]]).astype(BF16)
    w2 = jnp.stack([p["cmp_w2_k"], p["cmp_w2_v"]]).astype(BF16)
    cmp_out = _compress(x2, pe2, w1, w2)
    n_rows = seq // per_row
    kc = cmp_out[0].reshape(batch, N_KV_A, n_rows, HEAD_DIM)
    vct = jnp.transpose(cmp_out[1].reshape(batch, N_KV_A, n_rows, HEAD_DIM), (0, 1, 3, 2))

    vst = _tiles_transposed(unit_cols("v_slc", N_KV_A), batch, seq, N_KV_A)
    vwt = _tiles_transposed(unit_cols("v_win", N_KV_A), batch, seq, N_KV_A)
    gb = jnp.pad(p["gate_bias"].reshape(N_KV_A, 1, 3 * HPG_A),
                 ((0, 0), (0, 0), (0, LANES - 3 * HPG_A)))
    o_nsa = _nsa_attention(z, kc, vct, vst, vwt, gb, batch=batch, seq=seq, units=A_UNITS)

    mkv = _mem_kv(mem, p["norm_mem"], p["w_mem_kv"])
    o_mem = _memory_attention(z, mkv, batch=batch, seq=seq,
                              q_block=A_UNITS["mem_q"] * HEAD_DIM // MEM_Q)
    return _out_proj(o_nsa, o_mem, p["w_out"].astype(BF16), h)


def _mem_kv(mem, norm_mem, w_mem_kv):
    b, m, d = mem.shape
    ones = jnp.ones((1, w_mem_kv.shape[1]), F32)
    dummy = jnp.zeros((m, HEAD_DIM), F32)
    return _norm_proj(mem.reshape(b * m, d), norm_mem, w_mem_kv.astype(BF16), ones, dummy, dummy,
                      tn=MEM_Q, n_rope_blocks=0, seq=m, tm=m)


def kernel(x, mem, a_norm_attn, a_w_in, a_gate_bias, a_cmp_pe_k, a_cmp_w1_k, a_cmp_w2_k, a_cmp_pe_v, a_cmp_w1_v, a_cmp_w2_v, a_norm_mem, a_w_mem_kv, a_w_out, a_norm_ffn, a_w_gate, a_w_up, a_w_down, kv_norm, w_kv_shared, b_norm_attn, b_w_in, b_norm_mem, b_w_mem_kv, b_w_out, b_norm_ffn, b_w_gate, b_w_up, b_w_down, final_norm):
    batch, seq, d = x.shape
    n_a = a_w_in.shape[0]
    n_b = b_w_in.shape[0]
    cosf, sinf = _rope_tables(seq)
    h = x.reshape(batch * seq, d)
    unit_gain = jnp.ones((d,), F32)

    for l in range(n_a):
        p = {"norm_attn": a_norm_attn[l], "w_in": a_w_in[l], "gate_bias": a_gate_bias[l],
             "cmp_pe_k": a_cmp_pe_k[l], "cmp_w1_k": a_cmp_w1_k[l], "cmp_w2_k": a_cmp_w2_k[l],
             "cmp_pe_v": a_cmp_pe_v[l], "cmp_w1_v": a_cmp_w1_v[l], "cmp_w2_v": a_cmp_w2_v[l],
             "norm_mem": a_norm_mem[l], "w_mem_kv": a_w_mem_kv[l], "w_out": a_w_out[l]}
        h = _layer_a(h, mem, cosf, sinf, p, batch=batch, seq=seq)
        last = (l == n_a - 1) and n_b == 0
        h = _ffn(h, a_norm_ffn[l], a_w_gate[l].astype(BF16), a_w_up[l].astype(BF16),
                 a_w_down[l].astype(BF16), final_norm if last else unit_gain, final_norm=last)

    if n_b > 0:
        assert n_b == 1, "the shared K/V projection is fused into the single mixer-B layer"
        n_kv_half = w_kv_shared.shape[1] // 2
        for l in range(n_b):
            w_q = b_norm_attn[l][:, None] * b_w_in[l]
            w_kv = kv_norm[:, None] * w_kv_shared
            w_cat = jnp.concatenate([w_q[:, :B_Q], w_kv[:, :n_kv_half], w_q[:, B_Q:],
                                     w_kv[:, n_kv_half:]], axis=1).astype(BF16)
            b_scale = np.ones((1, w_cat.shape[1]), np.float32)
            b_scale[0, :B_Q] = SCALE * LOG2E
            b_scale[0, B_Q + n_kv_half:B_Q + n_kv_half + MEM_Q] = SCALE
            zb = _norm_proj(h, unit_gain, w_cat, jnp.asarray(b_scale), cosf, sinf,
                            tn=B_TN, n_rope_blocks=(B_Q + n_kv_half) // B_TN, seq=seq)
            o_dil = _dilated_attention(zb, batch=batch, seq=seq, units=B_UNITS)
            mkv = _mem_kv(mem, b_norm_mem[l], b_w_mem_kv[l])
            o_mem = _memory_attention(zb, mkv, batch=batch, seq=seq, q_block=B_UNITS["mem_q"])
            h = _out_proj(o_dil, o_mem, b_w_out[l].astype(BF16), h)
            last = l == n_b - 1
            h = _ffn(h, b_norm_ffn[l], b_w_gate[l].astype(BF16), b_w_up[l].astype(BF16),
                     b_w_down[l].astype(BF16), final_norm if last else unit_gain, final_norm=last)

    return h.reshape(batch, seq, d)
```

```python
import functools
import math

import numpy as np
import jax
import jax.numpy as jnp
from jax import lax
from jax.experimental import pallas as pl
from jax.experimental.pallas import tpu as pltpu

F32 = jnp.float32
BF16 = jnp.bfloat16

HEAD_DIM = 128
N_HEADS_A = 12
N_KV_A = 2
HPG_A = N_HEADS_A // N_KV_A
CMP_LEN = 32
CMP_STRIDE = 16
CMP_HIDDEN = 256
SLC_BLK = 64
SLC_SHIFT = SLC_BLK.bit_length() - 1
N_SEL = 16
WIN_A = 512
DIL_CONFIGS = ((128, 1), (512, 4), (2048, 16))
N_DIL_GROUPS = len(DIL_CONFIGS)
DIL_HEADS = 4
N_MEM_HEADS = 4
ROPE_THETA = 10000.0
EPS = 1e-6
NEG_INF = -1e30
TINY = 1e-30
SCALE = HEAD_DIM ** -0.5
LOG2E = math.log2(math.e)

A_Q = N_HEADS_A * HEAD_DIM
A_KV = 6 * N_KV_A * HEAD_DIM
A_GATE = 3 * N_HEADS_A
MEM_Q = N_MEM_HEADS * HEAD_DIM
B_Q = N_DIL_GROUPS * DIL_HEADS * HEAD_DIM

LANES = 128
SUBLANES = 8
VMEM_LIMIT_BYTES = 56 * 1024 * 1024

PROJ_TM = 1024
FFN_UP_TM = 1024
FFN_TM = 512
FFN_TF = 512
OUT_TM = 1024
OUT_TN = 1024
NSA_TQ = 128
NSA_KC = 512
DIL_TQ = 256
MEM_TQ = 1024

NT_DIMS = (((1,), (1,)), ((), ()))
TN_DIMS = (((0,), (0,)), ((), ()))


def _compiler_params(semantics):
    return pltpu.CompilerParams(dimension_semantics=semantics,
                                vmem_limit_bytes=VMEM_LIMIT_BYTES)


def _rms_rows(x, g):
    ms = jnp.mean(x * x, axis=-1, keepdims=True)
    return x * lax.rsqrt(ms + EPS) * g


def _dot(a, b):
    return jnp.dot(a, b, preferred_element_type=F32)


def _dot_nt(a, b):
    return lax.dot_general(a, b, NT_DIMS, preferred_element_type=F32)


def _dot_tn(a, b):
    return lax.dot_general(a, b, TN_DIMS, preferred_element_type=F32)


def _norm_proj_kernel(x_ref, g_ref, w_ref, cs_ref, cos_ref, sin_ref, o_ref, xn_ref, *,
                      n_rope_blocks, tn):
    j = pl.program_id(1)

    @pl.when(j == 0)
    def _():
        xn_ref[...] = _rms_rows(x_ref[...], g_ref[...]).astype(BF16)

    acc = _dot(xn_ref[...], w_ref[...]) * cs_ref[...]

    if n_rope_blocks > 0:
        roped = j < n_rope_blocks
        c = jnp.where(roped, cos_ref[...], 1.0)
        s = jnp.where(roped, sin_ref[...], 0.0)
        for h in range(tn // HEAD_DIM):
            y = acc[:, h * HEAD_DIM:(h + 1) * HEAD_DIM]
            rot = pltpu.roll(y, HEAD_DIM // 2, 1)
            o_ref[:, h * HEAD_DIM:(h + 1) * HEAD_DIM] = (y * c + rot * s).astype(o_ref.dtype)
    else:
        o_ref[...] = acc.astype(o_ref.dtype)


def _norm_proj(x, g, w_bf, col_scale, cosf, sinf, *, tn, n_rope_blocks, seq, tm=PROJ_TM):
    m, d = x.shape
    n = w_bf.shape[1]
    tm = min(tm, m)
    assert m % tm == 0 and n % tn == 0 and seq % tm == 0
    pos_blocks = seq // tm
    kern = functools.partial(_norm_proj_kernel, n_rope_blocks=n_rope_blocks, tn=tn)
    return pl.pallas_call(
        kern,
        grid=(m // tm, n // tn),
        in_specs=[
            pl.BlockSpec((tm, d), lambda i, j: (i, 0)),
            pl.BlockSpec((1, d), lambda i, j: (0, 0)),
            pl.BlockSpec((d, tn), lambda i, j: (0, j)),
            pl.BlockSpec((1, tn), lambda i, j: (0, j)),
            pl.BlockSpec((tm, HEAD_DIM), lambda i, j: (i % pos_blocks, 0)),
            pl.BlockSpec((tm, HEAD_DIM), lambda i, j: (i % pos_blocks, 0)),
        ],
        out_specs=pl.BlockSpec((tm, tn), lambda i, j: (i, j)),
        out_shape=jax.ShapeDtypeStruct((m, n), BF16),
        scratch_shapes=[pltpu.VMEM((tm, d), BF16)],
        compiler_params=_compiler_params(("parallel", "arbitrary")),
        name="norm_proj",
    )(x, g.reshape(1, d), w_bf, col_scale, cosf, sinf)


def _compress_kernel(x_ref, pe_ref, w1_ref, w2_ref, o_ref):
    half = (CMP_LEN // 2) * HEAD_DIM
    x = x_ref[...].astype(F32)
    xlo = (x + pe_ref[0:1, :]).astype(BF16)
    xhi = (x + pe_ref[1:2, :]).astype(BF16)
    ylo = _dot(xlo, w1_ref[:half, :])
    yhi = _dot(xhi, w1_ref[half:, :])
    n_rows = x.shape[0]
    hid = ylo + pltpu.roll(yhi, n_rows - 1, 0)
    act = (hid * jax.nn.sigmoid(hid)).astype(BF16)
    o_ref[...] = _dot(act, w2_ref[...]).astype(o_ref.dtype)


def _compress(x2, pe2, w1_bf, w2_bf):
    _, bg, nrow, wide = x2.shape
    return pl.pallas_call(
        _compress_kernel,
        grid=(2, bg),
        in_specs=[
            pl.BlockSpec((None, None, nrow, wide), lambda t, i: (t, i, 0, 0)),
            pl.BlockSpec((None, 8, wide), lambda t, i: (t, 0, 0)),
            pl.BlockSpec((None, 2 * wide, CMP_HIDDEN), lambda t, i: (t, 0, 0)),
            pl.BlockSpec((None, CMP_HIDDEN, HEAD_DIM), lambda t, i: (t, 0, 0)),
        ],
        out_specs=pl.BlockSpec((None, None, nrow, HEAD_DIM), lambda t, i: (t, i, 0, 0)),
        out_shape=jax.ShapeDtypeStruct((2, bg, nrow, HEAD_DIM), BF16),
        compiler_params=_compiler_params(("parallel", "arbitrary")),
        name="nsa_compress",
    )(x2, pe2, w1_bf, w2_bf)


def _block_ranks(score, jrow):
    n_blk = score.shape[0]
    groups = n_blk // SUBLANES
    blocks = [score[SUBLANES * r:SUBLANES * (r + 1), :] for r in range(groups)]
    rows = [jrow[SUBLANES * r:SUBLANES * (r + 1), :] for r in range(groups)]
    ranks = [jnp.zeros(blocks[0].shape, F32) for _ in range(groups)]
    for j in range(n_blk):
        rj = score[j:j + 1, :]
        for r in range(groups):
            if r > j // SUBLANES:
                ahead = rj >= blocks[r]
            elif r < j // SUBLANES:
                ahead = rj > blocks[r]
            else:
                ahead = (rj > blocks[r]) | ((rj == blocks[r]) & (rows[r] > j))
            ranks[r] = ranks[r] + jnp.where(ahead, 1.0, 0.0)
    return jnp.concatenate(ranks, axis=0)


def _nsa_kernel(q_ref, kc_ref, vc_ref, ks_ref, e_ref, vs_ref, kw_ref, vw_ref, gz_ref, gb_ref,
                o_ref, qa_ref, sa_ref, sb_ref, sw_ref, ow_ref, m_ref, l_ref, acc_ref, *, tq, seq):
    hq = HPG_A
    groups = range(N_KV_A)
    n_cmp_rows = kc_ref.shape[1]
    n_slc = seq // SLC_BLK
    qi = pl.program_id(1)
    s0 = qi * tq
    t_row = s0 + lax.broadcasted_iota(jnp.int32, (1, tq), 1)

    def lanes(h):
        return slice(h * tq, (h + 1) * tq)

    def gcols(g):
        return slice(g * HEAD_DIM, (g + 1) * HEAD_DIM)

    def tile_heads(x):
        return jnp.concatenate([x] * hq, axis=1)

    wlen = WIN_A + tq
    ws = pl.multiple_of(jnp.maximum(s0 - WIN_A, 0), LANES)
    c_end = lax.broadcasted_iota(jnp.int32, (n_cmp_rows, 1), 0) * CMP_STRIDE + (CMP_LEN - 1)
    cbias = tile_heads(jnp.where(c_end <= t_row, 0.0, NEG_INF))
    any_cmp = tile_heads(t_row >= CMP_LEN - 1)
    jrow = lax.broadcasted_iota(jnp.int32, (n_slc, 1), 0)
    ccol = lax.broadcasted_iota(jnp.int32, (1, n_cmp_rows), 1)
    lo = (SLC_BLK // CMP_STRIDE) * jrow - (CMP_LEN // CMP_STRIDE - 1)
    hi = (SLC_BLK // CMP_STRIDE) * jrow + (SLC_BLK // CMP_STRIDE - 1)
    mmap = jnp.where((ccol >= lo) & (ccol <= hi), 1.0, 0.0).astype(BF16)
    cur = t_row >> SLC_SHIFT
    forced = (jrow == 0) | (jrow == cur) | (jrow == cur - 1)
    assert n_slc <= LANES and tq == LANES

    o_cmp = []
    for g in groups:
        for h in range(hq):
            head = g * hq + h
            qa_ref[g, lanes(h), 0:HEAD_DIM] = q_ref[:, head * HEAD_DIM:(head + 1) * HEAD_DIM]
        q6 = qa_ref[g, :, 0:HEAD_DIM]

        sw_ref[g] = _dot_nt(kw_ref[pl.ds(ws, wlen), gcols(g)], q6)

        sc = _dot_nt(kc_ref[g], q6) + cbias
        ec = jnp.exp2(sc - jnp.max(sc, axis=0, keepdims=True))
        den = jnp.maximum(jnp.sum(ec, axis=0, keepdims=True), TINY)
        pc = ec * jnp.where(any_cmp, 1.0 / den, 0.0)
        o_cmp.append(_dot_tn(vc_ref[g], pc.astype(BF16)))
        psum = pc[:, lanes(0)]
        for h in range(1, hq):
            psum = psum + pc[:, lanes(h)]

        p1 = psum.astype(BF16)
        r1 = psum - p1.astype(F32)
        p2 = r1.astype(BF16)
        p3 = (r1 - p2.astype(F32)).astype(BF16)
        imp = _dot(mmap, p1) + _dot(mmap, p2) + _dot(mmap, p3)
        score = jnp.where(forced, 1e9, jnp.where(jrow <= cur, imp, -1e9))
        rank = _block_ranks(score, jrow)
        sel_bias = jnp.where((rank < min(N_SEL, n_slc)) & (jrow <= cur), 0.0, NEG_INF)

        bias_q = jnp.transpose(jnp.concatenate(
            [sel_bias, jnp.zeros((LANES - n_slc, tq), F32)], axis=0)).astype(BF16)
        for h in range(hq):
            qa_ref[g, lanes(h), HEAD_DIM:HEAD_DIM + LANES] = bias_q

    m_ref[...] = jnp.full(m_ref.shape, NEG_INF, F32)
    l_ref[...] = jnp.zeros(l_ref.shape, F32)
    acc_ref[...] = jnp.zeros(acc_ref.shape, F32)

    def scores(g, c, s_ref):
        k0 = pl.multiple_of(c * NSA_KC, NSA_KC)
        k_aug = jnp.concatenate([ks_ref[pl.ds(k0, NSA_KC), gcols(g)],
                                 e_ref[pl.ds(k0, NSA_KC), :]], axis=1)
        s_ref[g] = _dot_nt(k_aug, qa_ref[g])

    def softmax_pv(g, c, s_ref):
        s = s_ref[g]
        m_old = m_ref[g]
        m_new = jnp.maximum(m_old, jnp.max(s, axis=0, keepdims=True))
        alpha = jnp.exp2(m_old - m_new)
        p = jnp.exp2(s - m_new)
        l_ref[g] = alpha * l_ref[g] + jnp.sum(p, axis=0, keepdims=True)
        k0 = pl.multiple_of(c * NSA_KC, NSA_KC)
        pv = _dot_tn(vs_ref[pl.ds(k0, NSA_KC), gcols(g)], p.astype(BF16))
        acc_ref[g] = alpha * acc_ref[g] + pv
        m_ref[g] = m_new

    last = s0 // NSA_KC
    for g in groups:
        scores(g, 0, sa_ref)

    dist = t_row - (ws + lax.broadcasted_iota(jnp.int32, (wlen, 1), 0))
    wbias = tile_heads(jnp.where((dist >= 0) & (dist < WIN_A), 0.0, NEG_INF))
    for g in groups:
        sw = sw_ref[g] + wbias
        ew = jnp.exp2(sw - jnp.max(sw, axis=0, keepdims=True))
        ow_ref[g] = (_dot_tn(vw_ref[pl.ds(ws, wlen), gcols(g)], ew.astype(BF16))
                     * (1.0 / jnp.sum(ew, axis=0, keepdims=True)))

    def chunk_pair(i, carry):
        for g in groups:
            scores(g, 2 * i + 1, sb_ref)
            softmax_pv(g, 2 * i, sa_ref)
        for g in groups:
            scores(g, 2 * i + 2, sa_ref)
            softmax_pv(g, 2 * i + 1, sb_ref)
        return carry

    lax.fori_loop(0, last // 2, chunk_pair, 0)

    def last_chunk(g, s_ref):
        diag = pl.multiple_of(s0 - last * NSA_KC, LANES)
        krow = lax.broadcasted_iota(jnp.int32, (tq, 1), 0)
        lane = lax.broadcasted_iota(jnp.int32, (1, tq), 1)
        causal = tile_heads(jnp.where(krow <= lane, 0.0, NEG_INF))
        s_ref[g, pl.ds(diag, tq), :] = s_ref[g, pl.ds(diag, tq), :] + causal
        softmax_pv(g, last, s_ref)

    @pl.when(last % 2 == 0)
    def _():
        for g in groups:
            last_chunk(g, sa_ref)

    @pl.when(last % 2 == 1)
    def _():
        for g in groups:
            scores(g, last, sb_ref)
            softmax_pv(g, last - 1, sa_ref)
        for g in groups:
            last_chunk(g, sb_ref)

    gates_t = jnp.transpose(jax.nn.sigmoid(gz_ref[...].astype(F32) + gb_ref[...]))
    for g in groups:
        o_slc = acc_ref[g] * (1.0 / l_ref[g])
        for h in range(hq):
            head = g * hq + h
            o_h = (gates_t[3 * head:3 * head + 1, :] * o_cmp[g][:, lanes(h)]
                   + gates_t[3 * head + 1:3 * head + 2, :] * o_slc[:, lanes(h)]
                   + gates_t[3 * head + 2:3 * head + 3, :] * ow_ref[g, :, lanes(h)])
            o_ref[:, head * HEAD_DIM:(head + 1) * HEAD_DIM] = jnp.transpose(o_h).astype(o_ref.dtype)


def _nsa_attention(z, cmp_kv, gate_bias, *, batch, seq, units):
    tq = NSA_TQ
    nq = seq // tq
    n_cmp_rows = cmp_kv.shape[3]
    ng = N_KV_A
    hl = HPG_A * tq
    gw = ng * HEAD_DIM
    for name in ("k_slc", "v_slc", "k_win", "v_win"):
        assert units[name] % ng == 0
    kern = functools.partial(_nsa_kernel, tq=tq, seq=seq)

    def slab(name):
        return pl.BlockSpec((seq, gw), lambda b, i: (b, units[name] // ng))

    def cmp_spec(which):
        return pl.BlockSpec((None, None, ng, n_cmp_rows, HEAD_DIM),
                            lambda b, i: (which, b, 0, 0, 0))

    onehot = jnp.asarray(np.arange(seq)[:, None] // SLC_BLK == np.arange(LANES)[None, :], BF16)
    return pl.pallas_call(
        kern,
        grid=(batch, nq),
        in_specs=[
            pl.BlockSpec((tq, A_Q), lambda b, i: (b * nq + i, 0)),
            cmp_spec(0),
            cmp_spec(1),
            slab("k_slc"),
            pl.BlockSpec((seq, LANES), lambda b, i: (0, 0)),
            slab("v_slc"),
            slab("k_win"),
            slab("v_win"),
            pl.BlockSpec((tq, LANES), lambda b, i: (b * nq + i, units["tail"])),
            pl.BlockSpec((1, LANES), lambda b, i: (0, 0)),
        ],
        out_specs=pl.BlockSpec((tq, A_Q), lambda b, i: (b * nq + i, 0)),
        out_shape=jax.ShapeDtypeStruct((batch * seq, A_Q), BF16),
        scratch_shapes=[
            pltpu.VMEM((ng, hl, HEAD_DIM + LANES), BF16),
            pltpu.VMEM((ng, NSA_KC, hl), F32),
            pltpu.VMEM((ng, NSA_KC, hl), F32),
            pltpu.VMEM((ng, WIN_A + tq, hl), F32),
            pltpu.VMEM((ng, HEAD_DIM, hl), F32),
            pltpu.VMEM((ng, 1, hl), F32),
            pltpu.VMEM((ng, 1, hl), F32),
            pltpu.VMEM((ng, HEAD_DIM, hl), F32),
        ],
        compiler_params=_compiler_params(("parallel", "arbitrary")),
        name="nsa_attention",
    )(z, cmp_kv, cmp_kv, z, onehot, z, z, z, z, gate_bias)


def _band_attn_kernel(q_ref, k_ref, v_ref, o_ref, lse_ref, *, tu, lk, span):
    u0 = pl.program_id(2) * tu
    n_seq = k_ref.shape[0]
    ks = pl.multiple_of(jnp.clip(u0 - span, 0, n_seq - lk), LANES)
    dist = (u0 + lax.broadcasted_iota(jnp.int32, (tu, 1), 0)
            - (ks + lax.broadcasted_iota(jnp.int32, (1, lk), 1)))
    bias = jnp.where((dist >= 0) & (dist <= span), 0.0, NEG_INF)
    lane = lax.broadcasted_iota(jnp.int32, (1, LANES), 1)
    lse_tile = jnp.zeros((tu, LANES), F32)
    for h in range(DIL_HEADS):
        cols = slice(h * HEAD_DIM, (h + 1) * HEAD_DIM)
        s = _dot_nt(q_ref[:, cols], k_ref[pl.ds(ks, lk), cols]) + bias
        m = jnp.max(s, axis=-1, keepdims=True)
        p = jnp.exp2(s - m)
        l = jnp.sum(p, axis=-1, keepdims=True)
        o = _dot(p.astype(BF16), v_ref[pl.ds(ks, lk), cols]) * (1.0 / l)
        o_ref[:, cols] = o.astype(o_ref.dtype)
        lse_tile = jnp.where(lane == h, m + jnp.log2(l), lse_tile)
    lse_ref[...] = lse_tile


def _dil_merge_kernel(o0_ref, l0_ref, o1_ref, l1_ref, o2_ref, l2_ref, out_ref):
    groups = ((o0_ref, l0_ref), (o1_ref, l1_ref), (o2_ref, l2_ref))
    for h in range(DIL_HEADS):
        cols = slice(h * HEAD_DIM, (h + 1) * HEAD_DIM)
        shape = (out_ref.shape[0], HEAD_DIM)
        lses = [jnp.broadcast_to(l_ref[:, h:h + 1], shape) for _, l_ref in groups]
        top = jnp.maximum(jnp.maximum(lses[0], lses[1]), lses[2])
        ws = [jnp.exp2(lse - top) for lse in lses]
        num = ws[0] * o0_ref[:, cols].astype(F32)
        for w, (o_ref, _) in zip(ws[1:], groups[1:]):
            num = num + w * o_ref[:, cols].astype(F32)
        out_ref[:, cols] = (num * (1.0 / (ws[0] + ws[1] + ws[2]))).astype(out_ref.dtype)


def _dilated_attention(zb, *, batch, seq, units):
    n = zb.shape[1]
    width = DIL_HEADS * HEAD_DIM
    results = []
    for gi, (w, r) in enumerate(DIL_CONFIGS):
        n_seq = seq // r
        span = w // r
        tu = min(DIL_TQ, n_seq)
        lk = min(tu + span, n_seq)
        assert n_seq % tu == 0 and span % LANES == 0 and tu % LANES == 0
        kern = functools.partial(_band_attn_kernel, tu=tu, lk=lk, span=span)
        if r == 1:
            per_row = n // width
            q_unit, k_unit, v_unit = units["q"] + gi, units["k"], units["v"]
            zv = zb.reshape(batch, seq, n)
            operands = [zv, zv, zv]
        else:
            per_row, q_unit, k_unit, v_unit = 1, 0, 0, 0

            def class_view(unit):
                return zb[:, unit * width:(unit + 1) * width].reshape(batch, n_seq, r * width)

            operands = [class_view(units["q"] + gi), class_view(units["k"]),
                        class_view(units["v"])]

        def z_spec(rows, unit, whole):
            return pl.BlockSpec((None, rows, width),
                                lambda b, c, i: (b, 0 if whole else i, c * per_row + unit))

        o_g, lse_g = pl.pallas_call(
            kern,
            grid=(batch, r, n_seq // tu),
            in_specs=[z_spec(tu, q_unit, False), z_spec(n_seq, k_unit, True),
                      z_spec(n_seq, v_unit, True)],
            out_specs=[pl.BlockSpec((None, tu, width), lambda b, c, i: (b, i, c)),
                       pl.BlockSpec((None, tu, LANES), lambda b, c, i: (b, i, c))],
            out_shape=[jax.ShapeDtypeStruct((batch, n_seq, r * width), BF16),
                       jax.ShapeDtypeStruct((batch, n_seq, r * LANES), F32)],
            compiler_params=_compiler_params(("parallel", "parallel", "arbitrary")),
            name="dilated_attention",
        )(*operands)
        results += [o_g.reshape(batch * seq, width), lse_g.reshape(batch * seq, LANES)]

    m = batch * seq
    tm = min(OUT_TM, m)
    o_spec = pl.BlockSpec((tm, width), lambda i: (i, 0))
    l_spec = pl.BlockSpec((tm, LANES), lambda i: (i, 0))
    return pl.pallas_call(
        _dil_merge_kernel,
        grid=(m // tm,),
        in_specs=[o_spec, l_spec] * N_DIL_GROUPS,
        out_specs=o_spec,
        out_shape=jax.ShapeDtypeStruct((m, width), BF16),
        compiler_params=_compiler_params(("parallel",)),
        name="dilated_merge",
    )(*results)


def _mem_attn_kernel(q_ref, kv_ref, o_ref, *, q_offset):
    for h in range(N_MEM_HEADS):
        cols = slice(h * HEAD_DIM, (h + 1) * HEAD_DIM)
        k = kv_ref[:, cols]
        v = kv_ref[:, MEM_Q + h * HEAD_DIM:MEM_Q + (h + 1) * HEAD_DIM]
        q = q_ref[:, q_offset + h * HEAD_DIM:q_offset + (h + 1) * HEAD_DIM]
        s = _dot_nt(q, k)
        e = jnp.exp(s - jnp.max(s, axis=-1, keepdims=True))
        p = e / jnp.sum(e, axis=-1, keepdims=True)
        o_ref[:, cols] = _dot(p.astype(BF16), v).astype(o_ref.dtype)


def _memory_attention(z, mkv, *, batch, seq, q_col, block_width):
    tq = MEM_TQ
    nq = seq // tq
    n_mem = mkv.shape[0] // batch
    q_block, q_offset = divmod(q_col, block_width)
    assert q_offset + MEM_Q <= block_width
    return pl.pallas_call(
        functools.partial(_mem_attn_kernel, q_offset=q_offset),
        grid=(batch, nq),
        in_specs=[
            pl.BlockSpec((tq, block_width), lambda b, i: (b * nq + i, q_block)),
            pl.BlockSpec((n_mem, 2 * MEM_Q), lambda b, i: (b, 0)),
        ],
        out_specs=pl.BlockSpec((tq, MEM_Q), lambda b, i: (b * nq + i, 0)),
        out_shape=jax.ShapeDtypeStruct((batch * seq, MEM_Q), BF16),
        compiler_params=_compiler_params(("parallel", "arbitrary")),
        name="memory_attention",
    )(z, mkv)


def _out_proj_kernel(a1_ref, a2_ref, w_ref, h_ref, o_ref):
    a = jnp.concatenate([a1_ref[...], a2_ref[...]], axis=1)
    o_ref[...] = h_ref[...] + _dot(a, w_ref[...])


def _out_proj(a1, a2, w_bf, h):
    m, d = h.shape
    tm, tn = min(OUT_TM, m), OUT_TN
    k1, k2 = a1.shape[1], a2.shape[1]
    assert w_bf.shape[0] == k1 + k2
    return pl.pallas_call(
        _out_proj_kernel,
        grid=(m // tm, d // tn),
        in_specs=[
            pl.BlockSpec((tm, k1), lambda i, j: (i, 0)),
            pl.BlockSpec((tm, k2), lambda i, j: (i, 0)),
            pl.BlockSpec((k1 + k2, tn), lambda i, j: (0, j)),
            pl.BlockSpec((tm, tn), lambda i, j: (i, j)),
        ],
        out_specs=pl.BlockSpec((tm, tn), lambda i, j: (i, j)),
        out_shape=jax.ShapeDtypeStruct((m, d), F32),
        compiler_params=_compiler_params(("parallel", "arbitrary")),
        name="out_proj",
    )(a1, a2, w_bf, h)


def _ffn_up_kernel(x_ref, g_ref, wg_ref, wu_ref, o_ref, xn_ref):
    @pl.when(pl.program_id(1) == 0)
    def _():
        xn_ref[...] = _rms_rows(x_ref[...], g_ref[...]).astype(BF16)

    xn = xn_ref[...]
    gate = _dot(xn, wg_ref[...])
    up = _dot(xn, wu_ref[...])
    o_ref[...] = (gate * jax.nn.sigmoid(gate) * up).astype(o_ref.dtype)


def _ffn_down_kernel(a_ref, w_ref, h_ref, fg_ref, o_ref, *, final_norm):
    y = h_ref[...] + _dot(a_ref[...], w_ref[...])
    if final_norm:
        y = _rms_rows(y, fg_ref[...])
    o_ref[...] = y


def _ffn(h, g, wg_bf, wu_bf, wd_bf, final_gain, *, final_norm):
    m, d = h.shape
    dff = wg_bf.shape[1]
    tm, tf = min(FFN_UP_TM, m), FFN_TF
    assert m % tm == 0 and dff % tf == 0
    act = pl.pallas_call(
        _ffn_up_kernel,
        grid=(m // tm, dff // tf),
        in_specs=[
            pl.BlockSpec((tm, d), lambda i, f: (i, 0)),
            pl.BlockSpec((1, d), lambda i, f: (0, 0)),
            pl.BlockSpec((d, tf), lambda i, f: (0, f)),
            pl.BlockSpec((d, tf), lambda i, f: (0, f)),
        ],
        out_specs=pl.BlockSpec((tm, tf), lambda i, f: (i, f)),
        out_shape=jax.ShapeDtypeStruct((m, dff), BF16),
        scratch_shapes=[pltpu.VMEM((tm, d), BF16)],
        compiler_params=_compiler_params(("parallel", "arbitrary")),
        name="ffn_up",
    )(h, g.reshape(1, d), wg_bf, wu_bf)

    tm = min(FFN_TM, m)
    kern = functools.partial(_ffn_down_kernel, final_norm=final_norm)
    return pl.pallas_call(
        kern,
        grid=(m // tm,),
        in_specs=[
            pl.BlockSpec((tm, dff), lambda i: (i, 0)),
            pl.BlockSpec((dff, d), lambda i: (0, 0), pipeline_mode=pl.Buffered(1)),
            pl.BlockSpec((tm, d), lambda i: (i, 0)),
            pl.BlockSpec((1, d), lambda i: (0, 0)),
        ],
        out_specs=pl.BlockSpec((tm, d), lambda i: (i, 0)),
        out_shape=jax.ShapeDtypeStruct((m, d), F32),
        compiler_params=_compiler_params(("parallel",)),
        name="ffn_down",
    )(act, wd_bf, h, final_gain.reshape(1, d))


def _rope_tables(seq):
    inv = 1.0 / (ROPE_THETA ** (jnp.arange(0, HEAD_DIM, 2, dtype=F32) / HEAD_DIM))
    ang = jnp.arange(seq, dtype=F32)[:, None] * inv[None, :]
    cos, sin = jnp.cos(ang), jnp.sin(ang)
    return jnp.concatenate([cos, cos], axis=1), jnp.concatenate([-sin, sin], axis=1)


A_UNITS = {"q": 0, "k_cmp": 12, "k_slc": 14, "k_win": 16, "v_cmp": 18, "v_slc": 20,
           "v_win": 22, "tail": 24}
B_UNITS = {"q": 0, "k": 3, "mem_q": 4, "v": 5}
B_TN = 2 * DIL_HEADS * HEAD_DIM
A_TN = 6 * HEAD_DIM
A_NPAD = 30 * HEAD_DIM
A_ROPE_BLOCKS = 3


def _layer_a_weight(w_in):
    kv0 = A_Q

    def kv_cols(branch):
        return w_in[:, kv0 + branch * N_KV_A * HEAD_DIM:kv0 + (branch + 1) * N_KV_A * HEAD_DIM]

    tail = w_in[:, A_Q + A_KV:]
    w = jnp.concatenate([w_in[:, :A_Q], kv_cols(0), kv_cols(2), kv_cols(4), kv_cols(1),
                         kv_cols(3), kv_cols(5), tail], axis=1)
    w = jnp.pad(w, ((0, 0), (0, A_NPAD - w.shape[1])))
    mem0 = A_UNITS["tail"] * HEAD_DIM + A_GATE
    scale = np.ones((1, A_NPAD), np.float32)
    scale[0, :A_Q] = SCALE * LOG2E
    scale[0, mem0:mem0 + MEM_Q] = SCALE
    return w.astype(BF16), jnp.asarray(scale)


def _layer_a(h, mem, cosf, sinf, p, *, batch, seq):
    w_in_bf, col_scale = _layer_a_weight(p["w_in"])
    z = _norm_proj(h, p["norm_attn"], w_in_bf, col_scale, cosf, sinf,
                   tn=A_TN, n_rope_blocks=A_ROPE_BLOCKS, seq=seq)

    def unit_cols(name, n_units):
        return z[:, A_UNITS[name] * HEAD_DIM:(A_UNITS[name] + n_units) * HEAD_DIM]

    per_row = CMP_LEN // 2

    def cmp_rows(name):
        x = unit_cols(name, N_KV_A).reshape(batch, seq // per_row, per_row, N_KV_A, HEAD_DIM)
        return jnp.transpose(x, (0, 3, 1, 2, 4)).reshape(batch * N_KV_A, seq // per_row,
                                                          per_row * HEAD_DIM)

    x2 = jnp.stack([cmp_rows("k_cmp"), cmp_rows("v_cmp")])

    def pe_rows(pe):
        return jnp.pad(pe.reshape(2, per_row * HEAD_DIM), ((0, 6), (0, 0)))

    pe2 = jnp.stack([pe_rows(p["cmp_pe_k"]), pe_rows(p["cmp_pe_v"])])
    w1 = jnp.stack([p["cmp_w1_k"], p["cmp_w1_v"]]).astype(BF16)
    w2 = jnp.stack([p["cmp_w2_k"], p["cmp_w2_v"]]).astype(BF16)
    cmp_out = _compress(x2, pe2, w1, w2)
    cmp_kv = cmp_out.reshape(2, batch, N_KV_A, seq // per_row, HEAD_DIM)
    gb = jnp.pad(p["gate_bias"], (0, LANES - A_GATE)).reshape(1, LANES)
    o_nsa = _nsa_attention(z, cmp_kv, gb, batch=batch, seq=seq, units=A_UNITS)

    mkv = _mem_kv(mem, p["norm_mem"], p["w_mem_kv"])
    o_mem = _memory_attention(z, mkv, batch=batch, seq=seq,
                              q_col=A_UNITS["tail"] * HEAD_DIM + A_GATE, block_width=A_TN)
    return _out_proj(o_nsa, o_mem, p["w_out"].astype(BF16), h)


def _mem_kv(mem, norm_mem, w_mem_kv):
    b, m, d = mem.shape
    ones = jnp.ones((1, w_mem_kv.shape[1]), F32)
    dummy = jnp.zeros((m, HEAD_DIM), F32)
    return _norm_proj(mem.reshape(b * m, d), norm_mem, w_mem_kv.astype(BF16), ones, dummy, dummy,
                      tn=MEM_Q, n_rope_blocks=0, seq=m, tm=m)


def kernel(x, mem, a_norm_attn, a_w_in, a_gate_bias, a_cmp_pe_k, a_cmp_w1_k, a_cmp_w2_k, a_cmp_pe_v, a_cmp_w1_v, a_cmp_w2_v, a_norm_mem, a_w_mem_kv, a_w_out, a_norm_ffn, a_w_gate, a_w_up, a_w_down, kv_norm, w_kv_shared, b_norm_attn, b_w_in, b_norm_mem, b_w_mem_kv, b_w_out, b_norm_ffn, b_w_gate, b_w_up, b_w_down, final_norm):
    batch, seq, d = x.shape
    n_a = a_w_in.shape[0]
    n_b = b_w_in.shape[0]
    cosf, sinf = _rope_tables(seq)
    h = x.reshape(batch * seq, d)
    unit_gain = jnp.ones((d,), F32)

    for l in range(n_a):
        p = {"norm_attn": a_norm_attn[l], "w_in": a_w_in[l], "gate_bias": a_gate_bias[l],
             "cmp_pe_k": a_cmp_pe_k[l], "cmp_w1_k": a_cmp_w1_k[l], "cmp_w2_k": a_cmp_w2_k[l],
             "cmp_pe_v": a_cmp_pe_v[l], "cmp_w1_v": a_cmp_w1_v[l], "cmp_w2_v": a_cmp_w2_v[l],
             "norm_mem": a_norm_mem[l], "w_mem_kv": a_w_mem_kv[l], "w_out": a_w_out[l]}
        h = _layer_a(h, mem, cosf, sinf, p, batch=batch, seq=seq)
        last = (l == n_a - 1) and n_b == 0
        h = _ffn(h, a_norm_ffn[l], a_w_gate[l].astype(BF16), a_w_up[l].astype(BF16),
                 a_w_down[l].astype(BF16), final_norm if last else unit_gain, final_norm=last)

    if n_b > 0:
        assert n_b == 1, "the shared K/V projection is fused into the single mixer-B layer"
        n_kv_half = w_kv_shared.shape[1] // 2
        for l in range(n_b):
            w_q = b_norm_attn[l][:, None] * b_w_in[l]
            w_kv = kv_norm[:, None] * w_kv_shared
            w_cat = jnp.concatenate([w_q[:, :B_Q], w_kv[:, :n_kv_half], w_q[:, B_Q:],
                                     w_kv[:, n_kv_half:]], axis=1).astype(BF16)
            b_scale = np.ones((1, w_cat.shape[1]), np.float32)
            b_scale[0, :B_Q] = SCALE * LOG2E
            b_scale[0, B_Q + n_kv_half:B_Q + n_kv_half + MEM_Q] = SCALE
            zb = _norm_proj(h, unit_gain, w_cat, jnp.asarray(b_scale), cosf, sinf,
                            tn=B_TN, n_rope_blocks=(B_Q + n_kv_half) // B_TN, seq=seq)
            o_dil = _dilated_attention(zb, batch=batch, seq=seq, units=B_UNITS)
            mkv = _mem_kv(mem, b_norm_mem[l], b_w_mem_kv[l])
            o_mem = _memory_attention(zb, mkv, batch=batch, seq=seq,
                                      q_col=B_UNITS["mem_q"] * MEM_Q, block_width=MEM_Q)
            h = _out_proj(o_dil, o_mem, b_w_out[l].astype(BF16), h)
            last = l == n_b - 1
            h = _ffn(h, b_norm_ffn[l], b_w_gate[l].astype(BF16), b_w_up[l].astype(BF16),
                     b_w_down[l].astype(BF16), final_norm if last else unit_gain, final_norm=last)

    return h.reshape(batch, seq, d)
```

```python
import functools
import math

import numpy as np
import jax
import jax.numpy as jnp
from jax import lax
from jax.experimental import pallas as pl
from jax.experimental.pallas import tpu as pltpu

F32 = jnp.float32
BF16 = jnp.bfloat16

HEAD_DIM = 128
N_HEADS_A = 12
N_KV_A = 2
HPG_A = N_HEADS_A // N_KV_A
CMP_LEN = 32
CMP_STRIDE = 16
CMP_HIDDEN = 256
SLC_BLK = 64
SLC_SHIFT = SLC_BLK.bit_length() - 1
N_SEL = 16
WIN_A = 512
DIL_CONFIGS = ((128, 1), (512, 4), (2048, 16))
N_DIL_GROUPS = len(DIL_CONFIGS)
DIL_HEADS = 4
N_MEM_HEADS = 4
ROPE_THETA = 10000.0
EPS = 1e-6
NEG_INF = -1e30
TINY = 1e-30
SCALE = HEAD_DIM ** -0.5
LOG2E = math.log2(math.e)

A_Q = N_HEADS_A * HEAD_DIM
A_KV = 6 * N_KV_A * HEAD_DIM
A_GATE = 3 * N_HEADS_A
MEM_Q = N_MEM_HEADS * HEAD_DIM
B_Q = N_DIL_GROUPS * DIL_HEADS * HEAD_DIM

LANES = 128
SUBLANES = 8
VMEM_LIMIT_BYTES = 56 * 1024 * 1024

PROJ_TM = 1024
FFN_UP_TM = 1024
FFN_TM = 512
FFN_TF = 512
OUT_TM = 1024
OUT_TN = 1024
NSA_TQ = 128
NSA_KC = 512
DIL_TQ = 256
MEM_TQ = 1024

NT_DIMS = (((1,), (1,)), ((), ()))
TN_DIMS = (((0,), (0,)), ((), ()))


def _compiler_params(semantics):
    return pltpu.CompilerParams(dimension_semantics=semantics,
                                vmem_limit_bytes=VMEM_LIMIT_BYTES)


def _rms_rows(x, g):
    ms = jnp.mean(x * x, axis=-1, keepdims=True)
    return x * lax.rsqrt(ms + EPS) * g


def _dot(a, b):
    return jnp.dot(a, b, preferred_element_type=F32)


def _dot_nt(a, b):
    return lax.dot_general(a, b, NT_DIMS, preferred_element_type=F32)


def _dot_tn(a, b):
    return lax.dot_general(a, b, TN_DIMS, preferred_element_type=F32)


def _norm_proj_kernel(x_ref, g_ref, w_ref, cs_ref, cos_ref, sin_ref, o_ref, xn_ref, *,
                      n_rope_blocks, tn):
    j = pl.program_id(1)

    @pl.when(j == 0)
    def _():
        xn_ref[...] = _rms_rows(x_ref[...], g_ref[...]).astype(BF16)

    acc = _dot(xn_ref[...], w_ref[...]) * cs_ref[...]

    if n_rope_blocks > 0:
        roped = j < n_rope_blocks
        c = jnp.where(roped, cos_ref[...], 1.0)
        s = jnp.where(roped, sin_ref[...], 0.0)
        for h in range(tn // HEAD_DIM):
            y = acc[:, h * HEAD_DIM:(h + 1) * HEAD_DIM]
            rot = pltpu.roll(y, HEAD_DIM // 2, 1)
            o_ref[:, h * HEAD_DIM:(h + 1) * HEAD_DIM] = (y * c + rot * s).astype(o_ref.dtype)
    else:
        o_ref[...] = acc.astype(o_ref.dtype)


def _norm_proj(x, g, w_bf, col_scale, cosf, sinf, *, tn, n_rope_blocks, seq, tm=PROJ_TM):
    m, d = x.shape
    n = w_bf.shape[1]
    tm = min(tm, m)
    assert m % tm == 0 and n % tn == 0 and seq % tm == 0
    pos_blocks = seq // tm
    kern = functools.partial(_norm_proj_kernel, n_rope_blocks=n_rope_blocks, tn=tn)
    return pl.pallas_call(
        kern,
        grid=(m // tm, n // tn),
        in_specs=[
            pl.BlockSpec((tm, d), lambda i, j: (i, 0)),
            pl.BlockSpec((1, d), lambda i, j: (0, 0)),
            pl.BlockSpec((d, tn), lambda i, j: (0, j)),
            pl.BlockSpec((1, tn), lambda i, j: (0, j)),
            pl.BlockSpec((tm, HEAD_DIM), lambda i, j: (i % pos_blocks, 0)),
            pl.BlockSpec((tm, HEAD_DIM), lambda i, j: (i % pos_blocks, 0)),
        ],
        out_specs=pl.BlockSpec((tm, tn), lambda i, j: (i, j)),
        out_shape=jax.ShapeDtypeStruct((m, n), BF16),
        scratch_shapes=[pltpu.VMEM((tm, d), BF16)],
        compiler_params=_compiler_params(("parallel", "arbitrary")),
        name="norm_proj",
    )(x, g.reshape(1, d), w_bf, col_scale, cosf, sinf)


def _compress_kernel(x_ref, pe_ref, w1_ref, w2_ref, o_ref):
    half = (CMP_LEN // 2) * HEAD_DIM
    x = x_ref[...].astype(F32)
    xlo = (x + pe_ref[0:1, :]).astype(BF16)
    xhi = (x + pe_ref[1:2, :]).astype(BF16)
    ylo = _dot(xlo, w1_ref[:half, :])
    yhi = _dot(xhi, w1_ref[half:, :])
    n_rows = x.shape[0]
    hid = ylo + pltpu.roll(yhi, n_rows - 1, 0)
    act = (hid * jax.nn.sigmoid(hid)).astype(BF16)
    o_ref[...] = _dot(act, w2_ref[...]).astype(o_ref.dtype)


def _compress(x2, pe2, w1_bf, w2_bf):
    _, bg, nrow, wide = x2.shape
    return pl.pallas_call(
        _compress_kernel,
        grid=(2, bg),
        in_specs=[
            pl.BlockSpec((None, None, nrow, wide), lambda t, i: (t, i, 0, 0)),
            pl.BlockSpec((None, 8, wide), lambda t, i: (t, 0, 0)),
            pl.BlockSpec((None, 2 * wide, CMP_HIDDEN), lambda t, i: (t, 0, 0)),
            pl.BlockSpec((None, CMP_HIDDEN, HEAD_DIM), lambda t, i: (t, 0, 0)),
        ],
        out_specs=pl.BlockSpec((None, None, nrow, HEAD_DIM), lambda t, i: (t, i, 0, 0)),
        out_shape=jax.ShapeDtypeStruct((2, bg, nrow, HEAD_DIM), BF16),
        compiler_params=_compiler_params(("parallel", "arbitrary")),
        name="nsa_compress",
    )(x2, pe2, w1_bf, w2_bf)


def _block_ranks(score, jrow):
    n_blk = score.shape[0]
    groups = n_blk // SUBLANES
    blocks = [score[SUBLANES * r:SUBLANES * (r + 1), :] for r in range(groups)]
    rows = [jrow[SUBLANES * r:SUBLANES * (r + 1), :] for r in range(groups)]
    ranks = [jnp.zeros(blocks[0].shape, F32) for _ in range(groups)]
    for j in range(n_blk):
        rj = score[j:j + 1, :]
        for r in range(groups):
            if r > j // SUBLANES:
                ahead = rj >= blocks[r]
            elif r < j // SUBLANES:
                ahead = rj > blocks[r]
            else:
                ahead = (rj > blocks[r]) | ((rj == blocks[r]) & (rows[r] > j))
            ranks[r] = ranks[r] + jnp.where(ahead, 1.0, 0.0)
    return jnp.concatenate(ranks, axis=0)


def _nsa_kernel(q_ref, kc_ref, vc_ref, ks_ref, e_ref, vs_ref, kw_ref, vw_ref, gz_ref, gb_ref,
                o_ref, qa_ref, sa_ref, sb_ref, sw_ref, ow_ref, m_ref, l_ref, acc_ref, *, tq, seq):
    hq = HPG_A
    groups = range(N_KV_A)
    n_cmp_rows = kc_ref.shape[1]
    n_slc = seq // SLC_BLK
    qi = pl.program_id(1)
    s0 = qi * tq
    t_row = s0 + lax.broadcasted_iota(jnp.int32, (1, tq), 1)

    def lanes(h):
        return slice(h * tq, (h + 1) * tq)

    def gcols(g):
        return slice(g * HEAD_DIM, (g + 1) * HEAD_DIM)

    def tile_heads(x):
        return jnp.concatenate([x] * hq, axis=1)

    wlen = WIN_A + tq
    ws = pl.multiple_of(jnp.maximum(s0 - WIN_A, 0), LANES)
    c_end = lax.broadcasted_iota(jnp.int32, (n_cmp_rows, 1), 0) * CMP_STRIDE + (CMP_LEN - 1)
    cbias = tile_heads(jnp.where(c_end <= t_row, 0.0, NEG_INF))
    any_cmp = tile_heads(t_row >= CMP_LEN - 1)
    jrow = lax.broadcasted_iota(jnp.int32, (n_slc, 1), 0)
    ccol = lax.broadcasted_iota(jnp.int32, (1, n_cmp_rows), 1)
    lo = (SLC_BLK // CMP_STRIDE) * jrow - (CMP_LEN // CMP_STRIDE - 1)
    hi = (SLC_BLK // CMP_STRIDE) * jrow + (SLC_BLK // CMP_STRIDE - 1)
    mmap = jnp.where((ccol >= lo) & (ccol <= hi), 1.0, 0.0).astype(BF16)
    cur = t_row >> SLC_SHIFT
    forced = (jrow == 0) | (jrow == cur) | (jrow == cur - 1)
    assert n_slc <= LANES and tq == LANES

    q6, sc = [], []
    for g in groups:
        for h in range(hq):
            head = g * hq + h
            qa_ref[g, lanes(h), 0:HEAD_DIM] = q_ref[:, head * HEAD_DIM:(head + 1) * HEAD_DIM]
        q6.append(qa_ref[g, :, 0:HEAD_DIM])
        sc.append(_dot_nt(kc_ref[g], q6[g]) + cbias)
    for g in groups:
        sw_ref[g] = _dot_nt(kw_ref[pl.ds(ws, wlen), gcols(g)], q6[g])

    o_cmp, score = [], []
    for g in groups:
        ec = jnp.exp2(sc[g] - jnp.max(sc[g], axis=0, keepdims=True))
        den = jnp.maximum(jnp.sum(ec, axis=0, keepdims=True), TINY)
        pc = ec * jnp.where(any_cmp, 1.0 / den, 0.0)
        o_cmp.append(_dot_tn(vc_ref[g], pc.astype(BF16)))
        psum = pc[:, lanes(0)]
        for h in range(1, hq):
            psum = psum + pc[:, lanes(h)]
        p1 = psum.astype(BF16)
        r1 = psum - p1.astype(F32)
        p2 = r1.astype(BF16)
        p3 = (r1 - p2.astype(F32)).astype(BF16)
        imp = _dot(mmap, p1) + _dot(mmap, p2) + _dot(mmap, p3)
        score.append(jnp.where(forced, 1e9, jnp.where(jrow <= cur, imp, -1e9)))

    for g in groups:
        rank = _block_ranks(score[g], jrow)
        sel_bias = jnp.where((rank < min(N_SEL, n_slc)) & (jrow <= cur), 0.0, NEG_INF)
        bias_q = jnp.transpose(jnp.concatenate(
            [sel_bias, jnp.zeros((LANES - n_slc, tq), F32)], axis=0)).astype(BF16)
        for h in range(hq):
            qa_ref[g, lanes(h), HEAD_DIM:HEAD_DIM + LANES] = bias_q

    m_ref[...] = jnp.full(m_ref.shape, NEG_INF, F32)
    l_ref[...] = jnp.zeros(l_ref.shape, F32)
    acc_ref[...] = jnp.zeros(acc_ref.shape, F32)

    def scores(g, c, s_ref):
        k0 = pl.multiple_of(c * NSA_KC, NSA_KC)
        k_aug = jnp.concatenate([ks_ref[pl.ds(k0, NSA_KC), gcols(g)],
                                 e_ref[pl.ds(k0, NSA_KC), :]], axis=1)
        s_ref[g] = _dot_nt(k_aug, qa_ref[g])

    def softmax_pv(g, c, s_ref):
        s = s_ref[g]
        m_old = m_ref[g]
        m_new = jnp.maximum(m_old, jnp.max(s, axis=0, keepdims=True))
        alpha = jnp.exp2(m_old - m_new)
        p = jnp.exp2(s - m_new)
        l_ref[g] = alpha * l_ref[g] + jnp.sum(p, axis=0, keepdims=True)
        k0 = pl.multiple_of(c * NSA_KC, NSA_KC)
        pv = _dot_tn(vs_ref[pl.ds(k0, NSA_KC), gcols(g)], p.astype(BF16))
        acc_ref[g] = alpha * acc_ref[g] + pv
        m_ref[g] = m_new

    last = s0 // NSA_KC
    for g in groups:
        scores(g, 0, sa_ref)

    dist = t_row - (ws + lax.broadcasted_iota(jnp.int32, (wlen, 1), 0))
    wbias = tile_heads(jnp.where((dist >= 0) & (dist < WIN_A), 0.0, NEG_INF))
    for g in groups:
        sw = sw_ref[g] + wbias
        ew = jnp.exp2(sw - jnp.max(sw, axis=0, keepdims=True))
        ow_ref[g] = (_dot_tn(vw_ref[pl.ds(ws, wlen), gcols(g)], ew.astype(BF16))
                     * (1.0 / jnp.sum(ew, axis=0, keepdims=True)))

    def chunk_pair(i, carry):
        for g in groups:
            scores(g, 2 * i + 1, sb_ref)
            softmax_pv(g, 2 * i, sa_ref)
        for g in groups:
            scores(g, 2 * i + 2, sa_ref)
            softmax_pv(g, 2 * i + 1, sb_ref)
        return carry

    lax.fori_loop(0, last // 2, chunk_pair, 0)

    def last_chunk(g, s_ref):
        diag = pl.multiple_of(s0 - last * NSA_KC, LANES)
        krow = lax.broadcasted_iota(jnp.int32, (tq, 1), 0)
        lane = lax.broadcasted_iota(jnp.int32, (1, tq), 1)
        causal = tile_heads(jnp.where(krow <= lane, 0.0, NEG_INF))
        s_ref[g, pl.ds(diag, tq), :] = s_ref[g, pl.ds(diag, tq), :] + causal
        softmax_pv(g, last, s_ref)

    @pl.when(last % 2 == 0)
    def _():
        for g in groups:
            last_chunk(g, sa_ref)

    @pl.when(last % 2 == 1)
    def _():
        for g in groups:
            scores(g, last, sb_ref)
            softmax_pv(g, last - 1, sa_ref)
        for g in groups:
            last_chunk(g, sb_ref)

    gates_t = jnp.transpose(jax.nn.sigmoid(gz_ref[...].astype(F32) + gb_ref[...]))
    for g in groups:
        o_slc = acc_ref[g] * (1.0 / l_ref[g])
        for h in range(hq):
            head = g * hq + h
            o_h = (gates_t[3 * head:3 * head + 1, :] * o_cmp[g][:, lanes(h)]
                   + gates_t[3 * head + 1:3 * head + 2, :] * o_slc[:, lanes(h)]
                   + gates_t[3 * head + 2:3 * head + 3, :] * ow_ref[g, :, lanes(h)])
            o_ref[:, head * HEAD_DIM:(head + 1) * HEAD_DIM] = jnp.transpose(o_h).astype(o_ref.dtype)


def _nsa_attention(z, cmp_kv, gate_bias, *, batch, seq, units):
    tq = NSA_TQ
    nq = seq // tq
    n_cmp_rows = cmp_kv.shape[3]
    ng = N_KV_A
    hl = HPG_A * tq
    gw = ng * HEAD_DIM
    for name in ("k_slc", "v_slc", "k_win", "v_win"):
        assert units[name] % ng == 0
    kern = functools.partial(_nsa_kernel, tq=tq, seq=seq)

    def slab(name):
        return pl.BlockSpec((seq, gw), lambda b, i: (b, units[name] // ng))

    def cmp_spec(which):
        return pl.BlockSpec((None, None, ng, n_cmp_rows, HEAD_DIM),
                            lambda b, i: (which, b, 0, 0, 0))

    onehot = jnp.asarray(np.arange(seq)[:, None] // SLC_BLK == np.arange(LANES)[None, :], BF16)
    return pl.pallas_call(
        kern,
        grid=(batch, nq),
        in_specs=[
            pl.BlockSpec((tq, A_Q), lambda b, i: (b * nq + i, 0)),
            cmp_spec(0),
            cmp_spec(1),
            slab("k_slc"),
            pl.BlockSpec((seq, LANES), lambda b, i: (0, 0)),
            slab("v_slc"),
            slab("k_win"),
            slab("v_win"),
            pl.BlockSpec((tq, LANES), lambda b, i: (b * nq + i, units["gates"])),
            pl.BlockSpec((1, LANES), lambda b, i: (0, 0)),
        ],
        out_specs=pl.BlockSpec((tq, A_Q), lambda b, i: (b * nq + i, 0)),
        out_shape=jax.ShapeDtypeStruct((batch * seq, A_Q), BF16),
        scratch_shapes=[
            pltpu.VMEM((ng, hl, HEAD_DIM + LANES), BF16),
            pltpu.VMEM((ng, NSA_KC, hl), F32),
            pltpu.VMEM((ng, NSA_KC, hl), F32),
            pltpu.VMEM((ng, WIN_A + tq, hl), F32),
            pltpu.VMEM((ng, HEAD_DIM, hl), F32),
            pltpu.VMEM((ng, 1, hl), F32),
            pltpu.VMEM((ng, 1, hl), F32),
            pltpu.VMEM((ng, HEAD_DIM, hl), F32),
        ],
        compiler_params=_compiler_params(("parallel", "arbitrary")),
        name="nsa_attention",
    )(z, cmp_kv, cmp_kv, z, onehot, z, z, z, z, gate_bias)


def _band_attn_kernel(q_ref, k_ref, v_ref, o_ref, lse_ref, *, tu, lk, span):
    u0 = pl.program_id(2) * tu
    n_seq = k_ref.shape[0]
    ks = pl.multiple_of(jnp.clip(u0 - span, 0, n_seq - lk), LANES)
    dist = (u0 + lax.broadcasted_iota(jnp.int32, (tu, 1), 0)
            - (ks + lax.broadcasted_iota(jnp.int32, (1, lk), 1)))
    bias = jnp.where((dist >= 0) & (dist <= span), 0.0, NEG_INF)
    lane = lax.broadcasted_iota(jnp.int32, (1, LANES), 1)
    lse_tile = jnp.zeros((tu, LANES), F32)

    def head_cols(h):
        return slice(h * HEAD_DIM, (h + 1) * HEAD_DIM)

    scores = [_dot_nt(q_ref[:, head_cols(h)], k_ref[pl.ds(ks, lk), head_cols(h)]) + bias
              for h in range(DIL_HEADS)]
    probs, inv_l = [], []
    for h, s in enumerate(scores):
        m = jnp.max(s, axis=-1, keepdims=True)
        p = jnp.exp2(s - m)
        l = jnp.sum(p, axis=-1, keepdims=True)
        probs.append(p.astype(BF16))
        inv_l.append(1.0 / l)
        lse_tile = jnp.where(lane == h, m + jnp.log2(l), lse_tile)
    for h in range(DIL_HEADS):
        o = _dot(probs[h], v_ref[pl.ds(ks, lk), head_cols(h)]) * inv_l[h]
        o_ref[:, head_cols(h)] = o.astype(o_ref.dtype)
    lse_ref[...] = lse_tile


def _dil_merge_kernel(o0_ref, l0_ref, o1_ref, l1_ref, o2_ref, l2_ref, out_ref):
    groups = ((o0_ref, l0_ref), (o1_ref, l1_ref), (o2_ref, l2_ref))
    for h in range(DIL_HEADS):
        cols = slice(h * HEAD_DIM, (h + 1) * HEAD_DIM)
        shape = (out_ref.shape[0], HEAD_DIM)
        lses = [jnp.broadcast_to(l_ref[:, h:h + 1], shape) for _, l_ref in groups]
        top = jnp.maximum(jnp.maximum(lses[0], lses[1]), lses[2])
        ws = [jnp.exp2(lse - top) for lse in lses]
        num = ws[0] * o0_ref[:, cols].astype(F32)
        for w, (o_ref, _) in zip(ws[1:], groups[1:]):
            num = num + w * o_ref[:, cols].astype(F32)
        out_ref[:, cols] = (num * (1.0 / (ws[0] + ws[1] + ws[2]))).astype(out_ref.dtype)


def _dilated_attention(zb, *, batch, seq, units):
    n = zb.shape[1]
    width = DIL_HEADS * HEAD_DIM
    results = []
    for gi, (w, r) in enumerate(DIL_CONFIGS):
        n_seq = seq // r
        span = w // r
        tu = min(DIL_TQ, n_seq)
        lk = min(tu + span, n_seq)
        assert n_seq % tu == 0 and span % LANES == 0 and tu % LANES == 0
        kern = functools.partial(_band_attn_kernel, tu=tu, lk=lk, span=span)
        if r == 1:
            per_row = n // width
            q_unit, k_unit, v_unit = units["q"] + gi, units["k"], units["v"]
            zv = zb.reshape(batch, seq, n)
            operands = [zv, zv, zv]
        else:
            per_row, q_unit, k_unit, v_unit = 1, 0, 0, 0

            def class_view(unit):
                return zb[:, unit * width:(unit + 1) * width].reshape(batch, n_seq, r * width)

            operands = [class_view(units["q"] + gi), class_view(units["k"]),
                        class_view(units["v"])]

        def z_spec(rows, unit, whole):
            return pl.BlockSpec((None, rows, width),
                                lambda b, c, i: (b, 0 if whole else i, c * per_row + unit))

        o_g, lse_g = pl.pallas_call(
            kern,
            grid=(batch, r, n_seq // tu),
            in_specs=[z_spec(tu, q_unit, False), z_spec(n_seq, k_unit, True),
                      z_spec(n_seq, v_unit, True)],
            out_specs=[pl.BlockSpec((None, tu, width), lambda b, c, i: (b, i, c)),
                       pl.BlockSpec((None, tu, LANES), lambda b, c, i: (b, i, c))],
            out_shape=[jax.ShapeDtypeStruct((batch, n_seq, r * width), BF16),
                       jax.ShapeDtypeStruct((batch, n_seq, r * LANES), F32)],
            compiler_params=_compiler_params(("parallel", "parallel", "arbitrary")),
            name="dilated_attention",
        )(*operands)
        results += [o_g.reshape(batch * seq, width), lse_g.reshape(batch * seq, LANES)]

    m = batch * seq
    tm = min(OUT_TM, m)
    o_spec = pl.BlockSpec((tm, width), lambda i: (i, 0))
    l_spec = pl.BlockSpec((tm, LANES), lambda i: (i, 0))
    return pl.pallas_call(
        _dil_merge_kernel,
        grid=(m // tm,),
        in_specs=[o_spec, l_spec] * N_DIL_GROUPS,
        out_specs=o_spec,
        out_shape=jax.ShapeDtypeStruct((m, width), BF16),
        compiler_params=_compiler_params(("parallel",)),
        name="dilated_merge",
    )(*results)


def _mem_attn_kernel(q_ref, kv_ref, o_ref, *, q_offset):
    def cols(h, base=0):
        return slice(base + h * HEAD_DIM, base + (h + 1) * HEAD_DIM)

    heads = range(N_MEM_HEADS)
    scores = [_dot_nt(q_ref[:, cols(h, q_offset)], kv_ref[:, cols(h)]) for h in heads]
    probs = []
    for s in scores:
        e = jnp.exp(s - jnp.max(s, axis=-1, keepdims=True))
        probs.append((e / jnp.sum(e, axis=-1, keepdims=True)).astype(BF16))
    for h in heads:
        o_ref[:, cols(h)] = _dot(probs[h], kv_ref[:, cols(h, MEM_Q)]).astype(o_ref.dtype)


def _memory_attention(z, mkv, *, batch, seq, q_col, block_width):
    tq = MEM_TQ
    nq = seq // tq
    n_mem = mkv.shape[0] // batch
    q_block, q_offset = divmod(q_col, block_width)
    assert q_offset + MEM_Q <= block_width
    return pl.pallas_call(
        functools.partial(_mem_attn_kernel, q_offset=q_offset),
        grid=(batch, nq),
        in_specs=[
            pl.BlockSpec((tq, block_width), lambda b, i: (b * nq + i, q_block)),
            pl.BlockSpec((n_mem, 2 * MEM_Q), lambda b, i: (b, 0)),
        ],
        out_specs=pl.BlockSpec((tq, MEM_Q), lambda b, i: (b * nq + i, 0)),
        out_shape=jax.ShapeDtypeStruct((batch * seq, MEM_Q), BF16),
        compiler_params=_compiler_params(("parallel", "arbitrary")),
        name="memory_attention",
    )(z, mkv)


def _out_proj_kernel(a1_ref, a2_ref, w_ref, h_ref, o_ref):
    a = jnp.concatenate([a1_ref[...], a2_ref[...]], axis=1)
    o_ref[...] = h_ref[...] + _dot(a, w_ref[...])


def _out_proj(a1, a2, w_bf, h):
    m, d = h.shape
    tm, tn = min(OUT_TM, m), OUT_TN
    k1, k2 = a1.shape[1], a2.shape[1]
    assert w_bf.shape[0] == k1 + k2
    return pl.pallas_call(
        _out_proj_kernel,
        grid=(m // tm, d // tn),
        in_specs=[
            pl.BlockSpec((tm, k1), lambda i, j: (i, 0)),
            pl.BlockSpec((tm, k2), lambda i, j: (i, 0)),
            pl.BlockSpec((k1 + k2, tn), lambda i, j: (0, j)),
            pl.BlockSpec((tm, tn), lambda i, j: (i, j)),
        ],
        out_specs=pl.BlockSpec((tm, tn), lambda i, j: (i, j)),
        out_shape=jax.ShapeDtypeStruct((m, d), F32),
        compiler_params=_compiler_params(("parallel", "arbitrary")),
        name="out_proj",
    )(a1, a2, w_bf, h)


def _ffn_up_kernel(x_ref, g_ref, wg_ref, wu_ref, o_ref, xn_ref):
    @pl.when(pl.program_id(1) == 0)
    def _():
        xn_ref[...] = _rms_rows(x_ref[...], g_ref[...]).astype(BF16)

    xn = xn_ref[...]
    gate = _dot(xn, wg_ref[...])
    up = _dot(xn, wu_ref[...])
    o_ref[...] = (gate * jax.nn.sigmoid(gate) * up).astype(o_ref.dtype)


def _ffn_down_kernel(a_ref, w_ref, h_ref, fg_ref, o_ref, *, final_norm):
    y = h_ref[...] + _dot(a_ref[...], w_ref[...])
    if final_norm:
        y = _rms_rows(y, fg_ref[...])
    o_ref[...] = y


def _ffn(h, g, wg_bf, wu_bf, wd_bf, final_gain, *, final_norm):
    m, d = h.shape
    dff = wg_bf.shape[1]
    tm, tf = min(FFN_UP_TM, m), FFN_TF
    assert m % tm == 0 and dff % tf == 0
    act = pl.pallas_call(
        _ffn_up_kernel,
        grid=(m // tm, dff // tf),
        in_specs=[
            pl.BlockSpec((tm, d), lambda i, f: (i, 0)),
            pl.BlockSpec((1, d), lambda i, f: (0, 0)),
            pl.BlockSpec((d, tf), lambda i, f: (0, f)),
            pl.BlockSpec((d, tf), lambda i, f: (0, f)),
        ],
        out_specs=pl.BlockSpec((tm, tf), lambda i, f: (i, f)),
        out_shape=jax.ShapeDtypeStruct((m, dff), BF16),
        scratch_shapes=[pltpu.VMEM((tm, d), BF16)],
        compiler_params=_compiler_params(("parallel", "arbitrary")),
        name="ffn_up",
    )(h, g.reshape(1, d), wg_bf, wu_bf)

    tm = min(FFN_TM, m)
    kern = functools.partial(_ffn_down_kernel, final_norm=final_norm)
    return pl.pallas_call(
        kern,
        grid=(m // tm,),
        in_specs=[
            pl.BlockSpec((tm, dff), lambda i: (i, 0)),
            pl.BlockSpec((dff, d), lambda i: (0, 0), pipeline_mode=pl.Buffered(1)),
            pl.BlockSpec((tm, d), lambda i: (i, 0)),
            pl.BlockSpec((1, d), lambda i: (0, 0)),
        ],
        out_specs=pl.BlockSpec((tm, d), lambda i: (i, 0)),
        out_shape=jax.ShapeDtypeStruct((m, d), F32),
        compiler_params=_compiler_params(("parallel",)),
        name="ffn_down",
    )(act, wd_bf, h, final_gain.reshape(1, d))


def _rope_tables(seq):
    inv = 1.0 / (ROPE_THETA ** (jnp.arange(0, HEAD_DIM, 2, dtype=F32) / HEAD_DIM))
    ang = jnp.arange(seq, dtype=F32)[:, None] * inv[None, :]
    cos, sin = jnp.cos(ang), jnp.sin(ang)
    return jnp.concatenate([cos, cos], axis=1), jnp.concatenate([-sin, sin], axis=1)


A_UNITS = {"q": 0, "k_cmp": 12, "k_slc": 14, "k_win": 16, "v_cmp": 18, "v_slc": 20,
           "v_win": 22, "mem_q": 24, "gates": 28}
B_UNITS = {"q": 0, "k": 3, "mem_q": 4, "v": 5}
B_TN = 2 * DIL_HEADS * HEAD_DIM
A_TN = 6 * HEAD_DIM
A_NPAD = 30 * HEAD_DIM
A_ROPE_BLOCKS = 3


def _layer_a_weight(w_in):
    kv0 = A_Q

    def kv_cols(branch):
        return w_in[:, kv0 + branch * N_KV_A * HEAD_DIM:kv0 + (branch + 1) * N_KV_A * HEAD_DIM]

    gate0 = A_Q + A_KV
    mem0 = gate0 + A_GATE
    w = jnp.concatenate([w_in[:, :A_Q], kv_cols(0), kv_cols(2), kv_cols(4), kv_cols(1),
                         kv_cols(3), kv_cols(5), w_in[:, mem0:mem0 + MEM_Q],
                         w_in[:, gate0:mem0]], axis=1)
    w = jnp.pad(w, ((0, 0), (0, A_NPAD - w.shape[1])))
    scale = np.ones((1, A_NPAD), np.float32)
    scale[0, :A_Q] = SCALE * LOG2E
    scale[0, A_UNITS["mem_q"] * HEAD_DIM:A_UNITS["mem_q"] * HEAD_DIM + MEM_Q] = SCALE
    return w.astype(BF16), jnp.asarray(scale)


def _layer_a(h, mem, cosf, sinf, p, *, batch, seq):
    w_in_bf, col_scale = _layer_a_weight(p["w_in"])
    z = _norm_proj(h, p["norm_attn"], w_in_bf, col_scale, cosf, sinf,
                   tn=A_TN, n_rope_blocks=A_ROPE_BLOCKS, seq=seq)

    def unit_cols(name, n_units):
        return z[:, A_UNITS[name] * HEAD_DIM:(A_UNITS[name] + n_units) * HEAD_DIM]

    per_row = CMP_LEN // 2

    def cmp_rows(name):
        x = unit_cols(name, N_KV_A).reshape(batch, seq // per_row, per_row, N_KV_A, HEAD_DIM)
        return jnp.transpose(x, (0, 3, 1, 2, 4)).reshape(batch * N_KV_A, seq // per_row,
                                                          per_row * HEAD_DIM)

    x2 = jnp.stack([cmp_rows("k_cmp"), cmp_rows("v_cmp")])

    def pe_rows(pe):
        return jnp.pad(pe.reshape(2, per_row * HEAD_DIM), ((0, 6), (0, 0)))

    pe2 = jnp.stack([pe_rows(p["cmp_pe_k"]), pe_rows(p["cmp_pe_v"])])
    w1 = jnp.stack([p["cmp_w1_k"], p["cmp_w1_v"]]).astype(BF16)
    w2 = jnp.stack([p["cmp_w2_k"], p["cmp_w2_v"]]).astype(BF16)
    cmp_out = _compress(x2, pe2, w1, w2)
    cmp_kv = cmp_out.reshape(2, batch, N_KV_A, seq // per_row, HEAD_DIM)
    gb = jnp.pad(p["gate_bias"], (0, LANES - A_GATE)).reshape(1, LANES)
    o_nsa = _nsa_attention(z, cmp_kv, gb, batch=batch, seq=seq, units=A_UNITS)

    mkv = _mem_kv(mem, p["norm_mem"], p["w_mem_kv"])
    o_mem = _memory_attention(z, mkv, batch=batch, seq=seq,
                              q_col=A_UNITS["mem_q"] * HEAD_DIM, block_width=MEM_Q)
    return _out_proj(o_nsa, o_mem, p["w_out"].astype(BF16), h)


def _mem_kv(mem, norm_mem, w_mem_kv):
    b, m, d = mem.shape
    ones = jnp.ones((1, w_mem_kv.shape[1]), F32)
    dummy = jnp.zeros((m, HEAD_DIM), F32)
    return _norm_proj(mem.reshape(b * m, d), norm_mem, w_mem_kv.astype(BF16), ones, dummy, dummy,
                      tn=MEM_Q, n_rope_blocks=0, seq=m, tm=m)


def kernel(x, mem, a_norm_attn, a_w_in, a_gate_bias, a_cmp_pe_k, a_cmp_w1_k, a_cmp_w2_k, a_cmp_pe_v, a_cmp_w1_v, a_cmp_w2_v, a_norm_mem, a_w_mem_kv, a_w_out, a_norm_ffn, a_w_gate, a_w_up, a_w_down, kv_norm, w_kv_shared, b_norm_attn, b_w_in, b_norm_mem, b_w_mem_kv, b_w_out, b_norm_ffn, b_w_gate, b_w_up, b_w_down, final_norm):
    batch, seq, d = x.shape
    n_a = a_w_in.shape[0]
    n_b = b_w_in.shape[0]
    cosf, sinf = _rope_tables(seq)
    h = x.reshape(batch * seq, d)
    unit_gain = jnp.ones((d,), F32)

    for l in range(n_a):
        p = {"norm_attn": a_norm_attn[l], "w_in": a_w_in[l], "gate_bias": a_gate_bias[l],
             "cmp_pe_k": a_cmp_pe_k[l], "cmp_w1_k": a_cmp_w1_k[l], "cmp_w2_k": a_cmp_w2_k[l],
             "cmp_pe_v": a_cmp_pe_v[l], "cmp_w1_v": a_cmp_w1_v[l], "cmp_w2_v": a_cmp_w2_v[l],
             "norm_mem": a_norm_mem[l], "w_mem_kv": a_w_mem_kv[l], "w_out": a_w_out[l]}
        h = _layer_a(h, mem, cosf, sinf, p, batch=batch, seq=seq)
        last = (l == n_a - 1) and n_b == 0
        h = _ffn(h, a_norm_ffn[l], a_w_gate[l].astype(BF16), a_w_up[l].astype(BF16),
                 a_w_down[l].astype(BF16), final_norm if last else unit_gain, final_norm=last)

    if n_b > 0:
        assert n_b == 1, "the shared K/V projection is fused into the single mixer-B layer"
        n_kv_half = w_kv_shared.shape[1] // 2
        for l in range(n_b):
            w_q = b_norm_attn[l][:, None] * b_w_in[l]
            w_kv = kv_norm[:, None] * w_kv_shared
            w_cat = jnp.concatenate([w_q[:, :B_Q], w_kv[:, :n_kv_half], w_q[:, B_Q:],
                                     w_kv[:, n_kv_half:]], axis=1).astype(BF16)
            b_scale = np.ones((1, w_cat.shape[1]), np.float32)
            b_scale[0, :B_Q] = SCALE * LOG2E
            b_scale[0, B_Q + n_kv_half:B_Q + n_kv_half + MEM_Q] = SCALE
            zb = _norm_proj(h, unit_gain, w_cat, jnp.asarray(b_scale), cosf, sinf,
                            tn=B_TN, n_rope_blocks=(B_Q + n_kv_half) // B_TN, seq=seq)
            o_dil = _dilated_attention(zb, batch=batch, seq=seq, units=B_UNITS)
            mkv = _mem_kv(mem, b_norm_mem[l], b_w_mem_kv[l])
            o_mem = _memory_attention(zb, mkv, batch=batch, seq=seq,
                                      q_col=B_UNITS["mem_q"] * MEM_Q, block_width=MEM_Q)
            h = _out_proj(o_dil, o_mem, b_w_out[l].astype(BF16), h)
            last = l == n_b - 1
            h = _ffn(h, b_norm_ffn[l], b_w_gate[l].astype(BF16), b_w_up[l].astype(BF16),
                     b_w_down[l].astype(BF16), final_norm if last else unit_gain, final_norm=last)

    return h.reshape(batch, seq, d)
```

```python
import functools
import math

import numpy as np
import jax
import jax.numpy as jnp
from jax import lax
from jax.experimental import pallas as pl
from jax.experimental.pallas import tpu as pltpu

F32 = jnp.float32
BF16 = jnp.bfloat16

HEAD_DIM = 128
N_HEADS_A = 12
N_KV_A = 2
HPG_A = N_HEADS_A // N_KV_A
CMP_LEN = 32
CMP_STRIDE = 16
CMP_HIDDEN = 256
SLC_BLK = 64
SLC_SHIFT = SLC_BLK.bit_length() - 1
N_SEL = 16
WIN_A = 512
DIL_CONFIGS = ((128, 1), (512, 4), (2048, 16))
N_DIL_GROUPS = len(DIL_CONFIGS)
DIL_HEADS = 4
N_MEM_HEADS = 4
ROPE_THETA = 10000.0
EPS = 1e-6
NEG_INF = -1e30
TINY = 1e-30
SCALE = HEAD_DIM ** -0.5
LOG2E = math.log2(math.e)

A_Q = N_HEADS_A * HEAD_DIM
A_KV = 6 * N_KV_A * HEAD_DIM
A_GATE = 3 * N_HEADS_A
MEM_Q = N_MEM_HEADS * HEAD_DIM
B_Q = N_DIL_GROUPS * DIL_HEADS * HEAD_DIM

LANES = 128
SUBLANES = 8
VMEM_LIMIT_BYTES = 56 * 1024 * 1024

PROJ_TM = 1024
FFN_UP_TM = 1024
FFN_TM = 512
FFN_TF = 512
OUT_TM = 1024
OUT_TN = 1024
NSA_TQ = 128
NSA_KC = 512
DIL_TQ = 256
DIL_DENSE_MAX = 4
MEM_TQ = 1024

NT_DIMS = (((1,), (1,)), ((), ()))
TN_DIMS = (((0,), (0,)), ((), ()))


def _compiler_params(semantics):
    return pltpu.CompilerParams(dimension_semantics=semantics,
                                vmem_limit_bytes=VMEM_LIMIT_BYTES)


def _rms_rows(x, g):
    ms = jnp.mean(x * x, axis=-1, keepdims=True)
    return x * lax.rsqrt(ms + EPS) * g


def _dot(a, b):
    return jnp.dot(a, b, preferred_element_type=F32)


def _dot_nt(a, b):
    return lax.dot_general(a, b, NT_DIMS, preferred_element_type=F32)


def _dot_tn(a, b):
    return lax.dot_general(a, b, TN_DIMS, preferred_element_type=F32)


def _norm_proj_kernel(x_ref, g_ref, w_ref, cs_ref, cos_ref, sin_ref, o_ref, xn_ref, *,
                      n_rope_blocks, tn):
    j = pl.program_id(1)

    @pl.when(j == 0)
    def _():
        xn_ref[...] = _rms_rows(x_ref[...], g_ref[...]).astype(BF16)

    acc = _dot(xn_ref[...], w_ref[...]) * cs_ref[...]

    if n_rope_blocks > 0:
        roped = j < n_rope_blocks
        c = jnp.where(roped, cos_ref[...], 1.0)
        s = jnp.where(roped, sin_ref[...], 0.0)
        for h in range(tn // HEAD_DIM):
            y = acc[:, h * HEAD_DIM:(h + 1) * HEAD_DIM]
            rot = pltpu.roll(y, HEAD_DIM // 2, 1)
            o_ref[:, h * HEAD_DIM:(h + 1) * HEAD_DIM] = (y * c + rot * s).astype(o_ref.dtype)
    else:
        o_ref[...] = acc.astype(o_ref.dtype)


def _norm_proj(x, g, w_bf, col_scale, cosf, sinf, *, tn, n_rope_blocks, seq, tm=PROJ_TM):
    m, d = x.shape
    n = w_bf.shape[1]
    tm = min(tm, m)
    assert m % tm == 0 and n % tn == 0 and seq % tm == 0
    pos_blocks = seq // tm
    kern = functools.partial(_norm_proj_kernel, n_rope_blocks=n_rope_blocks, tn=tn)
    return pl.pallas_call(
        kern,
        grid=(m // tm, n // tn),
        in_specs=[
            pl.BlockSpec((tm, d), lambda i, j: (i, 0)),
            pl.BlockSpec((1, d), lambda i, j: (0, 0)),
            pl.BlockSpec((d, tn), lambda i, j: (0, j)),
            pl.BlockSpec((1, tn), lambda i, j: (0, j)),
            pl.BlockSpec((tm, HEAD_DIM), lambda i, j: (i % pos_blocks, 0)),
            pl.BlockSpec((tm, HEAD_DIM), lambda i, j: (i % pos_blocks, 0)),
        ],
        out_specs=pl.BlockSpec((tm, tn), lambda i, j: (i, j)),
        out_shape=jax.ShapeDtypeStruct((m, n), BF16),
        scratch_shapes=[pltpu.VMEM((tm, d), BF16)],
        compiler_params=_compiler_params(("parallel", "arbitrary")),
        name="norm_proj",
    )(x, g.reshape(1, d), w_bf, col_scale, cosf, sinf)


def _compress_kernel(x_ref, pe_ref, w1_ref, w2_ref, o_ref):
    half = (CMP_LEN // 2) * HEAD_DIM
    x = x_ref[...].astype(F32)
    xlo = (x + pe_ref[0:1, :]).astype(BF16)
    xhi = (x + pe_ref[1:2, :]).astype(BF16)
    ylo = _dot(xlo, w1_ref[:half, :])
    yhi = _dot(xhi, w1_ref[half:, :])
    n_rows = x.shape[0]
    hid = ylo + pltpu.roll(yhi, n_rows - 1, 0)
    act = (hid * jax.nn.sigmoid(hid)).astype(BF16)
    o_ref[...] = _dot(act, w2_ref[...]).astype(o_ref.dtype)


def _compress(x2, pe2, w1_bf, w2_bf):
    _, bg, nrow, wide = x2.shape
    return pl.pallas_call(
        _compress_kernel,
        grid=(2, bg),
        in_specs=[
            pl.BlockSpec((None, None, nrow, wide), lambda t, i: (t, i, 0, 0)),
            pl.BlockSpec((None, 8, wide), lambda t, i: (t, 0, 0)),
            pl.BlockSpec((None, 2 * wide, CMP_HIDDEN), lambda t, i: (t, 0, 0)),
            pl.BlockSpec((None, CMP_HIDDEN, HEAD_DIM), lambda t, i: (t, 0, 0)),
        ],
        out_specs=pl.BlockSpec((None, None, nrow, HEAD_DIM), lambda t, i: (t, i, 0, 0)),
        out_shape=jax.ShapeDtypeStruct((2, bg, nrow, HEAD_DIM), BF16),
        compiler_params=_compiler_params(("parallel", "arbitrary")),
        name="nsa_compress",
    )(x2, pe2, w1_bf, w2_bf)


def _block_ranks(score, jrow):
    n_blk = score.shape[0]
    groups = n_blk // SUBLANES
    blocks = [score[SUBLANES * r:SUBLANES * (r + 1), :] for r in range(groups)]
    rows = [jrow[SUBLANES * r:SUBLANES * (r + 1), :] for r in range(groups)]
    ranks = [jnp.zeros(blocks[0].shape, F32) for _ in range(groups)]
    for j in range(n_blk):
        rj = score[j:j + 1, :]
        for r in range(groups):
            if r > j // SUBLANES:
                ahead = rj >= blocks[r]
            elif r < j // SUBLANES:
                ahead = rj > blocks[r]
            else:
                ahead = (rj > blocks[r]) | ((rj == blocks[r]) & (rows[r] > j))
            ranks[r] = ranks[r] + jnp.where(ahead, 1.0, 0.0)
    return jnp.concatenate(ranks, axis=0)


def _nsa_kernel(q_ref, kc_ref, vc_ref, ks_ref, e_ref, vs_ref, kw_ref, vw_ref, gz_ref, gb_ref,
                o_ref, qa_ref, sa_ref, sb_ref, sw_ref, ow_ref, m_ref, l_ref, acc_ref, *, tq, seq):
    hq = HPG_A
    groups = range(N_KV_A)
    n_cmp_rows = kc_ref.shape[1]
    n_slc = seq // SLC_BLK
    qi = pl.program_id(1)
    s0 = qi * tq
    t_row = s0 + lax.broadcasted_iota(jnp.int32, (1, tq), 1)

    def lanes(h):
        return slice(h * tq, (h + 1) * tq)

    def gcols(g):
        return slice(g * HEAD_DIM, (g + 1) * HEAD_DIM)

    def tile_heads(x):
        return jnp.concatenate([x] * hq, axis=1)

    wlen = WIN_A + tq
    ws = pl.multiple_of(jnp.maximum(s0 - WIN_A, 0), LANES)
    c_end = lax.broadcasted_iota(jnp.int32, (n_cmp_rows, 1), 0) * CMP_STRIDE + (CMP_LEN - 1)
    cbias = tile_heads(jnp.where(c_end <= t_row, 0.0, NEG_INF))
    any_cmp = tile_heads(t_row >= CMP_LEN - 1)
    jrow = lax.broadcasted_iota(jnp.int32, (n_slc, 1), 0)
    ccol = lax.broadcasted_iota(jnp.int32, (1, n_cmp_rows), 1)
    lo = (SLC_BLK // CMP_STRIDE) * jrow - (CMP_LEN // CMP_STRIDE - 1)
    hi = (SLC_BLK // CMP_STRIDE) * jrow + (SLC_BLK // CMP_STRIDE - 1)
    mmap = jnp.where((ccol >= lo) & (ccol <= hi), 1.0, 0.0).astype(BF16)
    cur = t_row >> SLC_SHIFT
    forced = (jrow == 0) | (jrow == cur) | (jrow == cur - 1)
    assert n_slc <= LANES and tq == LANES

    q6, sc = [], []
    for g in groups:
        for h in range(hq):
            head = g * hq + h
            qa_ref[g, lanes(h), 0:HEAD_DIM] = q_ref[:, head * HEAD_DIM:(head + 1) * HEAD_DIM]
        q6.append(qa_ref[g, :, 0:HEAD_DIM])
        sc.append(_dot_nt(kc_ref[g], q6[g]) + cbias)
    for g in groups:
        sw_ref[g] = _dot_nt(kw_ref[pl.ds(ws, wlen), gcols(g)], q6[g])

    o_cmp, score = [], []
    for g in groups:
        ec = jnp.exp2(sc[g] - jnp.max(sc[g], axis=0, keepdims=True))
        den = jnp.maximum(jnp.sum(ec, axis=0, keepdims=True), TINY)
        pc = ec * jnp.where(any_cmp, 1.0 / den, 0.0)
        o_cmp.append(_dot_tn(vc_ref[g], pc.astype(BF16)))
        psum = pc[:, lanes(0)]
        for h in range(1, hq):
            psum = psum + pc[:, lanes(h)]
        p1 = psum.astype(BF16)
        r1 = psum - p1.astype(F32)
        p2 = r1.astype(BF16)
        p3 = (r1 - p2.astype(F32)).astype(BF16)
        imp = _dot(mmap, p1) + _dot(mmap, p2) + _dot(mmap, p3)
        score.append(jnp.where(forced, 1e9, jnp.where(jrow <= cur, imp, -1e9)))

    for g in groups:
        rank = _block_ranks(score[g], jrow)
        sel_bias = jnp.where((rank < min(N_SEL, n_slc)) & (jrow <= cur), 0.0, NEG_INF)
        bias_q = jnp.transpose(jnp.concatenate(
            [sel_bias, jnp.zeros((LANES - n_slc, tq), F32)], axis=0)).astype(BF16)
        for h in range(hq):
            qa_ref[g, lanes(h), HEAD_DIM:HEAD_DIM + LANES] = bias_q

    m_ref[...] = jnp.full(m_ref.shape, NEG_INF, F32)
    l_ref[...] = jnp.zeros(l_ref.shape, F32)
    acc_ref[...] = jnp.zeros(acc_ref.shape, F32)

    def scores(g, c, s_ref):
        k0 = pl.multiple_of(c * NSA_KC, NSA_KC)
        k_aug = jnp.concatenate([ks_ref[pl.ds(k0, NSA_KC), gcols(g)],
                                 e_ref[pl.ds(k0, NSA_KC), :]], axis=1)
        s_ref[g] = _dot_nt(k_aug, qa_ref[g])

    def softmax_pv(g, c, s_ref):
        s = s_ref[g]
        m_old = m_ref[g]
        m_new = jnp.maximum(m_old, jnp.max(s, axis=0, keepdims=True))
        alpha = jnp.exp2(m_old - m_new)
        p = jnp.exp2(s - m_new)
        l_ref[g] = alpha * l_ref[g] + jnp.sum(p, axis=0, keepdims=True)
        k0 = pl.multiple_of(c * NSA_KC, NSA_KC)
        pv = _dot_tn(vs_ref[pl.ds(k0, NSA_KC), gcols(g)], p.astype(BF16))
        acc_ref[g] = alpha * acc_ref[g] + pv
        m_ref[g] = m_new

    last = s0 // NSA_KC
    for g in groups:
        scores(g, 0, sa_ref)

    dist = t_row - (ws + lax.broadcasted_iota(jnp.int32, (wlen, 1), 0))
    wbias = tile_heads(jnp.where((dist >= 0) & (dist < WIN_A), 0.0, NEG_INF))
    for g in groups:
        sw = sw_ref[g] + wbias
        ew = jnp.exp2(sw - jnp.max(sw, axis=0, keepdims=True))
        ow_ref[g] = (_dot_tn(vw_ref[pl.ds(ws, wlen), gcols(g)], ew.astype(BF16))
                     * (1.0 / jnp.sum(ew, axis=0, keepdims=True)))

    def chunk_pair(i, carry):
        for g in groups:
            scores(g, 2 * i + 1, sb_ref)
            softmax_pv(g, 2 * i, sa_ref)
        for g in groups:
            scores(g, 2 * i + 2, sa_ref)
            softmax_pv(g, 2 * i + 1, sb_ref)
        return carry

    lax.fori_loop(0, last // 2, chunk_pair, 0)

    def last_chunk(g, s_ref):
        diag = pl.multiple_of(s0 - last * NSA_KC, LANES)
        krow = lax.broadcasted_iota(jnp.int32, (tq, 1), 0)
        lane = lax.broadcasted_iota(jnp.int32, (1, tq), 1)
        causal = tile_heads(jnp.where(krow <= lane, 0.0, NEG_INF))
        s_ref[g, pl.ds(diag, tq), :] = s_ref[g, pl.ds(diag, tq), :] + causal
        softmax_pv(g, last, s_ref)

    @pl.when(last % 2 == 0)
    def _():
        for g in groups:
            last_chunk(g, sa_ref)

    @pl.when(last % 2 == 1)
    def _():
        for g in groups:
            scores(g, last, sb_ref)
            softmax_pv(g, last - 1, sa_ref)
        for g in groups:
            last_chunk(g, sb_ref)

    gates_t = jnp.transpose(jax.nn.sigmoid(gz_ref[...].astype(F32) + gb_ref[...]))
    for g in groups:
        o_slc = acc_ref[g] * (1.0 / l_ref[g])
        for h in range(hq):
            head = g * hq + h
            o_h = (gates_t[3 * head:3 * head + 1, :] * o_cmp[g][:, lanes(h)]
                   + gates_t[3 * head + 1:3 * head + 2, :] * o_slc[:, lanes(h)]
                   + gates_t[3 * head + 2:3 * head + 3, :] * ow_ref[g, :, lanes(h)])
            o_ref[:, head * HEAD_DIM:(head + 1) * HEAD_DIM] = jnp.transpose(o_h).astype(o_ref.dtype)


def _nsa_attention(z, cmp_kv, gate_bias, *, batch, seq, units):
    tq = NSA_TQ
    nq = seq // tq
    n_cmp_rows = cmp_kv.shape[3]
    ng = N_KV_A
    hl = HPG_A * tq
    gw = ng * HEAD_DIM
    for name in ("k_slc", "v_slc", "k_win", "v_win"):
        assert units[name] % ng == 0
    kern = functools.partial(_nsa_kernel, tq=tq, seq=seq)

    def slab(name):
        return pl.BlockSpec((seq, gw), lambda b, i: (b, units[name] // ng))

    def cmp_spec(which):
        return pl.BlockSpec((None, None, ng, n_cmp_rows, HEAD_DIM),
                            lambda b, i: (which, b, 0, 0, 0))

    onehot = jnp.asarray(np.arange(seq)[:, None] // SLC_BLK == np.arange(LANES)[None, :], BF16)
    return pl.pallas_call(
        kern,
        grid=(batch, nq),
        in_specs=[
            pl.BlockSpec((tq, A_Q), lambda b, i: (b * nq + i, 0)),
            cmp_spec(0),
            cmp_spec(1),
            slab("k_slc"),
            pl.BlockSpec((seq, LANES), lambda b, i: (0, 0)),
            slab("v_slc"),
            slab("k_win"),
            slab("v_win"),
            pl.BlockSpec((tq, LANES), lambda b, i: (b * nq + i, units["gates"])),
            pl.BlockSpec((1, LANES), lambda b, i: (0, 0)),
        ],
        out_specs=pl.BlockSpec((tq, A_Q), lambda b, i: (b * nq + i, 0)),
        out_shape=jax.ShapeDtypeStruct((batch * seq, A_Q), BF16),
        scratch_shapes=[
            pltpu.VMEM((ng, hl, HEAD_DIM + LANES), BF16),
            pltpu.VMEM((ng, NSA_KC, hl), F32),
            pltpu.VMEM((ng, NSA_KC, hl), F32),
            pltpu.VMEM((ng, WIN_A + tq, hl), F32),
            pltpu.VMEM((ng, HEAD_DIM, hl), F32),
            pltpu.VMEM((ng, 1, hl), F32),
            pltpu.VMEM((ng, 1, hl), F32),
            pltpu.VMEM((ng, HEAD_DIM, hl), F32),
        ],
        compiler_params=_compiler_params(("parallel", "arbitrary")),
        name="nsa_attention",
    )(z, cmp_kv, cmp_kv, z, onehot, z, z, z, z, gate_bias)


def _band_attn_kernel(*refs, tu, parts):
    q_refs = refs[:len(parts)]
    k_ref, v_ref, o_ref, lse_ref = refs[len(parts):]
    u0 = pl.program_id(2) * tu
    n_seq = k_ref.shape[0]
    windows = []
    for lk, span, stride in parts:
        ks = pl.multiple_of(jnp.clip(u0 - span, 0, n_seq - lk), LANES)
        dist = (u0 + lax.broadcasted_iota(jnp.int32, (tu, 1), 0)
                - (ks + lax.broadcasted_iota(jnp.int32, (1, lk), 1)))
        keep = (dist >= 0) & (dist <= span)
        if stride > 1:
            keep = keep & ((dist & (stride - 1)) == 0)
        windows.append((ks, lk, jnp.where(keep, 0.0, NEG_INF)))
    lane = lax.broadcasted_iota(jnp.int32, (1, LANES), 1)
    lse_tile = jnp.zeros((tu, LANES), F32)

    def head_cols(h):
        return slice(h * HEAD_DIM, (h + 1) * HEAD_DIM)

    scores = [[_dot_nt(q_ref[:, head_cols(h)], k_ref[pl.ds(ks, lk), head_cols(h)]) + bias
               for q_ref, (ks, lk, bias) in zip(q_refs, windows)]
              for h in range(DIL_HEADS)]
    probs, inv_l = [], []
    for h, s_parts in enumerate(scores):
        s = jnp.concatenate(s_parts, axis=1)
        m = jnp.max(s, axis=-1, keepdims=True)
        p = jnp.exp2(s - m)
        l = jnp.sum(p, axis=-1, keepdims=True)
        probs.append(p.astype(BF16))
        inv_l.append(1.0 / l)
        lse_tile = jnp.where(lane == h, m + jnp.log2(l), lse_tile)
    for h in range(DIL_HEADS):
        o, col = None, 0
        for ks, lk, _ in windows:
            pv = _dot(probs[h][:, col:col + lk], v_ref[pl.ds(ks, lk), head_cols(h)])
            o = pv if o is None else o + pv
            col += lk
        o_ref[:, head_cols(h)] = (o * inv_l[h]).astype(o_ref.dtype)
    lse_ref[...] = lse_tile


def _dil_merge_kernel(*refs):
    out_ref = refs[-1]
    calls = list(zip(refs[0:-1:2], refs[1:-1:2]))
    for h in range(DIL_HEADS):
        cols = slice(h * HEAD_DIM, (h + 1) * HEAD_DIM)
        shape = (out_ref.shape[0], HEAD_DIM)
        lses = [jnp.broadcast_to(l_ref[:, h:h + 1], shape) for _, l_ref in calls]
        top = functools.reduce(jnp.maximum, lses)
        ws = [jnp.exp2(lse - top) for lse in lses]
        num = sum(w * o_ref[:, cols].astype(F32) for w, (o_ref, _) in zip(ws, calls))
        out_ref[:, cols] = (num * (1.0 / sum(ws))).astype(out_ref.dtype)


def _dilated_attention(zb, *, batch, seq, units):
    n = zb.shape[1]
    width = DIL_HEADS * HEAD_DIM
    tu = DIL_TQ

    def band_call(r, q_units, parts, operands, per_row, k_unit, v_unit):
        n_seq = seq // r

        def z_spec(rows, unit, whole):
            return pl.BlockSpec((None, rows, width),
                                lambda b, c, i: (b, 0 if whole else i, c * per_row + unit))

        o_g, lse_g = pl.pallas_call(
            functools.partial(_band_attn_kernel, tu=tu, parts=parts),
            grid=(batch, r, n_seq // tu),
            in_specs=[z_spec(tu, u, False) for u in q_units]
            + [z_spec(n_seq, k_unit, True), z_spec(n_seq, v_unit, True)],
            out_specs=[pl.BlockSpec((None, tu, width), lambda b, c, i: (b, i, c)),
                       pl.BlockSpec((None, tu, LANES), lambda b, c, i: (b, i, c))],
            out_shape=[jax.ShapeDtypeStruct((batch, n_seq, r * width), BF16),
                       jax.ShapeDtypeStruct((batch, n_seq, r * LANES), F32)],
            compiler_params=_compiler_params(("parallel", "parallel", "arbitrary")),
            name="dilated_attention",
        )(*operands)
        return [o_g.reshape(batch * seq, width), lse_g.reshape(batch * seq, LANES)]

    def window(n_seq, span):
        assert n_seq % tu == 0 and span % LANES == 0 and tu % LANES == 0
        return min(tu + span, n_seq)

    dense = [(gi, w, r) for gi, (w, r) in enumerate(DIL_CONFIGS) if r <= DIL_DENSE_MAX]
    zv = zb.reshape(batch, seq, n)
    results = band_call(1, [units["q"] + gi for gi, _, _ in dense],
                        tuple((window(seq, w), w, r) for _, w, r in dense),
                        [zv] * (len(dense) + 2), n // width, units["k"], units["v"])
    for gi, (w, r) in enumerate(DIL_CONFIGS):
        if r <= DIL_DENSE_MAX:
            continue

        def class_view(unit):
            return zb[:, unit * width:(unit + 1) * width].reshape(batch, seq // r, r * width)

        results += band_call(r, [0], ((window(seq // r, w // r), w // r, 1),),
                             [class_view(units["q"] + gi), class_view(units["k"]),
                              class_view(units["v"])], 1, 0, 0)

    m = batch * seq
    tm = min(OUT_TM, m)
    o_spec = pl.BlockSpec((tm, width), lambda i: (i, 0))
    l_spec = pl.BlockSpec((tm, LANES), lambda i: (i, 0))
    return pl.pallas_call(
        _dil_merge_kernel,
        grid=(m // tm,),
        in_specs=[o_spec, l_spec] * (len(results) // 2),
        out_specs=o_spec,
        out_shape=jax.ShapeDtypeStruct((m, width), BF16),
        compiler_params=_compiler_params(("parallel",)),
        name="dilated_merge",
    )(*results)


def _mem_attn_kernel(q_ref, kv_ref, o_ref, *, q_offset):
    def cols(h, base=0):
        return slice(base + h * HEAD_DIM, base + (h + 1) * HEAD_DIM)

    heads = range(N_MEM_HEADS)
    scores = [_dot_nt(q_ref[:, cols(h, q_offset)], kv_ref[:, cols(h)]) for h in heads]
    probs = []
    for s in scores:
        e = jnp.exp(s - jnp.max(s, axis=-1, keepdims=True))
        probs.append((e / jnp.sum(e, axis=-1, keepdims=True)).astype(BF16))
    for h in heads:
        o_ref[:, cols(h)] = _dot(probs[h], kv_ref[:, cols(h, MEM_Q)]).astype(o_ref.dtype)


def _memory_attention(z, mkv, *, batch, seq, q_col, block_width):
    tq = MEM_TQ
    nq = seq // tq
    n_mem = mkv.shape[0] // batch
    q_block, q_offset = divmod(q_col, block_width)
    assert q_offset + MEM_Q <= block_width
    return pl.pallas_call(
        functools.partial(_mem_attn_kernel, q_offset=q_offset),
        grid=(batch, nq),
        in_specs=[
            pl.BlockSpec((tq, block_width), lambda b, i: (b * nq + i, q_block)),
            pl.BlockSpec((n_mem, 2 * MEM_Q), lambda b, i: (b, 0)),
        ],
        out_specs=pl.BlockSpec((tq, MEM_Q), lambda b, i: (b * nq + i, 0)),
        out_shape=jax.ShapeDtypeStruct((batch * seq, MEM_Q), BF16),
        compiler_params=_compiler_params(("parallel", "arbitrary")),
        name="memory_attention",
    )(z, mkv)


def _out_proj_kernel(a1_ref, a2_ref, w_ref, h_ref, o_ref):
    a = jnp.concatenate([a1_ref[...], a2_ref[...]], axis=1)
    o_ref[...] = h_ref[...] + _dot(a, w_ref[...])


def _out_proj(a1, a2, w_bf, h):
    m, d = h.shape
    tm, tn = min(OUT_TM, m), OUT_TN
    k1, k2 = a1.shape[1], a2.shape[1]
    assert w_bf.shape[0] == k1 + k2
    return pl.pallas_call(
        _out_proj_kernel,
        grid=(m // tm, d // tn),
        in_specs=[
            pl.BlockSpec((tm, k1), lambda i, j: (i, 0)),
            pl.BlockSpec((tm, k2), lambda i, j: (i, 0)),
            pl.BlockSpec((k1 + k2, tn), lambda i, j: (0, j)),
            pl.BlockSpec((tm, tn), lambda i, j: (i, j)),
        ],
        out_specs=pl.BlockSpec((tm, tn), lambda i, j: (i, j)),
        out_shape=jax.ShapeDtypeStruct((m, d), F32),
        compiler_params=_compiler_params(("parallel", "arbitrary")),
        name="out_proj",
    )(a1, a2, w_bf, h)


def _ffn_up_kernel(x_ref, g_ref, wg_ref, wu_ref, o_ref, xn_ref):
    @pl.when(pl.program_id(1) == 0)
    def _():
        xn_ref[...] = _rms_rows(x_ref[...], g_ref[...]).astype(BF16)

    xn = xn_ref[...]
    gate = _dot(xn, wg_ref[...])
    up = _dot(xn, wu_ref[...])
    o_ref[...] = (gate * jax.nn.sigmoid(gate) * up).astype(o_ref.dtype)


def _ffn_down_kernel(a_ref, w_ref, h_ref, fg_ref, o_ref, *, final_norm):
    y = h_ref[...] + _dot(a_ref[...], w_ref[...])
    if final_norm:
        y = _rms_rows(y, fg_ref[...])
    o_ref[...] = y


def _ffn(h, g, wg_bf, wu_bf, wd_bf, final_gain, *, final_norm):
    m, d = h.shape
    dff = wg_bf.shape[1]
    tm, tf = min(FFN_UP_TM, m), FFN_TF
    assert m % tm == 0 and dff % tf == 0
    act = pl.pallas_call(
        _ffn_up_kernel,
        grid=(m // tm, dff // tf),
        in_specs=[
            pl.BlockSpec((tm, d), lambda i, f: (i, 0)),
            pl.BlockSpec((1, d), lambda i, f: (0, 0)),
            pl.BlockSpec((d, tf), lambda i, f: (0, f)),
            pl.BlockSpec((d, tf), lambda i, f: (0, f)),
        ],
        out_specs=pl.BlockSpec((tm, tf), lambda i, f: (i, f)),
        out_shape=jax.ShapeDtypeStruct((m, dff), BF16),
        scratch_shapes=[pltpu.VMEM((tm, d), BF16)],
        compiler_params=_compiler_params(("parallel", "arbitrary")),
        name="ffn_up",
    )(h, g.reshape(1, d), wg_bf, wu_bf)

    tm = min(FFN_TM, m)
    kern = functools.partial(_ffn_down_kernel, final_norm=final_norm)
    return pl.pallas_call(
        kern,
        grid=(m // tm,),
        in_specs=[
            pl.BlockSpec((tm, dff), lambda i: (i, 0)),
            pl.BlockSpec((dff, d), lambda i: (0, 0), pipeline_mode=pl.Buffered(1)),
            pl.BlockSpec((tm, d), lambda i: (i, 0)),
            pl.BlockSpec((1, d), lambda i: (0, 0)),
        ],
        out_specs=pl.BlockSpec((tm, d), lambda i: (i, 0)),
        out_shape=jax.ShapeDtypeStruct((m, d), F32),
        compiler_params=_compiler_params(("parallel",)),
        name="ffn_down",
    )(act, wd_bf, h, final_gain.reshape(1, d))


def _rope_tables(seq):
    inv = 1.0 / (ROPE_THETA ** (jnp.arange(0, HEAD_DIM, 2, dtype=F32) / HEAD_DIM))
    ang = jnp.arange(seq, dtype=F32)[:, None] * inv[None, :]
    cos, sin = jnp.cos(ang), jnp.sin(ang)
    return jnp.concatenate([cos, cos], axis=1), jnp.concatenate([-sin, sin], axis=1)


A_UNITS = {"q": 0, "k_cmp": 12, "k_slc": 14, "k_win": 16, "v_cmp": 18, "v_slc": 20,
           "v_win": 22, "mem_q": 24, "gates": 28}
B_UNITS = {"q": 0, "k": 3, "mem_q": 4, "v": 5}
B_TN = 2 * DIL_HEADS * HEAD_DIM
A_TN = 6 * HEAD_DIM
A_NPAD = 30 * HEAD_DIM
A_ROPE_BLOCKS = 3


def _layer_a_weight(w_in):
    kv0 = A_Q

    def kv_cols(branch):
        return w_in[:, kv0 + branch * N_KV_A * HEAD_DIM:kv0 + (branch + 1) * N_KV_A * HEAD_DIM]

    gate0 = A_Q + A_KV
    mem0 = gate0 + A_GATE
    w = jnp.concatenate([w_in[:, :A_Q], kv_cols(0), kv_cols(2), kv_cols(4), kv_cols(1),
                         kv_cols(3), kv_cols(5), w_in[:, mem0:mem0 + MEM_Q],
                         w_in[:, gate0:mem0]], axis=1)
    w = jnp.pad(w, ((0, 0), (0, A_NPAD - w.shape[1])))
    scale = np.ones((1, A_NPAD), np.float32)
    scale[0, :A_Q] = SCALE * LOG2E
    scale[0, A_UNITS["mem_q"] * HEAD_DIM:A_UNITS["mem_q"] * HEAD_DIM + MEM_Q] = SCALE
    return w.astype(BF16), jnp.asarray(scale)


def _layer_a(h, mem, cosf, sinf, p, *, batch, seq):
    w_in_bf, col_scale = _layer_a_weight(p["w_in"])
    z = _norm_proj(h, p["norm_attn"], w_in_bf, col_scale, cosf, sinf,
                   tn=A_TN, n_rope_blocks=A_ROPE_BLOCKS, seq=seq)

    def unit_cols(name, n_units):
        return z[:, A_UNITS[name] * HEAD_DIM:(A_UNITS[name] + n_units) * HEAD_DIM]

    per_row = CMP_LEN // 2

    def cmp_rows(name):
        x = unit_cols(name, N_KV_A).reshape(batch, seq // per_row, per_row, N_KV_A, HEAD_DIM)
        return jnp.transpose(x, (0, 3, 1, 2, 4)).reshape(batch * N_KV_A, seq // per_row,
                                                          per_row * HEAD_DIM)

    x2 = jnp.stack([cmp_rows("k_cmp"), cmp_rows("v_cmp")])

    def pe_rows(pe):
        return jnp.pad(pe.reshape(2, per_row * HEAD_DIM), ((0, 6), (0, 0)))

    pe2 = jnp.stack([pe_rows(p["cmp_pe_k"]), pe_rows(p["cmp_pe_v"])])
    w1 = jnp.stack([p["cmp_w1_k"], p["cmp_w1_v"]]).astype(BF16)
    w2 = jnp.stack([p["cmp_w2_k"], p["cmp_w2_v"]]).astype(BF16)
    cmp_out = _compress(x2, pe2, w1, w2)
    cmp_kv = cmp_out.reshape(2, batch, N_KV_A, seq // per_row, HEAD_DIM)
    gb = jnp.pad(p["gate_bias"], (0, LANES - A_GATE)).reshape(1, LANES)
    o_nsa = _nsa_attention(z, cmp_kv, gb, batch=batch, seq=seq, units=A_UNITS)

    mkv = _mem_kv(mem, p["norm_mem"], p["w_mem_kv"])
    o_mem = _memory_attention(z, mkv, batch=batch, seq=seq,
                              q_col=A_UNITS["mem_q"] * HEAD_DIM, block_width=MEM_Q)
    return _out_proj(o_nsa, o_mem, p["w_out"].astype(BF16), h)


def _mem_kv(mem, norm_mem, w_mem_kv):
    b, m, d = mem.shape
    ones = jnp.ones((1, w_mem_kv.shape[1]), F32)
    dummy = jnp.zeros((m, HEAD_DIM), F32)
    return _norm_proj(mem.reshape(b * m, d), norm_mem, w_mem_kv.astype(BF16), ones, dummy, dummy,
                      tn=MEM_Q, n_rope_blocks=0, seq=m, tm=m)


def kernel(x, mem, a_norm_attn, a_w_in, a_gate_bias, a_cmp_pe_k, a_cmp_w1_k, a_cmp_w2_k, a_cmp_pe_v, a_cmp_w1_v, a_cmp_w2_v, a_norm_mem, a_w_mem_kv, a_w_out, a_norm_ffn, a_w_gate, a_w_up, a_w_down, kv_norm, w_kv_shared, b_norm_attn, b_w_in, b_norm_mem, b_w_mem_kv, b_w_out, b_norm_ffn, b_w_gate, b_w_up, b_w_down, final_norm):
    batch, seq, d = x.shape
    n_a = a_w_in.shape[0]
    n_b = b_w_in.shape[0]
    cosf, sinf = _rope_tables(seq)
    h = x.reshape(batch * seq, d)
    unit_gain = jnp.ones((d,), F32)

    for l in range(n_a):
        p = {"norm_attn": a_norm_attn[l], "w_in": a_w_in[l], "gate_bias": a_gate_bias[l],
             "cmp_pe_k": a_cmp_pe_k[l], "cmp_w1_k": a_cmp_w1_k[l], "cmp_w2_k": a_cmp_w2_k[l],
             "cmp_pe_v": a_cmp_pe_v[l], "cmp_w1_v": a_cmp_w1_v[l], "cmp_w2_v": a_cmp_w2_v[l],
             "norm_mem": a_norm_mem[l], "w_mem_kv": a_w_mem_kv[l], "w_out": a_w_out[l]}
        h = _layer_a(h, mem, cosf, sinf, p, batch=batch, seq=seq)
        last = (l == n_a - 1) and n_b == 0
        h = _ffn(h, a_norm_ffn[l], a_w_gate[l].astype(BF16), a_w_up[l].astype(BF16),
                 a_w_down[l].astype(BF16), final_norm if last else unit_gain, final_norm=last)

    if n_b > 0:
        assert n_b == 1, "the shared K/V projection is fused into the single mixer-B layer"
        n_kv_half = w_kv_shared.shape[1] // 2
        for l in range(n_b):
            w_q = b_norm_attn[l][:, None] * b_w_in[l]
            w_kv = kv_norm[:, None] * w_kv_shared
            w_cat = jnp.concatenate([w_q[:, :B_Q], w_kv[:, :n_kv_half], w_q[:, B_Q:],
                                     w_kv[:, n_kv_half:]], axis=1).astype(BF16)
            b_scale = np.ones((1, w_cat.shape[1]), np.float32)
            b_scale[0, :B_Q] = SCALE * LOG2E
            b_scale[0, B_Q + n_kv_half:B_Q + n_kv_half + MEM_Q] = SCALE
            zb = _norm_proj(h, unit_gain, w_cat, jnp.asarray(b_scale), cosf, sinf,
                            tn=B_TN, n_rope_blocks=(B_Q + n_kv_half) // B_TN, seq=seq)
            o_dil = _dilated_attention(zb, batch=batch, seq=seq, units=B_UNITS)
            mkv = _mem_kv(mem, b_norm_mem[l], b_w_mem_kv[l])
            o_mem = _memory_attention(zb, mkv, batch=batch, seq=seq,
                                      q_col=B_UNITS["mem_q"] * MEM_Q, block_width=MEM_Q)
            h = _out_proj(o_dil, o_mem, b_w_out[l].astype(BF16), h)
            last = l == n_b - 1
            h = _ffn(h, b_norm_ffn[l], b_w_gate[l].astype(BF16), b_w_up[l].astype(BF16),
                     b_w_down[l].astype(BF16), final_norm if last else unit_gain, final_norm=last)

    return h.reshape(batch, seq, d)
```

```python
import functools
import math

import numpy as np
import jax
import jax.numpy as jnp
from jax import lax
from jax.experimental import pallas as pl
from jax.experimental.pallas import tpu as pltpu

F32 = jnp.float32
BF16 = jnp.bfloat16

HEAD_DIM = 128
N_HEADS_A = 12
N_KV_A = 2
HPG_A = N_HEADS_A // N_KV_A
CMP_LEN = 32
CMP_STRIDE = 16
CMP_HIDDEN = 256
SLC_BLK = 64
SLC_SHIFT = SLC_BLK.bit_length() - 1
N_SEL = 16
WIN_A = 512
DIL_CONFIGS = ((128, 1), (512, 4), (2048, 16))
N_DIL_GROUPS = len(DIL_CONFIGS)
DIL_HEADS = 4
N_MEM_HEADS = 4
ROPE_THETA = 10000.0
EPS = 1e-6
NEG_INF = -1e30
TINY = 1e-30
SCALE = HEAD_DIM ** -0.5
LOG2E = math.log2(math.e)

A_Q = N_HEADS_A * HEAD_DIM
A_KV = 6 * N_KV_A * HEAD_DIM
A_GATE = 3 * N_HEADS_A
MEM_Q = N_MEM_HEADS * HEAD_DIM
B_Q = N_DIL_GROUPS * DIL_HEADS * HEAD_DIM

LANES = 128
SUBLANES = 8
VMEM_LIMIT_BYTES = 56 * 1024 * 1024

PROJ_TM = 1024
FFN_UP_TM = 1024
FFN_TM = 512
FFN_TF = 512
OUT_TM = 1024
OUT_TN = 1024
NSA_TQ = 128
NSA_KC = 512
DIL_TQ = 256
DIL_DENSE_MAX = 4
MEM_TQ = 1024

NT_DIMS = (((1,), (1,)), ((), ()))
TN_DIMS = (((0,), (0,)), ((), ()))


def _compiler_params(semantics):
    return pltpu.CompilerParams(dimension_semantics=semantics,
                                vmem_limit_bytes=VMEM_LIMIT_BYTES)


def _rms_rows(x, g):
    ms = jnp.mean(x * x, axis=-1, keepdims=True)
    return x * lax.rsqrt(ms + EPS) * g


def _dot(a, b):
    return jnp.dot(a, b, preferred_element_type=F32)


def _dot_nt(a, b):
    return lax.dot_general(a, b, NT_DIMS, preferred_element_type=F32)


def _dot_tn(a, b):
    return lax.dot_general(a, b, TN_DIMS, preferred_element_type=F32)


def _norm_proj_kernel(x_ref, g_ref, w_ref, cs_ref, cos_ref, sin_ref, o_ref, xn_ref, *,
                      n_rope_blocks, tn):
    j = pl.program_id(1)

    @pl.when(j == 0)
    def _():
        xn_ref[...] = _rms_rows(x_ref[...], g_ref[...]).astype(BF16)

    acc = _dot(xn_ref[...], w_ref[...]) * cs_ref[...]

    if n_rope_blocks > 0:
        roped = j < n_rope_blocks
        c = jnp.where(roped, cos_ref[...], 1.0)
        s = jnp.where(roped, sin_ref[...], 0.0)
        for h in range(tn // HEAD_DIM):
            y = acc[:, h * HEAD_DIM:(h + 1) * HEAD_DIM]
            rot = pltpu.roll(y, HEAD_DIM // 2, 1)
            o_ref[:, h * HEAD_DIM:(h + 1) * HEAD_DIM] = (y * c + rot * s).astype(o_ref.dtype)
    else:
        o_ref[...] = acc.astype(o_ref.dtype)


def _norm_proj(x, g, w_bf, col_scale, cosf, sinf, *, tn, n_rope_blocks, seq, tm=PROJ_TM):
    m, d = x.shape
    n = w_bf.shape[1]
    tm = min(tm, m)
    assert m % tm == 0 and n % tn == 0 and seq % tm == 0
    pos_blocks = seq // tm
    kern = functools.partial(_norm_proj_kernel, n_rope_blocks=n_rope_blocks, tn=tn)
    return pl.pallas_call(
        kern,
        grid=(m // tm, n // tn),
        in_specs=[
            pl.BlockSpec((tm, d), lambda i, j: (i, 0)),
            pl.BlockSpec((1, d), lambda i, j: (0, 0)),
            pl.BlockSpec((d, tn), lambda i, j: (0, j)),
            pl.BlockSpec((1, tn), lambda i, j: (0, j)),
            pl.BlockSpec((tm, HEAD_DIM), lambda i, j: (i % pos_blocks, 0)),
            pl.BlockSpec((tm, HEAD_DIM), lambda i, j: (i % pos_blocks, 0)),
        ],
        out_specs=pl.BlockSpec((tm, tn), lambda i, j: (i, j)),
        out_shape=jax.ShapeDtypeStruct((m, n), BF16),
        scratch_shapes=[pltpu.VMEM((tm, d), BF16)],
        compiler_params=_compiler_params(("parallel", "arbitrary")),
        name="norm_proj",
    )(x, g.reshape(1, d), w_bf, col_scale, cosf, sinf)


def _class_perm(tm, r):
    dst = np.arange(tm)
    c, u = dst // (tm // r), dst % (tm // r)
    perm = np.zeros((tm, tm), np.float32)
    perm[dst, u * r + c] = 1.0
    return perm


def _to_class_kernel(p_ref, *refs, r):
    n = len(refs) // 2
    for x_ref, o_ref in zip(refs[:n], refs[n:]):
        y = _dot(p_ref[...], x_ref[...]).astype(o_ref.dtype)
        rows = y.shape[0] // r
        for c in range(r):
            o_ref[c] = y[c * rows:(c + 1) * rows, :]


def _to_class_order(x, slabs, r, *, batch, seq):
    tm = min(PROJ_TM, seq)
    nblk = seq // tm
    perm = jnp.asarray(_class_perm(tm, r), x.dtype)
    return pl.pallas_call(
        functools.partial(_to_class_kernel, r=r),
        grid=(batch * nblk,),
        in_specs=[pl.BlockSpec((tm, tm), lambda i: (0, 0))]
        + [pl.BlockSpec((tm, w), lambda i, cb=cb: (i, cb)) for cb, w in slabs],
        out_specs=[pl.BlockSpec((None, r, tm // r, w), lambda i: (i // nblk, 0, i % nblk, 0))
                   for _, w in slabs],
        out_shape=[jax.ShapeDtypeStruct((batch, r, seq // r, w), x.dtype) for _, w in slabs],
        compiler_params=_compiler_params(("parallel",)),
        name="to_class_order",
    )(perm, *([x] * len(slabs)))


def _from_class_kernel(pt_ref, x_ref, o_ref, *, r):
    x = jnp.concatenate([x_ref[c] for c in range(r)], axis=0)
    o_ref[...] = _dot(pt_ref[...], x).astype(o_ref.dtype)


def _from_class_order(xc, *, batch, seq):
    _, r, _, w = xc.shape
    tm = min(PROJ_TM, seq)
    nblk = seq // tm
    perm_t = jnp.asarray(_class_perm(tm, r).T, xc.dtype)
    return pl.pallas_call(
        functools.partial(_from_class_kernel, r=r),
        grid=(batch * nblk,),
        in_specs=[pl.BlockSpec((tm, tm), lambda i: (0, 0)),
                  pl.BlockSpec((None, r, tm // r, w), lambda i: (i // nblk, 0, i % nblk, 0))],
        out_specs=pl.BlockSpec((tm, w), lambda i: (i, 0)),
        out_shape=jax.ShapeDtypeStruct((batch * seq, w), xc.dtype),
        compiler_params=_compiler_params(("parallel",)),
        name="from_class_order",
    )(perm_t, xc)


def _compress_kernel(x_ref, pe_ref, w1_ref, w2_ref, o_ref):
    n_planes, n_rows, _ = x_ref.shape
    ylo = yhi = None
    for l in range(n_planes):
        x = x_ref[l].astype(F32)
        xlo = (x + pe_ref[l:l + 1, :]).astype(BF16)
        xhi = (x + pe_ref[n_planes + l:n_planes + l + 1, :]).astype(BF16)
        dlo = _dot(xlo, w1_ref[l * HEAD_DIM:(l + 1) * HEAD_DIM, :])
        dhi = _dot(xhi, w1_ref[(n_planes + l) * HEAD_DIM:(n_planes + l + 1) * HEAD_DIM, :])
        ylo = dlo if ylo is None else ylo + dlo
        yhi = dhi if yhi is None else yhi + dhi
    hid = ylo + pltpu.roll(yhi, n_rows - 1, 0)
    act = (hid * jax.nn.sigmoid(hid)).astype(BF16)
    o_ref[...] = _dot(act, w2_ref[...]).astype(o_ref.dtype)


def _compress(xc, pe, w1_bf, w2_bf):
    batch, planes, nrow, gd = xc.shape
    ng = gd // HEAD_DIM
    return pl.pallas_call(
        _compress_kernel,
        grid=(batch, ng),
        in_specs=[
            pl.BlockSpec((None, planes, nrow, HEAD_DIM), lambda b, g: (b, 0, 0, g)),
            pl.BlockSpec((CMP_LEN, HEAD_DIM), lambda b, g: (0, 0)),
            pl.BlockSpec((CMP_LEN * HEAD_DIM, CMP_HIDDEN), lambda b, g: (0, 0)),
            pl.BlockSpec((CMP_HIDDEN, HEAD_DIM), lambda b, g: (0, 0)),
        ],
        out_specs=pl.BlockSpec((None, None, nrow, HEAD_DIM), lambda b, g: (b, g, 0, 0)),
        out_shape=jax.ShapeDtypeStruct((batch, ng, nrow, HEAD_DIM), BF16),
        compiler_params=_compiler_params(("parallel", "arbitrary")),
        name="nsa_compress",
    )(xc, pe, w1_bf, w2_bf)


def _block_ranks(score, jrow):
    n_blk = score.shape[0]
    groups = n_blk // SUBLANES
    blocks = [score[SUBLANES * r:SUBLANES * (r + 1), :] for r in range(groups)]
    rows = [jrow[SUBLANES * r:SUBLANES * (r + 1), :] for r in range(groups)]
    ranks = [jnp.zeros(blocks[0].shape, F32) for _ in range(groups)]
    for j in range(n_blk):
        rj = score[j:j + 1, :]
        for r in range(groups):
            if r > j // SUBLANES:
                ahead = rj >= blocks[r]
            elif r < j // SUBLANES:
                ahead = rj > blocks[r]
            else:
                ahead = (rj > blocks[r]) | ((rj == blocks[r]) & (rows[r] > j))
            ranks[r] = ranks[r] + jnp.where(ahead, 1.0, 0.0)
    return jnp.concatenate(ranks, axis=0)


def _nsa_kernel(q_ref, kc_ref, vc_ref, ks_ref, e_ref, vs_ref, kw_ref, vw_ref, gz_ref, gb_ref,
                o_ref, qa_ref, sa_ref, sb_ref, sw_ref, ow_ref, m_ref, l_ref, acc_ref, *, tq, seq):
    hq = HPG_A
    groups = range(N_KV_A)
    n_cmp_rows = kc_ref.shape[1]
    n_slc = seq // SLC_BLK
    qi = pl.program_id(1)
    s0 = qi * tq
    t_row = s0 + lax.broadcasted_iota(jnp.int32, (1, tq), 1)

    def lanes(h):
        return slice(h * tq, (h + 1) * tq)

    def gcols(g):
        return slice(g * HEAD_DIM, (g + 1) * HEAD_DIM)

    def tile_heads(x):
        return jnp.concatenate([x] * hq, axis=1)

    wlen = WIN_A + tq
    ws = pl.multiple_of(jnp.maximum(s0 - WIN_A, 0), LANES)
    c_end = lax.broadcasted_iota(jnp.int32, (n_cmp_rows, 1), 0) * CMP_STRIDE + (CMP_LEN - 1)
    cbias = tile_heads(jnp.where(c_end <= t_row, 0.0, NEG_INF))
    any_cmp = tile_heads(t_row >= CMP_LEN - 1)
    jrow = lax.broadcasted_iota(jnp.int32, (n_slc, 1), 0)
    ccol = lax.broadcasted_iota(jnp.int32, (1, n_cmp_rows), 1)
    lo = (SLC_BLK // CMP_STRIDE) * jrow - (CMP_LEN // CMP_STRIDE - 1)
    hi = (SLC_BLK // CMP_STRIDE) * jrow + (SLC_BLK // CMP_STRIDE - 1)
    mmap = jnp.where((ccol >= lo) & (ccol <= hi), 1.0, 0.0).astype(BF16)
    cur = t_row >> SLC_SHIFT
    forced = (jrow == 0) | (jrow == cur) | (jrow == cur - 1)
    assert n_slc <= LANES and tq == LANES

    q6, sc = [], []
    for g in groups:
        for h in range(hq):
            head = g * hq + h
            qa_ref[g, lanes(h), 0:HEAD_DIM] = q_ref[:, head * HEAD_DIM:(head + 1) * HEAD_DIM]
        q6.append(qa_ref[g, :, 0:HEAD_DIM])
        sc.append(_dot_nt(kc_ref[g], q6[g]) + cbias)
    for g in groups:
        sw_ref[g] = _dot_nt(kw_ref[pl.ds(ws, wlen), gcols(g)], q6[g])

    o_cmp, score = [], []
    for g in groups:
        ec = jnp.exp2(sc[g] - jnp.max(sc[g], axis=0, keepdims=True))
        den = jnp.maximum(jnp.sum(ec, axis=0, keepdims=True), TINY)
        pc = ec * jnp.where(any_cmp, 1.0 / den, 0.0)
        o_cmp.append(_dot_tn(vc_ref[g], pc.astype(BF16)))
        psum = pc[:, lanes(0)]
        for h in range(1, hq):
            psum = psum + pc[:, lanes(h)]
        p1 = psum.astype(BF16)
        r1 = psum - p1.astype(F32)
        p2 = r1.astype(BF16)
        p3 = (r1 - p2.astype(F32)).astype(BF16)
        imp = _dot(mmap, p1) + _dot(mmap, p2) + _dot(mmap, p3)
        score.append(jnp.where(forced, 1e9, jnp.where(jrow <= cur, imp, -1e9)))

    for g in groups:
        rank = _block_ranks(score[g], jrow)
        sel_bias = jnp.where((rank < min(N_SEL, n_slc)) & (jrow <= cur), 0.0, NEG_INF)
        bias_q = jnp.transpose(jnp.concatenate(
            [sel_bias, jnp.zeros((LANES - n_slc, tq), F32)], axis=0)).astype(BF16)
        for h in range(hq):
            qa_ref[g, lanes(h), HEAD_DIM:HEAD_DIM + LANES] = bias_q

    m_ref[...] = jnp.full(m_ref.shape, NEG_INF, F32)
    l_ref[...] = jnp.zeros(l_ref.shape, F32)
    acc_ref[...] = jnp.zeros(acc_ref.shape, F32)

    def scores(g, c, s_ref):
        k0 = pl.multiple_of(c * NSA_KC, NSA_KC)
        k_aug = jnp.concatenate([ks_ref[pl.ds(k0, NSA_KC), gcols(g)],
                                 e_ref[pl.ds(k0, NSA_KC), :]], axis=1)
        s_ref[g] = _dot_nt(k_aug, qa_ref[g])

    def softmax_pv(g, c, s_ref):
        s = s_ref[g]
        m_old = m_ref[g]
        m_new = jnp.maximum(m_old, jnp.max(s, axis=0, keepdims=True))
        alpha = jnp.exp2(m_old - m_new)
        p = jnp.exp2(s - m_new)
        l_ref[g] = alpha * l_ref[g] + jnp.sum(p, axis=0, keepdims=True)
        k0 = pl.multiple_of(c * NSA_KC, NSA_KC)
        pv = _dot_tn(vs_ref[pl.ds(k0, NSA_KC), gcols(g)], p.astype(BF16))
        acc_ref[g] = alpha * acc_ref[g] + pv
        m_ref[g] = m_new

    last = s0 // NSA_KC
    for g in groups:
        scores(g, 0, sa_ref)

    dist = t_row - (ws + lax.broadcasted_iota(jnp.int32, (wlen, 1), 0))
    wbias = tile_heads(jnp.where((dist >= 0) & (dist < WIN_A), 0.0, NEG_INF))
    for g in groups:
        sw = sw_ref[g] + wbias
        ew = jnp.exp2(sw - jnp.max(sw, axis=0, keepdims=True))
        ow_ref[g] = (_dot_tn(vw_ref[pl.ds(ws, wlen), gcols(g)], ew.astype(BF16))
                     * (1.0 / jnp.sum(ew, axis=0, keepdims=True)))

    def chunk_pair(i, carry):
        for g in groups:
            scores(g, 2 * i + 1, sb_ref)
            softmax_pv(g, 2 * i, sa_ref)
        for g in groups:
            scores(g, 2 * i + 2, sa_ref)
            softmax_pv(g, 2 * i + 1, sb_ref)
        return carry

    lax.fori_loop(0, last // 2, chunk_pair, 0)

    def last_chunk(g, s_ref):
        diag = pl.multiple_of(s0 - last * NSA_KC, LANES)
        krow = lax.broadcasted_iota(jnp.int32, (tq, 1), 0)
        lane = lax.broadcasted_iota(jnp.int32, (1, tq), 1)
        causal = tile_heads(jnp.where(krow <= lane, 0.0, NEG_INF))
        s_ref[g, pl.ds(diag, tq), :] = s_ref[g, pl.ds(diag, tq), :] + causal
        softmax_pv(g, last, s_ref)

    @pl.when(last % 2 == 0)
    def _():
        for g in groups:
            last_chunk(g, sa_ref)

    @pl.when(last % 2 == 1)
    def _():
        for g in groups:
            scores(g, last, sb_ref)
            softmax_pv(g, last - 1, sa_ref)
        for g in groups:
            last_chunk(g, sb_ref)

    gates_t = jnp.transpose(jax.nn.sigmoid(gz_ref[...].astype(F32) + gb_ref[...]))
    for g in groups:
        o_slc = acc_ref[g] * (1.0 / l_ref[g])
        for h in range(hq):
            head = g * hq + h
            o_h = (gates_t[3 * head:3 * head + 1, :] * o_cmp[g][:, lanes(h)]
                   + gates_t[3 * head + 1:3 * head + 2, :] * o_slc[:, lanes(h)]
                   + gates_t[3 * head + 2:3 * head + 3, :] * ow_ref[g, :, lanes(h)])
            o_ref[:, head * HEAD_DIM:(head + 1) * HEAD_DIM] = jnp.transpose(o_h).astype(o_ref.dtype)


def _nsa_attention(z, kc, vc, gate_bias, *, batch, seq, units):
    tq = NSA_TQ
    nq = seq // tq
    n_cmp_rows = kc.shape[2]
    ng = N_KV_A
    hl = HPG_A * tq
    gw = ng * HEAD_DIM
    for name in ("k_slc", "v_slc", "k_win", "v_win"):
        assert units[name] % ng == 0
    kern = functools.partial(_nsa_kernel, tq=tq, seq=seq)

    def slab(name):
        return pl.BlockSpec((seq, gw), lambda b, i: (b, units[name] // ng))

    cmp_spec = pl.BlockSpec((None, ng, n_cmp_rows, HEAD_DIM), lambda b, i: (b, 0, 0, 0))

    onehot = jnp.asarray(np.arange(seq)[:, None] // SLC_BLK == np.arange(LANES)[None, :], BF16)
    return pl.pallas_call(
        kern,
        grid=(batch, nq),
        in_specs=[
            pl.BlockSpec((tq, A_Q), lambda b, i: (b * nq + i, 0)),
            cmp_spec,
            cmp_spec,
            slab("k_slc"),
            pl.BlockSpec((seq, LANES), lambda b, i: (0, 0)),
            slab("v_slc"),
            slab("k_win"),
            slab("v_win"),
            pl.BlockSpec((tq, LANES), lambda b, i: (b * nq + i, units["gates"])),
            pl.BlockSpec((1, LANES), lambda b, i: (0, 0)),
        ],
        out_specs=pl.BlockSpec((tq, A_Q), lambda b, i: (b * nq + i, 0)),
        out_shape=jax.ShapeDtypeStruct((batch * seq, A_Q), BF16),
        scratch_shapes=[
            pltpu.VMEM((ng, hl, HEAD_DIM + LANES), BF16),
            pltpu.VMEM((ng, NSA_KC, hl), F32),
            pltpu.VMEM((ng, NSA_KC, hl), F32),
            pltpu.VMEM((ng, WIN_A + tq, hl), F32),
            pltpu.VMEM((ng, HEAD_DIM, hl), F32),
            pltpu.VMEM((ng, 1, hl), F32),
            pltpu.VMEM((ng, 1, hl), F32),
            pltpu.VMEM((ng, HEAD_DIM, hl), F32),
        ],
        compiler_params=_compiler_params(("parallel", "arbitrary")),
        name="nsa_attention",
    )(z, kc, vc, z, onehot, z, z, z, z, gate_bias)


def _band_attn_kernel(*refs, tu, parts):
    q_refs = refs[:len(parts)]
    k_ref, v_ref, o_ref, lse_ref = refs[len(parts):]
    u0 = pl.program_id(2) * tu
    n_seq = k_ref.shape[0]
    windows = []
    for lk, span, stride in parts:
        ks = pl.multiple_of(jnp.clip(u0 - span, 0, n_seq - lk), LANES)
        dist = (u0 + lax.broadcasted_iota(jnp.int32, (tu, 1), 0)
                - (ks + lax.broadcasted_iota(jnp.int32, (1, lk), 1)))
        keep = (dist >= 0) & (dist <= span)
        if stride > 1:
            keep = keep & ((dist & (stride - 1)) == 0)
        windows.append((ks, lk, jnp.where(keep, 0.0, NEG_INF)))
    lane = lax.broadcasted_iota(jnp.int32, (1, LANES), 1)
    lse_tile = jnp.zeros((tu, LANES), F32)

    def head_cols(h):
        return slice(h * HEAD_DIM, (h + 1) * HEAD_DIM)

    scores = [[_dot_nt(q_ref[:, head_cols(h)], k_ref[pl.ds(ks, lk), head_cols(h)]) + bias
               for q_ref, (ks, lk, bias) in zip(q_refs, windows)]
              for h in range(DIL_HEADS)]
    probs, inv_l = [], []
    for h, s_parts in enumerate(scores):
        s = jnp.concatenate(s_parts, axis=1)
        m = jnp.max(s, axis=-1, keepdims=True)
        p = jnp.exp2(s - m)
        l = jnp.sum(p, axis=-1, keepdims=True)
        probs.append(p.astype(BF16))
        inv_l.append(1.0 / l)
        lse_tile = jnp.where(lane == h, m + jnp.log2(l), lse_tile)
    for h in range(DIL_HEADS):
        o, col = None, 0
        for ks, lk, _ in windows:
            pv = _dot(probs[h][:, col:col + lk], v_ref[pl.ds(ks, lk), head_cols(h)])
            o = pv if o is None else o + pv
            col += lk
        o_ref[:, head_cols(h)] = (o * inv_l[h]).astype(o_ref.dtype)
    lse_ref[...] = lse_tile


def _dil_merge_kernel(*refs):
    out_ref = refs[-1]
    calls = list(zip(refs[0:-1:2], refs[1:-1:2]))
    for h in range(DIL_HEADS):
        cols = slice(h * HEAD_DIM, (h + 1) * HEAD_DIM)
        shape = (out_ref.shape[0], HEAD_DIM)
        lses = [jnp.broadcast_to(l_ref[:, h:h + 1], shape) for _, l_ref in calls]
        top = functools.reduce(jnp.maximum, lses)
        ws = [jnp.exp2(lse - top) for lse in lses]
        num = sum(w * o_ref[:, cols].astype(F32) for w, (o_ref, _) in zip(ws, calls))
        out_ref[:, cols] = (num * (1.0 / sum(ws))).astype(out_ref.dtype)


def _dilated_attention(zb, *, batch, seq, units):
    n = zb.shape[1]
    width = DIL_HEADS * HEAD_DIM
    tu = DIL_TQ

    def band_call(r, q_units, parts, operands, k_unit, v_unit):
        n_seq = seq // r

        def z_spec(rows, unit, whole):
            return pl.BlockSpec((None, None, rows, width),
                                lambda b, c, i: (b, c, 0 if whole else i, unit))

        def out_spec(cols):
            return pl.BlockSpec((None, None, tu, cols), lambda b, c, i: (b, c, i, 0))

        return pl.pallas_call(
            functools.partial(_band_attn_kernel, tu=tu, parts=parts),
            grid=(batch, r, n_seq // tu),
            in_specs=[z_spec(tu, u, False) for u in q_units]
            + [z_spec(n_seq, k_unit, True), z_spec(n_seq, v_unit, True)],
            out_specs=[out_spec(width), out_spec(LANES)],
            out_shape=[jax.ShapeDtypeStruct((batch, r, n_seq, width), BF16),
                       jax.ShapeDtypeStruct((batch, r, n_seq, LANES), F32)],
            compiler_params=_compiler_params(("parallel", "parallel", "arbitrary")),
            name="dilated_attention",
        )(*operands)

    def window(n_seq, span):
        assert n_seq % tu == 0 and span % LANES == 0 and tu % LANES == 0
        return min(tu + span, n_seq)

    dense = [(gi, w, r) for gi, (w, r) in enumerate(DIL_CONFIGS) if r <= DIL_DENSE_MAX]
    zv = zb.reshape(batch, 1, seq, n)
    o_d, lse_d = band_call(1, [units["q"] + gi for gi, _, _ in dense],
                           tuple((window(seq, w), w, r) for _, w, r in dense),
                           [zv] * (len(dense) + 2), units["k"], units["v"])
    results = [o_d.reshape(batch * seq, width), lse_d.reshape(batch * seq, LANES)]
    for gi, (w, r) in enumerate(DIL_CONFIGS):
        if r <= DIL_DENSE_MAX:
            continue
        slabs = [(units["q"] + gi, width), (units["k"], width), (units["v"], width)]
        o_c, lse_c = band_call(r, [0], ((window(seq // r, w // r), w // r, 1),),
                               _to_class_order(zb, slabs, r, batch=batch, seq=seq), 0, 0)
        results += [_from_class_order(o_c, batch=batch, seq=seq),
                    jnp.transpose(lse_c, (0, 2, 1, 3)).reshape(batch * seq, LANES)]

    m = batch * seq
    tm = min(OUT_TM, m)
    o_spec = pl.BlockSpec((tm, width), lambda i: (i, 0))
    l_spec = pl.BlockSpec((tm, LANES), lambda i: (i, 0))
    return pl.pallas_call(
        _dil_merge_kernel,
        grid=(m // tm,),
        in_specs=[o_spec, l_spec] * (len(results) // 2),
        out_specs=o_spec,
        out_shape=jax.ShapeDtypeStruct((m, width), BF16),
        compiler_params=_compiler_params(("parallel",)),
        name="dilated_merge",
    )(*results)


def _mem_attn_kernel(q_ref, kv_ref, o_ref, *, q_offset):
    def cols(h, base=0):
        return slice(base + h * HEAD_DIM, base + (h + 1) * HEAD_DIM)

    heads = range(N_MEM_HEADS)
    scores = [_dot_nt(q_ref[:, cols(h, q_offset)], kv_ref[:, cols(h)]) for h in heads]
    probs = []
    for s in scores:
        e = jnp.exp(s - jnp.max(s, axis=-1, keepdims=True))
        probs.append((e / jnp.sum(e, axis=-1, keepdims=True)).astype(BF16))
    for h in heads:
        o_ref[:, cols(h)] = _dot(probs[h], kv_ref[:, cols(h, MEM_Q)]).astype(o_ref.dtype)


def _memory_attention(z, mkv, *, batch, seq, q_col, block_width):
    tq = MEM_TQ
    nq = seq // tq
    n_mem = mkv.shape[0] // batch
    q_block, q_offset = divmod(q_col, block_width)
    assert q_offset + MEM_Q <= block_width
    return pl.pallas_call(
        functools.partial(_mem_attn_kernel, q_offset=q_offset),
        grid=(batch, nq),
        in_specs=[
            pl.BlockSpec((tq, block_width), lambda b, i: (b * nq + i, q_block)),
            pl.BlockSpec((n_mem, 2 * MEM_Q), lambda b, i: (b, 0)),
        ],
        out_specs=pl.BlockSpec((tq, MEM_Q), lambda b, i: (b * nq + i, 0)),
        out_shape=jax.ShapeDtypeStruct((batch * seq, MEM_Q), BF16),
        compiler_params=_compiler_params(("parallel", "arbitrary")),
        name="memory_attention",
    )(z, mkv)


def _out_proj_kernel(a1_ref, a2_ref, w_ref, h_ref, o_ref):
    a = jnp.concatenate([a1_ref[...], a2_ref[...]], axis=1)
    o_ref[...] = h_ref[...] + _dot(a, w_ref[...])


def _out_proj(a1, a2, w_bf, h):
    m, d = h.shape
    tm, tn = min(OUT_TM, m), OUT_TN
    k1, k2 = a1.shape[1], a2.shape[1]
    assert w_bf.shape[0] == k1 + k2
    return pl.pallas_call(
        _out_proj_kernel,
        grid=(m // tm, d // tn),
        in_specs=[
            pl.BlockSpec((tm, k1), lambda i, j: (i, 0)),
            pl.BlockSpec((tm, k2), lambda i, j: (i, 0)),
            pl.BlockSpec((k1 + k2, tn), lambda i, j: (0, j)),
            pl.BlockSpec((tm, tn), lambda i, j: (i, j)),
        ],
        out_specs=pl.BlockSpec((tm, tn), lambda i, j: (i, j)),
        out_shape=jax.ShapeDtypeStruct((m, d), F32),
        compiler_params=_compiler_params(("parallel", "arbitrary")),
        name="out_proj",
    )(a1, a2, w_bf, h)


def _ffn_up_kernel(x_ref, g_ref, wg_ref, wu_ref, o_ref, xn_ref):
    @pl.when(pl.program_id(1) == 0)
    def _():
        xn_ref[...] = _rms_rows(x_ref[...], g_ref[...]).astype(BF16)

    xn = xn_ref[...]
    gate = _dot(xn, wg_ref[...])
    up = _dot(xn, wu_ref[...])
    o_ref[...] = (gate * jax.nn.sigmoid(gate) * up).astype(o_ref.dtype)


def _ffn_down_kernel(a_ref, w_ref, h_ref, fg_ref, o_ref, *, final_norm):
    y = h_ref[...] + _dot(a_ref[...], w_ref[...])
    if final_norm:
        y = _rms_rows(y, fg_ref[...])
    o_ref[...] = y


def _ffn(h, g, wg_bf, wu_bf, wd_bf, final_gain, *, final_norm):
    m, d = h.shape
    dff = wg_bf.shape[1]
    tm, tf = min(FFN_UP_TM, m), FFN_TF
    assert m % tm == 0 and dff % tf == 0
    act = pl.pallas_call(
        _ffn_up_kernel,
        grid=(m // tm, dff // tf),
        in_specs=[
            pl.BlockSpec((tm, d), lambda i, f: (i, 0)),
            pl.BlockSpec((1, d), lambda i, f: (0, 0)),
            pl.BlockSpec((d, tf), lambda i, f: (0, f)),
            pl.BlockSpec((d, tf), lambda i, f: (0, f)),
        ],
        out_specs=pl.BlockSpec((tm, tf), lambda i, f: (i, f)),
        out_shape=jax.ShapeDtypeStruct((m, dff), BF16),
        scratch_shapes=[pltpu.VMEM((tm, d), BF16)],
        compiler_params=_compiler_params(("parallel", "arbitrary")),
        name="ffn_up",
    )(h, g.reshape(1, d), wg_bf, wu_bf)

    tm = min(FFN_TM, m)
    kern = functools.partial(_ffn_down_kernel, final_norm=final_norm)
    return pl.pallas_call(
        kern,
        grid=(m // tm,),
        in_specs=[
            pl.BlockSpec((tm, dff), lambda i: (i, 0)),
            pl.BlockSpec((dff, d), lambda i: (0, 0), pipeline_mode=pl.Buffered(1)),
            pl.BlockSpec((tm, d), lambda i: (i, 0)),
            pl.BlockSpec((1, d), lambda i: (0, 0)),
        ],
        out_specs=pl.BlockSpec((tm, d), lambda i: (i, 0)),
        out_shape=jax.ShapeDtypeStruct((m, d), F32),
        compiler_params=_compiler_params(("parallel",)),
        name="ffn_down",
    )(act, wd_bf, h, final_gain.reshape(1, d))


def _rope_tables(seq):
    inv = 1.0 / (ROPE_THETA ** (jnp.arange(0, HEAD_DIM, 2, dtype=F32) / HEAD_DIM))
    ang = jnp.arange(seq, dtype=F32)[:, None] * inv[None, :]
    cos, sin = jnp.cos(ang), jnp.sin(ang)
    return jnp.concatenate([cos, cos], axis=1), jnp.concatenate([-sin, sin], axis=1)


A_UNITS = {"q": 0, "k_cmp": 12, "k_slc": 14, "k_win": 16, "v_cmp": 18, "v_slc": 20,
           "v_win": 22, "mem_q": 24, "gates": 28}
B_UNITS = {"q": 0, "k": 3, "mem_q": 4, "v": 5}
B_TN = 2 * DIL_HEADS * HEAD_DIM
A_TN = 6 * HEAD_DIM
A_NPAD = 30 * HEAD_DIM
A_ROPE_BLOCKS = 3


def _layer_a_weight(w_in):
    kv0 = A_Q

    def kv_cols(branch):
        return w_in[:, kv0 + branch * N_KV_A * HEAD_DIM:kv0 + (branch + 1) * N_KV_A * HEAD_DIM]

    gate0 = A_Q + A_KV
    mem0 = gate0 + A_GATE
    w = jnp.concatenate([w_in[:, :A_Q], kv_cols(0), kv_cols(2), kv_cols(4), kv_cols(1),
                         kv_cols(3), kv_cols(5), w_in[:, mem0:mem0 + MEM_Q],
                         w_in[:, gate0:mem0]], axis=1)
    w = jnp.pad(w, ((0, 0), (0, A_NPAD - w.shape[1])))
    scale = np.ones((1, A_NPAD), np.float32)
    scale[0, :A_Q] = SCALE * LOG2E
    scale[0, A_UNITS["mem_q"] * HEAD_DIM:A_UNITS["mem_q"] * HEAD_DIM + MEM_Q] = SCALE
    return w.astype(BF16), jnp.asarray(scale)


def _layer_a(h, mem, cosf, sinf, p, *, batch, seq):
    w_in_bf, col_scale = _layer_a_weight(p["w_in"])
    z = _norm_proj(h, p["norm_attn"], w_in_bf, col_scale, cosf, sinf,
                   tn=A_TN, n_rope_blocks=A_ROPE_BLOCKS, seq=seq)

    gw = N_KV_A * HEAD_DIM
    k_raw, v_raw = _to_class_order(
        z, [(A_UNITS["k_cmp"] * HEAD_DIM // gw, gw), (A_UNITS["v_cmp"] * HEAD_DIM // gw, gw)],
        CMP_STRIDE, batch=batch, seq=seq)
    kc = _compress(k_raw, p["cmp_pe_k"], p["cmp_w1_k"].astype(BF16), p["cmp_w2_k"].astype(BF16))
    vc = _compress(v_raw, p["cmp_pe_v"], p["cmp_w1_v"].astype(BF16), p["cmp_w2_v"].astype(BF16))
    gb = jnp.pad(p["gate_bias"], (0, LANES - A_GATE)).reshape(1, LANES)
    o_nsa = _nsa_attention(z, kc, vc, gb, batch=batch, seq=seq, units=A_UNITS)

    mkv = _mem_kv(mem, p["norm_mem"], p["w_mem_kv"])
    o_mem = _memory_attention(z, mkv, batch=batch, seq=seq,
                              q_col=A_UNITS["mem_q"] * HEAD_DIM, block_width=MEM_Q)
    return _out_proj(o_nsa, o_mem, p["w_out"].astype(BF16), h)


def _mem_kv(mem, norm_mem, w_mem_kv):
    b, m, d = mem.shape
    ones = jnp.ones((1, w_mem_kv.shape[1]), F32)
    dummy = jnp.zeros((m, HEAD_DIM), F32)
    return _norm_proj(mem.reshape(b * m, d), norm_mem, w_mem_kv.astype(BF16), ones, dummy, dummy,
                      tn=MEM_Q, n_rope_blocks=0, seq=m, tm=m)


def kernel(x, mem, a_norm_attn, a_w_in, a_gate_bias, a_cmp_pe_k, a_cmp_w1_k, a_cmp_w2_k, a_cmp_pe_v, a_cmp_w1_v, a_cmp_w2_v, a_norm_mem, a_w_mem_kv, a_w_out, a_norm_ffn, a_w_gate, a_w_up, a_w_down, kv_norm, w_kv_shared, b_norm_attn, b_w_in, b_norm_mem, b_w_mem_kv, b_w_out, b_norm_ffn, b_w_gate, b_w_up, b_w_down, final_norm):
    batch, seq, d = x.shape
    n_a = a_w_in.shape[0]
    n_b = b_w_in.shape[0]
    cosf, sinf = _rope_tables(seq)
    h = x.reshape(batch * seq, d)
    unit_gain = jnp.ones((d,), F32)

    for l in range(n_a):
        p = {"norm_attn": a_norm_attn[l], "w_in": a_w_in[l], "gate_bias": a_gate_bias[l],
             "cmp_pe_k": a_cmp_pe_k[l], "cmp_w1_k": a_cmp_w1_k[l], "cmp_w2_k": a_cmp_w2_k[l],
             "cmp_pe_v": a_cmp_pe_v[l], "cmp_w1_v": a_cmp_w1_v[l], "cmp_w2_v": a_cmp_w2_v[l],
             "norm_mem": a_norm_mem[l], "w_mem_kv": a_w_mem_kv[l], "w_out": a_w_out[l]}
        h = _layer_a(h, mem, cosf, sinf, p, batch=batch, seq=seq)
        last = (l == n_a - 1) and n_b == 0
        h = _ffn(h, a_norm_ffn[l], a_w_gate[l].astype(BF16), a_w_up[l].astype(BF16),
                 a_w_down[l].astype(BF16), final_norm if last else unit_gain, final_norm=last)

    if n_b > 0:
        assert n_b == 1, "the shared K/V projection is fused into the single mixer-B layer"
        n_kv_half = w_kv_shared.shape[1] // 2
        for l in range(n_b):
            w_q = b_norm_attn[l][:, None] * b_w_in[l]
            w_kv = kv_norm[:, None] * w_kv_shared
            w_cat = jnp.concatenate([w_q[:, :B_Q], w_kv[:, :n_kv_half], w_q[:, B_Q:],
                                     w_kv[:, n_kv_half:]], axis=1).astype(BF16)
            b_scale = np.ones((1, w_cat.shape[1]), np.float32)
            b_scale[0, :B_Q] = SCALE * LOG2E
            b_scale[0, B_Q + n_kv_half:B_Q + n_kv_half + MEM_Q] = SCALE
            zb = _norm_proj(h, unit_gain, w_cat, jnp.asarray(b_scale), cosf, sinf,
                            tn=B_TN, n_rope_blocks=(B_Q + n_kv_half) // B_TN, seq=seq)
            o_dil = _dilated_attention(zb, batch=batch, seq=seq, units=B_UNITS)
            mkv = _mem_kv(mem, b_norm_mem[l], b_w_mem_kv[l])
            o_mem = _memory_attention(zb, mkv, batch=batch, seq=seq,
                                      q_col=B_UNITS["mem_q"] * MEM_Q, block_width=MEM_Q)
            h = _out_proj(o_dil, o_mem, b_w_out[l].astype(BF16), h)
            last = l == n_b - 1
            h = _ffn(h, b_norm_ffn[l], b_w_gate[l].astype(BF16), b_w_up[l].astype(BF16),
                     b_w_down[l].astype(BF16), final_norm if last else unit_gain, final_norm=last)

    return h.reshape(batch, seq, d)
```

```python
import functools
import math

import numpy as np
import jax
import jax.numpy as jnp
from jax import lax
from jax.experimental import pallas as pl
from jax.experimental.pallas import tpu as pltpu

F32 = jnp.float32
BF16 = jnp.bfloat16

HEAD_DIM = 128
N_HEADS_A = 12
N_KV_A = 2
HPG_A = N_HEADS_A // N_KV_A
CMP_LEN = 32
CMP_STRIDE = 16
CMP_HIDDEN = 256
SLC_BLK = 64
SLC_SHIFT = SLC_BLK.bit_length() - 1
N_SEL = 16
WIN_A = 512
DIL_CONFIGS = ((128, 1), (512, 4), (2048, 16))
N_DIL_GROUPS = len(DIL_CONFIGS)
DIL_HEADS = 4
N_MEM_HEADS = 4
ROPE_THETA = 10000.0
EPS = 1e-6
NEG_INF = -1e30
TINY = 1e-30
SCALE = HEAD_DIM ** -0.5
LOG2E = math.log2(math.e)

A_Q = N_HEADS_A * HEAD_DIM
A_KV = 6 * N_KV_A * HEAD_DIM
A_GATE = 3 * N_HEADS_A
MEM_Q = N_MEM_HEADS * HEAD_DIM
B_Q = N_DIL_GROUPS * DIL_HEADS * HEAD_DIM

LANES = 128
SUBLANES = 8
VMEM_LIMIT_BYTES = 56 * 1024 * 1024

PROJ_TM = 1024
PROJ_DOT_ROWS = 256
WEIGHT_PREP_ROWS = 256
FFN_UP_TM = 1024
FFN_TM = 512
FFN_TF = 512
OUT_TM = 1024
OUT_TN = 1024
NSA_TQ = 128
NSA_KC = 512
DIL_TQ = 256
DIL_DENSE_MAX = 4
MEM_TQ = 1024

NT_DIMS = (((1,), (1,)), ((), ()))
TN_DIMS = (((0,), (0,)), ((), ()))


def _compiler_params(semantics):
    return pltpu.CompilerParams(dimension_semantics=semantics,
                                vmem_limit_bytes=VMEM_LIMIT_BYTES)


def _rms_rows(x, g):
    ms = jnp.mean(x * x, axis=-1, keepdims=True)
    return x * lax.rsqrt(ms + EPS) * g


def _dot(a, b):
    return jnp.dot(a, b, preferred_element_type=F32)


def _dot_nt(a, b):
    return lax.dot_general(a, b, NT_DIMS, preferred_element_type=F32)


def _dot_tn(a, b):
    return lax.dot_general(a, b, TN_DIMS, preferred_element_type=F32)


def _norm_proj_kernel(x_ref, g_ref, w_ref, cs_ref, cos_ref, sin_ref, o_ref, xn_ref, *,
                      n_rope_blocks, tn):
    j = pl.program_id(1)

    @pl.when(j == 0)
    def _():
        xn_ref[...] = _rms_rows(x_ref[...], g_ref[...]).astype(BF16)

    for r0 in range(0, xn_ref.shape[0], PROJ_DOT_ROWS):
        rows = slice(r0, r0 + PROJ_DOT_ROWS)
        acc = _dot(xn_ref[rows, :], w_ref[...]) * cs_ref[...]
        if n_rope_blocks > 0:
            roped = j < n_rope_blocks
            c = jnp.where(roped, cos_ref[rows, :], 1.0)
            s = jnp.where(roped, sin_ref[rows, :], 0.0)
            for h in range(tn // HEAD_DIM):
                y = acc[:, h * HEAD_DIM:(h + 1) * HEAD_DIM]
                rot = pltpu.roll(y, HEAD_DIM // 2, 1)
                o_ref[rows, h * HEAD_DIM:(h + 1) * HEAD_DIM] = (y * c + rot * s).astype(o_ref.dtype)
        else:
            o_ref[rows, :] = acc.astype(o_ref.dtype)


def _norm_proj(x, g, w_bf, col_scale, cosf, sinf, *, tn, n_rope_blocks, seq, tm=PROJ_TM):
    m, d = x.shape
    n = w_bf.shape[1]
    tm = min(tm, m)
    assert m % tm == 0 and n % tn == 0 and seq % tm == 0 and tm % PROJ_DOT_ROWS == 0
    pos_blocks = seq // tm
    kern = functools.partial(_norm_proj_kernel, n_rope_blocks=n_rope_blocks, tn=tn)
    return pl.pallas_call(
        kern,
        grid=(m // tm, n // tn),
        in_specs=[
            pl.BlockSpec((tm, d), lambda i, j: (i, 0)),
            pl.BlockSpec((1, d), lambda i, j: (0, 0)),
            pl.BlockSpec((d, tn), lambda i, j: (0, j)),
            pl.BlockSpec((1, tn), lambda i, j: (0, j)),
            pl.BlockSpec((tm, HEAD_DIM), lambda i, j: (i % pos_blocks, 0)),
            pl.BlockSpec((tm, HEAD_DIM), lambda i, j: (i % pos_blocks, 0)),
        ],
        out_specs=pl.BlockSpec((tm, tn), lambda i, j: (i, j)),
        out_shape=jax.ShapeDtypeStruct((m, n), BF16),
        scratch_shapes=[pltpu.VMEM((tm, d), BF16)],
        compiler_params=_compiler_params(("parallel", "arbitrary")),
        name="norm_proj",
    )(x, g.reshape(1, d), w_bf, col_scale, cosf, sinf)


def _class_perm(tm, r):
    dst = np.arange(tm)
    c, u = dst // (tm // r), dst % (tm // r)
    perm = np.zeros((tm, tm), np.float32)
    perm[dst, u * r + c] = 1.0
    return perm


def _to_class_kernel(p_ref, *refs, r):
    n = len(refs) // 2
    for x_ref, o_ref in zip(refs[:n], refs[n:]):
        y = _dot(p_ref[...], x_ref[...]).astype(o_ref.dtype)
        rows = y.shape[0] // r
        for c in range(r):
            o_ref[c] = y[c * rows:(c + 1) * rows, :]


def _to_class_order(x, slabs, r, *, batch, seq):
    tm = min(PROJ_TM, seq)
    nblk = seq // tm
    perm = jnp.asarray(_class_perm(tm, r), x.dtype)
    return pl.pallas_call(
        functools.partial(_to_class_kernel, r=r),
        grid=(batch * nblk,),
        in_specs=[pl.BlockSpec((tm, tm), lambda i: (0, 0))]
        + [pl.BlockSpec((tm, w), lambda i, cb=cb: (i, cb)) for cb, w in slabs],
        out_specs=[pl.BlockSpec((None, r, tm // r, w), lambda i: (i // nblk, 0, i % nblk, 0))
                   for _, w in slabs],
        out_shape=[jax.ShapeDtypeStruct((batch, r, seq // r, w), x.dtype) for _, w in slabs],
        compiler_params=_compiler_params(("parallel",)),
        name="to_class_order",
    )(perm, *([x] * len(slabs)))


def _from_class_kernel(pt_ref, x_ref, o_ref, *, r):
    x = jnp.concatenate([x_ref[c] for c in range(r)], axis=0)
    o_ref[...] = _dot(pt_ref[...], x).astype(o_ref.dtype)


def _from_class_order(xc, *, batch, seq):
    _, r, _, w = xc.shape
    tm = min(PROJ_TM, seq)
    nblk = seq // tm
    perm_t = jnp.asarray(_class_perm(tm, r).T, xc.dtype)
    return pl.pallas_call(
        functools.partial(_from_class_kernel, r=r),
        grid=(batch * nblk,),
        in_specs=[pl.BlockSpec((tm, tm), lambda i: (0, 0)),
                  pl.BlockSpec((None, r, tm // r, w), lambda i: (i // nblk, 0, i % nblk, 0))],
        out_specs=pl.BlockSpec((tm, w), lambda i: (i, 0)),
        out_shape=jax.ShapeDtypeStruct((batch * seq, w), xc.dtype),
        compiler_params=_compiler_params(("parallel",)),
        name="from_class_order",
    )(perm_t, xc)


def _compress_kernel(x_ref, pe_ref, w1_ref, w2_ref, o_ref):
    n_planes, n_rows, _ = x_ref.shape
    ylo = yhi = None
    for l in range(n_planes):
        x = x_ref[l].astype(F32)
        xlo = (x + pe_ref[l:l + 1, :]).astype(BF16)
        xhi = (x + pe_ref[n_planes + l:n_planes + l + 1, :]).astype(BF16)
        dlo = _dot(xlo, w1_ref[l * HEAD_DIM:(l + 1) * HEAD_DIM, :])
        dhi = _dot(xhi, w1_ref[(n_planes + l) * HEAD_DIM:(n_planes + l + 1) * HEAD_DIM, :])
        ylo = dlo if ylo is None else ylo + dlo
        yhi = dhi if yhi is None else yhi + dhi
    hid = ylo + pltpu.roll(yhi, n_rows - 1, 0)
    act = (hid * jax.nn.sigmoid(hid)).astype(BF16)
    o_ref[...] = _dot(act, w2_ref[...]).astype(o_ref.dtype)


def _compress(xc, pe, w1_bf, w2_bf):
    batch, planes, nrow, gd = xc.shape
    ng = gd // HEAD_DIM
    return pl.pallas_call(
        _compress_kernel,
        grid=(batch, ng),
        in_specs=[
            pl.BlockSpec((None, planes, nrow, HEAD_DIM), lambda b, g: (b, 0, 0, g)),
            pl.BlockSpec((CMP_LEN, HEAD_DIM), lambda b, g: (0, 0)),
            pl.BlockSpec((CMP_LEN * HEAD_DIM, CMP_HIDDEN), lambda b, g: (0, 0)),
            pl.BlockSpec((CMP_HIDDEN, HEAD_DIM), lambda b, g: (0, 0)),
        ],
        out_specs=pl.BlockSpec((None, None, nrow, HEAD_DIM), lambda b, g: (b, g, 0, 0)),
        out_shape=jax.ShapeDtypeStruct((batch, ng, nrow, HEAD_DIM), BF16),
        compiler_params=_compiler_params(("parallel", "arbitrary")),
        name="nsa_compress",
    )(xc, pe, w1_bf, w2_bf)


def _block_ranks(score, jrow):
    n_blk = score.shape[0]
    groups = n_blk // SUBLANES
    blocks = [score[SUBLANES * r:SUBLANES * (r + 1), :] for r in range(groups)]
    rows = [jrow[SUBLANES * r:SUBLANES * (r + 1), :] for r in range(groups)]
    ranks = [jnp.zeros(blocks[0].shape, F32) for _ in range(groups)]
    for j in range(n_blk):
        rj = score[j:j + 1, :]
        for r in range(groups):
            if r > j // SUBLANES:
                ahead = rj >= blocks[r]
            elif r < j // SUBLANES:
                ahead = rj > blocks[r]
            else:
                ahead = (rj > blocks[r]) | ((rj == blocks[r]) & (rows[r] > j))
            ranks[r] = ranks[r] + jnp.where(ahead, 1.0, 0.0)
    return jnp.concatenate(ranks, axis=0)


def _nsa_kernel(q_ref, kc_ref, vc_ref, ks_ref, e_ref, vs_ref, kw_ref, vw_ref, gz_ref, gb_ref,
                o_ref, qa_ref, sa_ref, sb_ref, sw_ref, ow_ref, m_ref, l_ref, acc_ref, *, tq, seq):
    hq = HPG_A
    groups = range(N_KV_A)
    n_cmp_rows = kc_ref.shape[1]
    n_slc = seq // SLC_BLK
    qi = pl.program_id(1)
    s0 = qi * tq
    t_row = s0 + lax.broadcasted_iota(jnp.int32, (1, tq), 1)

    def lanes(h):
        return slice(h * tq, (h + 1) * tq)

    def gcols(g):
        return slice(g * HEAD_DIM, (g + 1) * HEAD_DIM)

    def tile_heads(x):
        return jnp.concatenate([x] * hq, axis=1)

    wlen = WIN_A + tq
    ws = pl.multiple_of(jnp.maximum(s0 - WIN_A, 0), LANES)
    c_end = lax.broadcasted_iota(jnp.int32, (n_cmp_rows, 1), 0) * CMP_STRIDE + (CMP_LEN - 1)
    cbias = tile_heads(jnp.where(c_end <= t_row, 0.0, NEG_INF))
    any_cmp = tile_heads(t_row >= CMP_LEN - 1)
    jrow = lax.broadcasted_iota(jnp.int32, (n_slc, 1), 0)
    ccol = lax.broadcasted_iota(jnp.int32, (1, n_cmp_rows), 1)
    lo = (SLC_BLK // CMP_STRIDE) * jrow - (CMP_LEN // CMP_STRIDE - 1)
    hi = (SLC_BLK // CMP_STRIDE) * jrow + (SLC_BLK // CMP_STRIDE - 1)
    mmap = jnp.where((ccol >= lo) & (ccol <= hi), 1.0, 0.0).astype(BF16)
    cur = t_row >> SLC_SHIFT
    forced = (jrow == 0) | (jrow == cur) | (jrow == cur - 1)
    assert n_slc <= LANES and tq == LANES

    q6, sc = [], []
    for g in groups:
        for h in range(hq):
            head = g * hq + h
            qa_ref[g, lanes(h), 0:HEAD_DIM] = q_ref[:, head * HEAD_DIM:(head + 1) * HEAD_DIM]
        q6.append(qa_ref[g, :, 0:HEAD_DIM])
        sc.append(_dot_nt(kc_ref[g], q6[g]) + cbias)
    for g in groups:
        sw_ref[g] = _dot_nt(kw_ref[pl.ds(ws, wlen), gcols(g)], q6[g])

    o_cmp, score = [], []
    for g in groups:
        ec = jnp.exp2(sc[g] - jnp.max(sc[g], axis=0, keepdims=True))
        den = jnp.maximum(jnp.sum(ec, axis=0, keepdims=True), TINY)
        pc = ec * jnp.where(any_cmp, 1.0 / den, 0.0)
        o_cmp.append(_dot_tn(vc_ref[g], pc.astype(BF16)))
        psum = pc[:, lanes(0)]
        for h in range(1, hq):
            psum = psum + pc[:, lanes(h)]
        p1 = psum.astype(BF16)
        r1 = psum - p1.astype(F32)
        p2 = r1.astype(BF16)
        p3 = (r1 - p2.astype(F32)).astype(BF16)
        imp = _dot(mmap, p1) + _dot(mmap, p2) + _dot(mmap, p3)
        score.append(jnp.where(forced, 1e9, jnp.where(jrow <= cur, imp, -1e9)))

    for g in groups:
        rank = _block_ranks(score[g], jrow)
        sel_bias = jnp.where((rank < min(N_SEL, n_slc)) & (jrow <= cur), 0.0, NEG_INF)
        bias_q = jnp.transpose(jnp.concatenate(
            [sel_bias, jnp.zeros((LANES - n_slc, tq), F32)], axis=0)).astype(BF16)
        for h in range(hq):
            qa_ref[g, lanes(h), HEAD_DIM:HEAD_DIM + LANES] = bias_q

    m_ref[...] = jnp.full(m_ref.shape, NEG_INF, F32)
    l_ref[...] = jnp.zeros(l_ref.shape, F32)
    acc_ref[...] = jnp.zeros(acc_ref.shape, F32)

    def scores(g, c, s_ref):
        k0 = pl.multiple_of(c * NSA_KC, NSA_KC)
        k_aug = jnp.concatenate([ks_ref[pl.ds(k0, NSA_KC), gcols(g)],
                                 e_ref[pl.ds(k0, NSA_KC), :]], axis=1)
        s_ref[g] = _dot_nt(k_aug, qa_ref[g])

    def softmax_pv(g, c, s_ref):
        s = s_ref[g]
        m_old = m_ref[g]
        m_new = jnp.maximum(m_old, jnp.max(s, axis=0, keepdims=True))
        alpha = jnp.exp2(m_old - m_new)
        p = jnp.exp2(s - m_new)
        l_ref[g] = alpha * l_ref[g] + jnp.sum(p, axis=0, keepdims=True)
        k0 = pl.multiple_of(c * NSA_KC, NSA_KC)
        pv = _dot_tn(vs_ref[pl.ds(k0, NSA_KC), gcols(g)], p.astype(BF16))
        acc_ref[g] = alpha * acc_ref[g] + pv
        m_ref[g] = m_new

    last = s0 // NSA_KC
    for g in groups:
        scores(g, 0, sa_ref)

    dist = t_row - (ws + lax.broadcasted_iota(jnp.int32, (wlen, 1), 0))
    wbias = tile_heads(jnp.where((dist >= 0) & (dist < WIN_A), 0.0, NEG_INF))
    for g in groups:
        sw = sw_ref[g] + wbias
        ew = jnp.exp2(sw - jnp.max(sw, axis=0, keepdims=True))
        ow_ref[g] = (_dot_tn(vw_ref[pl.ds(ws, wlen), gcols(g)], ew.astype(BF16))
                     * (1.0 / jnp.sum(ew, axis=0, keepdims=True)))

    def chunk_pair(i, carry):
        for g in groups:
            scores(g, 2 * i + 1, sb_ref)
            softmax_pv(g, 2 * i, sa_ref)
        for g in groups:
            scores(g, 2 * i + 2, sa_ref)
            softmax_pv(g, 2 * i + 1, sb_ref)
        return carry

    lax.fori_loop(0, last // 2, chunk_pair, 0)

    def last_chunk(g, s_ref):
        diag = pl.multiple_of(s0 - last * NSA_KC, LANES)
        krow = lax.broadcasted_iota(jnp.int32, (tq, 1), 0)
        lane = lax.broadcasted_iota(jnp.int32, (1, tq), 1)
        causal = tile_heads(jnp.where(krow <= lane, 0.0, NEG_INF))
        s_ref[g, pl.ds(diag, tq), :] = s_ref[g, pl.ds(diag, tq), :] + causal
        softmax_pv(g, last, s_ref)

    @pl.when(last % 2 == 0)
    def _():
        for g in groups:
            last_chunk(g, sa_ref)

    @pl.when(last % 2 == 1)
    def _():
        for g in groups:
            scores(g, last, sb_ref)
            softmax_pv(g, last - 1, sa_ref)
        for g in groups:
            last_chunk(g, sb_ref)

    gates_t = jnp.transpose(jax.nn.sigmoid(gz_ref[...].astype(F32) + gb_ref[...]))
    for g in groups:
        o_slc = acc_ref[g] * (1.0 / l_ref[g])
        for h in range(hq):
            head = g * hq + h
            o_h = (gates_t[3 * head:3 * head + 1, :] * o_cmp[g][:, lanes(h)]
                   + gates_t[3 * head + 1:3 * head + 2, :] * o_slc[:, lanes(h)]
                   + gates_t[3 * head + 2:3 * head + 3, :] * ow_ref[g, :, lanes(h)])
            o_ref[:, head * HEAD_DIM:(head + 1) * HEAD_DIM] = jnp.transpose(o_h).astype(o_ref.dtype)


def _nsa_attention(z, kc, vc, gate_bias, *, batch, seq, units):
    tq = NSA_TQ
    nq = seq // tq
    n_cmp_rows = kc.shape[2]
    ng = N_KV_A
    hl = HPG_A * tq
    gw = ng * HEAD_DIM
    for name in ("k_slc", "v_slc", "k_win", "v_win"):
        assert units[name] % ng == 0
    kern = functools.partial(_nsa_kernel, tq=tq, seq=seq)

    def slab(name):
        return pl.BlockSpec((seq, gw), lambda b, i: (b, units[name] // ng))

    cmp_spec = pl.BlockSpec((None, ng, n_cmp_rows, HEAD_DIM), lambda b, i: (b, 0, 0, 0))

    onehot = jnp.asarray(np.arange(seq)[:, None] // SLC_BLK == np.arange(LANES)[None, :], BF16)
    return pl.pallas_call(
        kern,
        grid=(batch, nq),
        in_specs=[
            pl.BlockSpec((tq, A_Q), lambda b, i: (b * nq + i, 0)),
            cmp_spec,
            cmp_spec,
            slab("k_slc"),
            pl.BlockSpec((seq, LANES), lambda b, i: (0, 0)),
            slab("v_slc"),
            slab("k_win"),
            slab("v_win"),
            pl.BlockSpec((tq, LANES), lambda b, i: (b * nq + i, units["gates"])),
            pl.BlockSpec((1, LANES), lambda b, i: (0, 0)),
        ],
        out_specs=pl.BlockSpec((tq, A_Q), lambda b, i: (b * nq + i, 0)),
        out_shape=jax.ShapeDtypeStruct((batch * seq, A_Q), BF16),
        scratch_shapes=[
            pltpu.VMEM((ng, hl, HEAD_DIM + LANES), BF16),
            pltpu.VMEM((ng, NSA_KC, hl), F32),
            pltpu.VMEM((ng, NSA_KC, hl), F32),
            pltpu.VMEM((ng, WIN_A + tq, hl), F32),
            pltpu.VMEM((ng, HEAD_DIM, hl), F32),
            pltpu.VMEM((ng, 1, hl), F32),
            pltpu.VMEM((ng, 1, hl), F32),
            pltpu.VMEM((ng, HEAD_DIM, hl), F32),
        ],
        compiler_params=_compiler_params(("parallel", "arbitrary")),
        name="nsa_attention",
    )(z, kc, vc, z, onehot, z, z, z, z, gate_bias)


def _band_attn_kernel(*refs, tu, parts):
    q_refs = refs[:len(parts)]
    k_ref, v_ref, o_ref, lse_ref = refs[len(parts):]
    u0 = pl.program_id(2) * tu
    n_seq = k_ref.shape[0]
    windows = []
    for lk, span, stride in parts:
        ks = pl.multiple_of(jnp.clip(u0 - span, 0, n_seq - lk), LANES)
        dist = (u0 + lax.broadcasted_iota(jnp.int32, (tu, 1), 0)
                - (ks + lax.broadcasted_iota(jnp.int32, (1, lk), 1)))
        keep = (dist >= 0) & (dist <= span)
        if stride > 1:
            keep = keep & ((dist & (stride - 1)) == 0)
        windows.append((ks, lk, jnp.where(keep, 0.0, NEG_INF)))
    lane = lax.broadcasted_iota(jnp.int32, (1, LANES), 1)
    lse_tile = jnp.zeros((tu, LANES), F32)

    def head_cols(h):
        return slice(h * HEAD_DIM, (h + 1) * HEAD_DIM)

    scores = [[_dot_nt(q_ref[:, head_cols(h)], k_ref[pl.ds(ks, lk), head_cols(h)]) + bias
               for q_ref, (ks, lk, bias) in zip(q_refs, windows)]
              for h in range(DIL_HEADS)]
    probs, inv_l = [], []
    for h, s_parts in enumerate(scores):
        s = jnp.concatenate(s_parts, axis=1)
        m = jnp.max(s, axis=-1, keepdims=True)
        p = jnp.exp2(s - m)
        l = jnp.sum(p, axis=-1, keepdims=True)
        probs.append(p.astype(BF16))
        inv_l.append(1.0 / l)
        lse_tile = jnp.where(lane == h, m + jnp.log2(l), lse_tile)
    for h in range(DIL_HEADS):
        o, col = None, 0
        for ks, lk, _ in windows:
            pv = _dot(probs[h][:, col:col + lk], v_ref[pl.ds(ks, lk), head_cols(h)])
            o = pv if o is None else o + pv
            col += lk
        o_ref[:, head_cols(h)] = (o * inv_l[h]).astype(o_ref.dtype)
    lse_ref[...] = lse_tile


def _dil_merge_kernel(*refs):
    out_ref = refs[-1]
    calls = list(zip(refs[0:-1:2], refs[1:-1:2]))
    for h in range(DIL_HEADS):
        cols = slice(h * HEAD_DIM, (h + 1) * HEAD_DIM)
        shape = (out_ref.shape[0], HEAD_DIM)
        lses = [jnp.broadcast_to(l_ref[:, h:h + 1], shape) for _, l_ref in calls]
        top = functools.reduce(jnp.maximum, lses)
        ws = [jnp.exp2(lse - top) for lse in lses]
        num = sum(w * o_ref[:, cols].astype(F32) for w, (o_ref, _) in zip(ws, calls))
        out_ref[:, cols] = (num * (1.0 / sum(ws))).astype(out_ref.dtype)


def _dilated_attention(zb, *, batch, seq, units):
    n = zb.shape[1]
    width = DIL_HEADS * HEAD_DIM
    tu = DIL_TQ

    def band_call(r, q_units, parts, operands, k_unit, v_unit):
        n_seq = seq // r

        def z_spec(rows, unit, whole):
            return pl.BlockSpec((None, None, rows, width),
                                lambda b, c, i: (b, c, 0 if whole else i, unit))

        def out_spec(cols):
            return pl.BlockSpec((None, None, tu, cols), lambda b, c, i: (b, c, i, 0))

        return pl.pallas_call(
            functools.partial(_band_attn_kernel, tu=tu, parts=parts),
            grid=(batch, r, n_seq // tu),
            in_specs=[z_spec(tu, u, False) for u in q_units]
            + [z_spec(n_seq, k_unit, True), z_spec(n_seq, v_unit, True)],
            out_specs=[out_spec(width), out_spec(LANES)],
            out_shape=[jax.ShapeDtypeStruct((batch, r, n_seq, width), BF16),
                       jax.ShapeDtypeStruct((batch, r, n_seq, LANES), F32)],
            compiler_params=_compiler_params(("parallel", "parallel", "arbitrary")),
            name="dilated_attention",
        )(*operands)

    def window(n_seq, span):
        assert n_seq % tu == 0 and span % LANES == 0 and tu % LANES == 0
        return min(tu + span, n_seq)

    dense = [(gi, w, r) for gi, (w, r) in enumerate(DIL_CONFIGS) if r <= DIL_DENSE_MAX]
    zv = zb.reshape(batch, 1, seq, n)
    o_d, lse_d = band_call(1, [units["q"] + gi for gi, _, _ in dense],
                           tuple((window(seq, w), w, r) for _, w, r in dense),
                           [zv] * (len(dense) + 2), units["k"], units["v"])
    results = [o_d.reshape(batch * seq, width), lse_d.reshape(batch * seq, LANES)]
    for gi, (w, r) in enumerate(DIL_CONFIGS):
        if r <= DIL_DENSE_MAX:
            continue
        slabs = [(units["q"] + gi, width), (units["k"], width), (units["v"], width)]
        o_c, lse_c = band_call(r, [0], ((window(seq // r, w // r), w // r, 1),),
                               _to_class_order(zb, slabs, r, batch=batch, seq=seq), 0, 0)
        results += [_from_class_order(o_c, batch=batch, seq=seq),
                    jnp.transpose(lse_c, (0, 2, 1, 3)).reshape(batch * seq, LANES)]

    m = batch * seq
    tm = min(OUT_TM, m)
    o_spec = pl.BlockSpec((tm, width), lambda i: (i, 0))
    l_spec = pl.BlockSpec((tm, LANES), lambda i: (i, 0))
    return pl.pallas_call(
        _dil_merge_kernel,
        grid=(m // tm,),
        in_specs=[o_spec, l_spec] * (len(results) // 2),
        out_specs=o_spec,
        out_shape=jax.ShapeDtypeStruct((m, width), BF16),
        compiler_params=_compiler_params(("parallel",)),
        name="dilated_merge",
    )(*results)


def _mem_attn_kernel(q_ref, kv_ref, o_ref, *, q_offset):
    def cols(h, base=0):
        return slice(base + h * HEAD_DIM, base + (h + 1) * HEAD_DIM)

    heads = range(N_MEM_HEADS)
    scores = [_dot_nt(q_ref[:, cols(h, q_offset)], kv_ref[:, cols(h)]) for h in heads]
    probs = []
    for s in scores:
        e = jnp.exp(s - jnp.max(s, axis=-1, keepdims=True))
        probs.append((e / jnp.sum(e, axis=-1, keepdims=True)).astype(BF16))
    for h in heads:
        o_ref[:, cols(h)] = _dot(probs[h], kv_ref[:, cols(h, MEM_Q)]).astype(o_ref.dtype)


def _memory_attention(z, mkv, *, batch, seq, q_col, block_width):
    tq = MEM_TQ
    nq = seq // tq
    n_mem = mkv.shape[0] // batch
    q_block, q_offset = divmod(q_col, block_width)
    assert q_offset + MEM_Q <= block_width
    return pl.pallas_call(
        functools.partial(_mem_attn_kernel, q_offset=q_offset),
        grid=(batch, nq),
        in_specs=[
            pl.BlockSpec((tq, block_width), lambda b, i: (b * nq + i, q_block)),
            pl.BlockSpec((n_mem, 2 * MEM_Q), lambda b, i: (b, 0)),
        ],
        out_specs=pl.BlockSpec((tq, MEM_Q), lambda b, i: (b * nq + i, 0)),
        out_shape=jax.ShapeDtypeStruct((batch * seq, MEM_Q), BF16),
        compiler_params=_compiler_params(("parallel", "arbitrary")),
        name="memory_attention",
    )(z, mkv)


def _out_proj_kernel(a1_ref, a2_ref, w_ref, h_ref, o_ref):
    a = jnp.concatenate([a1_ref[...], a2_ref[...]], axis=1)
    o_ref[...] = h_ref[...] + _dot(a, w_ref[...])


def _out_proj(a1, a2, w_bf, h):
    m, d = h.shape
    tm, tn = min(OUT_TM, m), OUT_TN
    k1, k2 = a1.shape[1], a2.shape[1]
    assert w_bf.shape[0] == k1 + k2
    return pl.pallas_call(
        _out_proj_kernel,
        grid=(m // tm, d // tn),
        in_specs=[
            pl.BlockSpec((tm, k1), lambda i, j: (i, 0)),
            pl.BlockSpec((tm, k2), lambda i, j: (i, 0)),
            pl.BlockSpec((k1 + k2, tn), lambda i, j: (0, j)),
            pl.BlockSpec((tm, tn), lambda i, j: (i, j)),
        ],
        out_specs=pl.BlockSpec((tm, tn), lambda i, j: (i, j)),
        out_shape=jax.ShapeDtypeStruct((m, d), F32),
        compiler_params=_compiler_params(("parallel", "arbitrary")),
        name="out_proj",
    )(a1, a2, w_bf, h)


def _ffn_up_kernel(x_ref, g_ref, wg_ref, wu_ref, o_ref, xn_ref):
    @pl.when(pl.program_id(1) == 0)
    def _():
        xn_ref[...] = _rms_rows(x_ref[...], g_ref[...]).astype(BF16)

    for r0 in range(0, xn_ref.shape[0], PROJ_DOT_ROWS):
        rows = slice(r0, r0 + PROJ_DOT_ROWS)
        gate = _dot(xn_ref[rows, :], wg_ref[...])
        up = _dot(xn_ref[rows, :], wu_ref[...])
        o_ref[rows, :] = (gate * jax.nn.sigmoid(gate) * up).astype(o_ref.dtype)


def _ffn_down_kernel(a_ref, w_ref, h_ref, fg_ref, o_ref, *, final_norm):
    for r0 in range(0, a_ref.shape[0], PROJ_DOT_ROWS):
        rows = slice(r0, r0 + PROJ_DOT_ROWS)
        y = h_ref[rows, :] + _dot(a_ref[rows, :], w_ref[...])
        if final_norm:
            y = _rms_rows(y, fg_ref[...])
        o_ref[rows, :] = y


def _ffn(h, g, wg_bf, wu_bf, wd_bf, final_gain, *, final_norm):
    m, d = h.shape
    dff = wg_bf.shape[1]
    tm, tf = min(FFN_UP_TM, m), FFN_TF
    assert m % tm == 0 and dff % tf == 0
    act = pl.pallas_call(
        _ffn_up_kernel,
        grid=(m // tm, dff // tf),
        in_specs=[
            pl.BlockSpec((tm, d), lambda i, f: (i, 0)),
            pl.BlockSpec((1, d), lambda i, f: (0, 0)),
            pl.BlockSpec((d, tf), lambda i, f: (0, f)),
            pl.BlockSpec((d, tf), lambda i, f: (0, f)),
        ],
        out_specs=pl.BlockSpec((tm, tf), lambda i, f: (i, f)),
        out_shape=jax.ShapeDtypeStruct((m, dff), BF16),
        scratch_shapes=[pltpu.VMEM((tm, d), BF16)],
        compiler_params=_compiler_params(("parallel", "arbitrary")),
        name="ffn_up",
    )(h, g.reshape(1, d), wg_bf, wu_bf)

    tm = min(FFN_TM, m)
    kern = functools.partial(_ffn_down_kernel, final_norm=final_norm)
    return pl.pallas_call(
        kern,
        grid=(m // tm,),
        in_specs=[
            pl.BlockSpec((tm, dff), lambda i: (i, 0)),
            pl.BlockSpec((dff, d), lambda i: (0, 0), pipeline_mode=pl.Buffered(1)),
            pl.BlockSpec((tm, d), lambda i: (i, 0)),
            pl.BlockSpec((1, d), lambda i: (0, 0)),
        ],
        out_specs=pl.BlockSpec((tm, d), lambda i: (i, 0)),
        out_shape=jax.ShapeDtypeStruct((m, d), F32),
        compiler_params=_compiler_params(("parallel",)),
        name="ffn_down",
    )(act, wd_bf, h, final_gain.reshape(1, d))


def _rope_tables(seq):
    inv = 1.0 / (ROPE_THETA ** (jnp.arange(0, HEAD_DIM, 2, dtype=F32) / HEAD_DIM))
    ang = jnp.arange(seq, dtype=F32)[:, None] * inv[None, :]
    cos, sin = jnp.cos(ang), jnp.sin(ang)
    return jnp.concatenate([cos, cos], axis=1), jnp.concatenate([-sin, sin], axis=1)


A_UNITS = {"q": 0, "k_cmp": 12, "k_slc": 14, "k_win": 16, "v_cmp": 18, "v_slc": 20,
           "v_win": 22, "mem_q": 24, "gates": 28}
B_UNITS = {"q": 0, "k": 3, "mem_q": 4, "v": 5}
B_TN = 2 * DIL_HEADS * HEAD_DIM
A_TN = 6 * HEAD_DIM
A_NPAD = 30 * HEAD_DIM
A_ROPE_BLOCKS = 3


def _a_weight_kernel(w_ref, o_ref):
    kv_w = N_KV_A * HEAD_DIM
    gate0 = A_Q + A_KV
    mem0 = gate0 + A_GATE

    def put(unit, src0, width):
        dst0 = A_UNITS[unit] * HEAD_DIM
        o_ref[:, dst0:dst0 + width] = w_ref[:, src0:src0 + width].astype(o_ref.dtype)

    put("q", 0, A_Q)
    for branch, unit in enumerate(("k_cmp", "v_cmp", "k_slc", "v_slc", "k_win", "v_win")):
        put(unit, A_Q + branch * kv_w, kv_w)
    put("mem_q", mem0, MEM_Q)
    pad0 = A_UNITS["gates"] * HEAD_DIM
    o_ref[:, pad0:] = jnp.zeros((o_ref.shape[0], o_ref.shape[1] - pad0), o_ref.dtype)
    put("gates", gate0, A_GATE)


def _layer_a_weight(w_in):
    d, n_in = w_in.shape
    rows = WEIGHT_PREP_ROWS
    assert d % rows == 0
    w = pl.pallas_call(
        _a_weight_kernel,
        grid=(d // rows,),
        in_specs=[pl.BlockSpec((rows, n_in), lambda i: (i, 0))],
        out_specs=pl.BlockSpec((rows, A_NPAD), lambda i: (i, 0)),
        out_shape=jax.ShapeDtypeStruct((d, A_NPAD), BF16),
        compiler_params=_compiler_params(("parallel",)),
        name="layer_a_weight",
    )(w_in)
    scale = np.ones((1, A_NPAD), np.float32)
    scale[0, :A_Q] = SCALE * LOG2E
    scale[0, A_UNITS["mem_q"] * HEAD_DIM:A_UNITS["mem_q"] * HEAD_DIM + MEM_Q] = SCALE
    return w, jnp.asarray(scale)


def _layer_a(h, mem, cosf, sinf, p, *, batch, seq):
    w_in_bf, col_scale = _layer_a_weight(p["w_in"])
    z = _norm_proj(h, p["norm_attn"], w_in_bf, col_scale, cosf, sinf,
                   tn=A_TN, n_rope_blocks=A_ROPE_BLOCKS, seq=seq)

    gw = N_KV_A * HEAD_DIM
    k_raw, v_raw = _to_class_order(
        z, [(A_UNITS["k_cmp"] * HEAD_DIM // gw, gw), (A_UNITS["v_cmp"] * HEAD_DIM // gw, gw)],
        CMP_STRIDE, batch=batch, seq=seq)
    kc = _compress(k_raw, p["cmp_pe_k"], p["cmp_w1_k"].astype(BF16), p["cmp_w2_k"].astype(BF16))
    vc = _compress(v_raw, p["cmp_pe_v"], p["cmp_w1_v"].astype(BF16), p["cmp_w2_v"].astype(BF16))
    gb = jnp.pad(p["gate_bias"], (0, LANES - A_GATE)).reshape(1, LANES)
    o_nsa = _nsa_attention(z, kc, vc, gb, batch=batch, seq=seq, units=A_UNITS)

    mkv = _mem_kv(mem, p["norm_mem"], p["w_mem_kv"])
    o_mem = _memory_attention(z, mkv, batch=batch, seq=seq,
                              q_col=A_UNITS["mem_q"] * HEAD_DIM, block_width=MEM_Q)
    return _out_proj(o_nsa, o_mem, p["w_out"].astype(BF16), h)


def _mem_kv(mem, norm_mem, w_mem_kv):
    b, m, d = mem.shape
    ones = jnp.ones((1, w_mem_kv.shape[1]), F32)
    dummy = jnp.zeros((m, HEAD_DIM), F32)
    return _norm_proj(mem.reshape(b * m, d), norm_mem, w_mem_kv.astype(BF16), ones, dummy, dummy,
                      tn=MEM_Q, n_rope_blocks=0, seq=m, tm=m)


def kernel(x, mem, a_norm_attn, a_w_in, a_gate_bias, a_cmp_pe_k, a_cmp_w1_k, a_cmp_w2_k, a_cmp_pe_v, a_cmp_w1_v, a_cmp_w2_v, a_norm_mem, a_w_mem_kv, a_w_out, a_norm_ffn, a_w_gate, a_w_up, a_w_down, kv_norm, w_kv_shared, b_norm_attn, b_w_in, b_norm_mem, b_w_mem_kv, b_w_out, b_norm_ffn, b_w_gate, b_w_up, b_w_down, final_norm):
    batch, seq, d = x.shape
    n_a = a_w_in.shape[0]
    n_b = b_w_in.shape[0]
    cosf, sinf = _rope_tables(seq)
    h = x.reshape(batch * seq, d)
    unit_gain = jnp.ones((d,), F32)

    for l in range(n_a):
        p = {"norm_attn": a_norm_attn[l], "w_in": a_w_in[l], "gate_bias": a_gate_bias[l],
             "cmp_pe_k": a_cmp_pe_k[l], "cmp_w1_k": a_cmp_w1_k[l], "cmp_w2_k": a_cmp_w2_k[l],
             "cmp_pe_v": a_cmp_pe_v[l], "cmp_w1_v": a_cmp_w1_v[l], "cmp_w2_v": a_cmp_w2_v[l],
             "norm_mem": a_norm_mem[l], "w_mem_kv": a_w_mem_kv[l], "w_out": a_w_out[l]}
        h = _layer_a(h, mem, cosf, sinf, p, batch=batch, seq=seq)
        last = (l == n_a - 1) and n_b == 0
        h = _ffn(h, a_norm_ffn[l], a_w_gate[l].astype(BF16), a_w_up[l].astype(BF16),
                 a_w_down[l].astype(BF16), final_norm if last else unit_gain, final_norm=last)

    if n_b > 0:
        assert n_b == 1, "the shared K/V projection is fused into the single mixer-B layer"
        n_kv_half = w_kv_shared.shape[1] // 2
        for l in range(n_b):
            w_q = b_norm_attn[l][:, None] * b_w_in[l]
            w_kv = kv_norm[:, None] * w_kv_shared
            w_cat = jnp.concatenate([w_q[:, :B_Q], w_kv[:, :n_kv_half], w_q[:, B_Q:],
                                     w_kv[:, n_kv_half:]], axis=1).astype(BF16)
            b_scale = np.ones((1, w_cat.shape[1]), np.float32)
            b_scale[0, :B_Q] = SCALE * LOG2E
            b_scale[0, B_Q + n_kv_half:B_Q + n_kv_half + MEM_Q] = SCALE
            zb = _norm_proj(h, unit_gain, w_cat, jnp.asarray(b_scale), cosf, sinf,
                            tn=B_TN, n_rope_blocks=(B_Q + n_kv_half) // B_TN, seq=seq)
            o_dil = _dilated_attention(zb, batch=batch, seq=seq, units=B_UNITS)
            mkv = _mem_kv(mem, b_norm_mem[l], b_w_mem_kv[l])
            o_mem = _memory_attention(zb, mkv, batch=batch, seq=seq,
                                      q_col=B_UNITS["mem_q"] * MEM_Q, block_width=MEM_Q)
            h = _out_proj(o_dil, o_mem, b_w_out[l].astype(BF16), h)
            last = l == n_b - 1
            h = _ffn(h, b_norm_ffn[l], b_w_gate[l].astype(BF16), b_w_up[l].astype(BF16),
                     b_w_down[l].astype(BF16), final_norm if last else unit_gain, final_norm=last)

    return h.reshape(batch, seq, d)
```

```python
import functools
import math

import numpy as np
import jax
import jax.numpy as jnp
from jax import lax
from jax.experimental import pallas as pl
from jax.experimental.pallas import tpu as pltpu

F32 = jnp.float32
BF16 = jnp.bfloat16

HEAD_DIM = 128
N_HEADS_A = 12
N_KV_A = 2
HPG_A = N_HEADS_A // N_KV_A
CMP_LEN = 32
CMP_STRIDE = 16
CMP_HIDDEN = 256
SLC_BLK = 64
SLC_SHIFT = SLC_BLK.bit_length() - 1
N_SEL = 16
WIN_A = 512
DIL_CONFIGS = ((128, 1), (512, 4), (2048, 16))
N_DIL_GROUPS = len(DIL_CONFIGS)
DIL_HEADS = 4
N_MEM_HEADS = 4
ROPE_THETA = 10000.0
EPS = 1e-6
NEG_INF = -1e30
TINY = 1e-30
SCALE = HEAD_DIM ** -0.5
LOG2E = math.log2(math.e)

A_Q = N_HEADS_A * HEAD_DIM
A_KV = 6 * N_KV_A * HEAD_DIM
A_GATE = 3 * N_HEADS_A
MEM_Q = N_MEM_HEADS * HEAD_DIM
B_Q = N_DIL_GROUPS * DIL_HEADS * HEAD_DIM

LANES = 128
SUBLANES = 8
VMEM_LIMIT_BYTES = 56 * 1024 * 1024

PROJ_TM = 1024
RELAYOUT_TM = 512
FFN_UP_TM = 1024
FFN_TM = 512
FFN_TF = 512
OUT_TM = 1024
OUT_TN = 1024
NSA_TQ = 128
NSA_KC = 512
DIL_TQ = 256
DIL_DENSE_MAX = 4
MEM_TQ = 1024

NT_DIMS = (((1,), (1,)), ((), ()))
TN_DIMS = (((0,), (0,)), ((), ()))


def _compiler_params(semantics):
    return pltpu.CompilerParams(dimension_semantics=semantics,
                                vmem_limit_bytes=VMEM_LIMIT_BYTES)


def _rms_rows(x, g):
    ms = jnp.mean(x * x, axis=-1, keepdims=True)
    return x * lax.rsqrt(ms + EPS) * g


def _dot(a, b):
    return jnp.dot(a, b, preferred_element_type=F32)


def _dot_nt(a, b):
    return lax.dot_general(a, b, NT_DIMS, preferred_element_type=F32)


def _dot_tn(a, b):
    return lax.dot_general(a, b, TN_DIMS, preferred_element_type=F32)


def _norm_proj_kernel(x_ref, g_ref, w_ref, cs_ref, cos_ref, sin_ref, o_ref, xn_ref, *,
                      n_rope_blocks, tn):
    j = pl.program_id(1)

    @pl.when(j == 0)
    def _():
        xn_ref[...] = _rms_rows(x_ref[...], g_ref[...]).astype(BF16)

    acc = _dot(xn_ref[...], w_ref[...]) * cs_ref[...]

    if n_rope_blocks > 0:
        roped = j < n_rope_blocks
        c = jnp.where(roped, cos_ref[...], 1.0)
        s = jnp.where(roped, sin_ref[...], 0.0)
        for h in range(tn // HEAD_DIM):
            y = acc[:, h * HEAD_DIM:(h + 1) * HEAD_DIM]
            rot = pltpu.roll(y, HEAD_DIM // 2, 1)
            o_ref[:, h * HEAD_DIM:(h + 1) * HEAD_DIM] = (y * c + rot * s).astype(o_ref.dtype)
    else:
        o_ref[...] = acc.astype(o_ref.dtype)


def _norm_proj(x, g, w_bf, col_scale, cosf, sinf, *, tn, n_rope_blocks, seq, tm=PROJ_TM):
    m, d = x.shape
    n = w_bf.shape[1]
    tm = min(tm, m)
    assert m % tm == 0 and n % tn == 0 and seq % tm == 0
    pos_blocks = seq // tm
    kern = functools.partial(_norm_proj_kernel, n_rope_blocks=n_rope_blocks, tn=tn)
    return pl.pallas_call(
        kern,
        grid=(m // tm, n // tn),
        in_specs=[
            pl.BlockSpec((tm, d), lambda i, j: (i, 0)),
            pl.BlockSpec((1, d), lambda i, j: (0, 0)),
            pl.BlockSpec((d, tn), lambda i, j: (0, j)),
            pl.BlockSpec((1, tn), lambda i, j: (0, j)),
            pl.BlockSpec((tm, HEAD_DIM), lambda i, j: (i % pos_blocks, 0)),
            pl.BlockSpec((tm, HEAD_DIM), lambda i, j: (i % pos_blocks, 0)),
        ],
        out_specs=pl.BlockSpec((tm, tn), lambda i, j: (i, j)),
        out_shape=jax.ShapeDtypeStruct((m, n), BF16),
        scratch_shapes=[pltpu.VMEM((tm, d), BF16)],
        compiler_params=_compiler_params(("parallel", "arbitrary")),
        name="norm_proj",
    )(x, g.reshape(1, d), w_bf, col_scale, cosf, sinf)


def _class_perm(tm, r):
    dst = np.arange(tm)
    c, u = dst // (tm // r), dst % (tm // r)
    perm = np.zeros((tm, tm), np.float32)
    perm[dst, u * r + c] = 1.0
    return perm


def _to_class_kernel(p_ref, *refs, r):
    n = len(refs) // 2
    for x_ref, o_ref in zip(refs[:n], refs[n:]):
        y = _dot(p_ref[...], x_ref[...]).astype(o_ref.dtype)
        rows = y.shape[0] // r
        for c in range(r):
            o_ref[c] = y[c * rows:(c + 1) * rows, :]


def _to_class_order(x, slabs, r, *, batch, seq):
    tm = min(RELAYOUT_TM, seq)
    nblk = seq // tm
    perm = jnp.asarray(_class_perm(tm, r), x.dtype)
    return pl.pallas_call(
        functools.partial(_to_class_kernel, r=r),
        grid=(batch * nblk,),
        in_specs=[pl.BlockSpec((tm, tm), lambda i: (0, 0))]
        + [pl.BlockSpec((tm, w), lambda i, cb=cb: (i, cb)) for cb, w in slabs],
        out_specs=[pl.BlockSpec((None, r, tm // r, w), lambda i: (i // nblk, 0, i % nblk, 0))
                   for _, w in slabs],
        out_shape=[jax.ShapeDtypeStruct((batch, r, seq // r, w), x.dtype) for _, w in slabs],
        compiler_params=_compiler_params(("parallel",)),
        name="to_class_order",
    )(perm, *([x] * len(slabs)))


def _from_class_kernel(pt_ref, x_ref, o_ref, *, r):
    x = jnp.concatenate([x_ref[c] for c in range(r)], axis=0)
    o_ref[...] = _dot(pt_ref[...], x).astype(o_ref.dtype)


def _from_class_order(xc, *, batch, seq):
    _, r, _, w = xc.shape
    tm = min(RELAYOUT_TM, seq)
    nblk = seq // tm
    perm_t = jnp.asarray(_class_perm(tm, r).T, xc.dtype)
    return pl.pallas_call(
        functools.partial(_from_class_kernel, r=r),
        grid=(batch * nblk,),
        in_specs=[pl.BlockSpec((tm, tm), lambda i: (0, 0)),
                  pl.BlockSpec((None, r, tm // r, w), lambda i: (i // nblk, 0, i % nblk, 0))],
        out_specs=pl.BlockSpec((tm, w), lambda i: (i, 0)),
        out_shape=jax.ShapeDtypeStruct((batch * seq, w), xc.dtype),
        compiler_params=_compiler_params(("parallel",)),
        name="from_class_order",
    )(perm_t, xc)


def _compress_kernel(x_ref, pe_ref, w1_ref, w2_ref, o_ref):
    n_planes, n_rows, _ = x_ref.shape
    ylo = yhi = None
    for l in range(n_planes):
        x = x_ref[l].astype(F32)
        xlo = (x + pe_ref[l:l + 1, :]).astype(BF16)
        xhi = (x + pe_ref[n_planes + l:n_planes + l + 1, :]).astype(BF16)
        dlo = _dot(xlo, w1_ref[l * HEAD_DIM:(l + 1) * HEAD_DIM, :])
        dhi = _dot(xhi, w1_ref[(n_planes + l) * HEAD_DIM:(n_planes + l + 1) * HEAD_DIM, :])
        ylo = dlo if ylo is None else ylo + dlo
        yhi = dhi if yhi is None else yhi + dhi
    hid = ylo + pltpu.roll(yhi, n_rows - 1, 0)
    act = (hid * jax.nn.sigmoid(hid)).astype(BF16)
    o_ref[...] = _dot(act, w2_ref[...]).astype(o_ref.dtype)


def _compress(xc, pe, w1_bf, w2_bf):
    batch, planes, nrow, gd = xc.shape
    ng = gd // HEAD_DIM
    return pl.pallas_call(
        _compress_kernel,
        grid=(batch, ng),
        in_specs=[
            pl.BlockSpec((None, planes, nrow, HEAD_DIM), lambda b, g: (b, 0, 0, g)),
            pl.BlockSpec((CMP_LEN, HEAD_DIM), lambda b, g: (0, 0)),
            pl.BlockSpec((CMP_LEN * HEAD_DIM, CMP_HIDDEN), lambda b, g: (0, 0)),
            pl.BlockSpec((CMP_HIDDEN, HEAD_DIM), lambda b, g: (0, 0)),
        ],
        out_specs=pl.BlockSpec((None, None, nrow, HEAD_DIM), lambda b, g: (b, g, 0, 0)),
        out_shape=jax.ShapeDtypeStruct((batch, ng, nrow, HEAD_DIM), BF16),
        compiler_params=_compiler_params(("parallel", "arbitrary")),
        name="nsa_compress",
    )(xc, pe, w1_bf, w2_bf)


def _block_ranks(score, jrow):
    n_blk = score.shape[0]
    groups = n_blk // SUBLANES
    blocks = [score[SUBLANES * r:SUBLANES * (r + 1), :] for r in range(groups)]
    rows = [jrow[SUBLANES * r:SUBLANES * (r + 1), :] for r in range(groups)]
    ranks = [jnp.zeros(blocks[0].shape, F32) for _ in range(groups)]
    for j in range(n_blk):
        rj = score[j:j + 1, :]
        for r in range(groups):
            if r > j // SUBLANES:
                ahead = rj >= blocks[r]
            elif r < j // SUBLANES:
                ahead = rj > blocks[r]
            else:
                ahead = (rj > blocks[r]) | ((rj == blocks[r]) & (rows[r] > j))
            ranks[r] = ranks[r] + jnp.where(ahead, 1.0, 0.0)
    return jnp.concatenate(ranks, axis=0)


def _nsa_kernel(q_ref, kc_ref, vc_ref, ks_ref, e_ref, vs_ref, kw_ref, vw_ref, gz_ref, gb_ref,
                o_ref, qa_ref, sa_ref, sb_ref, sw_ref, ow_ref, m_ref, l_ref, acc_ref, *, tq, seq):
    hq = HPG_A
    groups = range(N_KV_A)
    n_cmp_rows = kc_ref.shape[1]
    n_slc = seq // SLC_BLK
    qi = pl.program_id(1)
    s0 = qi * tq
    t_row = s0 + lax.broadcasted_iota(jnp.int32, (1, tq), 1)

    def lanes(h):
        return slice(h * tq, (h + 1) * tq)

    def gcols(g):
        return slice(g * HEAD_DIM, (g + 1) * HEAD_DIM)

    def tile_heads(x):
        return jnp.concatenate([x] * hq, axis=1)

    wlen = WIN_A + tq
    ws = pl.multiple_of(jnp.maximum(s0 - WIN_A, 0), LANES)
    c_end = lax.broadcasted_iota(jnp.int32, (n_cmp_rows, 1), 0) * CMP_STRIDE + (CMP_LEN - 1)
    cbias = tile_heads(jnp.where(c_end <= t_row, 0.0, NEG_INF))
    any_cmp = tile_heads(t_row >= CMP_LEN - 1)
    jrow = lax.broadcasted_iota(jnp.int32, (n_slc, 1), 0)
    ccol = lax.broadcasted_iota(jnp.int32, (1, n_cmp_rows), 1)
    lo = (SLC_BLK // CMP_STRIDE) * jrow - (CMP_LEN // CMP_STRIDE - 1)
    hi = (SLC_BLK // CMP_STRIDE) * jrow + (SLC_BLK // CMP_STRIDE - 1)
    mmap = jnp.where((ccol >= lo) & (ccol <= hi), 1.0, 0.0).astype(BF16)
    cur = t_row >> SLC_SHIFT
    forced = (jrow == 0) | (jrow == cur) | (jrow == cur - 1)
    assert n_slc <= LANES and tq == LANES

    q6, sc = [], []
    for g in groups:
        for h in range(hq):
            head = g * hq + h
            qa_ref[g, lanes(h), 0:HEAD_DIM] = q_ref[:, head * HEAD_DIM:(head + 1) * HEAD_DIM]
        q6.append(qa_ref[g, :, 0:HEAD_DIM])
        sc.append(_dot_nt(kc_ref[g], q6[g]) + cbias)
    for g in groups:
        sw_ref[g] = _dot_nt(kw_ref[pl.ds(ws, wlen), gcols(g)], q6[g])

    o_cmp, score = [], []
    for g in groups:
        ec = jnp.exp2(sc[g] - jnp.max(sc[g], axis=0, keepdims=True))
        den = jnp.maximum(jnp.sum(ec, axis=0, keepdims=True), TINY)
        pc = ec * jnp.where(any_cmp, 1.0 / den, 0.0)
        o_cmp.append(_dot_tn(vc_ref[g], pc.astype(BF16)))
        psum = pc[:, lanes(0)]
        for h in range(1, hq):
            psum = psum + pc[:, lanes(h)]
        p1 = psum.astype(BF16)
        r1 = psum - p1.astype(F32)
        p2 = r1.astype(BF16)
        p3 = (r1 - p2.astype(F32)).astype(BF16)
        imp = _dot(mmap, p1) + _dot(mmap, p2) + _dot(mmap, p3)
        score.append(jnp.where(forced, 1e9, jnp.where(jrow <= cur, imp, -1e9)))

    for g in groups:
        rank = _block_ranks(score[g], jrow)
        sel_bias = jnp.where((rank < min(N_SEL, n_slc)) & (jrow <= cur), 0.0, NEG_INF)
        bias_q = jnp.transpose(jnp.concatenate(
            [sel_bias, jnp.zeros((LANES - n_slc, tq), F32)], axis=0)).astype(BF16)
        for h in range(hq):
            qa_ref[g, lanes(h), HEAD_DIM:HEAD_DIM + LANES] = bias_q

    m_ref[...] = jnp.full(m_ref.shape, NEG_INF, F32)
    l_ref[...] = jnp.zeros(l_ref.shape, F32)
    acc_ref[...] = jnp.zeros(acc_ref.shape, F32)

    def scores(g, c, s_ref):
        k0 = pl.multiple_of(c * NSA_KC, NSA_KC)
        k_aug = jnp.concatenate([ks_ref[pl.ds(k0, NSA_KC), gcols(g)],
                                 e_ref[pl.ds(k0, NSA_KC), :]], axis=1)
        s_ref[g] = _dot_nt(k_aug, qa_ref[g])

    def softmax_pv(g, c, s_ref):
        s = s_ref[g]
        m_old = m_ref[g]
        m_new = jnp.maximum(m_old, jnp.max(s, axis=0, keepdims=True))
        alpha = jnp.exp2(m_old - m_new)
        p = jnp.exp2(s - m_new)
        l_ref[g] = alpha * l_ref[g] + jnp.sum(p, axis=0, keepdims=True)
        k0 = pl.multiple_of(c * NSA_KC, NSA_KC)
        pv = _dot_tn(vs_ref[pl.ds(k0, NSA_KC), gcols(g)], p.astype(BF16))
        acc_ref[g] = alpha * acc_ref[g] + pv
        m_ref[g] = m_new

    last = s0 // NSA_KC
    for g in groups:
        scores(g, 0, sa_ref)

    dist = t_row - (ws + lax.broadcasted_iota(jnp.int32, (wlen, 1), 0))
    wbias = tile_heads(jnp.where((dist >= 0) & (dist < WIN_A), 0.0, NEG_INF))
    for g in groups:
        sw = sw_ref[g] + wbias
        ew = jnp.exp2(sw - jnp.max(sw, axis=0, keepdims=True))
        ow_ref[g] = (_dot_tn(vw_ref[pl.ds(ws, wlen), gcols(g)], ew.astype(BF16))
                     * (1.0 / jnp.sum(ew, axis=0, keepdims=True)))

    def chunk_pair(i, carry):
        for g in groups:
            scores(g, 2 * i + 1, sb_ref)
            softmax_pv(g, 2 * i, sa_ref)
        for g in groups:
            scores(g, 2 * i + 2, sa_ref)
            softmax_pv(g, 2 * i + 1, sb_ref)
        return carry

    lax.fori_loop(0, last // 2, chunk_pair, 0)

    def last_chunk(g, s_ref):
        diag = pl.multiple_of(s0 - last * NSA_KC, LANES)
        krow = lax.broadcasted_iota(jnp.int32, (tq, 1), 0)
        lane = lax.broadcasted_iota(jnp.int32, (1, tq), 1)
        causal = tile_heads(jnp.where(krow <= lane, 0.0, NEG_INF))
        s_ref[g, pl.ds(diag, tq), :] = s_ref[g, pl.ds(diag, tq), :] + causal
        softmax_pv(g, last, s_ref)

    @pl.when(last % 2 == 0)
    def _():
        for g in groups:
            last_chunk(g, sa_ref)

    @pl.when(last % 2 == 1)
    def _():
        for g in groups:
            scores(g, last, sb_ref)
            softmax_pv(g, last - 1, sa_ref)
        for g in groups:
            last_chunk(g, sb_ref)

    gates_t = jnp.transpose(jax.nn.sigmoid(gz_ref[...].astype(F32) + gb_ref[...]))
    for g in groups:
        o_slc = acc_ref[g] * (1.0 / l_ref[g])
        for h in range(hq):
            head = g * hq + h
            o_h = (gates_t[3 * head:3 * head + 1, :] * o_cmp[g][:, lanes(h)]
                   + gates_t[3 * head + 1:3 * head + 2, :] * o_slc[:, lanes(h)]
                   + gates_t[3 * head + 2:3 * head + 3, :] * ow_ref[g, :, lanes(h)])
            o_ref[:, head * HEAD_DIM:(head + 1) * HEAD_DIM] = jnp.transpose(o_h).astype(o_ref.dtype)


def _nsa_attention(z, kc, vc, gate_bias, *, batch, seq, units):
    tq = NSA_TQ
    nq = seq // tq
    n_cmp_rows = kc.shape[2]
    ng = N_KV_A
    hl = HPG_A * tq
    gw = ng * HEAD_DIM
    for name in ("k_slc", "v_slc", "k_win", "v_win"):
        assert units[name] % ng == 0
    kern = functools.partial(_nsa_kernel, tq=tq, seq=seq)

    def slab(name):
        return pl.BlockSpec((seq, gw), lambda b, i: (b, units[name] // ng))

    cmp_spec = pl.BlockSpec((None, ng, n_cmp_rows, HEAD_DIM), lambda b, i: (b, 0, 0, 0))

    onehot = jnp.asarray(np.arange(seq)[:, None] // SLC_BLK == np.arange(LANES)[None, :], BF16)
    return pl.pallas_call(
        kern,
        grid=(batch, nq),
        in_specs=[
            pl.BlockSpec((tq, A_Q), lambda b, i: (b * nq + i, 0)),
            cmp_spec,
            cmp_spec,
            slab("k_slc"),
            pl.BlockSpec((seq, LANES), lambda b, i: (0, 0)),
            slab("v_slc"),
            slab("k_win"),
            slab("v_win"),
            pl.BlockSpec((tq, LANES), lambda b, i: (b * nq + i, units["gates"])),
            pl.BlockSpec((1, LANES), lambda b, i: (0, 0)),
        ],
        out_specs=pl.BlockSpec((tq, A_Q), lambda b, i: (b * nq + i, 0)),
        out_shape=jax.ShapeDtypeStruct((batch * seq, A_Q), BF16),
        scratch_shapes=[
            pltpu.VMEM((ng, hl, HEAD_DIM + LANES), BF16),
            pltpu.VMEM((ng, NSA_KC, hl), F32),
            pltpu.VMEM((ng, NSA_KC, hl), F32),
            pltpu.VMEM((ng, WIN_A + tq, hl), F32),
            pltpu.VMEM((ng, HEAD_DIM, hl), F32),
            pltpu.VMEM((ng, 1, hl), F32),
            pltpu.VMEM((ng, 1, hl), F32),
            pltpu.VMEM((ng, HEAD_DIM, hl), F32),
        ],
        compiler_params=_compiler_params(("parallel", "arbitrary")),
        name="nsa_attention",
    )(z, kc, vc, z, onehot, z, z, z, z, gate_bias)


def _band_attn_kernel(*refs, tu, parts):
    q_refs = refs[:len(parts)]
    k_ref, v_ref, o_ref, lse_ref = refs[len(parts):]
    u0 = pl.program_id(2) * tu
    n_seq = k_ref.shape[0]
    windows = []
    for lk, span, stride in parts:
        ks = pl.multiple_of(jnp.clip(u0 - span, 0, n_seq - lk), LANES)
        dist = (u0 + lax.broadcasted_iota(jnp.int32, (tu, 1), 0)
                - (ks + lax.broadcasted_iota(jnp.int32, (1, lk), 1)))
        keep = (dist >= 0) & (dist <= span)
        if stride > 1:
            keep = keep & ((dist & (stride - 1)) == 0)
        windows.append((ks, lk, jnp.where(keep, 0.0, NEG_INF)))
    lane = lax.broadcasted_iota(jnp.int32, (1, LANES), 1)
    lse_tile = jnp.zeros((tu, LANES), F32)

    def head_cols(h):
        return slice(h * HEAD_DIM, (h + 1) * HEAD_DIM)

    scores = [[_dot_nt(q_ref[:, head_cols(h)], k_ref[pl.ds(ks, lk), head_cols(h)]) + bias
               for q_ref, (ks, lk, bias) in zip(q_refs, windows)]
              for h in range(DIL_HEADS)]
    probs, inv_l = [], []
    for h, s_parts in enumerate(scores):
        s = jnp.concatenate(s_parts, axis=1)
        m = jnp.max(s, axis=-1, keepdims=True)
        p = jnp.exp2(s - m)
        l = jnp.sum(p, axis=-1, keepdims=True)
        probs.append(p.astype(BF16))
        inv_l.append(1.0 / l)
        lse_tile = jnp.where(lane == h, m + jnp.log2(l), lse_tile)
    for h in range(DIL_HEADS):
        o, col = None, 0
        for ks, lk, _ in windows:
            pv = _dot(probs[h][:, col:col + lk], v_ref[pl.ds(ks, lk), head_cols(h)])
            o = pv if o is None else o + pv
            col += lk
        o_ref[:, head_cols(h)] = (o * inv_l[h]).astype(o_ref.dtype)
    lse_ref[...] = lse_tile


def _dil_merge_kernel(*refs):
    out_ref = refs[-1]
    calls = list(zip(refs[0:-1:2], refs[1:-1:2]))
    for h in range(DIL_HEADS):
        cols = slice(h * HEAD_DIM, (h + 1) * HEAD_DIM)
        shape = (out_ref.shape[0], HEAD_DIM)
        lses = [jnp.broadcast_to(l_ref[:, h:h + 1], shape) for _, l_ref in calls]
        top = functools.reduce(jnp.maximum, lses)
        ws = [jnp.exp2(lse - top) for lse in lses]
        num = sum(w * o_ref[:, cols].astype(F32) for w, (o_ref, _) in zip(ws, calls))
        out_ref[:, cols] = (num * (1.0 / sum(ws))).astype(out_ref.dtype)


def _dilated_attention(zb, *, batch, seq, units):
    n = zb.shape[1]
    width = DIL_HEADS * HEAD_DIM
    tu = DIL_TQ

    def band_call(r, q_units, parts, operands, k_unit, v_unit):
        n_seq = seq // r

        def z_spec(rows, unit, whole):
            return pl.BlockSpec((None, None, rows, width),
                                lambda b, c, i: (b, c, 0 if whole else i, unit))

        def out_spec(cols):
            return pl.BlockSpec((None, None, tu, cols), lambda b, c, i: (b, c, i, 0))

        return pl.pallas_call(
            functools.partial(_band_attn_kernel, tu=tu, parts=parts),
            grid=(batch, r, n_seq // tu),
            in_specs=[z_spec(tu, u, False) for u in q_units]
            + [z_spec(n_seq, k_unit, True), z_spec(n_seq, v_unit, True)],
            out_specs=[out_spec(width), out_spec(LANES)],
            out_shape=[jax.ShapeDtypeStruct((batch, r, n_seq, width), BF16),
                       jax.ShapeDtypeStruct((batch, r, n_seq, LANES), F32)],
            compiler_params=_compiler_params(("parallel", "parallel", "arbitrary")),
            name="dilated_attention",
        )(*operands)

    def window(n_seq, span):
        assert n_seq % tu == 0 and span % LANES == 0 and tu % LANES == 0
        return min(tu + span, n_seq)

    dense = [(gi, w, r) for gi, (w, r) in enumerate(DIL_CONFIGS) if r <= DIL_DENSE_MAX]
    zv = zb.reshape(batch, 1, seq, n)
    o_d, lse_d = band_call(1, [units["q"] + gi for gi, _, _ in dense],
                           tuple((window(seq, w), w, r) for _, w, r in dense),
                           [zv] * (len(dense) + 2), units["k"], units["v"])
    results = [o_d.reshape(batch * seq, width), lse_d.reshape(batch * seq, LANES)]
    for gi, (w, r) in enumerate(DIL_CONFIGS):
        if r <= DIL_DENSE_MAX:
            continue
        slabs = [(units["q"] + gi, width), (units["k"], width), (units["v"], width)]
        o_c, lse_c = band_call(r, [0], ((window(seq // r, w // r), w // r, 1),),
                               _to_class_order(zb, slabs, r, batch=batch, seq=seq), 0, 0)
        results += [_from_class_order(o_c, batch=batch, seq=seq),
                    jnp.transpose(lse_c, (0, 2, 1, 3)).reshape(batch * seq, LANES)]

    m = batch * seq
    tm = min(OUT_TM, m)
    o_spec = pl.BlockSpec((tm, width), lambda i: (i, 0))
    l_spec = pl.BlockSpec((tm, LANES), lambda i: (i, 0))
    return pl.pallas_call(
        _dil_merge_kernel,
        grid=(m // tm,),
        in_specs=[o_spec, l_spec] * (len(results) // 2),
        out_specs=o_spec,
        out_shape=jax.ShapeDtypeStruct((m, width), BF16),
        compiler_params=_compiler_params(("parallel",)),
        name="dilated_merge",
    )(*results)


def _mem_attn_kernel(q_ref, kv_ref, o_ref, *, q_offset):
    def cols(h, base=0):
        return slice(base + h * HEAD_DIM, base + (h + 1) * HEAD_DIM)

    heads = range(N_MEM_HEADS)
    scores = [_dot_nt(q_ref[:, cols(h, q_offset)], kv_ref[:, cols(h)]) for h in heads]
    probs = []
    for s in scores:
        e = jnp.exp(s - jnp.max(s, axis=-1, keepdims=True))
        probs.append((e / jnp.sum(e, axis=-1, keepdims=True)).astype(BF16))
    for h in heads:
        o_ref[:, cols(h)] = _dot(probs[h], kv_ref[:, cols(h, MEM_Q)]).astype(o_ref.dtype)


def _memory_attention(z, mkv, *, batch, seq, q_col, block_width):
    tq = MEM_TQ
    nq = seq // tq
    n_mem = mkv.shape[0] // batch
    q_block, q_offset = divmod(q_col, block_width)
    assert q_offset + MEM_Q <= block_width
    return pl.pallas_call(
        functools.partial(_mem_attn_kernel, q_offset=q_offset),
        grid=(batch, nq),
        in_specs=[
            pl.BlockSpec((tq, block_width), lambda b, i: (b * nq + i, q_block)),
            pl.BlockSpec((n_mem, 2 * MEM_Q), lambda b, i: (b, 0)),
        ],
        out_specs=pl.BlockSpec((tq, MEM_Q), lambda b, i: (b * nq + i, 0)),
        out_shape=jax.ShapeDtypeStruct((batch * seq, MEM_Q), BF16),
        compiler_params=_compiler_params(("parallel", "arbitrary")),
        name="memory_attention",
    )(z, mkv)


def _out_proj_kernel(a1_ref, a2_ref, w_ref, h_ref, o_ref):
    a = jnp.concatenate([a1_ref[...], a2_ref[...]], axis=1)
    o_ref[...] = h_ref[...] + _dot(a, w_ref[...])


def _out_proj(a1, a2, w_bf, h):
    m, d = h.shape
    tm, tn = min(OUT_TM, m), OUT_TN
    k1, k2 = a1.shape[1], a2.shape[1]
    assert w_bf.shape[0] == k1 + k2
    return pl.pallas_call(
        _out_proj_kernel,
        grid=(m // tm, d // tn),
        in_specs=[
            pl.BlockSpec((tm, k1), lambda i, j: (i, 0)),
            pl.BlockSpec((tm, k2), lambda i, j: (i, 0)),
            pl.BlockSpec((k1 + k2, tn), lambda i, j: (0, j)),
            pl.BlockSpec((tm, tn), lambda i, j: (i, j)),
        ],
        out_specs=pl.BlockSpec((tm, tn), lambda i, j: (i, j)),
        out_shape=jax.ShapeDtypeStruct((m, d), F32),
        compiler_params=_compiler_params(("parallel", "arbitrary")),
        name="out_proj",
    )(a1, a2, w_bf, h)


def _ffn_up_kernel(x_ref, g_ref, wg_ref, wu_ref, o_ref, xn_ref):
    @pl.when(pl.program_id(1) == 0)
    def _():
        xn_ref[...] = _rms_rows(x_ref[...], g_ref[...]).astype(BF16)

    xn = xn_ref[...]
    gate = _dot(xn, wg_ref[...])
    up = _dot(xn, wu_ref[...])
    o_ref[...] = (gate * jax.nn.sigmoid(gate) * up).astype(o_ref.dtype)


def _ffn_down_kernel(a_ref, w_ref, h_ref, fg_ref, o_ref, *, final_norm):
    y = h_ref[...] + _dot(a_ref[...], w_ref[...])
    if final_norm:
        y = _rms_rows(y, fg_ref[...])
    o_ref[...] = y


def _ffn(h, g, wg_bf, wu_bf, wd_bf, final_gain, *, final_norm):
    m, d = h.shape
    dff = wg_bf.shape[1]
    tm, tf = min(FFN_UP_TM, m), FFN_TF
    assert m % tm == 0 and dff % tf == 0
    act = pl.pallas_call(
        _ffn_up_kernel,
        grid=(m // tm, dff // tf),
        in_specs=[
            pl.BlockSpec((tm, d), lambda i, f: (i, 0)),
            pl.BlockSpec((1, d), lambda i, f: (0, 0)),
            pl.BlockSpec((d, tf), lambda i, f: (0, f)),
            pl.BlockSpec((d, tf), lambda i, f: (0, f)),
        ],
        out_specs=pl.BlockSpec((tm, tf), lambda i, f: (i, f)),
        out_shape=jax.ShapeDtypeStruct((m, dff), BF16),
        scratch_shapes=[pltpu.VMEM((tm, d), BF16)],
        compiler_params=_compiler_params(("parallel", "arbitrary")),
        name="ffn_up",
    )(h, g.reshape(1, d), wg_bf, wu_bf)

    tm = min(FFN_TM, m)
    kern = functools.partial(_ffn_down_kernel, final_norm=final_norm)
    return pl.pallas_call(
        kern,
        grid=(m // tm,),
        in_specs=[
            pl.BlockSpec((tm, dff), lambda i: (i, 0)),
            pl.BlockSpec((dff, d), lambda i: (0, 0), pipeline_mode=pl.Buffered(1)),
            pl.BlockSpec((tm, d), lambda i: (i, 0)),
            pl.BlockSpec((1, d), lambda i: (0, 0)),
        ],
        out_specs=pl.BlockSpec((tm, d), lambda i: (i, 0)),
        out_shape=jax.ShapeDtypeStruct((m, d), F32),
        compiler_params=_compiler_params(("parallel",)),
        name="ffn_down",
    )(act, wd_bf, h, final_gain.reshape(1, d))


def _rope_tables(seq):
    inv = 1.0 / (ROPE_THETA ** (jnp.arange(0, HEAD_DIM, 2, dtype=F32) / HEAD_DIM))
    ang = jnp.arange(seq, dtype=F32)[:, None] * inv[None, :]
    cos, sin = jnp.cos(ang), jnp.sin(ang)
    return jnp.concatenate([cos, cos], axis=1), jnp.concatenate([-sin, sin], axis=1)


A_UNITS = {"q": 0, "k_cmp": 12, "k_slc": 14, "k_win": 16, "v_cmp": 18, "v_slc": 20,
           "v_win": 22, "mem_q": 24, "gates": 28}
B_UNITS = {"q": 0, "k": 3, "mem_q": 4, "v": 5}
B_TN = 2 * DIL_HEADS * HEAD_DIM
A_TN = 6 * HEAD_DIM
A_NPAD = 30 * HEAD_DIM
A_ROPE_BLOCKS = 3


def _layer_a_weight(w_in):
    kv0 = A_Q

    def kv_cols(branch):
        return w_in[:, kv0 + branch * N_KV_A * HEAD_DIM:kv0 + (branch + 1) * N_KV_A * HEAD_DIM]

    gate0 = A_Q + A_KV
    mem0 = gate0 + A_GATE
    w = jnp.concatenate([w_in[:, :A_Q], kv_cols(0), kv_cols(2), kv_cols(4), kv_cols(1),
                         kv_cols(3), kv_cols(5), w_in[:, mem0:mem0 + MEM_Q],
                         w_in[:, gate0:mem0]], axis=1)
    w = jnp.pad(w, ((0, 0), (0, A_NPAD - w.shape[1])))
    scale = np.ones((1, A_NPAD), np.float32)
    scale[0, :A_Q] = SCALE * LOG2E
    scale[0, A_UNITS["mem_q"] * HEAD_DIM:A_UNITS["mem_q"] * HEAD_DIM + MEM_Q] = SCALE
    return w.astype(BF16), jnp.asarray(scale)


def _layer_a(h, mem, cosf, sinf, p, *, batch, seq):
    w_in_bf, col_scale = _layer_a_weight(p["w_in"])
    z = _norm_proj(h, p["norm_attn"], w_in_bf, col_scale, cosf, sinf,
                   tn=A_TN, n_rope_blocks=A_ROPE_BLOCKS, seq=seq)

    gw = N_KV_A * HEAD_DIM
    k_raw, v_raw = _to_class_order(
        z, [(A_UNITS["k_cmp"] * HEAD_DIM // gw, gw), (A_UNITS["v_cmp"] * HEAD_DIM // gw, gw)],
        CMP_STRIDE, batch=batch, seq=seq)
    kc = _compress(k_raw, p["cmp_pe_k"], p["cmp_w1_k"].astype(BF16), p["cmp_w2_k"].astype(BF16))
    vc = _compress(v_raw, p["cmp_pe_v"], p["cmp_w1_v"].astype(BF16), p["cmp_w2_v"].astype(BF16))
    gb = jnp.pad(p["gate_bias"], (0, LANES - A_GATE)).reshape(1, LANES)
    o_nsa = _nsa_attention(z, kc, vc, gb, batch=batch, seq=seq, units=A_UNITS)

    mkv = _mem_kv(mem, p["norm_mem"], p["w_mem_kv"])
    o_mem = _memory_attention(z, mkv, batch=batch, seq=seq,
                              q_col=A_UNITS["mem_q"] * HEAD_DIM, block_width=MEM_Q)
    return _out_proj(o_nsa, o_mem, p["w_out"].astype(BF16), h)


def _mem_kv(mem, norm_mem, w_mem_kv):
    b, m, d = mem.shape
    ones = jnp.ones((1, w_mem_kv.shape[1]), F32)
    dummy = jnp.zeros((m, HEAD_DIM), F32)
    return _norm_proj(mem.reshape(b * m, d), norm_mem, w_mem_kv.astype(BF16), ones, dummy, dummy,
                      tn=MEM_Q, n_rope_blocks=0, seq=m, tm=m)


def kernel(x, mem, a_norm_attn, a_w_in, a_gate_bias, a_cmp_pe_k, a_cmp_w1_k, a_cmp_w2_k, a_cmp_pe_v, a_cmp_w1_v, a_cmp_w2_v, a_norm_mem, a_w_mem_kv, a_w_out, a_norm_ffn, a_w_gate, a_w_up, a_w_down, kv_norm, w_kv_shared, b_norm_attn, b_w_in, b_norm_mem, b_w_mem_kv, b_w_out, b_norm_ffn, b_w_gate, b_w_up, b_w_down, final_norm):
    batch, seq, d = x.shape
    n_a = a_w_in.shape[0]
    n_b = b_w_in.shape[0]
    cosf, sinf = _rope_tables(seq)
    h = x.reshape(batch * seq, d)
    unit_gain = jnp.ones((d,), F32)

    for l in range(n_a):
        p = {"norm_attn": a_norm_attn[l], "w_in": a_w_in[l], "gate_bias": a_gate_bias[l],
             "cmp_pe_k": a_cmp_pe_k[l], "cmp_w1_k": a_cmp_w1_k[l], "cmp_w2_k": a_cmp_w2_k[l],
             "cmp_pe_v": a_cmp_pe_v[l], "cmp_w1_v": a_cmp_w1_v[l], "cmp_w2_v": a_cmp_w2_v[l],
             "norm_mem": a_norm_mem[l], "w_mem_kv": a_w_mem_kv[l], "w_out": a_w_out[l]}
        h = _layer_a(h, mem, cosf, sinf, p, batch=batch, seq=seq)
        last = (l == n_a - 1) and n_b == 0
        h = _ffn(h, a_norm_ffn[l], a_w_gate[l].astype(BF16), a_w_up[l].astype(BF16),
                 a_w_down[l].astype(BF16), final_norm if last else unit_gain, final_norm=last)

    if n_b > 0:
        assert n_b == 1, "the shared K/V projection is fused into the single mixer-B layer"
        n_kv_half = w_kv_shared.shape[1] // 2
        for l in range(n_b):
            w_q = b_norm_attn[l][:, None] * b_w_in[l]
            w_kv = kv_norm[:, None] * w_kv_shared
            w_cat = jnp.concatenate([w_q[:, :B_Q], w_kv[:, :n_kv_half], w_q[:, B_Q:],
                                     w_kv[:, n_kv_half:]], axis=1).astype(BF16)
            b_scale = np.ones((1, w_cat.shape[1]), np.float32)
            b_scale[0, :B_Q] = SCALE * LOG2E
            b_scale[0, B_Q + n_kv_half:B_Q + n_kv_half + MEM_Q] = SCALE
            zb = _norm_proj(h, unit_gain, w_cat, jnp.asarray(b_scale), cosf, sinf,
                            tn=B_TN, n_rope_blocks=(B_Q + n_kv_half) // B_TN, seq=seq)
            o_dil = _dilated_attention(zb, batch=batch, seq=seq, units=B_UNITS)
            mkv = _mem_kv(mem, b_norm_mem[l], b_w_mem_kv[l])
            o_mem = _memory_attention(zb, mkv, batch=batch, seq=seq,
                                      q_col=B_UNITS["mem_q"] * MEM_Q, block_width=MEM_Q)
            h = _out_proj(o_dil, o_mem, b_w_out[l].astype(BF16), h)
            last = l == n_b - 1
            h = _ffn(h, b_norm_ffn[l], b_w_gate[l].astype(BF16), b_w_up[l].astype(BF16),
                     b_w_down[l].astype(BF16), final_norm if last else unit_gain, final_norm=last)

    return h.reshape(batch, seq, d)
```

```python
import functools
import math

import numpy as np
import jax
import jax.numpy as jnp
from jax import lax
from jax.experimental import pallas as pl
from jax.experimental.pallas import tpu as pltpu

F32 = jnp.float32
BF16 = jnp.bfloat16

HEAD_DIM = 128
N_HEADS_A = 12
N_KV_A = 2
HPG_A = N_HEADS_A // N_KV_A
CMP_LEN = 32
CMP_STRIDE = 16
CMP_HIDDEN = 256
SLC_BLK = 64
SLC_SHIFT = SLC_BLK.bit_length() - 1
N_SEL = 16
WIN_A = 512
DIL_CONFIGS = ((128, 1), (512, 4), (2048, 16))
N_DIL_GROUPS = len(DIL_CONFIGS)
DIL_HEADS = 4
N_MEM_HEADS = 4
ROPE_THETA = 10000.0
EPS = 1e-6
NEG_INF = -1e30
TINY = 1e-30
SCALE = HEAD_DIM ** -0.5
LOG2E = math.log2(math.e)

A_Q = N_HEADS_A * HEAD_DIM
A_KV = 6 * N_KV_A * HEAD_DIM
A_GATE = 3 * N_HEADS_A
MEM_Q = N_MEM_HEADS * HEAD_DIM
B_Q = N_DIL_GROUPS * DIL_HEADS * HEAD_DIM

LANES = 128
SUBLANES = 8
VMEM_LIMIT_BYTES = 56 * 1024 * 1024

PROJ_TM = 1024
FFN_UP_TM = 2048
FFN_TM = 512
FFN_TF = 512
OUT_TM = 512
MERGE_TM = 1024
NSA_TQ = 128
NSA_KC = 512
DIL_TQ = 256
DIL_DENSE_MAX = 4
MEM_TQ = 1024

NT_DIMS = (((1,), (1,)), ((), ()))
TN_DIMS = (((0,), (0,)), ((), ()))


def _compiler_params(semantics):
    return pltpu.CompilerParams(dimension_semantics=semantics,
                                vmem_limit_bytes=VMEM_LIMIT_BYTES)


def _rms_rows(x, g):
    ms = jnp.mean(x * x, axis=-1, keepdims=True)
    return x * lax.rsqrt(ms + EPS) * g


def _dot(a, b):
    return jnp.dot(a, b, preferred_element_type=F32)


def _dot_nt(a, b):
    return lax.dot_general(a, b, NT_DIMS, preferred_element_type=F32)


def _dot_tn(a, b):
    return lax.dot_general(a, b, TN_DIMS, preferred_element_type=F32)


def _norm_proj_kernel(x_ref, g_ref, w_ref, cs_ref, cos_ref, sin_ref, o_ref, xn_ref, *,
                      n_rope_blocks, tn):
    j = pl.program_id(1)

    @pl.when(j == 0)
    def _():
        xn_ref[...] = _rms_rows(x_ref[...], g_ref[...]).astype(BF16)

    acc = _dot(xn_ref[...], w_ref[...]) * cs_ref[...]

    if n_rope_blocks > 0:
        roped = j < n_rope_blocks
        c = jnp.where(roped, cos_ref[...], 1.0)
        s = jnp.where(roped, sin_ref[...], 0.0)
        for h in range(tn // HEAD_DIM):
            y = acc[:, h * HEAD_DIM:(h + 1) * HEAD_DIM]
            rot = pltpu.roll(y, HEAD_DIM // 2, 1)
            o_ref[:, h * HEAD_DIM:(h + 1) * HEAD_DIM] = (y * c + rot * s).astype(o_ref.dtype)
    else:
        o_ref[...] = acc.astype(o_ref.dtype)


def _norm_proj(x, g, w_bf, col_scale, cosf, sinf, *, tn, n_rope_blocks, seq, tm=PROJ_TM):
    m, d = x.shape
    n = w_bf.shape[1]
    tm = min(tm, m)
    assert m % tm == 0 and n % tn == 0 and seq % tm == 0
    pos_blocks = seq // tm
    kern = functools.partial(_norm_proj_kernel, n_rope_blocks=n_rope_blocks, tn=tn)
    return pl.pallas_call(
        kern,
        grid=(m // tm, n // tn),
        in_specs=[
            pl.BlockSpec((tm, d), lambda i, j: (i, 0)),
            pl.BlockSpec((1, d), lambda i, j: (0, 0)),
            pl.BlockSpec((d, tn), lambda i, j: (0, j)),
            pl.BlockSpec((1, tn), lambda i, j: (0, j)),
            pl.BlockSpec((tm, HEAD_DIM), lambda i, j: (i % pos_blocks, 0)),
            pl.BlockSpec((tm, HEAD_DIM), lambda i, j: (i % pos_blocks, 0)),
        ],
        out_specs=pl.BlockSpec((tm, tn), lambda i, j: (i, j)),
        out_shape=jax.ShapeDtypeStruct((m, n), BF16),
        scratch_shapes=[pltpu.VMEM((tm, d), BF16)],
        compiler_params=_compiler_params(("parallel", "arbitrary")),
        name="norm_proj",
    )(x, g.reshape(1, d), w_bf, col_scale, cosf, sinf)


def _class_perm(tm, r):
    dst = np.arange(tm)
    c, u = dst // (tm // r), dst % (tm // r)
    perm = np.zeros((tm, tm), np.float32)
    perm[dst, u * r + c] = 1.0
    return perm


def _to_class_kernel(p_ref, *refs, r):
    n = len(refs) // 2
    for x_ref, o_ref in zip(refs[:n], refs[n:]):
        y = _dot(p_ref[...], x_ref[...]).astype(o_ref.dtype)
        rows = y.shape[0] // r
        for c in range(r):
            o_ref[c] = y[c * rows:(c + 1) * rows, :]


def _to_class_order(x, slabs, r, *, batch, seq):
    tm = min(PROJ_TM, seq)
    nblk = seq // tm
    perm = jnp.asarray(_class_perm(tm, r), x.dtype)
    return pl.pallas_call(
        functools.partial(_to_class_kernel, r=r),
        grid=(batch * nblk,),
        in_specs=[pl.BlockSpec((tm, tm), lambda i: (0, 0))]
        + [pl.BlockSpec((tm, w), lambda i, cb=cb: (i, cb)) for cb, w in slabs],
        out_specs=[pl.BlockSpec((None, r, tm // r, w), lambda i: (i // nblk, 0, i % nblk, 0))
                   for _, w in slabs],
        out_shape=[jax.ShapeDtypeStruct((batch, r, seq // r, w), x.dtype) for _, w in slabs],
        compiler_params=_compiler_params(("parallel",)),
        name="to_class_order",
    )(perm, *([x] * len(slabs)))


def _from_class_kernel(pt_ref, x_ref, o_ref, *, r):
    x = jnp.concatenate([x_ref[c] for c in range(r)], axis=0)
    o_ref[...] = _dot(pt_ref[...], x).astype(o_ref.dtype)


def _from_class_order(xc, *, batch, seq):
    _, r, _, w = xc.shape
    tm = min(PROJ_TM, seq)
    nblk = seq // tm
    perm_t = jnp.asarray(_class_perm(tm, r).T, xc.dtype)
    return pl.pallas_call(
        functools.partial(_from_class_kernel, r=r),
        grid=(batch * nblk,),
        in_specs=[pl.BlockSpec((tm, tm), lambda i: (0, 0)),
                  pl.BlockSpec((None, r, tm // r, w), lambda i: (i // nblk, 0, i % nblk, 0))],
        out_specs=pl.BlockSpec((tm, w), lambda i: (i, 0)),
        out_shape=jax.ShapeDtypeStruct((batch * seq, w), xc.dtype),
        compiler_params=_compiler_params(("parallel",)),
        name="from_class_order",
    )(perm_t, xc)


def _compress_kernel(x_ref, pe_ref, w1_ref, w2_ref, o_ref):
    n_planes, n_rows, _ = x_ref.shape
    ylo = yhi = None
    for l in range(n_planes):
        x = x_ref[l].astype(F32)
        xlo = (x + pe_ref[l:l + 1, :]).astype(BF16)
        xhi = (x + pe_ref[n_planes + l:n_planes + l + 1, :]).astype(BF16)
        dlo = _dot(xlo, w1_ref[l * HEAD_DIM:(l + 1) * HEAD_DIM, :])
        dhi = _dot(xhi, w1_ref[(n_planes + l) * HEAD_DIM:(n_planes + l + 1) * HEAD_DIM, :])
        ylo = dlo if ylo is None else ylo + dlo
        yhi = dhi if yhi is None else yhi + dhi
    hid = ylo + pltpu.roll(yhi, n_rows - 1, 0)
    act = (hid * jax.nn.sigmoid(hid)).astype(BF16)
    o_ref[...] = _dot(act, w2_ref[...]).astype(o_ref.dtype)


def _compress(xc, pe, w1_bf, w2_bf):
    batch, planes, nrow, gd = xc.shape
    ng = gd // HEAD_DIM
    return pl.pallas_call(
        _compress_kernel,
        grid=(batch, ng),
        in_specs=[
            pl.BlockSpec((None, planes, nrow, HEAD_DIM), lambda b, g: (b, 0, 0, g)),
            pl.BlockSpec((CMP_LEN, HEAD_DIM), lambda b, g: (0, 0)),
            pl.BlockSpec((CMP_LEN * HEAD_DIM, CMP_HIDDEN), lambda b, g: (0, 0)),
            pl.BlockSpec((CMP_HIDDEN, HEAD_DIM), lambda b, g: (0, 0)),
        ],
        out_specs=pl.BlockSpec((None, None, nrow, HEAD_DIM), lambda b, g: (b, g, 0, 0)),
        out_shape=jax.ShapeDtypeStruct((batch, ng, nrow, HEAD_DIM), BF16),
        compiler_params=_compiler_params(("parallel", "arbitrary")),
        name="nsa_compress",
    )(xc, pe, w1_bf, w2_bf)


def _block_ranks(score, jrow):
    n_blk = score.shape[0]
    groups = n_blk // SUBLANES
    blocks = [score[SUBLANES * r:SUBLANES * (r + 1), :] for r in range(groups)]
    rows = [jrow[SUBLANES * r:SUBLANES * (r + 1), :] for r in range(groups)]
    ranks = [jnp.zeros(blocks[0].shape, F32) for _ in range(groups)]
    for j in range(n_blk):
        rj = score[j:j + 1, :]
        for r in range(groups):
            if r > j // SUBLANES:
                ahead = rj >= blocks[r]
            elif r < j // SUBLANES:
                ahead = rj > blocks[r]
            else:
                ahead = (rj > blocks[r]) | ((rj == blocks[r]) & (rows[r] > j))
            ranks[r] = ranks[r] + jnp.where(ahead, 1.0, 0.0)
    return jnp.concatenate(ranks, axis=0)


def _nsa_kernel(q_ref, kc_ref, vc_ref, ks_ref, e_ref, vs_ref, kw_ref, vw_ref, gz_ref, gb_ref,
                o_ref, qa_ref, sa_ref, sb_ref, sw_ref, ow_ref, m_ref, l_ref, acc_ref, *, tq, seq):
    hq = HPG_A
    groups = range(N_KV_A)
    n_cmp_rows = kc_ref.shape[1]
    n_slc = seq // SLC_BLK
    qi = pl.program_id(1)
    s0 = qi * tq
    t_row = s0 + lax.broadcasted_iota(jnp.int32, (1, tq), 1)

    def lanes(h):
        return slice(h * tq, (h + 1) * tq)

    def gcols(g):
        return slice(g * HEAD_DIM, (g + 1) * HEAD_DIM)

    def tile_heads(x):
        return jnp.concatenate([x] * hq, axis=1)

    wlen = WIN_A + tq
    ws = pl.multiple_of(jnp.maximum(s0 - WIN_A, 0), LANES)
    c_end = lax.broadcasted_iota(jnp.int32, (n_cmp_rows, 1), 0) * CMP_STRIDE + (CMP_LEN - 1)
    cbias = tile_heads(jnp.where(c_end <= t_row, 0.0, NEG_INF))
    any_cmp = tile_heads(t_row >= CMP_LEN - 1)
    jrow = lax.broadcasted_iota(jnp.int32, (n_slc, 1), 0)
    ccol = lax.broadcasted_iota(jnp.int32, (1, n_cmp_rows), 1)
    lo = (SLC_BLK // CMP_STRIDE) * jrow - (CMP_LEN // CMP_STRIDE - 1)
    hi = (SLC_BLK // CMP_STRIDE) * jrow + (SLC_BLK // CMP_STRIDE - 1)
    mmap = jnp.where((ccol >= lo) & (ccol <= hi), 1.0, 0.0).astype(BF16)
    cur = t_row >> SLC_SHIFT
    forced = (jrow == 0) | (jrow == cur) | (jrow == cur - 1)
    assert n_slc <= LANES and tq == LANES

    q6, sc = [], []
    for g in groups:
        for h in range(hq):
            head = g * hq + h
            qa_ref[g, lanes(h), 0:HEAD_DIM] = q_ref[:, head * HEAD_DIM:(head + 1) * HEAD_DIM]
        q6.append(qa_ref[g, :, 0:HEAD_DIM])
        sc.append(_dot_nt(kc_ref[g], q6[g]) + cbias)
    for g in groups:
        sw_ref[g] = _dot_nt(kw_ref[pl.ds(ws, wlen), gcols(g)], q6[g])

    o_cmp, score = [], []
    for g in groups:
        ec = jnp.exp2(sc[g] - jnp.max(sc[g], axis=0, keepdims=True))
        den = jnp.maximum(jnp.sum(ec, axis=0, keepdims=True), TINY)
        pc = ec * jnp.where(any_cmp, 1.0 / den, 0.0)
        o_cmp.append(_dot_tn(vc_ref[g], pc.astype(BF16)))
        psum = pc[:, lanes(0)]
        for h in range(1, hq):
            psum = psum + pc[:, lanes(h)]
        p1 = psum.astype(BF16)
        r1 = psum - p1.astype(F32)
        p2 = r1.astype(BF16)
        p3 = (r1 - p2.astype(F32)).astype(BF16)
        imp = _dot(mmap, p1) + _dot(mmap, p2) + _dot(mmap, p3)
        score.append(jnp.where(forced, 1e9, jnp.where(jrow <= cur, imp, -1e9)))

    for g in groups:
        rank = _block_ranks(score[g], jrow)
        sel_bias = jnp.where((rank < min(N_SEL, n_slc)) & (jrow <= cur), 0.0, NEG_INF)
        bias_q = jnp.transpose(jnp.concatenate(
            [sel_bias, jnp.zeros((LANES - n_slc, tq), F32)], axis=0)).astype(BF16)
        for h in range(hq):
            qa_ref[g, lanes(h), HEAD_DIM:HEAD_DIM + LANES] = bias_q

    m_ref[...] = jnp.full(m_ref.shape, NEG_INF, F32)
    l_ref[...] = jnp.zeros(l_ref.shape, F32)
    acc_ref[...] = jnp.zeros(acc_ref.shape, F32)

    def scores(g, c, s_ref):
        k0 = pl.multiple_of(c * NSA_KC, NSA_KC)
        k_aug = jnp.concatenate([ks_ref[pl.ds(k0, NSA_KC), gcols(g)],
                                 e_ref[pl.ds(k0, NSA_KC), :]], axis=1)
        s_ref[g] = _dot_nt(k_aug, qa_ref[g])

    def softmax_pv(g, c, s_ref):
        s = s_ref[g]
        m_old = m_ref[g]
        m_new = jnp.maximum(m_old, jnp.max(s, axis=0, keepdims=True))
        alpha = jnp.exp2(m_old - m_new)
        p = jnp.exp2(s - m_new)
        l_ref[g] = alpha * l_ref[g] + jnp.sum(p, axis=0, keepdims=True)
        k0 = pl.multiple_of(c * NSA_KC, NSA_KC)
        pv = _dot_tn(vs_ref[pl.ds(k0, NSA_KC), gcols(g)], p.astype(BF16))
        acc_ref[g] = alpha * acc_ref[g] + pv
        m_ref[g] = m_new

    last = s0 // NSA_KC
    for g in groups:
        scores(g, 0, sa_ref)

    dist = t_row - (ws + lax.broadcasted_iota(jnp.int32, (wlen, 1), 0))
    wbias = tile_heads(jnp.where((dist >= 0) & (dist < WIN_A), 0.0, NEG_INF))
    for g in groups:
        sw = sw_ref[g] + wbias
        ew = jnp.exp2(sw - jnp.max(sw, axis=0, keepdims=True))
        ow_ref[g] = (_dot_tn(vw_ref[pl.ds(ws, wlen), gcols(g)], ew.astype(BF16))
                     * (1.0 / jnp.sum(ew, axis=0, keepdims=True)))

    def chunk_pair(i, carry):
        for g in groups:
            scores(g, 2 * i + 1, sb_ref)
            softmax_pv(g, 2 * i, sa_ref)
        for g in groups:
            scores(g, 2 * i + 2, sa_ref)
            softmax_pv(g, 2 * i + 1, sb_ref)
        return carry

    lax.fori_loop(0, last // 2, chunk_pair, 0)

    def last_chunk(g, s_ref):
        diag = pl.multiple_of(s0 - last * NSA_KC, LANES)
        krow = lax.broadcasted_iota(jnp.int32, (tq, 1), 0)
        lane = lax.broadcasted_iota(jnp.int32, (1, tq), 1)
        causal = tile_heads(jnp.where(krow <= lane, 0.0, NEG_INF))
        s_ref[g, pl.ds(diag, tq), :] = s_ref[g, pl.ds(diag, tq), :] + causal
        softmax_pv(g, last, s_ref)

    @pl.when(last % 2 == 0)
    def _():
        for g in groups:
            last_chunk(g, sa_ref)

    @pl.when(last % 2 == 1)
    def _():
        for g in groups:
            scores(g, last, sb_ref)
            softmax_pv(g, last - 1, sa_ref)
        for g in groups:
            last_chunk(g, sb_ref)

    gates_t = jnp.transpose(jax.nn.sigmoid(gz_ref[...].astype(F32) + gb_ref[...]))
    for g in groups:
        o_slc = acc_ref[g] * (1.0 / l_ref[g])
        for h in range(hq):
            head = g * hq + h
            o_h = (gates_t[3 * head:3 * head + 1, :] * o_cmp[g][:, lanes(h)]
                   + gates_t[3 * head + 1:3 * head + 2, :] * o_slc[:, lanes(h)]
                   + gates_t[3 * head + 2:3 * head + 3, :] * ow_ref[g, :, lanes(h)])
            o_ref[:, head * HEAD_DIM:(head + 1) * HEAD_DIM] = jnp.transpose(o_h).astype(o_ref.dtype)


def _nsa_attention(z, kc, vc, gate_bias, *, batch, seq, units):
    tq = NSA_TQ
    nq = seq // tq
    n_cmp_rows = kc.shape[2]
    ng = N_KV_A
    hl = HPG_A * tq
    gw = ng * HEAD_DIM
    for name in ("k_slc", "v_slc", "k_win", "v_win"):
        assert units[name] % ng == 0
    kern = functools.partial(_nsa_kernel, tq=tq, seq=seq)

    def slab(name):
        return pl.BlockSpec((seq, gw), lambda b, i: (b, units[name] // ng))

    cmp_spec = pl.BlockSpec((None, ng, n_cmp_rows, HEAD_DIM), lambda b, i: (b, 0, 0, 0))

    onehot = jnp.asarray(np.arange(seq)[:, None] // SLC_BLK == np.arange(LANES)[None, :], BF16)
    return pl.pallas_call(
        kern,
        grid=(batch, nq),
        in_specs=[
            pl.BlockSpec((tq, A_Q), lambda b, i: (b * nq + i, 0)),
            cmp_spec,
            cmp_spec,
            slab("k_slc"),
            pl.BlockSpec((seq, LANES), lambda b, i: (0, 0)),
            slab("v_slc"),
            slab("k_win"),
            slab("v_win"),
            pl.BlockSpec((tq, LANES), lambda b, i: (b * nq + i, units["gates"])),
            pl.BlockSpec((1, LANES), lambda b, i: (0, 0)),
        ],
        out_specs=pl.BlockSpec((tq, A_Q), lambda b, i: (b * nq + i, 0)),
        out_shape=jax.ShapeDtypeStruct((batch * seq, A_Q), BF16),
        scratch_shapes=[
            pltpu.VMEM((ng, hl, HEAD_DIM + LANES), BF16),
            pltpu.VMEM((ng, NSA_KC, hl), F32),
            pltpu.VMEM((ng, NSA_KC, hl), F32),
            pltpu.VMEM((ng, WIN_A + tq, hl), F32),
            pltpu.VMEM((ng, HEAD_DIM, hl), F32),
            pltpu.VMEM((ng, 1, hl), F32),
            pltpu.VMEM((ng, 1, hl), F32),
            pltpu.VMEM((ng, HEAD_DIM, hl), F32),
        ],
        compiler_params=_compiler_params(("parallel", "arbitrary")),
        name="nsa_attention",
    )(z, kc, vc, z, onehot, z, z, z, z, gate_bias)


def _band_attn_kernel(*refs, tu, parts):
    q_refs = refs[:len(parts)]
    k_ref, v_ref, o_ref, lse_ref = refs[len(parts):]
    u0 = pl.program_id(2) * tu
    n_seq = k_ref.shape[0]
    windows = []
    for lk, span, stride in parts:
        ks = pl.multiple_of(jnp.clip(u0 - span, 0, n_seq - lk), LANES)
        dist = (u0 + lax.broadcasted_iota(jnp.int32, (tu, 1), 0)
                - (ks + lax.broadcasted_iota(jnp.int32, (1, lk), 1)))
        keep = (dist >= 0) & (dist <= span)
        if stride > 1:
            keep = keep & ((dist & (stride - 1)) == 0)
        windows.append((ks, lk, jnp.where(keep, 0.0, NEG_INF)))
    lane = lax.broadcasted_iota(jnp.int32, (1, LANES), 1)
    lse_tile = jnp.zeros((tu, LANES), F32)

    def head_cols(h):
        return slice(h * HEAD_DIM, (h + 1) * HEAD_DIM)

    scores = [[_dot_nt(q_ref[:, head_cols(h)], k_ref[pl.ds(ks, lk), head_cols(h)]) + bias
               for q_ref, (ks, lk, bias) in zip(q_refs, windows)]
              for h in range(DIL_HEADS)]
    probs, inv_l = [], []
    for h, s_parts in enumerate(scores):
        s = jnp.concatenate(s_parts, axis=1)
        m = jnp.max(s, axis=-1, keepdims=True)
        p = jnp.exp2(s - m)
        l = jnp.sum(p, axis=-1, keepdims=True)
        probs.append(p.astype(BF16))
        inv_l.append(1.0 / l)
        lse_tile = jnp.where(lane == h, m + jnp.log2(l), lse_tile)
    for h in range(DIL_HEADS):
        o, col = None, 0
        for ks, lk, _ in windows:
            pv = _dot(probs[h][:, col:col + lk], v_ref[pl.ds(ks, lk), head_cols(h)])
            o = pv if o is None else o + pv
            col += lk
        o_ref[:, head_cols(h)] = (o * inv_l[h]).astype(o_ref.dtype)
    lse_ref[...] = lse_tile


def _dil_merge_kernel(*refs):
    out_ref = refs[-1]
    calls = list(zip(refs[0:-1:2], refs[1:-1:2]))
    for h in range(DIL_HEADS):
        cols = slice(h * HEAD_DIM, (h + 1) * HEAD_DIM)
        shape = (out_ref.shape[0], HEAD_DIM)
        lses = [jnp.broadcast_to(l_ref[:, h:h + 1], shape) for _, l_ref in calls]
        top = functools.reduce(jnp.maximum, lses)
        ws = [jnp.exp2(lse - top) for lse in lses]
        num = sum(w * o_ref[:, cols].astype(F32) for w, (o_ref, _) in zip(ws, calls))
        out_ref[:, cols] = (num * (1.0 / sum(ws))).astype(out_ref.dtype)


def _dilated_attention(zb, *, batch, seq, units):
    n = zb.shape[1]
    width = DIL_HEADS * HEAD_DIM
    tu = DIL_TQ

    def band_call(r, q_units, parts, operands, k_unit, v_unit):
        n_seq = seq // r

        def z_spec(rows, unit, whole):
            return pl.BlockSpec((None, None, rows, width),
                                lambda b, c, i: (b, c, 0 if whole else i, unit))

        def out_spec(cols):
            return pl.BlockSpec((None, None, tu, cols), lambda b, c, i: (b, c, i, 0))

        return pl.pallas_call(
            functools.partial(_band_attn_kernel, tu=tu, parts=parts),
            grid=(batch, r, n_seq // tu),
            in_specs=[z_spec(tu, u, False) for u in q_units]
            + [z_spec(n_seq, k_unit, True), z_spec(n_seq, v_unit, True)],
            out_specs=[out_spec(width), out_spec(LANES)],
            out_shape=[jax.ShapeDtypeStruct((batch, r, n_seq, width), BF16),
                       jax.ShapeDtypeStruct((batch, r, n_seq, LANES), F32)],
            compiler_params=_compiler_params(("parallel", "parallel", "arbitrary")),
            name="dilated_attention",
        )(*operands)

    def window(n_seq, span):
        assert n_seq % tu == 0 and span % LANES == 0 and tu % LANES == 0
        return min(tu + span, n_seq)

    dense = [(gi, w, r) for gi, (w, r) in enumerate(DIL_CONFIGS) if r <= DIL_DENSE_MAX]
    zv = zb.reshape(batch, 1, seq, n)
    o_d, lse_d = band_call(1, [units["q"] + gi for gi, _, _ in dense],
                           tuple((window(seq, w), w, r) for _, w, r in dense),
                           [zv] * (len(dense) + 2), units["k"], units["v"])
    results = [o_d.reshape(batch * seq, width), lse_d.reshape(batch * seq, LANES)]
    for gi, (w, r) in enumerate(DIL_CONFIGS):
        if r <= DIL_DENSE_MAX:
            continue
        slabs = [(units["q"] + gi, width), (units["k"], width), (units["v"], width)]
        o_c, lse_c = band_call(r, [0], ((window(seq // r, w // r), w // r, 1),),
                               _to_class_order(zb, slabs, r, batch=batch, seq=seq), 0, 0)
        results += [_from_class_order(o_c, batch=batch, seq=seq),
                    jnp.transpose(lse_c, (0, 2, 1, 3)).reshape(batch * seq, LANES)]

    m = batch * seq
    tm = min(MERGE_TM, m)
    o_spec = pl.BlockSpec((tm, width), lambda i: (i, 0))
    l_spec = pl.BlockSpec((tm, LANES), lambda i: (i, 0))
    return pl.pallas_call(
        _dil_merge_kernel,
        grid=(m // tm,),
        in_specs=[o_spec, l_spec] * (len(results) // 2),
        out_specs=o_spec,
        out_shape=jax.ShapeDtypeStruct((m, width), BF16),
        compiler_params=_compiler_params(("parallel",)),
        name="dilated_merge",
    )(*results)


def _mem_attn_kernel(q_ref, kv_ref, o_ref, *, q_offset):
    def cols(h, base=0):
        return slice(base + h * HEAD_DIM, base + (h + 1) * HEAD_DIM)

    heads = range(N_MEM_HEADS)
    scores = [_dot_nt(q_ref[:, cols(h, q_offset)], kv_ref[:, cols(h)]) for h in heads]
    probs = []
    for s in scores:
        e = jnp.exp(s - jnp.max(s, axis=-1, keepdims=True))
        probs.append((e / jnp.sum(e, axis=-1, keepdims=True)).astype(BF16))
    for h in heads:
        o_ref[:, cols(h)] = _dot(probs[h], kv_ref[:, cols(h, MEM_Q)]).astype(o_ref.dtype)


def _memory_attention(z, mkv, *, batch, seq, q_col, block_width):
    tq = MEM_TQ
    nq = seq // tq
    n_mem = mkv.shape[0] // batch
    q_block, q_offset = divmod(q_col, block_width)
    assert q_offset + MEM_Q <= block_width
    return pl.pallas_call(
        functools.partial(_mem_attn_kernel, q_offset=q_offset),
        grid=(batch, nq),
        in_specs=[
            pl.BlockSpec((tq, block_width), lambda b, i: (b * nq + i, q_block)),
            pl.BlockSpec((n_mem, 2 * MEM_Q), lambda b, i: (b, 0)),
        ],
        out_specs=pl.BlockSpec((tq, MEM_Q), lambda b, i: (b * nq + i, 0)),
        out_shape=jax.ShapeDtypeStruct((batch * seq, MEM_Q), BF16),
        compiler_params=_compiler_params(("parallel", "arbitrary")),
        name="memory_attention",
    )(z, mkv)


def _out_proj_kernel(a1_ref, a2_ref, w_ref, h_ref, g_ref, o_ref, xn_ref):
    a = jnp.concatenate([a1_ref[...], a2_ref[...]], axis=1)
    y = h_ref[...] + _dot(a, w_ref[...])
    o_ref[...] = y
    xn_ref[...] = _rms_rows(y, g_ref[...]).astype(xn_ref.dtype)


def _out_proj(a1, a2, w_bf, h, g_next):
    m, d = h.shape
    tm = min(OUT_TM, m)
    k1, k2 = a1.shape[1], a2.shape[1]
    assert w_bf.shape[0] == k1 + k2
    row = pl.BlockSpec((tm, d), lambda i: (i, 0))
    return pl.pallas_call(
        _out_proj_kernel,
        grid=(m // tm,),
        in_specs=[
            pl.BlockSpec((tm, k1), lambda i: (i, 0)),
            pl.BlockSpec((tm, k2), lambda i: (i, 0)),
            pl.BlockSpec((k1 + k2, d), lambda i: (0, 0), pipeline_mode=pl.Buffered(1)),
            row,
            pl.BlockSpec((1, d), lambda i: (0, 0)),
        ],
        out_specs=[row, row],
        out_shape=[jax.ShapeDtypeStruct((m, d), F32), jax.ShapeDtypeStruct((m, d), BF16)],
        compiler_params=_compiler_params(("parallel",)),
        name="out_proj",
    )(a1, a2, w_bf, h, g_next.reshape(1, d))


def _ffn_up_kernel(xn_ref, wg_ref, wu_ref, o_ref):
    xn = xn_ref[...]
    gate = _dot(xn, wg_ref[...])
    up = _dot(xn, wu_ref[...])
    o_ref[...] = (gate * jax.nn.sigmoid(gate) * up).astype(o_ref.dtype)


def _ffn_down_kernel(a_ref, w_ref, h_ref, fg_ref, o_ref, *, final_norm):
    y = h_ref[...] + _dot(a_ref[...], w_ref[...])
    if final_norm:
        y = _rms_rows(y, fg_ref[...])
    o_ref[...] = y


def _ffn(h, xn, wg_bf, wu_bf, wd_bf, final_gain, *, final_norm):
    m, d = h.shape
    dff = wg_bf.shape[1]
    tm, tf = min(FFN_UP_TM, m), FFN_TF
    assert m % tm == 0 and dff % tf == 0
    act = pl.pallas_call(
        _ffn_up_kernel,
        grid=(m // tm, dff // tf),
        in_specs=[
            pl.BlockSpec((tm, d), lambda i, f: (i, 0)),
            pl.BlockSpec((d, tf), lambda i, f: (0, f)),
            pl.BlockSpec((d, tf), lambda i, f: (0, f)),
        ],
        out_specs=pl.BlockSpec((tm, tf), lambda i, f: (i, f)),
        out_shape=jax.ShapeDtypeStruct((m, dff), BF16),
        compiler_params=_compiler_params(("parallel", "arbitrary")),
        name="ffn_up",
    )(xn, wg_bf, wu_bf)

    tm = min(FFN_TM, m)
    kern = functools.partial(_ffn_down_kernel, final_norm=final_norm)
    return pl.pallas_call(
        kern,
        grid=(m // tm,),
        in_specs=[
            pl.BlockSpec((tm, dff), lambda i: (i, 0)),
            pl.BlockSpec((dff, d), lambda i: (0, 0), pipeline_mode=pl.Buffered(1)),
            pl.BlockSpec((tm, d), lambda i: (i, 0)),
            pl.BlockSpec((1, d), lambda i: (0, 0)),
        ],
        out_specs=pl.BlockSpec((tm, d), lambda i: (i, 0)),
        out_shape=jax.ShapeDtypeStruct((m, d), F32),
        compiler_params=_compiler_params(("parallel",)),
        name="ffn_down",
    )(act, wd_bf, h, final_gain.reshape(1, d))


def _rope_tables(seq):
    inv = 1.0 / (ROPE_THETA ** (jnp.arange(0, HEAD_DIM, 2, dtype=F32) / HEAD_DIM))
    ang = jnp.arange(seq, dtype=F32)[:, None] * inv[None, :]
    cos, sin = jnp.cos(ang), jnp.sin(ang)
    return jnp.concatenate([cos, cos], axis=1), jnp.concatenate([-sin, sin], axis=1)


A_UNITS = {"q": 0, "k_cmp": 12, "k_slc": 14, "k_win": 16, "v_cmp": 18, "v_slc": 20,
           "v_win": 22, "mem_q": 24, "gates": 28}
B_UNITS = {"q": 0, "k": 3, "mem_q": 4, "v": 5}
B_TN = 2 * DIL_HEADS * HEAD_DIM
A_TN = 6 * HEAD_DIM
A_NPAD = 30 * HEAD_DIM
A_ROPE_BLOCKS = 3


def _layer_a_weight(w_in):
    kv0 = A_Q

    def kv_cols(branch):
        return w_in[:, kv0 + branch * N_KV_A * HEAD_DIM:kv0 + (branch + 1) * N_KV_A * HEAD_DIM]

    gate0 = A_Q + A_KV
    mem0 = gate0 + A_GATE
    w = jnp.concatenate([w_in[:, :A_Q], kv_cols(0), kv_cols(2), kv_cols(4), kv_cols(1),
                         kv_cols(3), kv_cols(5), w_in[:, mem0:mem0 + MEM_Q],
                         w_in[:, gate0:mem0]], axis=1)
    w = jnp.pad(w, ((0, 0), (0, A_NPAD - w.shape[1])))
    scale = np.ones((1, A_NPAD), np.float32)
    scale[0, :A_Q] = SCALE * LOG2E
    scale[0, A_UNITS["mem_q"] * HEAD_DIM:A_UNITS["mem_q"] * HEAD_DIM + MEM_Q] = SCALE
    return w.astype(BF16), jnp.asarray(scale)


def _layer_a(h, mem, cosf, sinf, p, *, batch, seq):
    w_in_bf, col_scale = _layer_a_weight(p["w_in"])
    z = _norm_proj(h, p["norm_attn"], w_in_bf, col_scale, cosf, sinf,
                   tn=A_TN, n_rope_blocks=A_ROPE_BLOCKS, seq=seq)

    gw = N_KV_A * HEAD_DIM
    k_raw, v_raw = _to_class_order(
        z, [(A_UNITS["k_cmp"] * HEAD_DIM // gw, gw), (A_UNITS["v_cmp"] * HEAD_DIM // gw, gw)],
        CMP_STRIDE, batch=batch, seq=seq)
    kc = _compress(k_raw, p["cmp_pe_k"], p["cmp_w1_k"].astype(BF16), p["cmp_w2_k"].astype(BF16))
    vc = _compress(v_raw, p["cmp_pe_v"], p["cmp_w1_v"].astype(BF16), p["cmp_w2_v"].astype(BF16))
    gb = jnp.pad(p["gate_bias"], (0, LANES - A_GATE)).reshape(1, LANES)
    o_nsa = _nsa_attention(z, kc, vc, gb, batch=batch, seq=seq, units=A_UNITS)

    mkv = _mem_kv(mem, p["norm_mem"], p["w_mem_kv"])
    o_mem = _memory_attention(z, mkv, batch=batch, seq=seq,
                              q_col=A_UNITS["mem_q"] * HEAD_DIM, block_width=MEM_Q)
    return _out_proj(o_nsa, o_mem, p["w_out"].astype(BF16), h, p["norm_ffn"])


def _mem_kv(mem, norm_mem, w_mem_kv):
    b, m, d = mem.shape
    ones = jnp.ones((1, w_mem_kv.shape[1]), F32)
    dummy = jnp.zeros((m, HEAD_DIM), F32)
    return _norm_proj(mem.reshape(b * m, d), norm_mem, w_mem_kv.astype(BF16), ones, dummy, dummy,
                      tn=MEM_Q, n_rope_blocks=0, seq=m, tm=m)


def kernel(x, mem, a_norm_attn, a_w_in, a_gate_bias, a_cmp_pe_k, a_cmp_w1_k, a_cmp_w2_k, a_cmp_pe_v, a_cmp_w1_v, a_cmp_w2_v, a_norm_mem, a_w_mem_kv, a_w_out, a_norm_ffn, a_w_gate, a_w_up, a_w_down, kv_norm, w_kv_shared, b_norm_attn, b_w_in, b_norm_mem, b_w_mem_kv, b_w_out, b_norm_ffn, b_w_gate, b_w_up, b_w_down, final_norm):
    batch, seq, d = x.shape
    n_a = a_w_in.shape[0]
    n_b = b_w_in.shape[0]
    cosf, sinf = _rope_tables(seq)
    h = x.reshape(batch * seq, d)
    unit_gain = jnp.ones((d,), F32)

    for l in range(n_a):
        p = {"norm_attn": a_norm_attn[l], "w_in": a_w_in[l], "gate_bias": a_gate_bias[l],
             "cmp_pe_k": a_cmp_pe_k[l], "cmp_w1_k": a_cmp_w1_k[l], "cmp_w2_k": a_cmp_w2_k[l],
             "cmp_pe_v": a_cmp_pe_v[l], "cmp_w1_v": a_cmp_w1_v[l], "cmp_w2_v": a_cmp_w2_v[l],
             "norm_mem": a_norm_mem[l], "w_mem_kv": a_w_mem_kv[l], "w_out": a_w_out[l],
             "norm_ffn": a_norm_ffn[l]}
        h, xn = _layer_a(h, mem, cosf, sinf, p, batch=batch, seq=seq)
        last = (l == n_a - 1) and n_b == 0
        h = _ffn(h, xn, a_w_gate[l].astype(BF16), a_w_up[l].astype(BF16),
                 a_w_down[l].astype(BF16), final_norm if last else unit_gain, final_norm=last)

    if n_b > 0:
        assert n_b == 1, "the shared K/V projection is fused into the single mixer-B layer"
        n_kv_half = w_kv_shared.shape[1] // 2
        for l in range(n_b):
            w_q = b_norm_attn[l][:, None] * b_w_in[l]
            w_kv = kv_norm[:, None] * w_kv_shared
            w_cat = jnp.concatenate([w_q[:, :B_Q], w_kv[:, :n_kv_half], w_q[:, B_Q:],
                                     w_kv[:, n_kv_half:]], axis=1).astype(BF16)
            b_scale = np.ones((1, w_cat.shape[1]), np.float32)
            b_scale[0, :B_Q] = SCALE * LOG2E
            b_scale[0, B_Q + n_kv_half:B_Q + n_kv_half + MEM_Q] = SCALE
            zb = _norm_proj(h, unit_gain, w_cat, jnp.asarray(b_scale), cosf, sinf,
                            tn=B_TN, n_rope_blocks=(B_Q + n_kv_half) // B_TN, seq=seq)
            o_dil = _dilated_attention(zb, batch=batch, seq=seq, units=B_UNITS)
            mkv = _mem_kv(mem, b_norm_mem[l], b_w_mem_kv[l])
            o_mem = _memory_attention(zb, mkv, batch=batch, seq=seq,
                                      q_col=B_UNITS["mem_q"] * MEM_Q, block_width=MEM_Q)
            h, xn = _out_proj(o_dil, o_mem, b_w_out[l].astype(BF16), h, b_norm_ffn[l])
            last = l == n_b - 1
            h = _ffn(h, xn, b_w_gate[l].astype(BF16), b_w_up[l].astype(BF16),
                     b_w_down[l].astype(BF16), final_norm if last else unit_gain, final_norm=last)

    return h.reshape(batch, seq, d)
```

```python
import functools
import math

import numpy as np
import jax
import jax.numpy as jnp
from jax import lax
from jax.experimental import pallas as pl
from jax.experimental.pallas import tpu as pltpu

F32 = jnp.float32
BF16 = jnp.bfloat16

HEAD_DIM = 128
N_HEADS_A = 12
N_KV_A = 2
HPG_A = N_HEADS_A // N_KV_A
CMP_LEN = 32
CMP_STRIDE = 16
CMP_HIDDEN = 256
SLC_BLK = 64
SLC_SHIFT = SLC_BLK.bit_length() - 1
N_SEL = 16
WIN_A = 512
DIL_CONFIGS = ((128, 1), (512, 4), (2048, 16))
N_DIL_GROUPS = len(DIL_CONFIGS)
DIL_HEADS = 4
N_MEM_HEADS = 4
ROPE_THETA = 10000.0
EPS = 1e-6
NEG_INF = -1e30
TINY = 1e-30
SCALE = HEAD_DIM ** -0.5
LOG2E = math.log2(math.e)

A_Q = N_HEADS_A * HEAD_DIM
A_KV = 6 * N_KV_A * HEAD_DIM
A_GATE = 3 * N_HEADS_A
MEM_Q = N_MEM_HEADS * HEAD_DIM
B_Q = N_DIL_GROUPS * DIL_HEADS * HEAD_DIM

LANES = 128
SUBLANES = 8
VMEM_LIMIT_BYTES = 56 * 1024 * 1024

PROJ_TM = 1024
PROJ_TM_BF16 = 2048
FFN_UP_TM = 2048
FFN_TM = 512
FFN_TF = 512
OUT_TM = 512
MERGE_TM = 1024
NSA_TQ = 128
NSA_KC = 512
DIL_TQ = 256
DIL_DENSE_MAX = 4
MEM_TQ = 1024

NT_DIMS = (((1,), (1,)), ((), ()))
TN_DIMS = (((0,), (0,)), ((), ()))


def _compiler_params(semantics):
    return pltpu.CompilerParams(dimension_semantics=semantics,
                                vmem_limit_bytes=VMEM_LIMIT_BYTES)


def _rms_rows(x, g):
    ms = jnp.mean(x * x, axis=-1, keepdims=True)
    return x * lax.rsqrt(ms + EPS) * g


def _dot(a, b):
    return jnp.dot(a, b, preferred_element_type=F32)


def _dot_nt(a, b):
    return lax.dot_general(a, b, NT_DIMS, preferred_element_type=F32)


def _dot_tn(a, b):
    return lax.dot_general(a, b, TN_DIMS, preferred_element_type=F32)


def _norm_proj_kernel(x_ref, g_ref, w_ref, cs_ref, cos_ref, sin_ref, o_ref, *xn_ref,
                      n_rope_blocks, tn):
    j = pl.program_id(1)
    if xn_ref:
        @pl.when(j == 0)
        def _():
            xn_ref[0][...] = _rms_rows(x_ref[...], g_ref[...]).astype(BF16)

        xn = xn_ref[0][...]
    else:
        xn = x_ref[...]
    acc = _dot(xn, w_ref[...]) * cs_ref[...]

    if n_rope_blocks > 0:
        roped = j < n_rope_blocks
        c = jnp.where(roped, cos_ref[...], 1.0)
        s = jnp.where(roped, sin_ref[...], 0.0)
        for h in range(tn // HEAD_DIM):
            y = acc[:, h * HEAD_DIM:(h + 1) * HEAD_DIM]
            rot = pltpu.roll(y, HEAD_DIM // 2, 1)
            o_ref[:, h * HEAD_DIM:(h + 1) * HEAD_DIM] = (y * c + rot * s).astype(o_ref.dtype)
    else:
        o_ref[...] = acc.astype(o_ref.dtype)


def _norm_proj(x, g, w_bf, col_scale, cosf, sinf, *, tn, n_rope_blocks, seq, tm=PROJ_TM):
    m, d = x.shape
    n = w_bf.shape[1]
    tm = min(tm, m)
    assert m % tm == 0 and n % tn == 0 and seq % tm == 0
    pos_blocks = seq // tm
    scratch = [] if x.dtype == BF16 else [pltpu.VMEM((tm, d), BF16)]
    kern = functools.partial(_norm_proj_kernel, n_rope_blocks=n_rope_blocks, tn=tn)
    return pl.pallas_call(
        kern,
        grid=(m // tm, n // tn),
        in_specs=[
            pl.BlockSpec((tm, d), lambda i, j: (i, 0)),
            pl.BlockSpec((1, d), lambda i, j: (0, 0)),
            pl.BlockSpec((d, tn), lambda i, j: (0, j)),
            pl.BlockSpec((1, tn), lambda i, j: (0, j)),
            pl.BlockSpec((tm, HEAD_DIM), lambda i, j: (i % pos_blocks, 0)),
            pl.BlockSpec((tm, HEAD_DIM), lambda i, j: (i % pos_blocks, 0)),
        ],
        out_specs=pl.BlockSpec((tm, tn), lambda i, j: (i, j)),
        out_shape=jax.ShapeDtypeStruct((m, n), BF16),
        scratch_shapes=scratch,
        compiler_params=_compiler_params(("parallel", "arbitrary")),
        name="norm_proj",
    )(x, g.reshape(1, d), w_bf, col_scale, cosf, sinf)


def _class_perm(tm, r):
    dst = np.arange(tm)
    c, u = dst // (tm // r), dst % (tm // r)
    perm = np.zeros((tm, tm), np.float32)
    perm[dst, u * r + c] = 1.0
    return perm


def _to_class_kernel(p_ref, *refs, r):
    n = len(refs) // 2
    for x_ref, o_ref in zip(refs[:n], refs[n:]):
        y = _dot(p_ref[...], x_ref[...]).astype(o_ref.dtype)
        rows = y.shape[0] // r
        for c in range(r):
            o_ref[c] = y[c * rows:(c + 1) * rows, :]


def _to_class_order(x, slabs, r, *, batch, seq):
    tm = min(PROJ_TM, seq)
    nblk = seq // tm
    perm = jnp.asarray(_class_perm(tm, r), x.dtype)
    return pl.pallas_call(
        functools.partial(_to_class_kernel, r=r),
        grid=(batch * nblk,),
        in_specs=[pl.BlockSpec((tm, tm), lambda i: (0, 0))]
        + [pl.BlockSpec((tm, w), lambda i, cb=cb: (i, cb)) for cb, w in slabs],
        out_specs=[pl.BlockSpec((None, r, tm // r, w), lambda i: (i // nblk, 0, i % nblk, 0))
                   for _, w in slabs],
        out_shape=[jax.ShapeDtypeStruct((batch, r, seq // r, w), x.dtype) for _, w in slabs],
        compiler_params=_compiler_params(("parallel",)),
        name="to_class_order",
    )(perm, *([x] * len(slabs)))


def _from_class_kernel(pt_ref, x_ref, o_ref, *, r):
    x = jnp.concatenate([x_ref[c] for c in range(r)], axis=0)
    o_ref[...] = _dot(pt_ref[...], x).astype(o_ref.dtype)


def _from_class_order(xc, *, batch, seq):
    _, r, _, w = xc.shape
    tm = min(PROJ_TM, seq)
    nblk = seq // tm
    perm_t = jnp.asarray(_class_perm(tm, r).T, xc.dtype)
    return pl.pallas_call(
        functools.partial(_from_class_kernel, r=r),
        grid=(batch * nblk,),
        in_specs=[pl.BlockSpec((tm, tm), lambda i: (0, 0)),
                  pl.BlockSpec((None, r, tm // r, w), lambda i: (i // nblk, 0, i % nblk, 0))],
        out_specs=pl.BlockSpec((tm, w), lambda i: (i, 0)),
        out_shape=jax.ShapeDtypeStruct((batch * seq, w), xc.dtype),
        compiler_params=_compiler_params(("parallel",)),
        name="from_class_order",
    )(perm_t, xc)


def _compress_kernel(x_ref, pe_ref, w1_ref, w2_ref, o_ref):
    n_planes, n_rows, _ = x_ref.shape
    ylo = yhi = None
    for l in range(n_planes):
        x = x_ref[l].astype(F32)
        xlo = (x + pe_ref[l:l + 1, :]).astype(BF16)
        xhi = (x + pe_ref[n_planes + l:n_planes + l + 1, :]).astype(BF16)
        dlo = _dot(xlo, w1_ref[l * HEAD_DIM:(l + 1) * HEAD_DIM, :])
        dhi = _dot(xhi, w1_ref[(n_planes + l) * HEAD_DIM:(n_planes + l + 1) * HEAD_DIM, :])
        ylo = dlo if ylo is None else ylo + dlo
        yhi = dhi if yhi is None else yhi + dhi
    hid = ylo + pltpu.roll(yhi, n_rows - 1, 0)
    act = (hid * jax.nn.sigmoid(hid)).astype(BF16)
    o_ref[...] = _dot(act, w2_ref[...]).astype(o_ref.dtype)


def _compress(xc, pe, w1_bf, w2_bf):
    batch, planes, nrow, gd = xc.shape
    ng = gd // HEAD_DIM
    return pl.pallas_call(
        _compress_kernel,
        grid=(batch, ng),
        in_specs=[
            pl.BlockSpec((None, planes, nrow, HEAD_DIM), lambda b, g: (b, 0, 0, g)),
            pl.BlockSpec((CMP_LEN, HEAD_DIM), lambda b, g: (0, 0)),
            pl.BlockSpec((CMP_LEN * HEAD_DIM, CMP_HIDDEN), lambda b, g: (0, 0)),
            pl.BlockSpec((CMP_HIDDEN, HEAD_DIM), lambda b, g: (0, 0)),
        ],
        out_specs=pl.BlockSpec((None, None, nrow, HEAD_DIM), lambda b, g: (b, g, 0, 0)),
        out_shape=jax.ShapeDtypeStruct((batch, ng, nrow, HEAD_DIM), BF16),
        compiler_params=_compiler_params(("parallel", "arbitrary")),
        name="nsa_compress",
    )(xc, pe, w1_bf, w2_bf)


def _block_ranks(score, jrow):
    n_blk = score.shape[0]
    groups = n_blk // SUBLANES
    blocks = [score[SUBLANES * r:SUBLANES * (r + 1), :] for r in range(groups)]
    rows = [jrow[SUBLANES * r:SUBLANES * (r + 1), :] for r in range(groups)]
    ranks = [jnp.zeros(blocks[0].shape, F32) for _ in range(groups)]
    for j in range(n_blk):
        rj = score[j:j + 1, :]
        for r in range(groups):
            if r > j // SUBLANES:
                ahead = rj >= blocks[r]
            elif r < j // SUBLANES:
                ahead = rj > blocks[r]
            else:
                ahead = (rj > blocks[r]) | ((rj == blocks[r]) & (rows[r] > j))
            ranks[r] = ranks[r] + jnp.where(ahead, 1.0, 0.0)
    return jnp.concatenate(ranks, axis=0)


def _nsa_kernel(q_ref, kc_ref, vc_ref, ks_ref, e_ref, vs_ref, kw_ref, vw_ref, gz_ref, gb_ref,
                o_ref, qa_ref, sa_ref, sb_ref, sw_ref, ow_ref, m_ref, l_ref, acc_ref, *, tq, seq):
    hq = HPG_A
    groups = range(N_KV_A)
    n_cmp_rows = kc_ref.shape[1]
    n_slc = seq // SLC_BLK
    qi = pl.program_id(1)
    s0 = qi * tq
    t_row = s0 + lax.broadcasted_iota(jnp.int32, (1, tq), 1)

    def lanes(h):
        return slice(h * tq, (h + 1) * tq)

    def gcols(g):
        return slice(g * HEAD_DIM, (g + 1) * HEAD_DIM)

    def tile_heads(x):
        return jnp.concatenate([x] * hq, axis=1)

    wlen = WIN_A + tq
    ws = pl.multiple_of(jnp.maximum(s0 - WIN_A, 0), LANES)
    c_end = lax.broadcasted_iota(jnp.int32, (n_cmp_rows, 1), 0) * CMP_STRIDE + (CMP_LEN - 1)
    cbias = tile_heads(jnp.where(c_end <= t_row, 0.0, NEG_INF))
    any_cmp = tile_heads(t_row >= CMP_LEN - 1)
    jrow = lax.broadcasted_iota(jnp.int32, (n_slc, 1), 0)
    ccol = lax.broadcasted_iota(jnp.int32, (1, n_cmp_rows), 1)
    lo = (SLC_BLK // CMP_STRIDE) * jrow - (CMP_LEN // CMP_STRIDE - 1)
    hi = (SLC_BLK // CMP_STRIDE) * jrow + (SLC_BLK // CMP_STRIDE - 1)
    mmap = jnp.where((ccol >= lo) & (ccol <= hi), 1.0, 0.0).astype(BF16)
    cur = t_row >> SLC_SHIFT
    forced = (jrow == 0) | (jrow == cur) | (jrow == cur - 1)
    assert n_slc <= LANES and tq == LANES

    q6, sc = [], []
    for g in groups:
        for h in range(hq):
            head = g * hq + h
            qa_ref[g, lanes(h), 0:HEAD_DIM] = q_ref[:, head * HEAD_DIM:(head + 1) * HEAD_DIM]
        q6.append(qa_ref[g, :, 0:HEAD_DIM])
        sc.append(_dot_nt(kc_ref[g], q6[g]) + cbias)
    for g in groups:
        sw_ref[g] = _dot_nt(kw_ref[pl.ds(ws, wlen), gcols(g)], q6[g])

    o_cmp, score = [], []
    for g in groups:
        ec = jnp.exp2(sc[g] - jnp.max(sc[g], axis=0, keepdims=True))
        den = jnp.maximum(jnp.sum(ec, axis=0, keepdims=True), TINY)
        pc = ec * jnp.where(any_cmp, 1.0 / den, 0.0)
        o_cmp.append(_dot_tn(vc_ref[g], pc.astype(BF16)))
        psum = pc[:, lanes(0)]
        for h in range(1, hq):
            psum = psum + pc[:, lanes(h)]
        p1 = psum.astype(BF16)
        r1 = psum - p1.astype(F32)
        p2 = r1.astype(BF16)
        p3 = (r1 - p2.astype(F32)).astype(BF16)
        imp = _dot(mmap, p1) + _dot(mmap, p2) + _dot(mmap, p3)
        score.append(jnp.where(forced, 1e9, jnp.where(jrow <= cur, imp, -1e9)))

    for g in groups:
        rank = _block_ranks(score[g], jrow)
        sel_bias = jnp.where((rank < min(N_SEL, n_slc)) & (jrow <= cur), 0.0, NEG_INF)
        bias_q = jnp.transpose(jnp.concatenate(
            [sel_bias, jnp.zeros((LANES - n_slc, tq), F32)], axis=0)).astype(BF16)
        for h in range(hq):
            qa_ref[g, lanes(h), HEAD_DIM:HEAD_DIM + LANES] = bias_q

    m_ref[...] = jnp.full(m_ref.shape, NEG_INF, F32)
    l_ref[...] = jnp.zeros(l_ref.shape, F32)
    acc_ref[...] = jnp.zeros(acc_ref.shape, F32)

    def scores(g, c, s_ref):
        k0 = pl.multiple_of(c * NSA_KC, NSA_KC)
        k_aug = jnp.concatenate([ks_ref[pl.ds(k0, NSA_KC), gcols(g)],
                                 e_ref[pl.ds(k0, NSA_KC), :]], axis=1)
        s_ref[g] = _dot_nt(k_aug, qa_ref[g])

    def softmax_pv(g, c, s_ref):
        s = s_ref[g]
        m_old = m_ref[g]
        m_new = jnp.maximum(m_old, jnp.max(s, axis=0, keepdims=True))
        alpha = jnp.exp2(m_old - m_new)
        p = jnp.exp2(s - m_new)
        l_ref[g] = alpha * l_ref[g] + jnp.sum(p, axis=0, keepdims=True)
        k0 = pl.multiple_of(c * NSA_KC, NSA_KC)
        pv = _dot_tn(vs_ref[pl.ds(k0, NSA_KC), gcols(g)], p.astype(BF16))
        acc_ref[g] = alpha * acc_ref[g] + pv
        m_ref[g] = m_new

    last = s0 // NSA_KC
    for g in groups:
        scores(g, 0, sa_ref)

    dist = t_row - (ws + lax.broadcasted_iota(jnp.int32, (wlen, 1), 0))
    wbias = tile_heads(jnp.where((dist >= 0) & (dist < WIN_A), 0.0, NEG_INF))
    for g in groups:
        sw = sw_ref[g] + wbias
        ew = jnp.exp2(sw - jnp.max(sw, axis=0, keepdims=True))
        ow_ref[g] = (_dot_tn(vw_ref[pl.ds(ws, wlen), gcols(g)], ew.astype(BF16))
                     * (1.0 / jnp.sum(ew, axis=0, keepdims=True)))

    def chunk_pair(i, carry):
        for g in groups:
            scores(g, 2 * i + 1, sb_ref)
            softmax_pv(g, 2 * i, sa_ref)
        for g in groups:
            scores(g, 2 * i + 2, sa_ref)
            softmax_pv(g, 2 * i + 1, sb_ref)
        return carry

    lax.fori_loop(0, last // 2, chunk_pair, 0)

    def last_chunk(g, s_ref):
        diag = pl.multiple_of(s0 - last * NSA_KC, LANES)
        krow = lax.broadcasted_iota(jnp.int32, (tq, 1), 0)
        lane = lax.broadcasted_iota(jnp.int32, (1, tq), 1)
        causal = tile_heads(jnp.where(krow <= lane, 0.0, NEG_INF))
        s_ref[g, pl.ds(diag, tq), :] = s_ref[g, pl.ds(diag, tq), :] + causal
        softmax_pv(g, last, s_ref)

    @pl.when(last % 2 == 0)
    def _():
        for g in groups:
            last_chunk(g, sa_ref)

    @pl.when(last % 2 == 1)
    def _():
        for g in groups:
            scores(g, last, sb_ref)
            softmax_pv(g, last - 1, sa_ref)
        for g in groups:
            last_chunk(g, sb_ref)

    gates_t = jnp.transpose(jax.nn.sigmoid(gz_ref[...].astype(F32) + gb_ref[...]))
    for g in groups:
        o_slc = acc_ref[g] * (1.0 / l_ref[g])
        for h in range(hq):
            head = g * hq + h
            o_h = (gates_t[3 * head:3 * head + 1, :] * o_cmp[g][:, lanes(h)]
                   + gates_t[3 * head + 1:3 * head + 2, :] * o_slc[:, lanes(h)]
                   + gates_t[3 * head + 2:3 * head + 3, :] * ow_ref[g, :, lanes(h)])
            o_ref[:, head * HEAD_DIM:(head + 1) * HEAD_DIM] = jnp.transpose(o_h).astype(o_ref.dtype)


def _nsa_attention(z, kc, vc, gate_bias, *, batch, seq, units):
    tq = NSA_TQ
    nq = seq // tq
    n_cmp_rows = kc.shape[2]
    ng = N_KV_A
    hl = HPG_A * tq
    gw = ng * HEAD_DIM
    for name in ("k_slc", "v_slc", "k_win", "v_win"):
        assert units[name] % ng == 0
    kern = functools.partial(_nsa_kernel, tq=tq, seq=seq)

    def slab(name):
        return pl.BlockSpec((seq, gw), lambda b, i: (b, units[name] // ng))

    cmp_spec = pl.BlockSpec((None, ng, n_cmp_rows, HEAD_DIM), lambda b, i: (b, 0, 0, 0))

    onehot = jnp.asarray(np.arange(seq)[:, None] // SLC_BLK == np.arange(LANES)[None, :], BF16)
    return pl.pallas_call(
        kern,
        grid=(batch, nq),
        in_specs=[
            pl.BlockSpec((tq, A_Q), lambda b, i: (b * nq + i, 0)),
            cmp_spec,
            cmp_spec,
            slab("k_slc"),
            pl.BlockSpec((seq, LANES), lambda b, i: (0, 0)),
            slab("v_slc"),
            slab("k_win"),
            slab("v_win"),
            pl.BlockSpec((tq, LANES), lambda b, i: (b * nq + i, units["gates"])),
            pl.BlockSpec((1, LANES), lambda b, i: (0, 0)),
        ],
        out_specs=pl.BlockSpec((tq, A_Q), lambda b, i: (b * nq + i, 0)),
        out_shape=jax.ShapeDtypeStruct((batch * seq, A_Q), BF16),
        scratch_shapes=[
            pltpu.VMEM((ng, hl, HEAD_DIM + LANES), BF16),
            pltpu.VMEM((ng, NSA_KC, hl), F32),
            pltpu.VMEM((ng, NSA_KC, hl), F32),
            pltpu.VMEM((ng, WIN_A + tq, hl), F32),
            pltpu.VMEM((ng, HEAD_DIM, hl), F32),
            pltpu.VMEM((ng, 1, hl), F32),
            pltpu.VMEM((ng, 1, hl), F32),
            pltpu.VMEM((ng, HEAD_DIM, hl), F32),
        ],
        compiler_params=_compiler_params(("parallel", "arbitrary")),
        name="nsa_attention",
    )(z, kc, vc, z, onehot, z, z, z, z, gate_bias)


def _band_attn_kernel(*refs, tu, parts):
    q_refs = refs[:len(parts)]
    k_ref, v_ref, o_ref, lse_ref = refs[len(parts):]
    u0 = pl.program_id(2) * tu
    n_seq = k_ref.shape[0]
    windows = []
    for lk, span, stride in parts:
        ks = pl.multiple_of(jnp.clip(u0 - span, 0, n_seq - lk), LANES)
        dist = (u0 + lax.broadcasted_iota(jnp.int32, (tu, 1), 0)
                - (ks + lax.broadcasted_iota(jnp.int32, (1, lk), 1)))
        keep = (dist >= 0) & (dist <= span)
        if stride > 1:
            keep = keep & ((dist & (stride - 1)) == 0)
        windows.append((ks, lk, jnp.where(keep, 0.0, NEG_INF)))
    lane = lax.broadcasted_iota(jnp.int32, (1, LANES), 1)
    lse_tile = jnp.zeros((tu, LANES), F32)

    def head_cols(h):
        return slice(h * HEAD_DIM, (h + 1) * HEAD_DIM)

    scores = [[_dot_nt(q_ref[:, head_cols(h)], k_ref[pl.ds(ks, lk), head_cols(h)]) + bias
               for q_ref, (ks, lk, bias) in zip(q_refs, windows)]
              for h in range(DIL_HEADS)]
    probs, inv_l = [], []
    for h, s_parts in enumerate(scores):
        s = jnp.concatenate(s_parts, axis=1)
        m = jnp.max(s, axis=-1, keepdims=True)
        p = jnp.exp2(s - m)
        l = jnp.sum(p, axis=-1, keepdims=True)
        probs.append(p.astype(BF16))
        inv_l.append(1.0 / l)
        lse_tile = jnp.where(lane == h, m + jnp.log2(l), lse_tile)
    for h in range(DIL_HEADS):
        o, col = None, 0
        for ks, lk, _ in windows:
            pv = _dot(probs[h][:, col:col + lk], v_ref[pl.ds(ks, lk), head_cols(h)])
            o = pv if o is None else o + pv
            col += lk
        o_ref[:, head_cols(h)] = (o * inv_l[h]).astype(o_ref.dtype)
    lse_ref[...] = lse_tile


def _dil_merge_kernel(*refs):
    out_ref = refs[-1]
    calls = list(zip(refs[0:-1:2], refs[1:-1:2]))
    for h in range(DIL_HEADS):
        cols = slice(h * HEAD_DIM, (h + 1) * HEAD_DIM)
        shape = (out_ref.shape[0], HEAD_DIM)
        lses = [jnp.broadcast_to(l_ref[:, h:h + 1], shape) for _, l_ref in calls]
        top = functools.reduce(jnp.maximum, lses)
        ws = [jnp.exp2(lse - top) for lse in lses]
        num = sum(w * o_ref[:, cols].astype(F32) for w, (o_ref, _) in zip(ws, calls))
        out_ref[:, cols] = (num * (1.0 / sum(ws))).astype(out_ref.dtype)


def _dilated_attention(zb, *, batch, seq, units):
    n = zb.shape[1]
    width = DIL_HEADS * HEAD_DIM
    tu = DIL_TQ

    def band_call(r, q_units, parts, operands, k_unit, v_unit):
        n_seq = seq // r

        def z_spec(rows, unit, whole):
            return pl.BlockSpec((None, None, rows, width),
                                lambda b, c, i: (b, c, 0 if whole else i, unit))

        def out_spec(cols):
            return pl.BlockSpec((None, None, tu, cols), lambda b, c, i: (b, c, i, 0))

        return pl.pallas_call(
            functools.partial(_band_attn_kernel, tu=tu, parts=parts),
            grid=(batch, r, n_seq // tu),
            in_specs=[z_spec(tu, u, False) for u in q_units]
            + [z_spec(n_seq, k_unit, True), z_spec(n_seq, v_unit, True)],
            out_specs=[out_spec(width), out_spec(LANES)],
            out_shape=[jax.ShapeDtypeStruct((batch, r, n_seq, width), BF16),
                       jax.ShapeDtypeStruct((batch, r, n_seq, LANES), F32)],
            compiler_params=_compiler_params(("parallel", "parallel", "arbitrary")),
            name="dilated_attention",
        )(*operands)

    def window(n_seq, span):
        assert n_seq % tu == 0 and span % LANES == 0 and tu % LANES == 0
        return min(tu + span, n_seq)

    dense = [(gi, w, r) for gi, (w, r) in enumerate(DIL_CONFIGS) if r <= DIL_DENSE_MAX]
    zv = zb.reshape(batch, 1, seq, n)
    o_d, lse_d = band_call(1, [units["q"] + gi for gi, _, _ in dense],
                           tuple((window(seq, w), w, r) for _, w, r in dense),
                           [zv] * (len(dense) + 2), units["k"], units["v"])
    results = [o_d.reshape(batch * seq, width), lse_d.reshape(batch * seq, LANES)]
    for gi, (w, r) in enumerate(DIL_CONFIGS):
        if r <= DIL_DENSE_MAX:
            continue
        slabs = [(units["q"] + gi, width), (units["k"], width), (units["v"], width)]
        o_c, lse_c = band_call(r, [0], ((window(seq // r, w // r), w // r, 1),),
                               _to_class_order(zb, slabs, r, batch=batch, seq=seq), 0, 0)
        results += [_from_class_order(o_c, batch=batch, seq=seq),
                    jnp.transpose(lse_c, (0, 2, 1, 3)).reshape(batch * seq, LANES)]

    m = batch * seq
    tm = min(MERGE_TM, m)
    o_spec = pl.BlockSpec((tm, width), lambda i: (i, 0))
    l_spec = pl.BlockSpec((tm, LANES), lambda i: (i, 0))
    return pl.pallas_call(
        _dil_merge_kernel,
        grid=(m // tm,),
        in_specs=[o_spec, l_spec] * (len(results) // 2),
        out_specs=o_spec,
        out_shape=jax.ShapeDtypeStruct((m, width), BF16),
        compiler_params=_compiler_params(("parallel",)),
        name="dilated_merge",
    )(*results)


def _mem_attn_kernel(q_ref, kv_ref, o_ref, *, q_offset):
    def cols(h, base=0):
        return slice(base + h * HEAD_DIM, base + (h + 1) * HEAD_DIM)

    heads = range(N_MEM_HEADS)
    scores = [_dot_nt(q_ref[:, cols(h, q_offset)], kv_ref[:, cols(h)]) for h in heads]
    probs = []
    for s in scores:
        e = jnp.exp(s - jnp.max(s, axis=-1, keepdims=True))
        probs.append((e / jnp.sum(e, axis=-1, keepdims=True)).astype(BF16))
    for h in heads:
        o_ref[:, cols(h)] = _dot(probs[h], kv_ref[:, cols(h, MEM_Q)]).astype(o_ref.dtype)


def _memory_attention(z, mkv, *, batch, seq, q_col, block_width):
    tq = MEM_TQ
    nq = seq // tq
    n_mem = mkv.shape[0] // batch
    q_block, q_offset = divmod(q_col, block_width)
    assert q_offset + MEM_Q <= block_width
    return pl.pallas_call(
        functools.partial(_mem_attn_kernel, q_offset=q_offset),
        grid=(batch, nq),
        in_specs=[
            pl.BlockSpec((tq, block_width), lambda b, i: (b * nq + i, q_block)),
            pl.BlockSpec((n_mem, 2 * MEM_Q), lambda b, i: (b, 0)),
        ],
        out_specs=pl.BlockSpec((tq, MEM_Q), lambda b, i: (b * nq + i, 0)),
        out_shape=jax.ShapeDtypeStruct((batch * seq, MEM_Q), BF16),
        compiler_params=_compiler_params(("parallel", "arbitrary")),
        name="memory_attention",
    )(z, mkv)


def _out_proj_kernel(a1_ref, a2_ref, w_ref, h_ref, g_ref, o_ref, xn_ref):
    a = jnp.concatenate([a1_ref[...], a2_ref[...]], axis=1)
    y = h_ref[...] + _dot(a, w_ref[...])
    o_ref[...] = y
    xn_ref[...] = _rms_rows(y, g_ref[...]).astype(xn_ref.dtype)


def _out_proj(a1, a2, w_bf, h, g_next):
    m, d = h.shape
    tm = min(OUT_TM, m)
    k1, k2 = a1.shape[1], a2.shape[1]
    assert w_bf.shape[0] == k1 + k2
    row = pl.BlockSpec((tm, d), lambda i: (i, 0))
    return pl.pallas_call(
        _out_proj_kernel,
        grid=(m // tm,),
        in_specs=[
            pl.BlockSpec((tm, k1), lambda i: (i, 0)),
            pl.BlockSpec((tm, k2), lambda i: (i, 0)),
            pl.BlockSpec((k1 + k2, d), lambda i: (0, 0), pipeline_mode=pl.Buffered(1)),
            row,
            pl.BlockSpec((1, d), lambda i: (0, 0)),
        ],
        out_specs=[row, row],
        out_shape=[jax.ShapeDtypeStruct((m, d), F32), jax.ShapeDtypeStruct((m, d), BF16)],
        compiler_params=_compiler_params(("parallel",)),
        name="out_proj",
    )(a1, a2, w_bf, h, g_next.reshape(1, d))


def _ffn_up_kernel(xn_ref, wg_ref, wu_ref, o_ref):
    xn = xn_ref[...]
    gate = _dot(xn, wg_ref[...])
    up = _dot(xn, wu_ref[...])
    o_ref[...] = (gate * jax.nn.sigmoid(gate) * up).astype(o_ref.dtype)


def _ffn_down_kernel(a_ref, w_ref, h_ref, g_ref, o_ref, *xn_ref, final_norm):
    y = h_ref[...] + _dot(a_ref[...], w_ref[...])
    yn = _rms_rows(y, g_ref[...])
    if final_norm:
        o_ref[...] = yn
    else:
        o_ref[...] = y
        xn_ref[0][...] = yn.astype(xn_ref[0].dtype)


def _ffn(h, xn, wg_bf, wu_bf, wd_bf, gain, *, final_norm):
    m, d = h.shape
    dff = wg_bf.shape[1]
    tm, tf = min(FFN_UP_TM, m), FFN_TF
    assert m % tm == 0 and dff % tf == 0
    act = pl.pallas_call(
        _ffn_up_kernel,
        grid=(m // tm, dff // tf),
        in_specs=[
            pl.BlockSpec((tm, d), lambda i, f: (i, 0)),
            pl.BlockSpec((d, tf), lambda i, f: (0, f)),
            pl.BlockSpec((d, tf), lambda i, f: (0, f)),
        ],
        out_specs=pl.BlockSpec((tm, tf), lambda i, f: (i, f)),
        out_shape=jax.ShapeDtypeStruct((m, dff), BF16),
        compiler_params=_compiler_params(("parallel", "arbitrary")),
        name="ffn_up",
    )(xn, wg_bf, wu_bf)

    tm = min(FFN_TM, m)
    kern = functools.partial(_ffn_down_kernel, final_norm=final_norm)
    row = pl.BlockSpec((tm, d), lambda i: (i, 0))
    h_shape = jax.ShapeDtypeStruct((m, d), F32)
    return pl.pallas_call(
        kern,
        grid=(m // tm,),
        in_specs=[
            pl.BlockSpec((tm, dff), lambda i: (i, 0)),
            pl.BlockSpec((dff, d), lambda i: (0, 0), pipeline_mode=pl.Buffered(1)),
            row,
            pl.BlockSpec((1, d), lambda i: (0, 0)),
        ],
        out_specs=row if final_norm else [row, row],
        out_shape=h_shape if final_norm else [h_shape, jax.ShapeDtypeStruct((m, d), BF16)],
        compiler_params=_compiler_params(("parallel",)),
        name="ffn_down",
    )(act, wd_bf, h, gain.reshape(1, d))


def _rope_tables(seq):
    inv = 1.0 / (ROPE_THETA ** (jnp.arange(0, HEAD_DIM, 2, dtype=F32) / HEAD_DIM))
    ang = jnp.arange(seq, dtype=F32)[:, None] * inv[None, :]
    cos, sin = jnp.cos(ang), jnp.sin(ang)
    return jnp.concatenate([cos, cos], axis=1), jnp.concatenate([-sin, sin], axis=1)


A_UNITS = {"q": 0, "k_cmp": 12, "k_slc": 14, "k_win": 16, "v_cmp": 18, "v_slc": 20,
           "v_win": 22, "mem_q": 24, "gates": 28}
B_UNITS = {"q": 0, "k": 3, "mem_q": 4, "v": 5}
B_TN = 2 * DIL_HEADS * HEAD_DIM
A_TN = 6 * HEAD_DIM
A_NPAD = 30 * HEAD_DIM
A_ROPE_BLOCKS = 3


def _layer_a_weight(w_in):
    kv0 = A_Q

    def kv_cols(branch):
        return w_in[:, kv0 + branch * N_KV_A * HEAD_DIM:kv0 + (branch + 1) * N_KV_A * HEAD_DIM]

    gate0 = A_Q + A_KV
    mem0 = gate0 + A_GATE
    w = jnp.concatenate([w_in[:, :A_Q], kv_cols(0), kv_cols(2), kv_cols(4), kv_cols(1),
                         kv_cols(3), kv_cols(5), w_in[:, mem0:mem0 + MEM_Q],
                         w_in[:, gate0:mem0]], axis=1)
    w = jnp.pad(w, ((0, 0), (0, A_NPAD - w.shape[1])))
    scale = np.ones((1, A_NPAD), np.float32)
    scale[0, :A_Q] = SCALE * LOG2E
    scale[0, A_UNITS["mem_q"] * HEAD_DIM:A_UNITS["mem_q"] * HEAD_DIM + MEM_Q] = SCALE
    return w.astype(BF16), jnp.asarray(scale)


def _layer_a(h, hn, mem, cosf, sinf, p, *, batch, seq):
    w_in_bf, col_scale = _layer_a_weight(p["w_in"])
    z = _norm_proj(h if hn is None else hn, p["norm_attn"], w_in_bf, col_scale, cosf, sinf,
                   tn=A_TN, n_rope_blocks=A_ROPE_BLOCKS, seq=seq)

    gw = N_KV_A * HEAD_DIM
    k_raw, v_raw = _to_class_order(
        z, [(A_UNITS["k_cmp"] * HEAD_DIM // gw, gw), (A_UNITS["v_cmp"] * HEAD_DIM // gw, gw)],
        CMP_STRIDE, batch=batch, seq=seq)
    kc = _compress(k_raw, p["cmp_pe_k"], p["cmp_w1_k"].astype(BF16), p["cmp_w2_k"].astype(BF16))
    vc = _compress(v_raw, p["cmp_pe_v"], p["cmp_w1_v"].astype(BF16), p["cmp_w2_v"].astype(BF16))
    gb = jnp.pad(p["gate_bias"], (0, LANES - A_GATE)).reshape(1, LANES)
    o_nsa = _nsa_attention(z, kc, vc, gb, batch=batch, seq=seq, units=A_UNITS)

    mkv = _mem_kv(mem, p["norm_mem"], p["w_mem_kv"])
    o_mem = _memory_attention(z, mkv, batch=batch, seq=seq,
                              q_col=A_UNITS["mem_q"] * HEAD_DIM, block_width=MEM_Q)
    return _out_proj(o_nsa, o_mem, p["w_out"].astype(BF16), h, p["norm_ffn"])


def _mem_kv(mem, norm_mem, w_mem_kv):
    b, m, d = mem.shape
    ones = jnp.ones((1, w_mem_kv.shape[1]), F32)
    dummy = jnp.zeros((m, HEAD_DIM), F32)
    return _norm_proj(mem.reshape(b * m, d), norm_mem, w_mem_kv.astype(BF16), ones, dummy, dummy,
                      tn=MEM_Q, n_rope_blocks=0, seq=m, tm=m)


def kernel(x, mem, a_norm_attn, a_w_in, a_gate_bias, a_cmp_pe_k, a_cmp_w1_k, a_cmp_w2_k, a_cmp_pe_v, a_cmp_w1_v, a_cmp_w2_v, a_norm_mem, a_w_mem_kv, a_w_out, a_norm_ffn, a_w_gate, a_w_up, a_w_down, kv_norm, w_kv_shared, b_norm_attn, b_w_in, b_norm_mem, b_w_mem_kv, b_w_out, b_norm_ffn, b_w_gate, b_w_up, b_w_down, final_norm):
    batch, seq, d = x.shape
    n_a = a_w_in.shape[0]
    n_b = b_w_in.shape[0]
    cosf, sinf = _rope_tables(seq)
    h = x.reshape(batch * seq, d)
    unit_gain = jnp.ones((d,), F32)
    hn = None

    for l in range(n_a):
        p = {"norm_attn": a_norm_attn[l], "w_in": a_w_in[l], "gate_bias": a_gate_bias[l],
             "cmp_pe_k": a_cmp_pe_k[l], "cmp_w1_k": a_cmp_w1_k[l], "cmp_w2_k": a_cmp_w2_k[l],
             "cmp_pe_v": a_cmp_pe_v[l], "cmp_w1_v": a_cmp_w1_v[l], "cmp_w2_v": a_cmp_w2_v[l],
             "norm_mem": a_norm_mem[l], "w_mem_kv": a_w_mem_kv[l], "w_out": a_w_out[l],
             "norm_ffn": a_norm_ffn[l]}
        h, xn = _layer_a(h, hn, mem, cosf, sinf, p, batch=batch, seq=seq)
        last = (l == n_a - 1) and n_b == 0
        ffn_w = (a_w_gate[l].astype(BF16), a_w_up[l].astype(BF16), a_w_down[l].astype(BF16))
        if last:
            h = _ffn(h, xn, *ffn_w, final_norm, final_norm=True)
        else:
            next_gain = a_norm_attn[l + 1] if l + 1 < n_a else unit_gain
            h, hn = _ffn(h, xn, *ffn_w, next_gain, final_norm=False)

    if n_b > 0:
        assert n_b == 1, "the shared K/V projection is fused into the single mixer-B layer"
        n_kv_half = w_kv_shared.shape[1] // 2
        for l in range(n_b):
            w_q = b_norm_attn[l][:, None] * b_w_in[l]
            w_kv = kv_norm[:, None] * w_kv_shared
            w_cat = jnp.concatenate([w_q[:, :B_Q], w_kv[:, :n_kv_half], w_q[:, B_Q:],
                                     w_kv[:, n_kv_half:]], axis=1).astype(BF16)
            b_scale = np.ones((1, w_cat.shape[1]), np.float32)
            b_scale[0, :B_Q] = SCALE * LOG2E
            b_scale[0, B_Q + n_kv_half:B_Q + n_kv_half + MEM_Q] = SCALE
            zb = _norm_proj(h if hn is None else hn, unit_gain, w_cat, jnp.asarray(b_scale),
                            cosf, sinf, tm=PROJ_TM if hn is None else PROJ_TM_BF16,
                            tn=B_TN, n_rope_blocks=(B_Q + n_kv_half) // B_TN, seq=seq)
            o_dil = _dilated_attention(zb, batch=batch, seq=seq, units=B_UNITS)
            mkv = _mem_kv(mem, b_norm_mem[l], b_w_mem_kv[l])
            o_mem = _memory_attention(zb, mkv, batch=batch, seq=seq,
                                      q_col=B_UNITS["mem_q"] * MEM_Q, block_width=MEM_Q)
            h, xn = _out_proj(o_dil, o_mem, b_w_out[l].astype(BF16), h, b_norm_ffn[l])
            h = _ffn(h, xn, b_w_gate[l].astype(BF16), b_w_up[l].astype(BF16),
                     b_w_down[l].astype(BF16), final_norm, final_norm=True)

    return h.reshape(batch, seq, d)
```

```python
import functools
import math

import numpy as np
import jax
import jax.numpy as jnp
from jax import lax
from jax.experimental import pallas as pl
from jax.experimental.pallas import tpu as pltpu

F32 = jnp.float32
BF16 = jnp.bfloat16

HEAD_DIM = 128
N_HEADS_A = 12
N_KV_A = 2
HPG_A = N_HEADS_A // N_KV_A
CMP_LEN = 32
CMP_STRIDE = 16
CMP_HIDDEN = 256
SLC_BLK = 64
SLC_SHIFT = SLC_BLK.bit_length() - 1
N_SEL = 16
WIN_A = 512
DIL_CONFIGS = ((128, 1), (512, 4), (2048, 16))
N_DIL_GROUPS = len(DIL_CONFIGS)
DIL_HEADS = 4
N_MEM_HEADS = 4
ROPE_THETA = 10000.0
EPS = 1e-6
NEG_INF = -1e30
TINY = 1e-30
SCALE = HEAD_DIM ** -0.5
LOG2E = math.log2(math.e)

A_Q = N_HEADS_A * HEAD_DIM
A_KV = 6 * N_KV_A * HEAD_DIM
A_GATE = 3 * N_HEADS_A
MEM_Q = N_MEM_HEADS * HEAD_DIM
B_Q = N_DIL_GROUPS * DIL_HEADS * HEAD_DIM

LANES = 128
SUBLANES = 8
VMEM_LIMIT_BYTES = 56 * 1024 * 1024

PROJ_TM = 1024
PROJ_TM_BF16 = 2048
FFN_UP_TM = 2048
FFN_TM = 512
FFN_TF = 512
OUT_TM = 512
MERGE_TM = 1024
NSA_TQ = 128
NSA_KC = 512
DIL_TQ = 256
DIL_DENSE_MAX = 4

NT_DIMS = (((1,), (1,)), ((), ()))
TN_DIMS = (((0,), (0,)), ((), ()))


def _compiler_params(semantics):
    return pltpu.CompilerParams(dimension_semantics=semantics,
                                vmem_limit_bytes=VMEM_LIMIT_BYTES)


def _rms_rows(x, g):
    ms = jnp.mean(x * x, axis=-1, keepdims=True)
    return x * lax.rsqrt(ms + EPS) * g


def _dot(a, b):
    return jnp.dot(a, b, preferred_element_type=F32)


def _dot_nt(a, b):
    return lax.dot_general(a, b, NT_DIMS, preferred_element_type=F32)


def _dot_tn(a, b):
    return lax.dot_general(a, b, TN_DIMS, preferred_element_type=F32)


def _norm_proj_kernel(x_ref, g_ref, w_ref, cs_ref, cos_ref, sin_ref, o_ref, *xn_ref,
                      n_rope_blocks, tn):
    j = pl.program_id(1)
    if xn_ref:
        @pl.when(j == 0)
        def _():
            xn_ref[0][...] = _rms_rows(x_ref[...], g_ref[...]).astype(BF16)

        xn = xn_ref[0][...]
    else:
        xn = x_ref[...]
    acc = _dot(xn, w_ref[...]) * cs_ref[...]

    if n_rope_blocks > 0:
        roped = j < n_rope_blocks
        c = jnp.where(roped, cos_ref[...], 1.0)
        s = jnp.where(roped, sin_ref[...], 0.0)
        for h in range(tn // HEAD_DIM):
            y = acc[:, h * HEAD_DIM:(h + 1) * HEAD_DIM]
            rot = pltpu.roll(y, HEAD_DIM // 2, 1)
            o_ref[:, h * HEAD_DIM:(h + 1) * HEAD_DIM] = (y * c + rot * s).astype(o_ref.dtype)
    else:
        o_ref[...] = acc.astype(o_ref.dtype)


def _norm_proj(x, g, w_bf, col_scale, cosf, sinf, *, tn, n_rope_blocks, seq, tm=PROJ_TM):
    m, d = x.shape
    n = w_bf.shape[1]
    tm = min(tm, m)
    assert m % tm == 0 and n % tn == 0 and seq % tm == 0
    pos_blocks = seq // tm
    scratch = [] if x.dtype == BF16 else [pltpu.VMEM((tm, d), BF16)]
    kern = functools.partial(_norm_proj_kernel, n_rope_blocks=n_rope_blocks, tn=tn)
    return pl.pallas_call(
        kern,
        grid=(m // tm, n // tn),
        in_specs=[
            pl.BlockSpec((tm, d), lambda i, j: (i, 0)),
            pl.BlockSpec((1, d), lambda i, j: (0, 0)),
            pl.BlockSpec((d, tn), lambda i, j: (0, j)),
            pl.BlockSpec((1, tn), lambda i, j: (0, j)),
            pl.BlockSpec((tm, HEAD_DIM), lambda i, j: (i % pos_blocks, 0)),
            pl.BlockSpec((tm, HEAD_DIM), lambda i, j: (i % pos_blocks, 0)),
        ],
        out_specs=pl.BlockSpec((tm, tn), lambda i, j: (i, j)),
        out_shape=jax.ShapeDtypeStruct((m, n), BF16),
        scratch_shapes=scratch,
        compiler_params=_compiler_params(("parallel", "arbitrary")),
        name="norm_proj",
    )(x, g.reshape(1, d), w_bf, col_scale, cosf, sinf)


def _class_perm(tm, r):
    dst = np.arange(tm)
    c, u = dst // (tm // r), dst % (tm // r)
    perm = np.zeros((tm, tm), np.float32)
    perm[dst, u * r + c] = 1.0
    return perm


def _to_class_kernel(p_ref, *refs, r):
    n = len(refs) // 2
    for x_ref, o_ref in zip(refs[:n], refs[n:]):
        y = _dot(p_ref[...], x_ref[...]).astype(o_ref.dtype)
        rows = y.shape[0] // r
        for c in range(r):
            o_ref[c] = y[c * rows:(c + 1) * rows, :]


def _to_class_order(x, slabs, r, *, batch, seq):
    tm = min(PROJ_TM, seq)
    nblk = seq // tm
    perm = jnp.asarray(_class_perm(tm, r), x.dtype)
    return pl.pallas_call(
        functools.partial(_to_class_kernel, r=r),
        grid=(batch * nblk,),
        in_specs=[pl.BlockSpec((tm, tm), lambda i: (0, 0))]
        + [pl.BlockSpec((tm, w), lambda i, cb=cb: (i, cb)) for cb, w in slabs],
        out_specs=[pl.BlockSpec((None, r, tm // r, w), lambda i: (i // nblk, 0, i % nblk, 0))
                   for _, w in slabs],
        out_shape=[jax.ShapeDtypeStruct((batch, r, seq // r, w), x.dtype) for _, w in slabs],
        compiler_params=_compiler_params(("parallel",)),
        name="to_class_order",
    )(perm, *([x] * len(slabs)))


def _from_class_kernel(pt_ref, x_ref, o_ref, *, r):
    x = jnp.concatenate([x_ref[c] for c in range(r)], axis=0)
    o_ref[...] = _dot(pt_ref[...], x).astype(o_ref.dtype)


def _from_class_order(xc, *, batch, seq):
    _, r, _, w = xc.shape
    tm = min(PROJ_TM, seq)
    nblk = seq // tm
    perm_t = jnp.asarray(_class_perm(tm, r).T, xc.dtype)
    return pl.pallas_call(
        functools.partial(_from_class_kernel, r=r),
        grid=(batch * nblk,),
        in_specs=[pl.BlockSpec((tm, tm), lambda i: (0, 0)),
                  pl.BlockSpec((None, r, tm // r, w), lambda i: (i // nblk, 0, i % nblk, 0))],
        out_specs=pl.BlockSpec((tm, w), lambda i: (i, 0)),
        out_shape=jax.ShapeDtypeStruct((batch * seq, w), xc.dtype),
        compiler_params=_compiler_params(("parallel",)),
        name="from_class_order",
    )(perm_t, xc)


def _compress_kernel(x_ref, pe_ref, w1_ref, w2_ref, o_ref):
    n_planes, n_rows, _ = x_ref.shape
    ylo = yhi = None
    for l in range(n_planes):
        x = x_ref[l].astype(F32)
        xlo = (x + pe_ref[l:l + 1, :]).astype(BF16)
        xhi = (x + pe_ref[n_planes + l:n_planes + l + 1, :]).astype(BF16)
        dlo = _dot(xlo, w1_ref[l * HEAD_DIM:(l + 1) * HEAD_DIM, :])
        dhi = _dot(xhi, w1_ref[(n_planes + l) * HEAD_DIM:(n_planes + l + 1) * HEAD_DIM, :])
        ylo = dlo if ylo is None else ylo + dlo
        yhi = dhi if yhi is None else yhi + dhi
    hid = ylo + pltpu.roll(yhi, n_rows - 1, 0)
    act = (hid * jax.nn.sigmoid(hid)).astype(BF16)
    o_ref[...] = _dot(act, w2_ref[...]).astype(o_ref.dtype)


def _compress(xc, pe, w1_bf, w2_bf):
    batch, planes, nrow, gd = xc.shape
    ng = gd // HEAD_DIM
    return pl.pallas_call(
        _compress_kernel,
        grid=(batch, ng),
        in_specs=[
            pl.BlockSpec((None, planes, nrow, HEAD_DIM), lambda b, g: (b, 0, 0, g)),
            pl.BlockSpec((CMP_LEN, HEAD_DIM), lambda b, g: (0, 0)),
            pl.BlockSpec((CMP_LEN * HEAD_DIM, CMP_HIDDEN), lambda b, g: (0, 0)),
            pl.BlockSpec((CMP_HIDDEN, HEAD_DIM), lambda b, g: (0, 0)),
        ],
        out_specs=pl.BlockSpec((None, None, nrow, HEAD_DIM), lambda b, g: (b, g, 0, 0)),
        out_shape=jax.ShapeDtypeStruct((batch, ng, nrow, HEAD_DIM), BF16),
        compiler_params=_compiler_params(("parallel", "arbitrary")),
        name="nsa_compress",
    )(xc, pe, w1_bf, w2_bf)


def _block_ranks(score, jrow):
    n_blk = score.shape[0]
    groups = n_blk // SUBLANES
    blocks = [score[SUBLANES * r:SUBLANES * (r + 1), :] for r in range(groups)]
    rows = [jrow[SUBLANES * r:SUBLANES * (r + 1), :] for r in range(groups)]
    ranks = [jnp.zeros(blocks[0].shape, F32) for _ in range(groups)]
    for j in range(n_blk):
        rj = score[j:j + 1, :]
        for r in range(groups):
            if r > j // SUBLANES:
                ahead = rj >= blocks[r]
            elif r < j // SUBLANES:
                ahead = rj > blocks[r]
            else:
                ahead = (rj > blocks[r]) | ((rj == blocks[r]) & (rows[r] > j))
            ranks[r] = ranks[r] + jnp.where(ahead, 1.0, 0.0)
    return jnp.concatenate(ranks, axis=0)


def _nsa_kernel(q_ref, kc_ref, vc_ref, ks_ref, e_ref, vs_ref, kw_ref, vw_ref, gz_ref, gb_ref,
                o_ref, qa_ref, sa_ref, sb_ref, sw_ref, ow_ref, m_ref, l_ref, acc_ref, *, tq, seq):
    hq = HPG_A
    groups = range(N_KV_A)
    n_cmp_rows = kc_ref.shape[1]
    n_slc = seq // SLC_BLK
    qi = pl.program_id(1)
    s0 = qi * tq
    t_row = s0 + lax.broadcasted_iota(jnp.int32, (1, tq), 1)

    def lanes(h):
        return slice(h * tq, (h + 1) * tq)

    def gcols(g):
        return slice(g * HEAD_DIM, (g + 1) * HEAD_DIM)

    def tile_heads(x):
        return jnp.concatenate([x] * hq, axis=1)

    wlen = WIN_A + tq
    ws = pl.multiple_of(jnp.maximum(s0 - WIN_A, 0), LANES)
    c_end = lax.broadcasted_iota(jnp.int32, (n_cmp_rows, 1), 0) * CMP_STRIDE + (CMP_LEN - 1)
    cbias = tile_heads(jnp.where(c_end <= t_row, 0.0, NEG_INF))
    any_cmp = tile_heads(t_row >= CMP_LEN - 1)
    jrow = lax.broadcasted_iota(jnp.int32, (n_slc, 1), 0)
    ccol = lax.broadcasted_iota(jnp.int32, (1, n_cmp_rows), 1)
    lo = (SLC_BLK // CMP_STRIDE) * jrow - (CMP_LEN // CMP_STRIDE - 1)
    hi = (SLC_BLK // CMP_STRIDE) * jrow + (SLC_BLK // CMP_STRIDE - 1)
    mmap = jnp.where((ccol >= lo) & (ccol <= hi), 1.0, 0.0).astype(BF16)
    cur = t_row >> SLC_SHIFT
    forced = (jrow == 0) | (jrow == cur) | (jrow == cur - 1)
    assert n_slc <= LANES and tq == LANES

    q6, sc = [], []
    for g in groups:
        for h in range(hq):
            head = g * hq + h
            qa_ref[g, lanes(h), 0:HEAD_DIM] = q_ref[:, head * HEAD_DIM:(head + 1) * HEAD_DIM]
        q6.append(qa_ref[g, :, 0:HEAD_DIM])
        sc.append(_dot_nt(kc_ref[g], q6[g]) + cbias)
    for g in groups:
        sw_ref[g] = _dot_nt(kw_ref[pl.ds(ws, wlen), gcols(g)], q6[g])

    o_cmp, score = [], []
    for g in groups:
        ec = jnp.exp2(sc[g] - jnp.max(sc[g], axis=0, keepdims=True))
        den = jnp.maximum(jnp.sum(ec, axis=0, keepdims=True), TINY)
        pc = ec * jnp.where(any_cmp, 1.0 / den, 0.0)
        o_cmp.append(_dot_tn(vc_ref[g], pc.astype(BF16)))
        psum = pc[:, lanes(0)]
        for h in range(1, hq):
            psum = psum + pc[:, lanes(h)]
        p1 = psum.astype(BF16)
        r1 = psum - p1.astype(F32)
        p2 = r1.astype(BF16)
        p3 = (r1 - p2.astype(F32)).astype(BF16)
        imp = _dot(mmap, p1) + _dot(mmap, p2) + _dot(mmap, p3)
        score.append(jnp.where(forced, 1e9, jnp.where(jrow <= cur, imp, -1e9)))

    for g in groups:
        rank = _block_ranks(score[g], jrow)
        sel_bias = jnp.where((rank < min(N_SEL, n_slc)) & (jrow <= cur), 0.0, NEG_INF)
        bias_q = jnp.transpose(jnp.concatenate(
            [sel_bias, jnp.zeros((LANES - n_slc, tq), F32)], axis=0)).astype(BF16)
        for h in range(hq):
            qa_ref[g, lanes(h), HEAD_DIM:HEAD_DIM + LANES] = bias_q

    m_ref[...] = jnp.full(m_ref.shape, NEG_INF, F32)
    l_ref[...] = jnp.zeros(l_ref.shape, F32)
    acc_ref[...] = jnp.zeros(acc_ref.shape, F32)

    def scores(g, c, s_ref):
        k0 = pl.multiple_of(c * NSA_KC, NSA_KC)
        k_aug = jnp.concatenate([ks_ref[pl.ds(k0, NSA_KC), gcols(g)],
                                 e_ref[pl.ds(k0, NSA_KC), :]], axis=1)
        s_ref[g] = _dot_nt(k_aug, qa_ref[g])

    def softmax_pv(g, c, s_ref):
        s = s_ref[g]
        m_old = m_ref[g]
        m_new = jnp.maximum(m_old, jnp.max(s, axis=0, keepdims=True))
        alpha = jnp.exp2(m_old - m_new)
        p = jnp.exp2(s - m_new)
        l_ref[g] = alpha * l_ref[g] + jnp.sum(p, axis=0, keepdims=True)
        k0 = pl.multiple_of(c * NSA_KC, NSA_KC)
        pv = _dot_tn(vs_ref[pl.ds(k0, NSA_KC), gcols(g)], p.astype(BF16))
        acc_ref[g] = alpha * acc_ref[g] + pv
        m_ref[g] = m_new

    last = s0 // NSA_KC
    for g in groups:
        scores(g, 0, sa_ref)

    dist = t_row - (ws + lax.broadcasted_iota(jnp.int32, (wlen, 1), 0))
    wbias = tile_heads(jnp.where((dist >= 0) & (dist < WIN_A), 0.0, NEG_INF))
    for g in groups:
        sw = sw_ref[g] + wbias
        ew = jnp.exp2(sw - jnp.max(sw, axis=0, keepdims=True))
        ow_ref[g] = (_dot_tn(vw_ref[pl.ds(ws, wlen), gcols(g)], ew.astype(BF16))
                     * (1.0 / jnp.sum(ew, axis=0, keepdims=True)))

    def chunk_pair(i, carry):
        for g in groups:
            scores(g, 2 * i + 1, sb_ref)
            softmax_pv(g, 2 * i, sa_ref)
        for g in groups:
            scores(g, 2 * i + 2, sa_ref)
            softmax_pv(g, 2 * i + 1, sb_ref)
        return carry

    lax.fori_loop(0, last // 2, chunk_pair, 0)

    def last_chunk(g, s_ref):
        diag = pl.multiple_of(s0 - last * NSA_KC, LANES)
        krow = lax.broadcasted_iota(jnp.int32, (tq, 1), 0)
        lane = lax.broadcasted_iota(jnp.int32, (1, tq), 1)
        causal = tile_heads(jnp.where(krow <= lane, 0.0, NEG_INF))
        s_ref[g, pl.ds(diag, tq), :] = s_ref[g, pl.ds(diag, tq), :] + causal
        softmax_pv(g, last, s_ref)

    @pl.when(last % 2 == 0)
    def _():
        for g in groups:
            last_chunk(g, sa_ref)

    @pl.when(last % 2 == 1)
    def _():
        for g in groups:
            scores(g, last, sb_ref)
            softmax_pv(g, last - 1, sa_ref)
        for g in groups:
            last_chunk(g, sb_ref)

    gates_t = jnp.transpose(jax.nn.sigmoid(gz_ref[...].astype(F32) + gb_ref[...]))
    for g in groups:
        o_slc = acc_ref[g] * (1.0 / l_ref[g])
        for h in range(hq):
            head = g * hq + h
            o_h = (gates_t[3 * head:3 * head + 1, :] * o_cmp[g][:, lanes(h)]
                   + gates_t[3 * head + 1:3 * head + 2, :] * o_slc[:, lanes(h)]
                   + gates_t[3 * head + 2:3 * head + 3, :] * ow_ref[g, :, lanes(h)])
            o_ref[:, head * HEAD_DIM:(head + 1) * HEAD_DIM] = jnp.transpose(o_h).astype(o_ref.dtype)


def _nsa_attention(z, kc, vc, gate_bias, *, batch, seq, units):
    tq = NSA_TQ
    nq = seq // tq
    n_cmp_rows = kc.shape[2]
    ng = N_KV_A
    hl = HPG_A * tq
    gw = ng * HEAD_DIM
    for name in ("k_slc", "v_slc", "k_win", "v_win"):
        assert units[name] % ng == 0
    kern = functools.partial(_nsa_kernel, tq=tq, seq=seq)

    def slab(name):
        return pl.BlockSpec((seq, gw), lambda b, i: (b, units[name] // ng))

    cmp_spec = pl.BlockSpec((None, ng, n_cmp_rows, HEAD_DIM), lambda b, i: (b, 0, 0, 0))

    onehot = jnp.asarray(np.arange(seq)[:, None] // SLC_BLK == np.arange(LANES)[None, :], BF16)
    return pl.pallas_call(
        kern,
        grid=(batch, nq),
        in_specs=[
            pl.BlockSpec((tq, A_Q), lambda b, i: (b * nq + i, 0)),
            cmp_spec,
            cmp_spec,
            slab("k_slc"),
            pl.BlockSpec((seq, LANES), lambda b, i: (0, 0)),
            slab("v_slc"),
            slab("k_win"),
            slab("v_win"),
            pl.BlockSpec((tq, LANES), lambda b, i: (b * nq + i, units["gates"])),
            pl.BlockSpec((1, LANES), lambda b, i: (0, 0)),
        ],
        out_specs=pl.BlockSpec((tq, A_Q), lambda b, i: (b * nq + i, 0)),
        out_shape=jax.ShapeDtypeStruct((batch * seq, A_Q), BF16),
        scratch_shapes=[
            pltpu.VMEM((ng, hl, HEAD_DIM + LANES), BF16),
            pltpu.VMEM((ng, NSA_KC, hl), F32),
            pltpu.VMEM((ng, NSA_KC, hl), F32),
            pltpu.VMEM((ng, WIN_A + tq, hl), F32),
            pltpu.VMEM((ng, HEAD_DIM, hl), F32),
            pltpu.VMEM((ng, 1, hl), F32),
            pltpu.VMEM((ng, 1, hl), F32),
            pltpu.VMEM((ng, HEAD_DIM, hl), F32),
        ],
        compiler_params=_compiler_params(("parallel", "arbitrary")),
        name="nsa_attention",
    )(z, kc, vc, z, onehot, z, z, z, z, gate_bias)


def _band_attn_kernel(*refs, tu, parts):
    q_refs = refs[:len(parts)]
    k_ref, v_ref, o_ref, lse_ref = refs[len(parts):]
    u0 = pl.program_id(2) * tu
    n_seq = k_ref.shape[0]
    windows = []
    for lk, span, stride in parts:
        ks = pl.multiple_of(jnp.clip(u0 - span, 0, n_seq - lk), LANES)
        dist = (u0 + lax.broadcasted_iota(jnp.int32, (tu, 1), 0)
                - (ks + lax.broadcasted_iota(jnp.int32, (1, lk), 1)))
        keep = (dist >= 0) & (dist <= span)
        if stride > 1:
            keep = keep & ((dist & (stride - 1)) == 0)
        windows.append((ks, lk, jnp.where(keep, 0.0, NEG_INF)))
    lane = lax.broadcasted_iota(jnp.int32, (1, LANES), 1)
    lse_tile = jnp.zeros((tu, LANES), F32)

    def head_cols(h):
        return slice(h * HEAD_DIM, (h + 1) * HEAD_DIM)

    scores = [[_dot_nt(q_ref[:, head_cols(h)], k_ref[pl.ds(ks, lk), head_cols(h)]) + bias
               for q_ref, (ks, lk, bias) in zip(q_refs, windows)]
              for h in range(DIL_HEADS)]
    probs, inv_l = [], []
    for h, s_parts in enumerate(scores):
        s = jnp.concatenate(s_parts, axis=1)
        m = jnp.max(s, axis=-1, keepdims=True)
        p = jnp.exp2(s - m)
        l = jnp.sum(p, axis=-1, keepdims=True)
        probs.append(p.astype(BF16))
        inv_l.append(1.0 / l)
        lse_tile = jnp.where(lane == h, m + jnp.log2(l), lse_tile)
    for h in range(DIL_HEADS):
        o, col = None, 0
        for ks, lk, _ in windows:
            pv = _dot(probs[h][:, col:col + lk], v_ref[pl.ds(ks, lk), head_cols(h)])
            o = pv if o is None else o + pv
            col += lk
        o_ref[:, head_cols(h)] = (o * inv_l[h]).astype(o_ref.dtype)
    lse_ref[...] = lse_tile


def _dil_merge_kernel(*refs):
    out_ref = refs[-1]
    calls = list(zip(refs[0:-1:2], refs[1:-1:2]))
    for h in range(DIL_HEADS):
        cols = slice(h * HEAD_DIM, (h + 1) * HEAD_DIM)
        shape = (out_ref.shape[0], HEAD_DIM)
        lses = [jnp.broadcast_to(l_ref[:, h:h + 1], shape) for _, l_ref in calls]
        top = functools.reduce(jnp.maximum, lses)
        ws = [jnp.exp2(lse - top) for lse in lses]
        num = sum(w * o_ref[:, cols].astype(F32) for w, (o_ref, _) in zip(ws, calls))
        out_ref[:, cols] = (num * (1.0 / sum(ws))).astype(out_ref.dtype)


def _dilated_attention(zb, *, batch, seq, units):
    n = zb.shape[1]
    width = DIL_HEADS * HEAD_DIM
    tu = DIL_TQ

    def band_call(r, q_units, parts, operands, k_unit, v_unit):
        n_seq = seq // r

        def z_spec(rows, unit, whole):
            return pl.BlockSpec((None, None, rows, width),
                                lambda b, c, i: (b, c, 0 if whole else i, unit))

        def out_spec(cols):
            return pl.BlockSpec((None, None, tu, cols), lambda b, c, i: (b, c, i, 0))

        return pl.pallas_call(
            functools.partial(_band_attn_kernel, tu=tu, parts=parts),
            grid=(batch, r, n_seq // tu),
            in_specs=[z_spec(tu, u, False) for u in q_units]
            + [z_spec(n_seq, k_unit, True), z_spec(n_seq, v_unit, True)],
            out_specs=[out_spec(width), out_spec(LANES)],
            out_shape=[jax.ShapeDtypeStruct((batch, r, n_seq, width), BF16),
                       jax.ShapeDtypeStruct((batch, r, n_seq, LANES), F32)],
            compiler_params=_compiler_params(("parallel", "parallel", "arbitrary")),
            name="dilated_attention",
        )(*operands)

    def window(n_seq, span):
        assert n_seq % tu == 0 and span % LANES == 0 and tu % LANES == 0
        return min(tu + span, n_seq)

    dense = [(gi, w, r) for gi, (w, r) in enumerate(DIL_CONFIGS) if r <= DIL_DENSE_MAX]
    zv = zb.reshape(batch, 1, seq, n)
    o_d, lse_d = band_call(1, [units["q"] + gi for gi, _, _ in dense],
                           tuple((window(seq, w), w, r) for _, w, r in dense),
                           [zv] * (len(dense) + 2), units["k"], units["v"])
    results = [o_d.reshape(batch * seq, width), lse_d.reshape(batch * seq, LANES)]
    for gi, (w, r) in enumerate(DIL_CONFIGS):
        if r <= DIL_DENSE_MAX:
            continue
        slabs = [(units["q"] + gi, width), (units["k"], width), (units["v"], width)]
        o_c, lse_c = band_call(r, [0], ((window(seq // r, w // r), w // r, 1),),
                               _to_class_order(zb, slabs, r, batch=batch, seq=seq), 0, 0)
        results += [_from_class_order(o_c, batch=batch, seq=seq),
                    jnp.transpose(lse_c, (0, 2, 1, 3)).reshape(batch * seq, LANES)]

    m = batch * seq
    tm = min(MERGE_TM, m)
    o_spec = pl.BlockSpec((tm, width), lambda i: (i, 0))
    l_spec = pl.BlockSpec((tm, LANES), lambda i: (i, 0))
    return pl.pallas_call(
        _dil_merge_kernel,
        grid=(m // tm,),
        in_specs=[o_spec, l_spec] * (len(results) // 2),
        out_specs=o_spec,
        out_shape=jax.ShapeDtypeStruct((m, width), BF16),
        compiler_params=_compiler_params(("parallel",)),
        name="dilated_merge",
    )(*results)


def _memory_attention(q_ref, kv_ref):
    def cols(h, base=0):
        return slice(base + h * HEAD_DIM, base + (h + 1) * HEAD_DIM)

    heads = range(N_MEM_HEADS)
    scores = [_dot_nt(q_ref[:, cols(h)], kv_ref[:, cols(h)]) for h in heads]
    probs = []
    for s in scores:
        e = jnp.exp(s - jnp.max(s, axis=-1, keepdims=True))
        probs.append((e / jnp.sum(e, axis=-1, keepdims=True)).astype(BF16))
    return jnp.concatenate([_dot(probs[h], kv_ref[:, cols(h, MEM_Q)]).astype(BF16)
                            for h in heads], axis=1)


def _out_proj_kernel(a1_ref, qm_ref, mkv_ref, w_ref, h_ref, g_ref, o_ref, xn_ref):
    a = jnp.concatenate([a1_ref[...], _memory_attention(qm_ref, mkv_ref)], axis=1)
    y = h_ref[...] + _dot(a, w_ref[...])
    o_ref[...] = y
    xn_ref[...] = _rms_rows(y, g_ref[...]).astype(xn_ref.dtype)


def _out_proj(a1, z, q_unit, mkv, w_bf, h, g_next, *, batch):
    m, d = h.shape
    tm = min(OUT_TM, m)
    k1 = a1.shape[1]
    assert w_bf.shape[0] == k1 + MEM_Q and (m // batch) % tm == 0
    blocks_per_batch = m // batch // tm
    n_mem = mkv.shape[0] // batch
    row = pl.BlockSpec((tm, d), lambda i: (i, 0))
    return pl.pallas_call(
        _out_proj_kernel,
        grid=(m // tm,),
        in_specs=[
            pl.BlockSpec((tm, k1), lambda i: (i, 0)),
            pl.BlockSpec((tm, MEM_Q), lambda i: (i, q_unit)),
            pl.BlockSpec((n_mem, 2 * MEM_Q), lambda i: (i // blocks_per_batch, 0)),
            pl.BlockSpec((k1 + MEM_Q, d), lambda i: (0, 0), pipeline_mode=pl.Buffered(1)),
            row,
            pl.BlockSpec((1, d), lambda i: (0, 0)),
        ],
        out_specs=[row, row],
        out_shape=[jax.ShapeDtypeStruct((m, d), F32), jax.ShapeDtypeStruct((m, d), BF16)],
        compiler_params=_compiler_params(("parallel",)),
        name="out_proj",
    )(a1, z, mkv, w_bf, h, g_next.reshape(1, d))


def _ffn_up_kernel(xn_ref, wg_ref, wu_ref, o_ref):
    xn = xn_ref[...]
    gate = _dot(xn, wg_ref[...])
    up = _dot(xn, wu_ref[...])
    o_ref[...] = (gate * jax.nn.sigmoid(gate) * up).astype(o_ref.dtype)


def _ffn_down_kernel(a_ref, w_ref, h_ref, g_ref, o_ref, *xn_ref, final_norm):
    y = h_ref[...] + _dot(a_ref[...], w_ref[...])
    yn = _rms_rows(y, g_ref[...])
    if final_norm:
        o_ref[...] = yn
    else:
        o_ref[...] = y
        xn_ref[0][...] = yn.astype(xn_ref[0].dtype)


def _ffn(h, xn, wg_bf, wu_bf, wd_bf, gain, *, final_norm):
    m, d = h.shape
    dff = wg_bf.shape[1]
    tm, tf = min(FFN_UP_TM, m), FFN_TF
    assert m % tm == 0 and dff % tf == 0
    act = pl.pallas_call(
        _ffn_up_kernel,
        grid=(m // tm, dff // tf),
        in_specs=[
            pl.BlockSpec((tm, d), lambda i, f: (i, 0)),
            pl.BlockSpec((d, tf), lambda i, f: (0, f)),
            pl.BlockSpec((d, tf), lambda i, f: (0, f)),
        ],
        out_specs=pl.BlockSpec((tm, tf), lambda i, f: (i, f)),
        out_shape=jax.ShapeDtypeStruct((m, dff), BF16),
        compiler_params=_compiler_params(("parallel", "arbitrary")),
        name="ffn_up",
    )(xn, wg_bf, wu_bf)

    tm = min(FFN_TM, m)
    kern = functools.partial(_ffn_down_kernel, final_norm=final_norm)
    row = pl.BlockSpec((tm, d), lambda i: (i, 0))
    h_shape = jax.ShapeDtypeStruct((m, d), F32)
    return pl.pallas_call(
        kern,
        grid=(m // tm,),
        in_specs=[
            pl.BlockSpec((tm, dff), lambda i: (i, 0)),
            pl.BlockSpec((dff, d), lambda i: (0, 0), pipeline_mode=pl.Buffered(1)),
            row,
            pl.BlockSpec((1, d), lambda i: (0, 0)),
        ],
        out_specs=row if final_norm else [row, row],
        out_shape=h_shape if final_norm else [h_shape, jax.ShapeDtypeStruct((m, d), BF16)],
        compiler_params=_compiler_params(("parallel",)),
        name="ffn_down",
    )(act, wd_bf, h, gain.reshape(1, d))


def _rope_tables(seq):
    inv = 1.0 / (ROPE_THETA ** (jnp.arange(0, HEAD_DIM, 2, dtype=F32) / HEAD_DIM))
    ang = jnp.arange(seq, dtype=F32)[:, None] * inv[None, :]
    cos, sin = jnp.cos(ang), jnp.sin(ang)
    return jnp.concatenate([cos, cos], axis=1), jnp.concatenate([-sin, sin], axis=1)


A_UNITS = {"q": 0, "k_cmp": 12, "k_slc": 14, "k_win": 16, "v_cmp": 18, "v_slc": 20,
           "v_win": 22, "mem_q": 24, "gates": 28}
B_UNITS = {"q": 0, "k": 3, "mem_q": 4, "v": 5}
B_TN = 2 * DIL_HEADS * HEAD_DIM
A_TN = 6 * HEAD_DIM
A_NPAD = 30 * HEAD_DIM
A_ROPE_BLOCKS = 3


def _layer_a_weight(w_in):
    kv0 = A_Q

    def kv_cols(branch):
        return w_in[:, kv0 + branch * N_KV_A * HEAD_DIM:kv0 + (branch + 1) * N_KV_A * HEAD_DIM]

    gate0 = A_Q + A_KV
    mem0 = gate0 + A_GATE
    w = jnp.concatenate([w_in[:, :A_Q], kv_cols(0), kv_cols(2), kv_cols(4), kv_cols(1),
                         kv_cols(3), kv_cols(5), w_in[:, mem0:mem0 + MEM_Q],
                         w_in[:, gate0:mem0]], axis=1)
    w = jnp.pad(w, ((0, 0), (0, A_NPAD - w.shape[1])))
    scale = np.ones((1, A_NPAD), np.float32)
    scale[0, :A_Q] = SCALE * LOG2E
    scale[0, A_UNITS["mem_q"] * HEAD_DIM:A_UNITS["mem_q"] * HEAD_DIM + MEM_Q] = SCALE
    return w.astype(BF16), jnp.asarray(scale)


def _layer_a(h, hn, mem, cosf, sinf, p, *, batch, seq):
    w_in_bf, col_scale = _layer_a_weight(p["w_in"])
    z = _norm_proj(h if hn is None else hn, p["norm_attn"], w_in_bf, col_scale, cosf, sinf,
                   tn=A_TN, n_rope_blocks=A_ROPE_BLOCKS, seq=seq)

    gw = N_KV_A * HEAD_DIM
    k_raw, v_raw = _to_class_order(
        z, [(A_UNITS["k_cmp"] * HEAD_DIM // gw, gw), (A_UNITS["v_cmp"] * HEAD_DIM // gw, gw)],
        CMP_STRIDE, batch=batch, seq=seq)
    kc = _compress(k_raw, p["cmp_pe_k"], p["cmp_w1_k"].astype(BF16), p["cmp_w2_k"].astype(BF16))
    vc = _compress(v_raw, p["cmp_pe_v"], p["cmp_w1_v"].astype(BF16), p["cmp_w2_v"].astype(BF16))
    gb = jnp.pad(p["gate_bias"], (0, LANES - A_GATE)).reshape(1, LANES)
    o_nsa = _nsa_attention(z, kc, vc, gb, batch=batch, seq=seq, units=A_UNITS)

    mkv = _mem_kv(mem, p["norm_mem"], p["w_mem_kv"])
    return _out_proj(o_nsa, z, A_UNITS["mem_q"] * HEAD_DIM // MEM_Q, mkv,
                     p["w_out"].astype(BF16), h, p["norm_ffn"], batch=batch)


def _mem_kv(mem, norm_mem, w_mem_kv):
    b, m, d = mem.shape
    ones = jnp.ones((1, w_mem_kv.shape[1]), F32)
    dummy = jnp.zeros((m, HEAD_DIM), F32)
    return _norm_proj(mem.reshape(b * m, d), norm_mem, w_mem_kv.astype(BF16), ones, dummy, dummy,
                      tn=MEM_Q, n_rope_blocks=0, seq=m, tm=m)


def kernel(x, mem, a_norm_attn, a_w_in, a_gate_bias, a_cmp_pe_k, a_cmp_w1_k, a_cmp_w2_k, a_cmp_pe_v, a_cmp_w1_v, a_cmp_w2_v, a_norm_mem, a_w_mem_kv, a_w_out, a_norm_ffn, a_w_gate, a_w_up, a_w_down, kv_norm, w_kv_shared, b_norm_attn, b_w_in, b_norm_mem, b_w_mem_kv, b_w_out, b_norm_ffn, b_w_gate, b_w_up, b_w_down, final_norm):
    batch, seq, d = x.shape
    n_a = a_w_in.shape[0]
    n_b = b_w_in.shape[0]
    cosf, sinf = _rope_tables(seq)
    h = x.reshape(batch * seq, d)
    unit_gain = jnp.ones((d,), F32)
    hn = None

    for l in range(n_a):
        p = {"norm_attn": a_norm_attn[l], "w_in": a_w_in[l], "gate_bias": a_gate_bias[l],
             "cmp_pe_k": a_cmp_pe_k[l], "cmp_w1_k": a_cmp_w1_k[l], "cmp_w2_k": a_cmp_w2_k[l],
             "cmp_pe_v": a_cmp_pe_v[l], "cmp_w1_v": a_cmp_w1_v[l], "cmp_w2_v": a_cmp_w2_v[l],
             "norm_mem": a_norm_mem[l], "w_mem_kv": a_w_mem_kv[l], "w_out": a_w_out[l],
             "norm_ffn": a_norm_ffn[l]}
        h, xn = _layer_a(h, hn, mem, cosf, sinf, p, batch=batch, seq=seq)
        last = (l == n_a - 1) and n_b == 0
        ffn_w = (a_w_gate[l].astype(BF16), a_w_up[l].astype(BF16), a_w_down[l].astype(BF16))
        if last:
            h = _ffn(h, xn, *ffn_w, final_norm, final_norm=True)
        else:
            next_gain = a_norm_attn[l + 1] if l + 1 < n_a else unit_gain
            h, hn = _ffn(h, xn, *ffn_w, next_gain, final_norm=False)

    if n_b > 0:
        assert n_b == 1, "the shared K/V projection is fused into the single mixer-B layer"
        n_kv_half = w_kv_shared.shape[1] // 2
        for l in range(n_b):
            w_q = b_norm_attn[l][:, None] * b_w_in[l]
            w_kv = kv_norm[:, None] * w_kv_shared
            w_cat = jnp.concatenate([w_q[:, :B_Q], w_kv[:, :n_kv_half], w_q[:, B_Q:],
                                     w_kv[:, n_kv_half:]], axis=1).astype(BF16)
            b_scale = np.ones((1, w_cat.shape[1]), np.float32)
            b_scale[0, :B_Q] = SCALE * LOG2E
            b_scale[0, B_Q + n_kv_half:B_Q + n_kv_half + MEM_Q] = SCALE
            zb = _norm_proj(h if hn is None else hn, unit_gain, w_cat, jnp.asarray(b_scale),
                            cosf, sinf, tm=PROJ_TM if hn is None else PROJ_TM_BF16,
                            tn=B_TN, n_rope_blocks=(B_Q + n_kv_half) // B_TN, seq=seq)
            o_dil = _dilated_attention(zb, batch=batch, seq=seq, units=B_UNITS)
            mkv = _mem_kv(mem, b_norm_mem[l], b_w_mem_kv[l])
            h, xn = _out_proj(o_dil, zb, B_UNITS["mem_q"], mkv, b_w_out[l].astype(BF16), h,
                              b_norm_ffn[l], batch=batch)
            h = _ffn(h, xn, b_w_gate[l].astype(BF16), b_w_up[l].astype(BF16),
                     b_w_down[l].astype(BF16), final_norm, final_norm=True)

    return h.reshape(batch, seq, d)
```

```python
import functools
import math

import numpy as np
import jax
import jax.numpy as jnp
from jax import lax
from jax.experimental import pallas as pl
from jax.experimental.pallas import tpu as pltpu

F32 = jnp.float32
BF16 = jnp.bfloat16

HEAD_DIM = 128
N_HEADS_A = 12
N_KV_A = 2
HPG_A = N_HEADS_A // N_KV_A
CMP_LEN = 32
CMP_STRIDE = 16
CMP_HIDDEN = 256
SLC_BLK = 64
SLC_SHIFT = SLC_BLK.bit_length() - 1
N_SEL = 16
WIN_A = 512
DIL_CONFIGS = ((128, 1), (512, 4), (2048, 16))
N_DIL_GROUPS = len(DIL_CONFIGS)
DIL_HEADS = 4
N_MEM_HEADS = 4
ROPE_THETA = 10000.0
EPS = 1e-6
NEG_INF = -1e30
TINY = 1e-30
SCALE = HEAD_DIM ** -0.5
LOG2E = math.log2(math.e)

A_Q = N_HEADS_A * HEAD_DIM
A_KV = 6 * N_KV_A * HEAD_DIM
A_GATE = 3 * N_HEADS_A
MEM_Q = N_MEM_HEADS * HEAD_DIM
B_Q = N_DIL_GROUPS * DIL_HEADS * HEAD_DIM

LANES = 128
SUBLANES = 8
VMEM_LIMIT_BYTES = 56 * 1024 * 1024

PROJ_TM = 1024
PROJ_TM_BF16 = 2048
FFN_UP_TM = 2048
FFN_TM = 512
FFN_TF = 512
OUT_TM = 512
MERGE_TM = 1024
NSA_TQ = 128
NSA_KC = 512
DIL_TQ = 256
DIL_DENSE_MAX = 4

NT_DIMS = (((1,), (1,)), ((), ()))
TN_DIMS = (((0,), (0,)), ((), ()))


def _compiler_params(semantics):
    return pltpu.CompilerParams(dimension_semantics=semantics,
                                vmem_limit_bytes=VMEM_LIMIT_BYTES)


def _rms_rows(x, g):
    ms = jnp.mean(x * x, axis=-1, keepdims=True)
    return x * lax.rsqrt(ms + EPS) * g


def _dot(a, b):
    return jnp.dot(a, b, preferred_element_type=F32)


def _dot_nt(a, b):
    return lax.dot_general(a, b, NT_DIMS, preferred_element_type=F32)


def _dot_tn(a, b):
    return lax.dot_general(a, b, TN_DIMS, preferred_element_type=F32)


def _norm_proj_kernel(x_ref, g_ref, w_ref, cs_ref, cos_ref, sin_ref, o_ref, *xn_ref,
                      n_rope_blocks, tn, w_rows):
    j = pl.program_id(1)
    if xn_ref:
        @pl.when(j == 0)
        def _():
            xn_ref[0][...] = _rms_rows(x_ref[...], g_ref[...]).astype(BF16)

        xn = xn_ref[0][...]
    else:
        xn = x_ref[...]
    w = w_ref[...]
    acc = (_dot_nt(xn, w) if w_rows else _dot(xn, w)) * cs_ref[...]

    if n_rope_blocks > 0:
        roped = j < n_rope_blocks
        c = jnp.where(roped, cos_ref[...], 1.0)
        s = jnp.where(roped, sin_ref[...], 0.0)
        for h in range(tn // HEAD_DIM):
            y = acc[:, h * HEAD_DIM:(h + 1) * HEAD_DIM]
            rot = pltpu.roll(y, HEAD_DIM // 2, 1)
            o_ref[:, h * HEAD_DIM:(h + 1) * HEAD_DIM] = (y * c + rot * s).astype(o_ref.dtype)
    else:
        o_ref[...] = acc.astype(o_ref.dtype)


def _norm_proj(x, g, w_bf, col_scale, cosf, sinf, *, tn, n_rope_blocks, seq, tm=PROJ_TM,
               w_rows=False):
    m, d = x.shape
    n = w_bf.shape[0] if w_rows else w_bf.shape[1]
    tm = min(tm, m)
    assert m % tm == 0 and n % tn == 0 and seq % tm == 0
    pos_blocks = seq // tm
    scratch = [] if x.dtype == BF16 else [pltpu.VMEM((tm, d), BF16)]
    kern = functools.partial(_norm_proj_kernel, n_rope_blocks=n_rope_blocks, tn=tn,
                             w_rows=w_rows)
    return pl.pallas_call(
        kern,
        grid=(m // tm, n // tn),
        in_specs=[
            pl.BlockSpec((tm, d), lambda i, j: (i, 0)),
            pl.BlockSpec((1, d), lambda i, j: (0, 0)),
            (pl.BlockSpec((tn, d), lambda i, j: (j, 0)) if w_rows
             else pl.BlockSpec((d, tn), lambda i, j: (0, j))),
            pl.BlockSpec((1, tn), lambda i, j: (0, j)),
            pl.BlockSpec((tm, HEAD_DIM), lambda i, j: (i % pos_blocks, 0)),
            pl.BlockSpec((tm, HEAD_DIM), lambda i, j: (i % pos_blocks, 0)),
        ],
        out_specs=pl.BlockSpec((tm, tn), lambda i, j: (i, j)),
        out_shape=jax.ShapeDtypeStruct((m, n), BF16),
        scratch_shapes=scratch,
        compiler_params=_compiler_params(("parallel", "arbitrary")),
        name="norm_proj",
    )(x, g.reshape(1, d), w_bf, col_scale, cosf, sinf)


def _class_perm(tm, r):
    dst = np.arange(tm)
    c, u = dst // (tm // r), dst % (tm // r)
    perm = np.zeros((tm, tm), np.float32)
    perm[dst, u * r + c] = 1.0
    return perm


def _to_class_kernel(p_ref, *refs, r):
    n = len(refs) // 2
    for x_ref, o_ref in zip(refs[:n], refs[n:]):
        y = _dot(p_ref[...], x_ref[...]).astype(o_ref.dtype)
        rows = y.shape[0] // r
        for c in range(r):
            o_ref[c] = y[c * rows:(c + 1) * rows, :]


def _to_class_order(x, slabs, r, *, batch, seq):
    tm = min(PROJ_TM, seq)
    nblk = seq // tm
    perm = jnp.asarray(_class_perm(tm, r), x.dtype)
    return pl.pallas_call(
        functools.partial(_to_class_kernel, r=r),
        grid=(batch * nblk,),
        in_specs=[pl.BlockSpec((tm, tm), lambda i: (0, 0))]
        + [pl.BlockSpec((tm, w), lambda i, cb=cb: (i, cb)) for cb, w in slabs],
        out_specs=[pl.BlockSpec((None, r, tm // r, w), lambda i: (i // nblk, 0, i % nblk, 0))
                   for _, w in slabs],
        out_shape=[jax.ShapeDtypeStruct((batch, r, seq // r, w), x.dtype) for _, w in slabs],
        compiler_params=_compiler_params(("parallel",)),
        name="to_class_order",
    )(perm, *([x] * len(slabs)))


def _from_class_kernel(pt_ref, x_ref, o_ref, *, r):
    x = jnp.concatenate([x_ref[c] for c in range(r)], axis=0)
    o_ref[...] = _dot(pt_ref[...], x).astype(o_ref.dtype)


def _from_class_order(xc, *, batch, seq):
    _, r, _, w = xc.shape
    tm = min(PROJ_TM, seq)
    nblk = seq // tm
    perm_t = jnp.asarray(_class_perm(tm, r).T, xc.dtype)
    return pl.pallas_call(
        functools.partial(_from_class_kernel, r=r),
        grid=(batch * nblk,),
        in_specs=[pl.BlockSpec((tm, tm), lambda i: (0, 0)),
                  pl.BlockSpec((None, r, tm // r, w), lambda i: (i // nblk, 0, i % nblk, 0))],
        out_specs=pl.BlockSpec((tm, w), lambda i: (i, 0)),
        out_shape=jax.ShapeDtypeStruct((batch * seq, w), xc.dtype),
        compiler_params=_compiler_params(("parallel",)),
        name="from_class_order",
    )(perm_t, xc)


def _compress_kernel(x_ref, pe_ref, w1_ref, w2_ref, o_ref):
    n_planes, n_rows, _ = x_ref.shape
    ylo = yhi = None
    for l in range(n_planes):
        x = x_ref[l].astype(F32)
        xlo = (x + pe_ref[l:l + 1, :]).astype(BF16)
        xhi = (x + pe_ref[n_planes + l:n_planes + l + 1, :]).astype(BF16)
        dlo = _dot(xlo, w1_ref[l * HEAD_DIM:(l + 1) * HEAD_DIM, :])
        dhi = _dot(xhi, w1_ref[(n_planes + l) * HEAD_DIM:(n_planes + l + 1) * HEAD_DIM, :])
        ylo = dlo if ylo is None else ylo + dlo
        yhi = dhi if yhi is None else yhi + dhi
    hid = ylo + pltpu.roll(yhi, n_rows - 1, 0)
    act = (hid * jax.nn.sigmoid(hid)).astype(BF16)
    o_ref[...] = _dot(act, w2_ref[...]).astype(o_ref.dtype)


def _compress(xc, pe, w1_bf, w2_bf):
    batch, planes, nrow, gd = xc.shape
    ng = gd // HEAD_DIM
    return pl.pallas_call(
        _compress_kernel,
        grid=(batch, ng),
        in_specs=[
            pl.BlockSpec((None, planes, nrow, HEAD_DIM), lambda b, g: (b, 0, 0, g)),
            pl.BlockSpec((CMP_LEN, HEAD_DIM), lambda b, g: (0, 0)),
            pl.BlockSpec((CMP_LEN * HEAD_DIM, CMP_HIDDEN), lambda b, g: (0, 0)),
            pl.BlockSpec((CMP_HIDDEN, HEAD_DIM), lambda b, g: (0, 0)),
        ],
        out_specs=pl.BlockSpec((None, None, nrow, HEAD_DIM), lambda b, g: (b, g, 0, 0)),
        out_shape=jax.ShapeDtypeStruct((batch, ng, nrow, HEAD_DIM), BF16),
        compiler_params=_compiler_params(("parallel", "arbitrary")),
        name="nsa_compress",
    )(xc, pe, w1_bf, w2_bf)


def _block_ranks(score, jrow):
    n_blk = score.shape[0]
    groups = n_blk // SUBLANES
    blocks = [score[SUBLANES * r:SUBLANES * (r + 1), :] for r in range(groups)]
    rows = [jrow[SUBLANES * r:SUBLANES * (r + 1), :] for r in range(groups)]
    ranks = [jnp.zeros(blocks[0].shape, F32) for _ in range(groups)]
    for j in range(n_blk):
        rj = score[j:j + 1, :]
        for r in range(groups):
            if r > j // SUBLANES:
                ahead = rj >= blocks[r]
            elif r < j // SUBLANES:
                ahead = rj > blocks[r]
            else:
                ahead = (rj > blocks[r]) | ((rj == blocks[r]) & (rows[r] > j))
            ranks[r] = ranks[r] + jnp.where(ahead, 1.0, 0.0)
    return jnp.concatenate(ranks, axis=0)


def _nsa_kernel(q_ref, kc_ref, vc_ref, ks_ref, e_ref, vs_ref, kw_ref, vw_ref, gz_ref, gb_ref,
                o_ref, qa_ref, sa_ref, sb_ref, sw_ref, ow_ref, m_ref, l_ref, acc_ref, *, tq, seq):
    hq = HPG_A
    groups = range(N_KV_A)
    n_cmp_rows = kc_ref.shape[1]
    n_slc = seq // SLC_BLK
    qi = pl.program_id(1)
    s0 = qi * tq
    t_row = s0 + lax.broadcasted_iota(jnp.int32, (1, tq), 1)

    def lanes(h):
        return slice(h * tq, (h + 1) * tq)

    def gcols(g):
        return slice(g * HEAD_DIM, (g + 1) * HEAD_DIM)

    def tile_heads(x):
        return jnp.concatenate([x] * hq, axis=1)

    wlen = WIN_A + tq
    ws = pl.multiple_of(jnp.maximum(s0 - WIN_A, 0), LANES)
    c_end = lax.broadcasted_iota(jnp.int32, (n_cmp_rows, 1), 0) * CMP_STRIDE + (CMP_LEN - 1)
    cbias = tile_heads(jnp.where(c_end <= t_row, 0.0, NEG_INF))
    any_cmp = tile_heads(t_row >= CMP_LEN - 1)
    jrow = lax.broadcasted_iota(jnp.int32, (n_slc, 1), 0)
    ccol = lax.broadcasted_iota(jnp.int32, (1, n_cmp_rows), 1)
    lo = (SLC_BLK // CMP_STRIDE) * jrow - (CMP_LEN // CMP_STRIDE - 1)
    hi = (SLC_BLK // CMP_STRIDE) * jrow + (SLC_BLK // CMP_STRIDE - 1)
    mmap = jnp.where((ccol >= lo) & (ccol <= hi), 1.0, 0.0).astype(BF16)
    cur = t_row >> SLC_SHIFT
    forced = (jrow == 0) | (jrow == cur) | (jrow == cur - 1)
    assert n_slc <= LANES and tq == LANES

    q6, sc = [], []
    for g in groups:
        for h in range(hq):
            head = g * hq + h
            qa_ref[g, lanes(h), 0:HEAD_DIM] = q_ref[:, head * HEAD_DIM:(head + 1) * HEAD_DIM]
        q6.append(qa_ref[g, :, 0:HEAD_DIM])
        sc.append(_dot_nt(kc_ref[g], q6[g]) + cbias)
    for g in groups:
        sw_ref[g] = _dot_nt(kw_ref[pl.ds(ws, wlen), gcols(g)], q6[g])

    o_cmp, score = [], []
    for g in groups:
        ec = jnp.exp2(sc[g] - jnp.max(sc[g], axis=0, keepdims=True))
        den = jnp.maximum(jnp.sum(ec, axis=0, keepdims=True), TINY)
        pc = ec * jnp.where(any_cmp, 1.0 / den, 0.0)
        o_cmp.append(_dot_tn(vc_ref[g], pc.astype(BF16)))
        psum = pc[:, lanes(0)]
        for h in range(1, hq):
            psum = psum + pc[:, lanes(h)]
        p1 = psum.astype(BF16)
        r1 = psum - p1.astype(F32)
        p2 = r1.astype(BF16)
        p3 = (r1 - p2.astype(F32)).astype(BF16)
        imp = _dot(mmap, p1) + _dot(mmap, p2) + _dot(mmap, p3)
        score.append(jnp.where(forced, 1e9, jnp.where(jrow <= cur, imp, -1e9)))

    for g in groups:
        rank = _block_ranks(score[g], jrow)
        sel_bias = jnp.where((rank < min(N_SEL, n_slc)) & (jrow <= cur), 0.0, NEG_INF)
        bias_q = jnp.transpose(jnp.concatenate(
            [sel_bias, jnp.zeros((LANES - n_slc, tq), F32)], axis=0)).astype(BF16)
        for h in range(hq):
            qa_ref[g, lanes(h), HEAD_DIM:HEAD_DIM + LANES] = bias_q

    m_ref[...] = jnp.full(m_ref.shape, NEG_INF, F32)
    l_ref[...] = jnp.zeros(l_ref.shape, F32)
    acc_ref[...] = jnp.zeros(acc_ref.shape, F32)

    def scores(g, c, s_ref):
        k0 = pl.multiple_of(c * NSA_KC, NSA_KC)
        k_aug = jnp.concatenate([ks_ref[pl.ds(k0, NSA_KC), gcols(g)],
                                 e_ref[pl.ds(k0, NSA_KC), :]], axis=1)
        s_ref[g] = _dot_nt(k_aug, qa_ref[g])

    def softmax_pv(g, c, s_ref):
        s = s_ref[g]
        m_old = m_ref[g]
        m_new = jnp.maximum(m_old, jnp.max(s, axis=0, keepdims=True))
        alpha = jnp.exp2(m_old - m_new)
        p = jnp.exp2(s - m_new)
        l_ref[g] = alpha * l_ref[g] + jnp.sum(p, axis=0, keepdims=True)
        k0 = pl.multiple_of(c * NSA_KC, NSA_KC)
        pv = _dot_tn(vs_ref[pl.ds(k0, NSA_KC), gcols(g)], p.astype(BF16))
        acc_ref[g] = alpha * acc_ref[g] + pv
        m_ref[g] = m_new

    last = s0 // NSA_KC
    for g in groups:
        scores(g, 0, sa_ref)

    dist = t_row - (ws + lax.broadcasted_iota(jnp.int32, (wlen, 1), 0))
    wbias = tile_heads(jnp.where((dist >= 0) & (dist < WIN_A), 0.0, NEG_INF))
    for g in groups:
        sw = sw_ref[g] + wbias
        ew = jnp.exp2(sw - jnp.max(sw, axis=0, keepdims=True))
        ow_ref[g] = (_dot_tn(vw_ref[pl.ds(ws, wlen), gcols(g)], ew.astype(BF16))
                     * (1.0 / jnp.sum(ew, axis=0, keepdims=True)))

    def chunk_pair(i, carry):
        for g in groups:
            scores(g, 2 * i + 1, sb_ref)
            softmax_pv(g, 2 * i, sa_ref)
        for g in groups:
            scores(g, 2 * i + 2, sa_ref)
            softmax_pv(g, 2 * i + 1, sb_ref)
        return carry

    lax.fori_loop(0, last // 2, chunk_pair, 0)

    def last_chunk(g, s_ref):
        diag = pl.multiple_of(s0 - last * NSA_KC, LANES)
        krow = lax.broadcasted_iota(jnp.int32, (tq, 1), 0)
        lane = lax.broadcasted_iota(jnp.int32, (1, tq), 1)
        causal = tile_heads(jnp.where(krow <= lane, 0.0, NEG_INF))
        s_ref[g, pl.ds(diag, tq), :] = s_ref[g, pl.ds(diag, tq), :] + causal
        softmax_pv(g, last, s_ref)

    @pl.when(last % 2 == 0)
    def _():
        for g in groups:
            last_chunk(g, sa_ref)

    @pl.when(last % 2 == 1)
    def _():
        for g in groups:
            scores(g, last, sb_ref)
            softmax_pv(g, last - 1, sa_ref)
        for g in groups:
            last_chunk(g, sb_ref)

    gates_t = jnp.transpose(jax.nn.sigmoid(gz_ref[...].astype(F32) + gb_ref[...]))
    for g in groups:
        o_slc = acc_ref[g] * (1.0 / l_ref[g])
        for h in range(hq):
            head = g * hq + h
            o_h = (gates_t[3 * head:3 * head + 1, :] * o_cmp[g][:, lanes(h)]
                   + gates_t[3 * head + 1:3 * head + 2, :] * o_slc[:, lanes(h)]
                   + gates_t[3 * head + 2:3 * head + 3, :] * ow_ref[g, :, lanes(h)])
            o_ref[:, head * HEAD_DIM:(head + 1) * HEAD_DIM] = jnp.transpose(o_h).astype(o_ref.dtype)


def _nsa_attention(z, kc, vc, gate_bias, *, batch, seq, units):
    tq = NSA_TQ
    nq = seq // tq
    n_cmp_rows = kc.shape[2]
    ng = N_KV_A
    hl = HPG_A * tq
    gw = ng * HEAD_DIM
    for name in ("k_slc", "v_slc", "k_win", "v_win"):
        assert units[name] % ng == 0
    kern = functools.partial(_nsa_kernel, tq=tq, seq=seq)

    def slab(name):
        return pl.BlockSpec((seq, gw), lambda b, i: (b, units[name] // ng))

    cmp_spec = pl.BlockSpec((None, ng, n_cmp_rows, HEAD_DIM), lambda b, i: (b, 0, 0, 0))

    onehot = jnp.asarray(np.arange(seq)[:, None] // SLC_BLK == np.arange(LANES)[None, :], BF16)
    return pl.pallas_call(
        kern,
        grid=(batch, nq),
        in_specs=[
            pl.BlockSpec((tq, A_Q), lambda b, i: (b * nq + i, 0)),
            cmp_spec,
            cmp_spec,
            slab("k_slc"),
            pl.BlockSpec((seq, LANES), lambda b, i: (0, 0)),
            slab("v_slc"),
            slab("k_win"),
            slab("v_win"),
            pl.BlockSpec((tq, LANES), lambda b, i: (b * nq + i, units["gates"])),
            pl.BlockSpec((1, LANES), lambda b, i: (0, 0)),
        ],
        out_specs=pl.BlockSpec((tq, A_Q), lambda b, i: (b * nq + i, 0)),
        out_shape=jax.ShapeDtypeStruct((batch * seq, A_Q), BF16),
        scratch_shapes=[
            pltpu.VMEM((ng, hl, HEAD_DIM + LANES), BF16),
            pltpu.VMEM((ng, NSA_KC, hl), F32),
            pltpu.VMEM((ng, NSA_KC, hl), F32),
            pltpu.VMEM((ng, WIN_A + tq, hl), F32),
            pltpu.VMEM((ng, HEAD_DIM, hl), F32),
            pltpu.VMEM((ng, 1, hl), F32),
            pltpu.VMEM((ng, 1, hl), F32),
            pltpu.VMEM((ng, HEAD_DIM, hl), F32),
        ],
        compiler_params=_compiler_params(("parallel", "arbitrary")),
        name="nsa_attention",
    )(z, kc, vc, z, onehot, z, z, z, z, gate_bias)


def _band_attn_kernel(*refs, tu, parts):
    q_refs = refs[:len(parts)]
    k_ref, v_ref, o_ref, lse_ref = refs[len(parts):]
    u0 = pl.program_id(2) * tu
    n_seq = k_ref.shape[0]
    windows = []
    for lk, span, stride in parts:
        ks = pl.multiple_of(jnp.clip(u0 - span, 0, n_seq - lk), LANES)
        dist = (u0 + lax.broadcasted_iota(jnp.int32, (tu, 1), 0)
                - (ks + lax.broadcasted_iota(jnp.int32, (1, lk), 1)))
        keep = (dist >= 0) & (dist <= span)
        if stride > 1:
            keep = keep & ((dist & (stride - 1)) == 0)
        windows.append((ks, lk, jnp.where(keep, 0.0, NEG_INF)))
    lane = lax.broadcasted_iota(jnp.int32, (1, LANES), 1)
    lse_tile = jnp.zeros((tu, LANES), F32)

    def head_cols(h):
        return slice(h * HEAD_DIM, (h + 1) * HEAD_DIM)

    scores = [[_dot_nt(q_ref[:, head_cols(h)], k_ref[pl.ds(ks, lk), head_cols(h)]) + bias
               for q_ref, (ks, lk, bias) in zip(q_refs, windows)]
              for h in range(DIL_HEADS)]
    probs, inv_l = [], []
    for h, s_parts in enumerate(scores):
        s = jnp.concatenate(s_parts, axis=1)
        m = jnp.max(s, axis=-1, keepdims=True)
        p = jnp.exp2(s - m)
        l = jnp.sum(p, axis=-1, keepdims=True)
        probs.append(p.astype(BF16))
        inv_l.append(1.0 / l)
        lse_tile = jnp.where(lane == h, m + jnp.log2(l), lse_tile)
    for h in range(DIL_HEADS):
        o, col = None, 0
        for ks, lk, _ in windows:
            pv = _dot(probs[h][:, col:col + lk], v_ref[pl.ds(ks, lk), head_cols(h)])
            o = pv if o is None else o + pv
            col += lk
        o_ref[:, head_cols(h)] = (o * inv_l[h]).astype(o_ref.dtype)
    lse_ref[...] = lse_tile


def _dil_merge_kernel(*refs):
    out_ref = refs[-1]
    calls = list(zip(refs[0:-1:2], refs[1:-1:2]))
    for h in range(DIL_HEADS):
        cols = slice(h * HEAD_DIM, (h + 1) * HEAD_DIM)
        shape = (out_ref.shape[0], HEAD_DIM)
        lses = [jnp.broadcast_to(l_ref[:, h:h + 1], shape) for _, l_ref in calls]
        top = functools.reduce(jnp.maximum, lses)
        ws = [jnp.exp2(lse - top) for lse in lses]
        num = sum(w * o_ref[:, cols].astype(F32) for w, (o_ref, _) in zip(ws, calls))
        out_ref[:, cols] = (num * (1.0 / sum(ws))).astype(out_ref.dtype)


def _dilated_attention(zb, *, batch, seq, units):
    n = zb.shape[1]
    width = DIL_HEADS * HEAD_DIM
    tu = DIL_TQ

    def band_call(r, q_units, parts, operands, k_unit, v_unit):
        n_seq = seq // r

        def z_spec(rows, unit, whole):
            return pl.BlockSpec((None, None, rows, width),
                                lambda b, c, i: (b, c, 0 if whole else i, unit))

        def out_spec(cols):
            return pl.BlockSpec((None, None, tu, cols), lambda b, c, i: (b, c, i, 0))

        return pl.pallas_call(
            functools.partial(_band_attn_kernel, tu=tu, parts=parts),
            grid=(batch, r, n_seq // tu),
            in_specs=[z_spec(tu, u, False) for u in q_units]
            + [z_spec(n_seq, k_unit, True), z_spec(n_seq, v_unit, True)],
            out_specs=[out_spec(width), out_spec(LANES)],
            out_shape=[jax.ShapeDtypeStruct((batch, r, n_seq, width), BF16),
                       jax.ShapeDtypeStruct((batch, r, n_seq, LANES), F32)],
            compiler_params=_compiler_params(("parallel", "parallel", "arbitrary")),
            name="dilated_attention",
        )(*operands)

    def window(n_seq, span):
        assert n_seq % tu == 0 and span % LANES == 0 and tu % LANES == 0
        return min(tu + span, n_seq)

    dense = [(gi, w, r) for gi, (w, r) in enumerate(DIL_CONFIGS) if r <= DIL_DENSE_MAX]
    zv = zb.reshape(batch, 1, seq, n)
    o_d, lse_d = band_call(1, [units["q"] + gi for gi, _, _ in dense],
                           tuple((window(seq, w), w, r) for _, w, r in dense),
                           [zv] * (len(dense) + 2), units["k"], units["v"])
    results = [o_d.reshape(batch * seq, width), lse_d.reshape(batch * seq, LANES)]
    for gi, (w, r) in enumerate(DIL_CONFIGS):
        if r <= DIL_DENSE_MAX:
            continue
        slabs = [(units["q"] + gi, width), (units["k"], width), (units["v"], width)]
        o_c, lse_c = band_call(r, [0], ((window(seq // r, w // r), w // r, 1),),
                               _to_class_order(zb, slabs, r, batch=batch, seq=seq), 0, 0)
        results += [_from_class_order(o_c, batch=batch, seq=seq),
                    jnp.transpose(lse_c, (0, 2, 1, 3)).reshape(batch * seq, LANES)]

    m = batch * seq
    tm = min(MERGE_TM, m)
    o_spec = pl.BlockSpec((tm, width), lambda i: (i, 0))
    l_spec = pl.BlockSpec((tm, LANES), lambda i: (i, 0))
    return pl.pallas_call(
        _dil_merge_kernel,
        grid=(m // tm,),
        in_specs=[o_spec, l_spec] * (len(results) // 2),
        out_specs=o_spec,
        out_shape=jax.ShapeDtypeStruct((m, width), BF16),
        compiler_params=_compiler_params(("parallel",)),
        name="dilated_merge",
    )(*results)


def _memory_attention(q_ref, kv_ref):
    def cols(h, base=0):
        return slice(base + h * HEAD_DIM, base + (h + 1) * HEAD_DIM)

    heads = range(N_MEM_HEADS)
    scores = [_dot_nt(q_ref[:, cols(h)], kv_ref[:, cols(h)]) for h in heads]
    probs = []
    for s in scores:
        e = jnp.exp(s - jnp.max(s, axis=-1, keepdims=True))
        probs.append((e / jnp.sum(e, axis=-1, keepdims=True)).astype(BF16))
    return jnp.concatenate([_dot(probs[h], kv_ref[:, cols(h, MEM_Q)]).astype(BF16)
                            for h in heads], axis=1)


def _out_proj_kernel(a1_ref, qm_ref, mkv_ref, w_ref, h_ref, g_ref, o_ref, xn_ref):
    a = jnp.concatenate([a1_ref[...], _memory_attention(qm_ref, mkv_ref)], axis=1)
    y = h_ref[...] + _dot(a, w_ref[...])
    o_ref[...] = y
    xn_ref[...] = _rms_rows(y, g_ref[...]).astype(xn_ref.dtype)


def _out_proj(a1, z, q_unit, mkv, w_bf, h, g_next, *, batch):
    m, d = h.shape
    tm = min(OUT_TM, m)
    k1 = a1.shape[1]
    assert w_bf.shape[0] == k1 + MEM_Q and (m // batch) % tm == 0
    blocks_per_batch = m // batch // tm
    n_mem = mkv.shape[0] // batch
    row = pl.BlockSpec((tm, d), lambda i: (i, 0))
    return pl.pallas_call(
        _out_proj_kernel,
        grid=(m // tm,),
        in_specs=[
            pl.BlockSpec((tm, k1), lambda i: (i, 0)),
            pl.BlockSpec((tm, MEM_Q), lambda i: (i, q_unit)),
            pl.BlockSpec((n_mem, 2 * MEM_Q), lambda i: (i // blocks_per_batch, 0)),
            pl.BlockSpec((k1 + MEM_Q, d), lambda i: (0, 0), pipeline_mode=pl.Buffered(1)),
            row,
            pl.BlockSpec((1, d), lambda i: (0, 0)),
        ],
        out_specs=[row, row],
        out_shape=[jax.ShapeDtypeStruct((m, d), F32), jax.ShapeDtypeStruct((m, d), BF16)],
        compiler_params=_compiler_params(("parallel",)),
        name="out_proj",
    )(a1, z, mkv, w_bf, h, g_next.reshape(1, d))


def _ffn_up_kernel(xn_ref, wg_ref, wu_ref, o_ref):
    xn = xn_ref[...]
    gate = _dot(xn, wg_ref[...])
    up = _dot(xn, wu_ref[...])
    o_ref[...] = (gate * jax.nn.sigmoid(gate) * up).astype(o_ref.dtype)


def _ffn_down_kernel(a_ref, w_ref, h_ref, g_ref, o_ref, *xn_ref, final_norm):
    y = h_ref[...] + _dot(a_ref[...], w_ref[...])
    yn = _rms_rows(y, g_ref[...])
    if final_norm:
        o_ref[...] = yn
    else:
        o_ref[...] = y
        xn_ref[0][...] = yn.astype(xn_ref[0].dtype)


def _ffn(h, xn, wg_bf, wu_bf, wd_bf, gain, *, final_norm):
    m, d = h.shape
    dff = wg_bf.shape[1]
    tm, tf = min(FFN_UP_TM, m), FFN_TF
    assert m % tm == 0 and dff % tf == 0
    act = pl.pallas_call(
        _ffn_up_kernel,
        grid=(m // tm, dff // tf),
        in_specs=[
            pl.BlockSpec((tm, d), lambda i, f: (i, 0)),
            pl.BlockSpec((d, tf), lambda i, f: (0, f)),
            pl.BlockSpec((d, tf), lambda i, f: (0, f)),
        ],
        out_specs=pl.BlockSpec((tm, tf), lambda i, f: (i, f)),
        out_shape=jax.ShapeDtypeStruct((m, dff), BF16),
        compiler_params=_compiler_params(("parallel", "arbitrary")),
        name="ffn_up",
    )(xn, wg_bf, wu_bf)

    tm = min(FFN_TM, m)
    kern = functools.partial(_ffn_down_kernel, final_norm=final_norm)
    row = pl.BlockSpec((tm, d), lambda i: (i, 0))
    h_shape = jax.ShapeDtypeStruct((m, d), F32)
    return pl.pallas_call(
        kern,
        grid=(m // tm,),
        in_specs=[
            pl.BlockSpec((tm, dff), lambda i: (i, 0)),
            pl.BlockSpec((dff, d), lambda i: (0, 0), pipeline_mode=pl.Buffered(1)),
            row,
            pl.BlockSpec((1, d), lambda i: (0, 0)),
        ],
        out_specs=row if final_norm else [row, row],
        out_shape=h_shape if final_norm else [h_shape, jax.ShapeDtypeStruct((m, d), BF16)],
        compiler_params=_compiler_params(("parallel",)),
        name="ffn_down",
    )(act, wd_bf, h, gain.reshape(1, d))


def _rope_tables(seq):
    inv = 1.0 / (ROPE_THETA ** (jnp.arange(0, HEAD_DIM, 2, dtype=F32) / HEAD_DIM))
    ang = jnp.arange(seq, dtype=F32)[:, None] * inv[None, :]
    cos, sin = jnp.cos(ang), jnp.sin(ang)
    return jnp.concatenate([cos, cos], axis=1), jnp.concatenate([-sin, sin], axis=1)


A_UNITS = {"q": 0, "k_cmp": 12, "k_slc": 14, "k_win": 16, "v_cmp": 18, "v_slc": 20,
           "v_win": 22, "mem_q": 24, "gates": 28}
B_UNITS = {"q": 0, "k": 3, "mem_q": 4, "v": 5}
B_TN = 2 * DIL_HEADS * HEAD_DIM
A_TN = 6 * HEAD_DIM
A_NPAD = 30 * HEAD_DIM
A_ROPE_BLOCKS = 3


def _layer_a_weight(w_in):
    wt = jnp.swapaxes(w_in, 0, 1)
    kv0 = A_Q

    def kv_rows(branch):
        return wt[kv0 + branch * N_KV_A * HEAD_DIM:kv0 + (branch + 1) * N_KV_A * HEAD_DIM]

    gate0 = A_Q + A_KV
    mem0 = gate0 + A_GATE
    w = jnp.concatenate([wt[:A_Q], kv_rows(0), kv_rows(2), kv_rows(4), kv_rows(1),
                         kv_rows(3), kv_rows(5), wt[mem0:mem0 + MEM_Q], wt[gate0:mem0]], axis=0)
    w = jnp.pad(w, ((0, A_NPAD - w.shape[0]), (0, 0)))
    scale = np.ones((1, A_NPAD), np.float32)
    scale[0, :A_Q] = SCALE * LOG2E
    scale[0, A_UNITS["mem_q"] * HEAD_DIM:A_UNITS["mem_q"] * HEAD_DIM + MEM_Q] = SCALE
    return w.astype(BF16), jnp.asarray(scale)


def _layer_a(h, hn, mem, cosf, sinf, p, *, batch, seq):
    w_in_bf, col_scale = _layer_a_weight(p["w_in"])
    z = _norm_proj(h if hn is None else hn, p["norm_attn"], w_in_bf, col_scale, cosf, sinf,
                   tn=A_TN, n_rope_blocks=A_ROPE_BLOCKS, seq=seq, w_rows=True)

    gw = N_KV_A * HEAD_DIM
    k_raw, v_raw = _to_class_order(
        z, [(A_UNITS["k_cmp"] * HEAD_DIM // gw, gw), (A_UNITS["v_cmp"] * HEAD_DIM // gw, gw)],
        CMP_STRIDE, batch=batch, seq=seq)
    kc = _compress(k_raw, p["cmp_pe_k"], p["cmp_w1_k"].astype(BF16), p["cmp_w2_k"].astype(BF16))
    vc = _compress(v_raw, p["cmp_pe_v"], p["cmp_w1_v"].astype(BF16), p["cmp_w2_v"].astype(BF16))
    gb = jnp.pad(p["gate_bias"], (0, LANES - A_GATE)).reshape(1, LANES)
    o_nsa = _nsa_attention(z, kc, vc, gb, batch=batch, seq=seq, units=A_UNITS)

    mkv = _mem_kv(mem, p["norm_mem"], p["w_mem_kv"])
    return _out_proj(o_nsa, z, A_UNITS["mem_q"] * HEAD_DIM // MEM_Q, mkv,
                     p["w_out"].astype(BF16), h, p["norm_ffn"], batch=batch)


def _mem_kv(mem, norm_mem, w_mem_kv):
    b, m, d = mem.shape
    ones = jnp.ones((1, w_mem_kv.shape[1]), F32)
    dummy = jnp.zeros((m, HEAD_DIM), F32)
    return _norm_proj(mem.reshape(b * m, d), norm_mem, w_mem_kv.astype(BF16), ones, dummy, dummy,
                      tn=MEM_Q, n_rope_blocks=0, seq=m, tm=m)


def kernel(x, mem, a_norm_attn, a_w_in, a_gate_bias, a_cmp_pe_k, a_cmp_w1_k, a_cmp_w2_k, a_cmp_pe_v, a_cmp_w1_v, a_cmp_w2_v, a_norm_mem, a_w_mem_kv, a_w_out, a_norm_ffn, a_w_gate, a_w_up, a_w_down, kv_norm, w_kv_shared, b_norm_attn, b_w_in, b_norm_mem, b_w_mem_kv, b_w_out, b_norm_ffn, b_w_gate, b_w_up, b_w_down, final_norm):
    batch, seq, d = x.shape
    n_a = a_w_in.shape[0]
    n_b = b_w_in.shape[0]
    cosf, sinf = _rope_tables(seq)
    h = x.reshape(batch * seq, d)
    unit_gain = jnp.ones((d,), F32)
    hn = None

    for l in range(n_a):
        p = {"norm_attn": a_norm_attn[l], "w_in": a_w_in[l], "gate_bias": a_gate_bias[l],
             "cmp_pe_k": a_cmp_pe_k[l], "cmp_w1_k": a_cmp_w1_k[l], "cmp_w2_k": a_cmp_w2_k[l],
             "cmp_pe_v": a_cmp_pe_v[l], "cmp_w1_v": a_cmp_w1_v[l], "cmp_w2_v": a_cmp_w2_v[l],
             "norm_mem": a_norm_mem[l], "w_mem_kv": a_w_mem_kv[l], "w_out": a_w_out[l],
             "norm_ffn": a_norm_ffn[l]}
        h, xn = _layer_a(h, hn, mem, cosf, sinf, p, batch=batch, seq=seq)
        last = (l == n_a - 1) and n_b == 0
        ffn_w = (a_w_gate[l].astype(BF16), a_w_up[l].astype(BF16), a_w_down[l].astype(BF16))
        if last:
            h = _ffn(h, xn, *ffn_w, final_norm, final_norm=True)
        else:
            next_gain = a_norm_attn[l + 1] if l + 1 < n_a else unit_gain
            h, hn = _ffn(h, xn, *ffn_w, next_gain, final_norm=False)

    if n_b > 0:
        assert n_b == 1, "the shared K/V projection is fused into the single mixer-B layer"
        n_kv_half = w_kv_shared.shape[1] // 2
        for l in range(n_b):
            w_q = b_norm_attn[l][:, None] * b_w_in[l]
            w_kv = kv_norm[:, None] * w_kv_shared
            w_cat = jnp.concatenate([w_q[:, :B_Q], w_kv[:, :n_kv_half], w_q[:, B_Q:],
                                     w_kv[:, n_kv_half:]], axis=1).astype(BF16)
            b_scale = np.ones((1, w_cat.shape[1]), np.float32)
            b_scale[0, :B_Q] = SCALE * LOG2E
            b_scale[0, B_Q + n_kv_half:B_Q + n_kv_half + MEM_Q] = SCALE
            zb = _norm_proj(h if hn is None else hn, unit_gain, w_cat, jnp.asarray(b_scale),
                            cosf, sinf, tm=PROJ_TM if hn is None else PROJ_TM_BF16,
                            tn=B_TN, n_rope_blocks=(B_Q + n_kv_half) // B_TN, seq=seq)
            o_dil = _dilated_attention(zb, batch=batch, seq=seq, units=B_UNITS)
            mkv = _mem_kv(mem, b_norm_mem[l], b_w_mem_kv[l])
            h, xn = _out_proj(o_dil, zb, B_UNITS["mem_q"], mkv, b_w_out[l].astype(BF16), h,
                              b_norm_ffn[l], batch=batch)
            h = _ffn(h, xn, b_w_gate[l].astype(BF16), b_w_up[l].astype(BF16),
                     b_w_down[l].astype(BF16), final_norm, final_norm=True)

    return h.reshape(batch, seq, d)
```

```python
import functools
import math

import numpy as np
import jax
import jax.numpy as jnp
from jax import lax
from jax.experimental import pallas as pl
from jax.experimental.pallas import tpu as pltpu

F32 = jnp.float32
BF16 = jnp.bfloat16

HEAD_DIM = 128
N_HEADS_A = 12
N_KV_A = 2
HPG_A = N_HEADS_A // N_KV_A
CMP_LEN = 32
CMP_STRIDE = 16
CMP_HIDDEN = 256
SLC_BLK = 64
SLC_SHIFT = SLC_BLK.bit_length() - 1
N_SEL = 16
WIN_A = 512
DIL_CONFIGS = ((128, 1), (512, 4), (2048, 16))
N_DIL_GROUPS = len(DIL_CONFIGS)
DIL_HEADS = 4
N_MEM_HEADS = 4
ROPE_THETA = 10000.0
EPS = 1e-6
NEG_INF = -1e30
TINY = 1e-30
SCALE = HEAD_DIM ** -0.5
LOG2E = math.log2(math.e)

A_Q = N_HEADS_A * HEAD_DIM
A_KV = 6 * N_KV_A * HEAD_DIM
A_GATE = 3 * N_HEADS_A
MEM_Q = N_MEM_HEADS * HEAD_DIM
B_Q = N_DIL_GROUPS * DIL_HEADS * HEAD_DIM

LANES = 128
SUBLANES = 8
VMEM_LIMIT_BYTES = 56 * 1024 * 1024

PROJ_TM = 1024
PROJ_TM_BF16 = 2048
FFN_UP_TM = 2048
FFN_TM = 512
FFN_TF = 512
OUT_TM = 512
NSA_TQ = 128
NSA_KC = 512
DIL_TQ = 256
DIL_DENSE_MAX = 4

NT_DIMS = (((1,), (1,)), ((), ()))
TN_DIMS = (((0,), (0,)), ((), ()))


def _compiler_params(semantics):
    return pltpu.CompilerParams(dimension_semantics=semantics,
                                vmem_limit_bytes=VMEM_LIMIT_BYTES)


def _rms_rows(x, g):
    ms = jnp.mean(x * x, axis=-1, keepdims=True)
    return x * lax.rsqrt(ms + EPS) * g


def _dot(a, b):
    return jnp.dot(a, b, preferred_element_type=F32)


def _dot_nt(a, b):
    return lax.dot_general(a, b, NT_DIMS, preferred_element_type=F32)


def _dot_tn(a, b):
    return lax.dot_general(a, b, TN_DIMS, preferred_element_type=F32)


def _norm_proj_kernel(x_ref, g_ref, w_ref, cs_ref, cos_ref, sin_ref, o_ref, *xn_ref,
                      n_rope_blocks, tn, w_rows):
    j = pl.program_id(1)
    if xn_ref:
        @pl.when(j == 0)
        def _():
            xn_ref[0][...] = _rms_rows(x_ref[...], g_ref[...]).astype(BF16)

        xn = xn_ref[0][...]
    else:
        xn = x_ref[...]
    w = w_ref[...]
    acc = (_dot_nt(xn, w) if w_rows else _dot(xn, w)) * cs_ref[...]

    if n_rope_blocks > 0:
        roped = j < n_rope_blocks
        c = jnp.where(roped, cos_ref[...], 1.0)
        s = jnp.where(roped, sin_ref[...], 0.0)
        for h in range(tn // HEAD_DIM):
            y = acc[:, h * HEAD_DIM:(h + 1) * HEAD_DIM]
            rot = pltpu.roll(y, HEAD_DIM // 2, 1)
            o_ref[:, h * HEAD_DIM:(h + 1) * HEAD_DIM] = (y * c + rot * s).astype(o_ref.dtype)
    else:
        o_ref[...] = acc.astype(o_ref.dtype)


def _norm_proj(x, g, w_bf, col_scale, cosf, sinf, *, tn, n_rope_blocks, seq, tm=PROJ_TM,
               w_rows=False):
    m, d = x.shape
    n = w_bf.shape[0] if w_rows else w_bf.shape[1]
    tm = min(tm, m)
    assert m % tm == 0 and n % tn == 0 and seq % tm == 0
    pos_blocks = seq // tm
    scratch = [] if x.dtype == BF16 else [pltpu.VMEM((tm, d), BF16)]
    kern = functools.partial(_norm_proj_kernel, n_rope_blocks=n_rope_blocks, tn=tn,
                             w_rows=w_rows)
    return pl.pallas_call(
        kern,
        grid=(m // tm, n // tn),
        in_specs=[
            pl.BlockSpec((tm, d), lambda i, j: (i, 0)),
            pl.BlockSpec((1, d), lambda i, j: (0, 0)),
            (pl.BlockSpec((tn, d), lambda i, j: (j, 0)) if w_rows
             else pl.BlockSpec((d, tn), lambda i, j: (0, j))),
            pl.BlockSpec((1, tn), lambda i, j: (0, j)),
            pl.BlockSpec((tm, HEAD_DIM), lambda i, j: (i % pos_blocks, 0)),
            pl.BlockSpec((tm, HEAD_DIM), lambda i, j: (i % pos_blocks, 0)),
        ],
        out_specs=pl.BlockSpec((tm, tn), lambda i, j: (i, j)),
        out_shape=jax.ShapeDtypeStruct((m, n), BF16),
        scratch_shapes=scratch,
        compiler_params=_compiler_params(("parallel", "arbitrary")),
        name="norm_proj",
    )(x, g.reshape(1, d), w_bf, col_scale, cosf, sinf)


def _class_perm(tm, r):
    dst = np.arange(tm)
    c, u = dst // (tm // r), dst % (tm // r)
    perm = np.zeros((tm, tm), np.float32)
    perm[dst, u * r + c] = 1.0
    return perm


def _to_class_kernel(p_ref, *refs, r):
    n = len(refs) // 2
    for x_ref, o_ref in zip(refs[:n], refs[n:]):
        y = _dot(p_ref[...], x_ref[...]).astype(o_ref.dtype)
        rows = y.shape[0] // r
        for c in range(r):
            o_ref[c] = y[c * rows:(c + 1) * rows, :]


def _to_class_order(x, slabs, r, *, batch, seq):
    tm = min(PROJ_TM, seq)
    nblk = seq // tm
    perm = jnp.asarray(_class_perm(tm, r), x.dtype)
    return pl.pallas_call(
        functools.partial(_to_class_kernel, r=r),
        grid=(batch * nblk,),
        in_specs=[pl.BlockSpec((tm, tm), lambda i: (0, 0))]
        + [pl.BlockSpec((tm, w), lambda i, cb=cb: (i, cb)) for cb, w in slabs],
        out_specs=[pl.BlockSpec((None, r, tm // r, w), lambda i: (i // nblk, 0, i % nblk, 0))
                   for _, w in slabs],
        out_shape=[jax.ShapeDtypeStruct((batch, r, seq // r, w), x.dtype) for _, w in slabs],
        compiler_params=_compiler_params(("parallel",)),
        name="to_class_order",
    )(perm, *([x] * len(slabs)))


def _from_class_kernel(pt_ref, x_ref, o_ref, *, r):
    x = jnp.concatenate([x_ref[c] for c in range(r)], axis=0)
    o_ref[...] = _dot(pt_ref[...], x).astype(o_ref.dtype)


def _from_class_order(xc, *, batch, seq):
    _, r, _, w = xc.shape
    tm = min(PROJ_TM, seq)
    nblk = seq // tm
    perm_t = jnp.asarray(_class_perm(tm, r).T, xc.dtype)
    return pl.pallas_call(
        functools.partial(_from_class_kernel, r=r),
        grid=(batch * nblk,),
        in_specs=[pl.BlockSpec((tm, tm), lambda i: (0, 0)),
                  pl.BlockSpec((None, r, tm // r, w), lambda i: (i // nblk, 0, i % nblk, 0))],
        out_specs=pl.BlockSpec((tm, w), lambda i: (i, 0)),
        out_shape=jax.ShapeDtypeStruct((batch * seq, w), xc.dtype),
        compiler_params=_compiler_params(("parallel",)),
        name="from_class_order",
    )(perm_t, xc)


def _compress_kernel(x_ref, pe_ref, w1_ref, w2_ref, o_ref):
    n_planes, n_rows, _ = x_ref.shape
    ylo = yhi = None
    for l in range(n_planes):
        x = x_ref[l].astype(F32)
        xlo = (x + pe_ref[l:l + 1, :]).astype(BF16)
        xhi = (x + pe_ref[n_planes + l:n_planes + l + 1, :]).astype(BF16)
        dlo = _dot(xlo, w1_ref[l * HEAD_DIM:(l + 1) * HEAD_DIM, :])
        dhi = _dot(xhi, w1_ref[(n_planes + l) * HEAD_DIM:(n_planes + l + 1) * HEAD_DIM, :])
        ylo = dlo if ylo is None else ylo + dlo
        yhi = dhi if yhi is None else yhi + dhi
    hid = ylo + pltpu.roll(yhi, n_rows - 1, 0)
    act = (hid * jax.nn.sigmoid(hid)).astype(BF16)
    o_ref[...] = _dot(act, w2_ref[...]).astype(o_ref.dtype)


def _compress(xc, pe, w1_bf, w2_bf):
    batch, planes, nrow, gd = xc.shape
    ng = gd // HEAD_DIM
    return pl.pallas_call(
        _compress_kernel,
        grid=(batch, ng),
        in_specs=[
            pl.BlockSpec((None, planes, nrow, HEAD_DIM), lambda b, g: (b, 0, 0, g)),
            pl.BlockSpec((CMP_LEN, HEAD_DIM), lambda b, g: (0, 0)),
            pl.BlockSpec((CMP_LEN * HEAD_DIM, CMP_HIDDEN), lambda b, g: (0, 0)),
            pl.BlockSpec((CMP_HIDDEN, HEAD_DIM), lambda b, g: (0, 0)),
        ],
        out_specs=pl.BlockSpec((None, None, nrow, HEAD_DIM), lambda b, g: (b, g, 0, 0)),
        out_shape=jax.ShapeDtypeStruct((batch, ng, nrow, HEAD_DIM), BF16),
        compiler_params=_compiler_params(("parallel", "arbitrary")),
        name="nsa_compress",
    )(xc, pe, w1_bf, w2_bf)


def _block_ranks(score, jrow):
    n_blk = score.shape[0]
    groups = n_blk // SUBLANES
    blocks = [score[SUBLANES * r:SUBLANES * (r + 1), :] for r in range(groups)]
    rows = [jrow[SUBLANES * r:SUBLANES * (r + 1), :] for r in range(groups)]
    ranks = [jnp.zeros(blocks[0].shape, F32) for _ in range(groups)]
    for j in range(n_blk):
        rj = score[j:j + 1, :]
        for r in range(groups):
            if r > j // SUBLANES:
                ahead = rj >= blocks[r]
            elif r < j // SUBLANES:
                ahead = rj > blocks[r]
            else:
                ahead = (rj > blocks[r]) | ((rj == blocks[r]) & (rows[r] > j))
            ranks[r] = ranks[r] + jnp.where(ahead, 1.0, 0.0)
    return jnp.concatenate(ranks, axis=0)


def _nsa_kernel(q_ref, kc_ref, vc_ref, ks_ref, e_ref, vs_ref, kw_ref, vw_ref, gz_ref, gb_ref,
                o_ref, qa_ref, sa_ref, sb_ref, sw_ref, ow_ref, m_ref, l_ref, acc_ref, *, tq, seq):
    hq = HPG_A
    groups = range(N_KV_A)
    n_cmp_rows = kc_ref.shape[1]
    n_slc = seq // SLC_BLK
    qi = pl.program_id(1)
    s0 = qi * tq
    t_row = s0 + lax.broadcasted_iota(jnp.int32, (1, tq), 1)

    def lanes(h):
        return slice(h * tq, (h + 1) * tq)

    def gcols(g):
        return slice(g * HEAD_DIM, (g + 1) * HEAD_DIM)

    def tile_heads(x):
        return jnp.concatenate([x] * hq, axis=1)

    wlen = WIN_A + tq
    ws = pl.multiple_of(jnp.maximum(s0 - WIN_A, 0), LANES)
    c_end = lax.broadcasted_iota(jnp.int32, (n_cmp_rows, 1), 0) * CMP_STRIDE + (CMP_LEN - 1)
    cbias = tile_heads(jnp.where(c_end <= t_row, 0.0, NEG_INF))
    any_cmp = tile_heads(t_row >= CMP_LEN - 1)
    jrow = lax.broadcasted_iota(jnp.int32, (n_slc, 1), 0)
    ccol = lax.broadcasted_iota(jnp.int32, (1, n_cmp_rows), 1)
    lo = (SLC_BLK // CMP_STRIDE) * jrow - (CMP_LEN // CMP_STRIDE - 1)
    hi = (SLC_BLK // CMP_STRIDE) * jrow + (SLC_BLK // CMP_STRIDE - 1)
    mmap = jnp.where((ccol >= lo) & (ccol <= hi), 1.0, 0.0).astype(BF16)
    cur = t_row >> SLC_SHIFT
    forced = (jrow == 0) | (jrow == cur) | (jrow == cur - 1)
    assert n_slc <= LANES and tq == LANES

    q6, sc = [], []
    for g in groups:
        for h in range(hq):
            head = g * hq + h
            qa_ref[g, lanes(h), 0:HEAD_DIM] = q_ref[:, head * HEAD_DIM:(head + 1) * HEAD_DIM]
        q6.append(qa_ref[g, :, 0:HEAD_DIM])
        sc.append(_dot_nt(kc_ref[g], q6[g]) + cbias)
    for g in groups:
        sw_ref[g] = _dot_nt(kw_ref[pl.ds(ws, wlen), gcols(g)], q6[g])

    o_cmp, score = [], []
    for g in groups:
        ec = jnp.exp2(sc[g] - jnp.max(sc[g], axis=0, keepdims=True))
        den = jnp.maximum(jnp.sum(ec, axis=0, keepdims=True), TINY)
        pc = ec * jnp.where(any_cmp, 1.0 / den, 0.0)
        o_cmp.append(_dot_tn(vc_ref[g], pc.astype(BF16)))
        psum = pc[:, lanes(0)]
        for h in range(1, hq):
            psum = psum + pc[:, lanes(h)]
        p1 = psum.astype(BF16)
        r1 = psum - p1.astype(F32)
        p2 = r1.astype(BF16)
        p3 = (r1 - p2.astype(F32)).astype(BF16)
        imp = _dot(mmap, p1) + _dot(mmap, p2) + _dot(mmap, p3)
        score.append(jnp.where(forced, 1e9, jnp.where(jrow <= cur, imp, -1e9)))

    for g in groups:
        rank = _block_ranks(score[g], jrow)
        sel_bias = jnp.where((rank < min(N_SEL, n_slc)) & (jrow <= cur), 0.0, NEG_INF)
        bias_q = jnp.transpose(jnp.concatenate(
            [sel_bias, jnp.zeros((LANES - n_slc, tq), F32)], axis=0)).astype(BF16)
        for h in range(hq):
            qa_ref[g, lanes(h), HEAD_DIM:HEAD_DIM + LANES] = bias_q

    m_ref[...] = jnp.full(m_ref.shape, NEG_INF, F32)
    l_ref[...] = jnp.zeros(l_ref.shape, F32)
    acc_ref[...] = jnp.zeros(acc_ref.shape, F32)

    def scores(g, c, s_ref):
        k0 = pl.multiple_of(c * NSA_KC, NSA_KC)
        k_aug = jnp.concatenate([ks_ref[pl.ds(k0, NSA_KC), gcols(g)],
                                 e_ref[pl.ds(k0, NSA_KC), :]], axis=1)
        s_ref[g] = _dot_nt(k_aug, qa_ref[g])

    def softmax_pv(g, c, s_ref):
        s = s_ref[g]
        m_old = m_ref[g]
        m_new = jnp.maximum(m_old, jnp.max(s, axis=0, keepdims=True))
        alpha = jnp.exp2(m_old - m_new)
        p = jnp.exp2(s - m_new)
        l_ref[g] = alpha * l_ref[g] + jnp.sum(p, axis=0, keepdims=True)
        k0 = pl.multiple_of(c * NSA_KC, NSA_KC)
        pv = _dot_tn(vs_ref[pl.ds(k0, NSA_KC), gcols(g)], p.astype(BF16))
        acc_ref[g] = alpha * acc_ref[g] + pv
        m_ref[g] = m_new

    last = s0 // NSA_KC
    for g in groups:
        scores(g, 0, sa_ref)

    dist = t_row - (ws + lax.broadcasted_iota(jnp.int32, (wlen, 1), 0))
    wbias = tile_heads(jnp.where((dist >= 0) & (dist < WIN_A), 0.0, NEG_INF))
    for g in groups:
        sw = sw_ref[g] + wbias
        ew = jnp.exp2(sw - jnp.max(sw, axis=0, keepdims=True))
        ow_ref[g] = (_dot_tn(vw_ref[pl.ds(ws, wlen), gcols(g)], ew.astype(BF16))
                     * (1.0 / jnp.sum(ew, axis=0, keepdims=True)))

    def chunk_pair(i, carry):
        for g in groups:
            scores(g, 2 * i + 1, sb_ref)
            softmax_pv(g, 2 * i, sa_ref)
        for g in groups:
            scores(g, 2 * i + 2, sa_ref)
            softmax_pv(g, 2 * i + 1, sb_ref)
        return carry

    lax.fori_loop(0, last // 2, chunk_pair, 0)

    def last_chunk(g, s_ref):
        diag = pl.multiple_of(s0 - last * NSA_KC, LANES)
        krow = lax.broadcasted_iota(jnp.int32, (tq, 1), 0)
        lane = lax.broadcasted_iota(jnp.int32, (1, tq), 1)
        causal = tile_heads(jnp.where(krow <= lane, 0.0, NEG_INF))
        s_ref[g, pl.ds(diag, tq), :] = s_ref[g, pl.ds(diag, tq), :] + causal
        softmax_pv(g, last, s_ref)

    @pl.when(last % 2 == 0)
    def _():
        for g in groups:
            last_chunk(g, sa_ref)

    @pl.when(last % 2 == 1)
    def _():
        for g in groups:
            scores(g, last, sb_ref)
            softmax_pv(g, last - 1, sa_ref)
        for g in groups:
            last_chunk(g, sb_ref)

    gates_t = jnp.transpose(jax.nn.sigmoid(gz_ref[...].astype(F32) + gb_ref[...]))
    for g in groups:
        o_slc = acc_ref[g] * (1.0 / l_ref[g])
        for h in range(hq):
            head = g * hq + h
            o_h = (gates_t[3 * head:3 * head + 1, :] * o_cmp[g][:, lanes(h)]
                   + gates_t[3 * head + 1:3 * head + 2, :] * o_slc[:, lanes(h)]
                   + gates_t[3 * head + 2:3 * head + 3, :] * ow_ref[g, :, lanes(h)])
            o_ref[:, head * HEAD_DIM:(head + 1) * HEAD_DIM] = jnp.transpose(o_h).astype(o_ref.dtype)


def _nsa_attention(z, kc, vc, gate_bias, *, batch, seq, units):
    tq = NSA_TQ
    nq = seq // tq
    n_cmp_rows = kc.shape[2]
    ng = N_KV_A
    hl = HPG_A * tq
    gw = ng * HEAD_DIM
    for name in ("k_slc", "v_slc", "k_win", "v_win"):
        assert units[name] % ng == 0
    kern = functools.partial(_nsa_kernel, tq=tq, seq=seq)

    def slab(name):
        return pl.BlockSpec((seq, gw), lambda b, i: (b, units[name] // ng))

    cmp_spec = pl.BlockSpec((None, ng, n_cmp_rows, HEAD_DIM), lambda b, i: (b, 0, 0, 0))

    onehot = jnp.asarray(np.arange(seq)[:, None] // SLC_BLK == np.arange(LANES)[None, :], BF16)
    return pl.pallas_call(
        kern,
        grid=(batch, nq),
        in_specs=[
            pl.BlockSpec((tq, A_Q), lambda b, i: (b * nq + i, 0)),
            cmp_spec,
            cmp_spec,
            slab("k_slc"),
            pl.BlockSpec((seq, LANES), lambda b, i: (0, 0)),
            slab("v_slc"),
            slab("k_win"),
            slab("v_win"),
            pl.BlockSpec((tq, LANES), lambda b, i: (b * nq + i, units["gates"])),
            pl.BlockSpec((1, LANES), lambda b, i: (0, 0)),
        ],
        out_specs=pl.BlockSpec((tq, A_Q), lambda b, i: (b * nq + i, 0)),
        out_shape=jax.ShapeDtypeStruct((batch * seq, A_Q), BF16),
        scratch_shapes=[
            pltpu.VMEM((ng, hl, HEAD_DIM + LANES), BF16),
            pltpu.VMEM((ng, NSA_KC, hl), F32),
            pltpu.VMEM((ng, NSA_KC, hl), F32),
            pltpu.VMEM((ng, WIN_A + tq, hl), F32),
            pltpu.VMEM((ng, HEAD_DIM, hl), F32),
            pltpu.VMEM((ng, 1, hl), F32),
            pltpu.VMEM((ng, 1, hl), F32),
            pltpu.VMEM((ng, HEAD_DIM, hl), F32),
        ],
        compiler_params=_compiler_params(("parallel", "arbitrary")),
        name="nsa_attention",
    )(z, kc, vc, z, onehot, z, z, z, z, gate_bias)


def _band_attn_kernel(*refs, tu, parts):
    q_refs = refs[:len(parts)]
    k_ref, v_ref, o_ref, lse_ref = refs[len(parts):]
    u0 = pl.program_id(2) * tu
    n_seq = k_ref.shape[0]
    windows = []
    for lk, span, stride in parts:
        ks = pl.multiple_of(jnp.clip(u0 - span, 0, n_seq - lk), LANES)
        dist = (u0 + lax.broadcasted_iota(jnp.int32, (tu, 1), 0)
                - (ks + lax.broadcasted_iota(jnp.int32, (1, lk), 1)))
        keep = (dist >= 0) & (dist <= span)
        if stride > 1:
            keep = keep & ((dist & (stride - 1)) == 0)
        windows.append((ks, lk, jnp.where(keep, 0.0, NEG_INF)))
    lane = lax.broadcasted_iota(jnp.int32, (1, LANES), 1)
    lse_tile = jnp.zeros((tu, LANES), F32)

    def head_cols(h):
        return slice(h * HEAD_DIM, (h + 1) * HEAD_DIM)

    scores = [[_dot_nt(q_ref[:, head_cols(h)], k_ref[pl.ds(ks, lk), head_cols(h)]) + bias
               for q_ref, (ks, lk, bias) in zip(q_refs, windows)]
              for h in range(DIL_HEADS)]
    probs, inv_l = [], []
    for h, s_parts in enumerate(scores):
        s = jnp.concatenate(s_parts, axis=1)
        m = jnp.max(s, axis=-1, keepdims=True)
        p = jnp.exp2(s - m)
        l = jnp.sum(p, axis=-1, keepdims=True)
        probs.append(p.astype(BF16))
        inv_l.append(1.0 / l)
        lse_tile = jnp.where(lane == h, m + jnp.log2(l), lse_tile)
    for h in range(DIL_HEADS):
        o, col = None, 0
        for ks, lk, _ in windows:
            pv = _dot(probs[h][:, col:col + lk], v_ref[pl.ds(ks, lk), head_cols(h)])
            o = pv if o is None else o + pv
            col += lk
        o_ref[:, head_cols(h)] = (o * inv_l[h]).astype(o_ref.dtype)
    lse_ref[...] = lse_tile


def _merge_dilation_calls(refs):
    calls = list(zip(refs[0::2], refs[1::2]))
    merged = []
    for h in range(DIL_HEADS):
        cols = slice(h * HEAD_DIM, (h + 1) * HEAD_DIM)
        shape = (calls[0][0].shape[0], HEAD_DIM)
        lses = [jnp.broadcast_to(l_ref[:, h:h + 1], shape) for _, l_ref in calls]
        top = functools.reduce(jnp.maximum, lses)
        ws = [jnp.exp2(lse - top) for lse in lses]
        num = sum(w * o_ref[:, cols].astype(F32) for w, (o_ref, _) in zip(ws, calls))
        merged.append((num * (1.0 / sum(ws))).astype(BF16))
    return jnp.concatenate(merged, axis=1)


def _dilated_attention(zb, *, batch, seq, units):
    n = zb.shape[1]
    width = DIL_HEADS * HEAD_DIM
    tu = DIL_TQ

    def band_call(r, q_units, parts, operands, k_unit, v_unit):
        n_seq = seq // r

        def z_spec(rows, unit, whole):
            return pl.BlockSpec((None, None, rows, width),
                                lambda b, c, i: (b, c, 0 if whole else i, unit))

        def out_spec(cols):
            return pl.BlockSpec((None, None, tu, cols), lambda b, c, i: (b, c, i, 0))

        return pl.pallas_call(
            functools.partial(_band_attn_kernel, tu=tu, parts=parts),
            grid=(batch, r, n_seq // tu),
            in_specs=[z_spec(tu, u, False) for u in q_units]
            + [z_spec(n_seq, k_unit, True), z_spec(n_seq, v_unit, True)],
            out_specs=[out_spec(width), out_spec(LANES)],
            out_shape=[jax.ShapeDtypeStruct((batch, r, n_seq, width), BF16),
                       jax.ShapeDtypeStruct((batch, r, n_seq, LANES), F32)],
            compiler_params=_compiler_params(("parallel", "parallel", "arbitrary")),
            name="dilated_attention",
        )(*operands)

    def window(n_seq, span):
        assert n_seq % tu == 0 and span % LANES == 0 and tu % LANES == 0
        return min(tu + span, n_seq)

    dense = [(gi, w, r) for gi, (w, r) in enumerate(DIL_CONFIGS) if r <= DIL_DENSE_MAX]
    zv = zb.reshape(batch, 1, seq, n)
    o_d, lse_d = band_call(1, [units["q"] + gi for gi, _, _ in dense],
                           tuple((window(seq, w), w, r) for _, w, r in dense),
                           [zv] * (len(dense) + 2), units["k"], units["v"])
    results = [o_d.reshape(batch * seq, width), lse_d.reshape(batch * seq, LANES)]
    for gi, (w, r) in enumerate(DIL_CONFIGS):
        if r <= DIL_DENSE_MAX:
            continue
        slabs = [(units["q"] + gi, width), (units["k"], width), (units["v"], width)]
        o_c, lse_c = band_call(r, [0], ((window(seq // r, w // r), w // r, 1),),
                               _to_class_order(zb, slabs, r, batch=batch, seq=seq), 0, 0)
        results += [_from_class_order(o_c, batch=batch, seq=seq),
                    jnp.transpose(lse_c, (0, 2, 1, 3)).reshape(batch * seq, LANES)]

    return results


def _memory_attention(q_ref, kv_ref):
    def cols(h, base=0):
        return slice(base + h * HEAD_DIM, base + (h + 1) * HEAD_DIM)

    heads = range(N_MEM_HEADS)
    scores = [_dot_nt(q_ref[:, cols(h)], kv_ref[:, cols(h)]) for h in heads]
    probs = []
    for s in scores:
        e = jnp.exp(s - jnp.max(s, axis=-1, keepdims=True))
        probs.append((e / jnp.sum(e, axis=-1, keepdims=True)).astype(BF16))
    return jnp.concatenate([_dot(probs[h], kv_ref[:, cols(h, MEM_Q)]).astype(BF16)
                            for h in heads], axis=1)


def _out_proj_kernel(*refs):
    a1_refs = refs[:-7]
    qm_ref, mkv_ref, w_ref, h_ref, g_ref, o_ref, xn_ref = refs[-7:]
    a1 = a1_refs[0][...] if len(a1_refs) == 1 else _merge_dilation_calls(a1_refs)
    a = jnp.concatenate([a1, _memory_attention(qm_ref, mkv_ref)], axis=1)
    y = h_ref[...] + _dot(a, w_ref[...])
    o_ref[...] = y
    xn_ref[...] = _rms_rows(y, g_ref[...]).astype(xn_ref.dtype)


def _out_proj(a1, z, q_unit, mkv, w_bf, h, g_next, *, batch):
    m, d = h.shape
    tm = min(OUT_TM, m)
    a1_list = list(a1) if isinstance(a1, (list, tuple)) else [a1]
    k1 = a1_list[0].shape[1]
    assert w_bf.shape[0] == k1 + MEM_Q and (m // batch) % tm == 0
    blocks_per_batch = m // batch // tm
    n_mem = mkv.shape[0] // batch
    row = pl.BlockSpec((tm, d), lambda i: (i, 0))
    return pl.pallas_call(
        _out_proj_kernel,
        grid=(m // tm,),
        in_specs=[pl.BlockSpec((tm, x.shape[1]), lambda i: (i, 0)) for x in a1_list] + [
            pl.BlockSpec((tm, MEM_Q), lambda i: (i, q_unit)),
            pl.BlockSpec((n_mem, 2 * MEM_Q), lambda i: (i // blocks_per_batch, 0)),
            pl.BlockSpec((k1 + MEM_Q, d), lambda i: (0, 0), pipeline_mode=pl.Buffered(1)),
            row,
            pl.BlockSpec((1, d), lambda i: (0, 0)),
        ],
        out_specs=[row, row],
        out_shape=[jax.ShapeDtypeStruct((m, d), F32), jax.ShapeDtypeStruct((m, d), BF16)],
        compiler_params=_compiler_params(("parallel",)),
        name="out_proj",
    )(*a1_list, z, mkv, w_bf, h, g_next.reshape(1, d))


def _ffn_up_kernel(xn_ref, wg_ref, wu_ref, o_ref):
    xn = xn_ref[...]
    gate = _dot(xn, wg_ref[...])
    up = _dot(xn, wu_ref[...])
    o_ref[...] = (gate * jax.nn.sigmoid(gate) * up).astype(o_ref.dtype)


def _ffn_down_kernel(a_ref, w_ref, h_ref, g_ref, o_ref, *xn_ref, final_norm):
    y = h_ref[...] + _dot(a_ref[...], w_ref[...])
    yn = _rms_rows(y, g_ref[...])
    if final_norm:
        o_ref[...] = yn
    else:
        o_ref[...] = y
        xn_ref[0][...] = yn.astype(xn_ref[0].dtype)


def _ffn(h, xn, wg_bf, wu_bf, wd_bf, gain, *, final_norm):
    m, d = h.shape
    dff = wg_bf.shape[1]
    tm, tf = min(FFN_UP_TM, m), FFN_TF
    assert m % tm == 0 and dff % tf == 0
    act = pl.pallas_call(
        _ffn_up_kernel,
        grid=(m // tm, dff // tf),
        in_specs=[
            pl.BlockSpec((tm, d), lambda i, f: (i, 0)),
            pl.BlockSpec((d, tf), lambda i, f: (0, f)),
            pl.BlockSpec((d, tf), lambda i, f: (0, f)),
        ],
        out_specs=pl.BlockSpec((tm, tf), lambda i, f: (i, f)),
        out_shape=jax.ShapeDtypeStruct((m, dff), BF16),
        compiler_params=_compiler_params(("parallel", "arbitrary")),
        name="ffn_up",
    )(xn, wg_bf, wu_bf)

    tm = min(FFN_TM, m)
    kern = functools.partial(_ffn_down_kernel, final_norm=final_norm)
    row = pl.BlockSpec((tm, d), lambda i: (i, 0))
    h_shape = jax.ShapeDtypeStruct((m, d), F32)
    return pl.pallas_call(
        kern,
        grid=(m // tm,),
        in_specs=[
            pl.BlockSpec((tm, dff), lambda i: (i, 0)),
            pl.BlockSpec((dff, d), lambda i: (0, 0), pipeline_mode=pl.Buffered(1)),
            row,
            pl.BlockSpec((1, d), lambda i: (0, 0)),
        ],
        out_specs=row if final_norm else [row, row],
        out_shape=h_shape if final_norm else [h_shape, jax.ShapeDtypeStruct((m, d), BF16)],
        compiler_params=_compiler_params(("parallel",)),
        name="ffn_down",
    )(act, wd_bf, h, gain.reshape(1, d))


def _rope_tables(seq):
    inv = 1.0 / (ROPE_THETA ** (jnp.arange(0, HEAD_DIM, 2, dtype=F32) / HEAD_DIM))
    ang = jnp.arange(seq, dtype=F32)[:, None] * inv[None, :]
    cos, sin = jnp.cos(ang), jnp.sin(ang)
    return jnp.concatenate([cos, cos], axis=1), jnp.concatenate([-sin, sin], axis=1)


A_UNITS = {"q": 0, "k_cmp": 12, "k_slc": 14, "k_win": 16, "v_cmp": 18, "v_slc": 20,
           "v_win": 22, "mem_q": 24, "gates": 28}
B_UNITS = {"q": 0, "k": 3, "mem_q": 4, "v": 5}
B_TN = 2 * DIL_HEADS * HEAD_DIM
A_TN = 6 * HEAD_DIM
A_NPAD = 30 * HEAD_DIM
A_ROPE_BLOCKS = 3


def _layer_a_weight(w_in):
    wt = jnp.swapaxes(w_in, 0, 1)
    kv0 = A_Q

    def kv_rows(branch):
        return wt[kv0 + branch * N_KV_A * HEAD_DIM:kv0 + (branch + 1) * N_KV_A * HEAD_DIM]

    gate0 = A_Q + A_KV
    mem0 = gate0 + A_GATE
    w = jnp.concatenate([wt[:A_Q], kv_rows(0), kv_rows(2), kv_rows(4), kv_rows(1),
                         kv_rows(3), kv_rows(5), wt[mem0:mem0 + MEM_Q], wt[gate0:mem0]], axis=0)
    w = jnp.pad(w, ((0, A_NPAD - w.shape[0]), (0, 0)))
    scale = np.ones((1, A_NPAD), np.float32)
    scale[0, :A_Q] = SCALE * LOG2E
    scale[0, A_UNITS["mem_q"] * HEAD_DIM:A_UNITS["mem_q"] * HEAD_DIM + MEM_Q] = SCALE
    return w.astype(BF16), jnp.asarray(scale)


def _layer_a(h, hn, mem, cosf, sinf, p, *, batch, seq):
    w_in_bf, col_scale = _layer_a_weight(p["w_in"])
    z = _norm_proj(h if hn is None else hn, p["norm_attn"], w_in_bf, col_scale, cosf, sinf,
                   tn=A_TN, n_rope_blocks=A_ROPE_BLOCKS, seq=seq, w_rows=True)

    gw = N_KV_A * HEAD_DIM
    k_raw, v_raw = _to_class_order(
        z, [(A_UNITS["k_cmp"] * HEAD_DIM // gw, gw), (A_UNITS["v_cmp"] * HEAD_DIM // gw, gw)],
        CMP_STRIDE, batch=batch, seq=seq)
    kc = _compress(k_raw, p["cmp_pe_k"], p["cmp_w1_k"].astype(BF16), p["cmp_w2_k"].astype(BF16))
    vc = _compress(v_raw, p["cmp_pe_v"], p["cmp_w1_v"].astype(BF16), p["cmp_w2_v"].astype(BF16))
    gb = jnp.pad(p["gate_bias"], (0, LANES - A_GATE)).reshape(1, LANES)
    o_nsa = _nsa_attention(z, kc, vc, gb, batch=batch, seq=seq, units=A_UNITS)

    mkv = _mem_kv(mem, p["norm_mem"], p["w_mem_kv"])
    return _out_proj(o_nsa, z, A_UNITS["mem_q"] * HEAD_DIM // MEM_Q, mkv,
                     p["w_out"].astype(BF16), h, p["norm_ffn"], batch=batch)


def _mem_kv(mem, norm_mem, w_mem_kv):
    b, m, d = mem.shape
    ones = jnp.ones((1, w_mem_kv.shape[1]), F32)
    dummy = jnp.zeros((m, HEAD_DIM), F32)
    return _norm_proj(mem.reshape(b * m, d), norm_mem, w_mem_kv.astype(BF16), ones, dummy, dummy,
                      tn=MEM_Q, n_rope_blocks=0, seq=m, tm=m)


def kernel(x, mem, a_norm_attn, a_w_in, a_gate_bias, a_cmp_pe_k, a_cmp_w1_k, a_cmp_w2_k, a_cmp_pe_v, a_cmp_w1_v, a_cmp_w2_v, a_norm_mem, a_w_mem_kv, a_w_out, a_norm_ffn, a_w_gate, a_w_up, a_w_down, kv_norm, w_kv_shared, b_norm_attn, b_w_in, b_norm_mem, b_w_mem_kv, b_w_out, b_norm_ffn, b_w_gate, b_w_up, b_w_down, final_norm):
    batch, seq, d = x.shape
    n_a = a_w_in.shape[0]
    n_b = b_w_in.shape[0]
    cosf, sinf = _rope_tables(seq)
    h = x.reshape(batch * seq, d)
    unit_gain = jnp.ones((d,), F32)
    hn = None

    for l in range(n_a):
        p = {"norm_attn": a_norm_attn[l], "w_in": a_w_in[l], "gate_bias": a_gate_bias[l],
             "cmp_pe_k": a_cmp_pe_k[l], "cmp_w1_k": a_cmp_w1_k[l], "cmp_w2_k": a_cmp_w2_k[l],
             "cmp_pe_v": a_cmp_pe_v[l], "cmp_w1_v": a_cmp_w1_v[l], "cmp_w2_v": a_cmp_w2_v[l],
             "norm_mem": a_norm_mem[l], "w_mem_kv": a_w_mem_kv[l], "w_out": a_w_out[l],
             "norm_ffn": a_norm_ffn[l]}
        h, xn = _layer_a(h, hn, mem, cosf, sinf, p, batch=batch, seq=seq)
        last = (l == n_a - 1) and n_b == 0
        ffn_w = (a_w_gate[l].astype(BF16), a_w_up[l].astype(BF16), a_w_down[l].astype(BF16))
        if last:
            h = _ffn(h, xn, *ffn_w, final_norm, final_norm=True)
        else:
            next_gain = a_norm_attn[l + 1] if l + 1 < n_a else unit_gain
            h, hn = _ffn(h, xn, *ffn_w, next_gain, final_norm=False)

    if n_b > 0:
        assert n_b == 1, "the shared K/V projection is fused into the single mixer-B layer"
        n_kv_half = w_kv_shared.shape[1] // 2
        for l in range(n_b):
            w_q = b_norm_attn[l][:, None] * b_w_in[l]
            w_kv = kv_norm[:, None] * w_kv_shared
            w_cat = jnp.concatenate([w_q[:, :B_Q], w_kv[:, :n_kv_half], w_q[:, B_Q:],
                                     w_kv[:, n_kv_half:]], axis=1).astype(BF16)
            b_scale = np.ones((1, w_cat.shape[1]), np.float32)
            b_scale[0, :B_Q] = SCALE * LOG2E
            b_scale[0, B_Q + n_kv_half:B_Q + n_kv_half + MEM_Q] = SCALE
            zb = _norm_proj(h if hn is None else hn, unit_gain, w_cat, jnp.asarray(b_scale),
                            cosf, sinf, tm=PROJ_TM if hn is None else PROJ_TM_BF16,
                            tn=B_TN, n_rope_blocks=(B_Q + n_kv_half) // B_TN, seq=seq)
            o_dil = _dilated_attention(zb, batch=batch, seq=seq, units=B_UNITS)
            mkv = _mem_kv(mem, b_norm_mem[l], b_w_mem_kv[l])
            h, xn = _out_proj(o_dil, zb, B_UNITS["mem_q"], mkv, b_w_out[l].astype(BF16), h,
                              b_norm_ffn[l], batch=batch)
            h = _ffn(h, xn, b_w_gate[l].astype(BF16), b_w_up[l].astype(BF16),
                     b_w_down[l].astype(BF16), final_norm, final_norm=True)

    return h.reshape(batch, seq, d)
```

```python
import functools
import math

import numpy as np
import jax
import jax.numpy as jnp
from jax import lax
from jax.experimental import pallas as pl
from jax.experimental.pallas import tpu as pltpu

F32 = jnp.float32
BF16 = jnp.bfloat16

HEAD_DIM = 128
N_HEADS_A = 12
N_KV_A = 2
HPG_A = N_HEADS_A // N_KV_A
CMP_LEN = 32
CMP_STRIDE = 16
CMP_HIDDEN = 256
SLC_BLK = 64
SLC_SHIFT = SLC_BLK.bit_length() - 1
N_SEL = 16
WIN_A = 512
DIL_CONFIGS = ((128, 1), (512, 4), (2048, 16))
N_DIL_GROUPS = len(DIL_CONFIGS)
DIL_HEADS = 4
N_MEM_HEADS = 4
ROPE_THETA = 10000.0
EPS = 1e-6
NEG_INF = -1e30
TINY = 1e-30
SCALE = HEAD_DIM ** -0.5
LOG2E = math.log2(math.e)

A_Q = N_HEADS_A * HEAD_DIM
A_KV = 6 * N_KV_A * HEAD_DIM
A_GATE = 3 * N_HEADS_A
MEM_Q = N_MEM_HEADS * HEAD_DIM
B_Q = N_DIL_GROUPS * DIL_HEADS * HEAD_DIM

LANES = 128
SUBLANES = 8
VMEM_LIMIT_BYTES = 56 * 1024 * 1024

PROJ_TM = 1024
PROJ_TM_BF16 = 2048
FFN_UP_TM = 2048
FFN_TM = 512
FFN_TF = 512
FFN_GU_CHUNK = 256
OUT_TM = 512
NSA_TQ = 128
NSA_KC = 512
DIL_TQ = 256
DIL_DENSE_MAX = 4

NT_DIMS = (((1,), (1,)), ((), ()))
TN_DIMS = (((0,), (0,)), ((), ()))


def _compiler_params(semantics):
    return pltpu.CompilerParams(dimension_semantics=semantics,
                                vmem_limit_bytes=VMEM_LIMIT_BYTES)


def _rms_rows(x, g):
    ms = jnp.mean(x * x, axis=-1, keepdims=True)
    return x * lax.rsqrt(ms + EPS) * g


def _dot(a, b):
    return jnp.dot(a, b, preferred_element_type=F32)


def _dot_nt(a, b):
    return lax.dot_general(a, b, NT_DIMS, preferred_element_type=F32)


def _dot_tn(a, b):
    return lax.dot_general(a, b, TN_DIMS, preferred_element_type=F32)


def _norm_proj_kernel(x_ref, g_ref, w_ref, cs_ref, cos_ref, sin_ref, o_ref, *xn_ref,
                      n_rope_blocks, tn, w_rows):
    j = pl.program_id(1)
    if xn_ref:
        @pl.when(j == 0)
        def _():
            xn_ref[0][...] = _rms_rows(x_ref[...], g_ref[...]).astype(BF16)

        xn = xn_ref[0][...]
    else:
        xn = x_ref[...]
    w = w_ref[...]
    acc = (_dot_nt(xn, w) if w_rows else _dot(xn, w)) * cs_ref[...]

    if n_rope_blocks > 0:
        roped = j < n_rope_blocks
        c = jnp.where(roped, cos_ref[...], 1.0)
        s = jnp.where(roped, sin_ref[...], 0.0)
        for h in range(tn // HEAD_DIM):
            y = acc[:, h * HEAD_DIM:(h + 1) * HEAD_DIM]
            rot = pltpu.roll(y, HEAD_DIM // 2, 1)
            o_ref[:, h * HEAD_DIM:(h + 1) * HEAD_DIM] = (y * c + rot * s).astype(o_ref.dtype)
    else:
        o_ref[...] = acc.astype(o_ref.dtype)


def _norm_proj(x, g, w_bf, col_scale, cosf, sinf, *, tn, n_rope_blocks, seq, tm=PROJ_TM,
               w_rows=False):
    m, d = x.shape
    n = w_bf.shape[0] if w_rows else w_bf.shape[1]
    tm = min(tm, m)
    assert m % tm == 0 and n % tn == 0 and seq % tm == 0
    pos_blocks = seq // tm
    scratch = [] if x.dtype == BF16 else [pltpu.VMEM((tm, d), BF16)]
    kern = functools.partial(_norm_proj_kernel, n_rope_blocks=n_rope_blocks, tn=tn,
                             w_rows=w_rows)
    return pl.pallas_call(
        kern,
        grid=(m // tm, n // tn),
        in_specs=[
            pl.BlockSpec((tm, d), lambda i, j: (i, 0)),
            pl.BlockSpec((1, d), lambda i, j: (0, 0)),
            (pl.BlockSpec((tn, d), lambda i, j: (j, 0)) if w_rows
             else pl.BlockSpec((d, tn), lambda i, j: (0, j))),
            pl.BlockSpec((1, tn), lambda i, j: (0, j)),
            pl.BlockSpec((tm, HEAD_DIM), lambda i, j: (i % pos_blocks, 0)),
            pl.BlockSpec((tm, HEAD_DIM), lambda i, j: (i % pos_blocks, 0)),
        ],
        out_specs=pl.BlockSpec((tm, tn), lambda i, j: (i, j)),
        out_shape=jax.ShapeDtypeStruct((m, n), BF16),
        scratch_shapes=scratch,
        compiler_params=_compiler_params(("parallel", "arbitrary")),
        name="norm_proj",
    )(x, g.reshape(1, d), w_bf, col_scale, cosf, sinf)


def _class_perm(tm, r):
    dst = np.arange(tm)
    c, u = dst // (tm // r), dst % (tm // r)
    perm = np.zeros((tm, tm), np.float32)
    perm[dst, u * r + c] = 1.0
    return perm


def _to_class_kernel(p_ref, *refs, r):
    n = len(refs) // 2
    for x_ref, o_ref in zip(refs[:n], refs[n:]):
        y = _dot(p_ref[...], x_ref[...]).astype(o_ref.dtype)
        rows = y.shape[0] // r
        for c in range(r):
            o_ref[c] = y[c * rows:(c + 1) * rows, :]


def _to_class_order(x, slabs, r, *, batch, seq):
    tm = min(PROJ_TM, seq)
    nblk = seq // tm
    perm = jnp.asarray(_class_perm(tm, r), x.dtype)
    return pl.pallas_call(
        functools.partial(_to_class_kernel, r=r),
        grid=(batch * nblk,),
        in_specs=[pl.BlockSpec((tm, tm), lambda i: (0, 0))]
        + [pl.BlockSpec((tm, w), lambda i, cb=cb: (i, cb)) for cb, w in slabs],
        out_specs=[pl.BlockSpec((None, r, tm // r, w), lambda i: (i // nblk, 0, i % nblk, 0))
                   for _, w in slabs],
        out_shape=[jax.ShapeDtypeStruct((batch, r, seq // r, w), x.dtype) for _, w in slabs],
        compiler_params=_compiler_params(("parallel",)),
        name="to_class_order",
    )(perm, *([x] * len(slabs)))


def _from_class_kernel(pt_ref, x_ref, o_ref, *, r):
    x = jnp.concatenate([x_ref[c] for c in range(r)], axis=0)
    o_ref[...] = _dot(pt_ref[...], x).astype(o_ref.dtype)


def _from_class_order(xc, *, batch, seq):
    _, r, _, w = xc.shape
    tm = min(PROJ_TM, seq)
    nblk = seq // tm
    perm_t = jnp.asarray(_class_perm(tm, r).T, xc.dtype)
    return pl.pallas_call(
        functools.partial(_from_class_kernel, r=r),
        grid=(batch * nblk,),
        in_specs=[pl.BlockSpec((tm, tm), lambda i: (0, 0)),
                  pl.BlockSpec((None, r, tm // r, w), lambda i: (i // nblk, 0, i % nblk, 0))],
        out_specs=pl.BlockSpec((tm, w), lambda i: (i, 0)),
        out_shape=jax.ShapeDtypeStruct((batch * seq, w), xc.dtype),
        compiler_params=_compiler_params(("parallel",)),
        name="from_class_order",
    )(perm_t, xc)


def _compress_kernel(x_ref, pe_ref, w1_ref, w2_ref, o_ref):
    n_planes, n_rows, _ = x_ref.shape
    ylo = yhi = None
    for l in range(n_planes):
        x = x_ref[l].astype(F32)
        xlo = (x + pe_ref[l:l + 1, :]).astype(BF16)
        xhi = (x + pe_ref[n_planes + l:n_planes + l + 1, :]).astype(BF16)
        dlo = _dot(xlo, w1_ref[l * HEAD_DIM:(l + 1) * HEAD_DIM, :])
        dhi = _dot(xhi, w1_ref[(n_planes + l) * HEAD_DIM:(n_planes + l + 1) * HEAD_DIM, :])
        ylo = dlo if ylo is None else ylo + dlo
        yhi = dhi if yhi is None else yhi + dhi
    hid = ylo + pltpu.roll(yhi, n_rows - 1, 0)
    act = (hid * jax.nn.sigmoid(hid)).astype(BF16)
    o_ref[...] = _dot(act, w2_ref[...]).astype(o_ref.dtype)


def _compress(xc, pe, w1_bf, w2_bf):
    batch, planes, nrow, gd = xc.shape
    ng = gd // HEAD_DIM
    return pl.pallas_call(
        _compress_kernel,
        grid=(batch, ng),
        in_specs=[
            pl.BlockSpec((None, planes, nrow, HEAD_DIM), lambda b, g: (b, 0, 0, g)),
            pl.BlockSpec((CMP_LEN, HEAD_DIM), lambda b, g: (0, 0)),
            pl.BlockSpec((CMP_LEN * HEAD_DIM, CMP_HIDDEN), lambda b, g: (0, 0)),
            pl.BlockSpec((CMP_HIDDEN, HEAD_DIM), lambda b, g: (0, 0)),
        ],
        out_specs=pl.BlockSpec((None, None, nrow, HEAD_DIM), lambda b, g: (b, g, 0, 0)),
        out_shape=jax.ShapeDtypeStruct((batch, ng, nrow, HEAD_DIM), BF16),
        compiler_params=_compiler_params(("parallel", "arbitrary")),
        name="nsa_compress",
    )(xc, pe, w1_bf, w2_bf)


def _block_ranks(score, jrow):
    n_blk = score.shape[0]
    groups = n_blk // SUBLANES
    blocks = [score[SUBLANES * r:SUBLANES * (r + 1), :] for r in range(groups)]
    rows = [jrow[SUBLANES * r:SUBLANES * (r + 1), :] for r in range(groups)]
    ranks = [jnp.zeros(blocks[0].shape, F32) for _ in range(groups)]
    for j in range(n_blk):
        rj = score[j:j + 1, :]
        for r in range(groups):
            if r > j // SUBLANES:
                ahead = rj >= blocks[r]
            elif r < j // SUBLANES:
                ahead = rj > blocks[r]
            else:
                ahead = (rj > blocks[r]) | ((rj == blocks[r]) & (rows[r] > j))
            ranks[r] = ranks[r] + jnp.where(ahead, 1.0, 0.0)
    return jnp.concatenate(ranks, axis=0)


def _nsa_kernel(q_ref, kc_ref, vc_ref, ks_ref, e_ref, vs_ref, kw_ref, vw_ref, gz_ref, gb_ref,
                o_ref, qa_ref, sa_ref, sb_ref, sw_ref, ow_ref, m_ref, l_ref, acc_ref, *, tq, seq):
    hq = HPG_A
    groups = range(N_KV_A)
    n_cmp_rows = kc_ref.shape[1]
    n_slc = seq // SLC_BLK
    qi = pl.program_id(1)
    s0 = qi * tq
    t_row = s0 + lax.broadcasted_iota(jnp.int32, (1, tq), 1)

    def lanes(h):
        return slice(h * tq, (h + 1) * tq)

    def gcols(g):
        return slice(g * HEAD_DIM, (g + 1) * HEAD_DIM)

    def tile_heads(x):
        return jnp.concatenate([x] * hq, axis=1)

    wlen = WIN_A + tq
    ws = pl.multiple_of(jnp.maximum(s0 - WIN_A, 0), LANES)
    c_end = lax.broadcasted_iota(jnp.int32, (n_cmp_rows, 1), 0) * CMP_STRIDE + (CMP_LEN - 1)
    cbias = tile_heads(jnp.where(c_end <= t_row, 0.0, NEG_INF))
    any_cmp = tile_heads(t_row >= CMP_LEN - 1)
    jrow = lax.broadcasted_iota(jnp.int32, (n_slc, 1), 0)
    ccol = lax.broadcasted_iota(jnp.int32, (1, n_cmp_rows), 1)
    lo = (SLC_BLK // CMP_STRIDE) * jrow - (CMP_LEN // CMP_STRIDE - 1)
    hi = (SLC_BLK // CMP_STRIDE) * jrow + (SLC_BLK // CMP_STRIDE - 1)
    mmap = jnp.where((ccol >= lo) & (ccol <= hi), 1.0, 0.0).astype(BF16)
    cur = t_row >> SLC_SHIFT
    forced = (jrow == 0) | (jrow == cur) | (jrow == cur - 1)
    assert n_slc <= LANES and tq == LANES

    q6, sc = [], []
    for g in groups:
        for h in range(hq):
            head = g * hq + h
            qa_ref[g, lanes(h), 0:HEAD_DIM] = q_ref[:, head * HEAD_DIM:(head + 1) * HEAD_DIM]
        q6.append(qa_ref[g, :, 0:HEAD_DIM])
        sc.append(_dot_nt(kc_ref[g], q6[g]) + cbias)
    for g in groups:
        sw_ref[g] = _dot_nt(kw_ref[pl.ds(ws, wlen), gcols(g)], q6[g])

    o_cmp, score = [], []
    for g in groups:
        ec = jnp.exp2(sc[g] - jnp.max(sc[g], axis=0, keepdims=True))
        den = jnp.maximum(jnp.sum(ec, axis=0, keepdims=True), TINY)
        pc = ec * jnp.where(any_cmp, 1.0 / den, 0.0)
        o_cmp.append(_dot_tn(vc_ref[g], pc.astype(BF16)))
        psum = pc[:, lanes(0)]
        for h in range(1, hq):
            psum = psum + pc[:, lanes(h)]
        p1 = psum.astype(BF16)
        r1 = psum - p1.astype(F32)
        p2 = r1.astype(BF16)
        p3 = (r1 - p2.astype(F32)).astype(BF16)
        imp = _dot(mmap, p1) + _dot(mmap, p2) + _dot(mmap, p3)
        score.append(jnp.where(forced, 1e9, jnp.where(jrow <= cur, imp, -1e9)))

    for g in groups:
        rank = _block_ranks(score[g], jrow)
        sel_bias = jnp.where((rank < min(N_SEL, n_slc)) & (jrow <= cur), 0.0, NEG_INF)
        bias_q = jnp.transpose(jnp.concatenate(
            [sel_bias, jnp.zeros((LANES - n_slc, tq), F32)], axis=0)).astype(BF16)
        for h in range(hq):
            qa_ref[g, lanes(h), HEAD_DIM:HEAD_DIM + LANES] = bias_q

    m_ref[...] = jnp.full(m_ref.shape, NEG_INF, F32)
    l_ref[...] = jnp.zeros(l_ref.shape, F32)
    acc_ref[...] = jnp.zeros(acc_ref.shape, F32)

    def scores(g, c, s_ref):
        k0 = pl.multiple_of(c * NSA_KC, NSA_KC)
        k_aug = jnp.concatenate([ks_ref[pl.ds(k0, NSA_KC), gcols(g)],
                                 e_ref[pl.ds(k0, NSA_KC), :]], axis=1)
        s_ref[g] = _dot_nt(k_aug, qa_ref[g])

    def softmax_pv(g, c, s_ref):
        s = s_ref[g]
        m_old = m_ref[g]
        m_new = jnp.maximum(m_old, jnp.max(s, axis=0, keepdims=True))
        alpha = jnp.exp2(m_old - m_new)
        p = jnp.exp2(s - m_new)
        l_ref[g] = alpha * l_ref[g] + jnp.sum(p, axis=0, keepdims=True)
        k0 = pl.multiple_of(c * NSA_KC, NSA_KC)
        pv = _dot_tn(vs_ref[pl.ds(k0, NSA_KC), gcols(g)], p.astype(BF16))
        acc_ref[g] = alpha * acc_ref[g] + pv
        m_ref[g] = m_new

    last = s0 // NSA_KC
    for g in groups:
        scores(g, 0, sa_ref)

    dist = t_row - (ws + lax.broadcasted_iota(jnp.int32, (wlen, 1), 0))
    wbias = tile_heads(jnp.where((dist >= 0) & (dist < WIN_A), 0.0, NEG_INF))
    for g in groups:
        sw = sw_ref[g] + wbias
        ew = jnp.exp2(sw - jnp.max(sw, axis=0, keepdims=True))
        ow_ref[g] = (_dot_tn(vw_ref[pl.ds(ws, wlen), gcols(g)], ew.astype(BF16))
                     * (1.0 / jnp.sum(ew, axis=0, keepdims=True)))

    def chunk_pair(i, carry):
        for g in groups:
            scores(g, 2 * i + 1, sb_ref)
            softmax_pv(g, 2 * i, sa_ref)
        for g in groups:
            scores(g, 2 * i + 2, sa_ref)
            softmax_pv(g, 2 * i + 1, sb_ref)
        return carry

    lax.fori_loop(0, last // 2, chunk_pair, 0)

    def last_chunk(g, s_ref):
        diag = pl.multiple_of(s0 - last * NSA_KC, LANES)
        krow = lax.broadcasted_iota(jnp.int32, (tq, 1), 0)
        lane = lax.broadcasted_iota(jnp.int32, (1, tq), 1)
        causal = tile_heads(jnp.where(krow <= lane, 0.0, NEG_INF))
        s_ref[g, pl.ds(diag, tq), :] = s_ref[g, pl.ds(diag, tq), :] + causal
        softmax_pv(g, last, s_ref)

    @pl.when(last % 2 == 0)
    def _():
        for g in groups:
            last_chunk(g, sa_ref)

    @pl.when(last % 2 == 1)
    def _():
        for g in groups:
            scores(g, last, sb_ref)
            softmax_pv(g, last - 1, sa_ref)
        for g in groups:
            last_chunk(g, sb_ref)

    gates_t = jnp.transpose(jax.nn.sigmoid(gz_ref[...].astype(F32) + gb_ref[...]))
    for g in groups:
        o_slc = acc_ref[g] * (1.0 / l_ref[g])
        for h in range(hq):
            head = g * hq + h
            o_h = (gates_t[3 * head:3 * head + 1, :] * o_cmp[g][:, lanes(h)]
                   + gates_t[3 * head + 1:3 * head + 2, :] * o_slc[:, lanes(h)]
                   + gates_t[3 * head + 2:3 * head + 3, :] * ow_ref[g, :, lanes(h)])
            o_ref[:, head * HEAD_DIM:(head + 1) * HEAD_DIM] = jnp.transpose(o_h).astype(o_ref.dtype)


def _nsa_attention(z, kc, vc, gate_bias, *, batch, seq, units):
    tq = NSA_TQ
    nq = seq // tq
    n_cmp_rows = kc.shape[2]
    ng = N_KV_A
    hl = HPG_A * tq
    gw = ng * HEAD_DIM
    for name in ("k_slc", "v_slc", "k_win", "v_win"):
        assert units[name] % ng == 0
    kern = functools.partial(_nsa_kernel, tq=tq, seq=seq)

    def slab(name):
        return pl.BlockSpec((seq, gw), lambda b, i: (b, units[name] // ng))

    cmp_spec = pl.BlockSpec((None, ng, n_cmp_rows, HEAD_DIM), lambda b, i: (b, 0, 0, 0))

    onehot = jnp.asarray(np.arange(seq)[:, None] // SLC_BLK == np.arange(LANES)[None, :], BF16)
    return pl.pallas_call(
        kern,
        grid=(batch, nq),
        in_specs=[
            pl.BlockSpec((tq, A_Q), lambda b, i: (b * nq + i, 0)),
            cmp_spec,
            cmp_spec,
            slab("k_slc"),
            pl.BlockSpec((seq, LANES), lambda b, i: (0, 0)),
            slab("v_slc"),
            slab("k_win"),
            slab("v_win"),
            pl.BlockSpec((tq, LANES), lambda b, i: (b * nq + i, units["gates"])),
            pl.BlockSpec((1, LANES), lambda b, i: (0, 0)),
        ],
        out_specs=pl.BlockSpec((tq, A_Q), lambda b, i: (b * nq + i, 0)),
        out_shape=jax.ShapeDtypeStruct((batch * seq, A_Q), BF16),
        scratch_shapes=[
            pltpu.VMEM((ng, hl, HEAD_DIM + LANES), BF16),
            pltpu.VMEM((ng, NSA_KC, hl), F32),
            pltpu.VMEM((ng, NSA_KC, hl), F32),
            pltpu.VMEM((ng, WIN_A + tq, hl), F32),
            pltpu.VMEM((ng, HEAD_DIM, hl), F32),
            pltpu.VMEM((ng, 1, hl), F32),
            pltpu.VMEM((ng, 1, hl), F32),
            pltpu.VMEM((ng, HEAD_DIM, hl), F32),
        ],
        compiler_params=_compiler_params(("parallel", "arbitrary")),
        name="nsa_attention",
    )(z, kc, vc, z, onehot, z, z, z, z, gate_bias)


def _band_attn_kernel(*refs, tu, parts):
    q_refs = refs[:len(parts)]
    k_ref, v_ref, o_ref, lse_ref = refs[len(parts):]
    u0 = pl.program_id(2) * tu
    n_seq = k_ref.shape[0]
    windows = []
    for lk, span, stride in parts:
        ks = pl.multiple_of(jnp.clip(u0 - span, 0, n_seq - lk), LANES)
        dist = (u0 + lax.broadcasted_iota(jnp.int32, (tu, 1), 0)
                - (ks + lax.broadcasted_iota(jnp.int32, (1, lk), 1)))
        keep = (dist >= 0) & (dist <= span)
        if stride > 1:
            keep = keep & ((dist & (stride - 1)) == 0)
        windows.append((ks, lk, jnp.where(keep, 0.0, NEG_INF)))
    lane = lax.broadcasted_iota(jnp.int32, (1, LANES), 1)
    lse_tile = jnp.zeros((tu, LANES), F32)

    def head_cols(h):
        return slice(h * HEAD_DIM, (h + 1) * HEAD_DIM)

    scores = [[_dot_nt(q_ref[:, head_cols(h)], k_ref[pl.ds(ks, lk), head_cols(h)]) + bias
               for q_ref, (ks, lk, bias) in zip(q_refs, windows)]
              for h in range(DIL_HEADS)]
    probs, inv_l = [], []
    for h, s_parts in enumerate(scores):
        s = jnp.concatenate(s_parts, axis=1)
        m = jnp.max(s, axis=-1, keepdims=True)
        p = jnp.exp2(s - m)
        l = jnp.sum(p, axis=-1, keepdims=True)
        probs.append(p.astype(BF16))
        inv_l.append(1.0 / l)
        lse_tile = jnp.where(lane == h, m + jnp.log2(l), lse_tile)
    for h in range(DIL_HEADS):
        o, col = None, 0
        for ks, lk, _ in windows:
            pv = _dot(probs[h][:, col:col + lk], v_ref[pl.ds(ks, lk), head_cols(h)])
            o = pv if o is None else o + pv
            col += lk
        o_ref[:, head_cols(h)] = (o * inv_l[h]).astype(o_ref.dtype)
    lse_ref[...] = lse_tile


def _merge_dilation_calls(refs):
    calls = list(zip(refs[0::2], refs[1::2]))
    merged = []
    for h in range(DIL_HEADS):
        cols = slice(h * HEAD_DIM, (h + 1) * HEAD_DIM)
        shape = (calls[0][0].shape[0], HEAD_DIM)
        lses = [jnp.broadcast_to(l_ref[:, h:h + 1], shape) for _, l_ref in calls]
        top = functools.reduce(jnp.maximum, lses)
        ws = [jnp.exp2(lse - top) for lse in lses]
        num = sum(w * o_ref[:, cols].astype(F32) for w, (o_ref, _) in zip(ws, calls))
        merged.append((num * (1.0 / sum(ws))).astype(BF16))
    return jnp.concatenate(merged, axis=1)


def _dilated_attention(zb, *, batch, seq, units):
    n = zb.shape[1]
    width = DIL_HEADS * HEAD_DIM
    tu = DIL_TQ

    def band_call(r, q_units, parts, operands, k_unit, v_unit):
        n_seq = seq // r

        def z_spec(rows, unit, whole):
            return pl.BlockSpec((None, None, rows, width),
                                lambda b, c, i: (b, c, 0 if whole else i, unit))

        def out_spec(cols):
            return pl.BlockSpec((None, None, tu, cols), lambda b, c, i: (b, c, i, 0))

        return pl.pallas_call(
            functools.partial(_band_attn_kernel, tu=tu, parts=parts),
            grid=(batch, r, n_seq // tu),
            in_specs=[z_spec(tu, u, False) for u in q_units]
            + [z_spec(n_seq, k_unit, True), z_spec(n_seq, v_unit, True)],
            out_specs=[out_spec(width), out_spec(LANES)],
            out_shape=[jax.ShapeDtypeStruct((batch, r, n_seq, width), BF16),
                       jax.ShapeDtypeStruct((batch, r, n_seq, LANES), F32)],
            compiler_params=_compiler_params(("parallel", "parallel", "arbitrary")),
            name="dilated_attention",
        )(*operands)

    def window(n_seq, span):
        assert n_seq % tu == 0 and span % LANES == 0 and tu % LANES == 0
        return min(tu + span, n_seq)

    dense = [(gi, w, r) for gi, (w, r) in enumerate(DIL_CONFIGS) if r <= DIL_DENSE_MAX]
    zv = zb.reshape(batch, 1, seq, n)
    o_d, lse_d = band_call(1, [units["q"] + gi for gi, _, _ in dense],
                           tuple((window(seq, w), w, r) for _, w, r in dense),
                           [zv] * (len(dense) + 2), units["k"], units["v"])
    results = [o_d.reshape(batch * seq, width), lse_d.reshape(batch * seq, LANES)]
    for gi, (w, r) in enumerate(DIL_CONFIGS):
        if r <= DIL_DENSE_MAX:
            continue
        slabs = [(units["q"] + gi, width), (units["k"], width), (units["v"], width)]
        o_c, lse_c = band_call(r, [0], ((window(seq // r, w // r), w // r, 1),),
                               _to_class_order(zb, slabs, r, batch=batch, seq=seq), 0, 0)
        results += [_from_class_order(o_c, batch=batch, seq=seq),
                    jnp.transpose(lse_c, (0, 2, 1, 3)).reshape(batch * seq, LANES)]

    return results


def _memory_attention(q_ref, kv_ref):
    def cols(h, base=0):
        return slice(base + h * HEAD_DIM, base + (h + 1) * HEAD_DIM)

    heads = range(N_MEM_HEADS)
    scores = [_dot_nt(q_ref[:, cols(h)], kv_ref[:, cols(h)]) for h in heads]
    probs = []
    for s in scores:
        e = jnp.exp(s - jnp.max(s, axis=-1, keepdims=True))
        probs.append((e / jnp.sum(e, axis=-1, keepdims=True)).astype(BF16))
    return jnp.concatenate([_dot(probs[h], kv_ref[:, cols(h, MEM_Q)]).astype(BF16)
                            for h in heads], axis=1)


def _out_proj_kernel(*refs):
    a1_refs = refs[:-7]
    qm_ref, mkv_ref, w_ref, h_ref, g_ref, o_ref, xn_ref = refs[-7:]
    a1 = a1_refs[0][...] if len(a1_refs) == 1 else _merge_dilation_calls(a1_refs)
    a = jnp.concatenate([a1, _memory_attention(qm_ref, mkv_ref)], axis=1)
    y = h_ref[...] + _dot(a, w_ref[...])
    o_ref[...] = y
    xn_ref[...] = _rms_rows(y, g_ref[...]).astype(xn_ref.dtype)


def _out_proj(a1, z, q_unit, mkv, w_bf, h, g_next, *, batch):
    m, d = h.shape
    tm = min(OUT_TM, m)
    a1_list = list(a1) if isinstance(a1, (list, tuple)) else [a1]
    k1 = a1_list[0].shape[1]
    assert w_bf.shape[0] == k1 + MEM_Q and (m // batch) % tm == 0
    blocks_per_batch = m // batch // tm
    n_mem = mkv.shape[0] // batch
    row = pl.BlockSpec((tm, d), lambda i: (i, 0))
    return pl.pallas_call(
        _out_proj_kernel,
        grid=(m // tm,),
        in_specs=[pl.BlockSpec((tm, x.shape[1]), lambda i: (i, 0)) for x in a1_list] + [
            pl.BlockSpec((tm, MEM_Q), lambda i: (i, q_unit)),
            pl.BlockSpec((n_mem, 2 * MEM_Q), lambda i: (i // blocks_per_batch, 0)),
            pl.BlockSpec((k1 + MEM_Q, d), lambda i: (0, 0), pipeline_mode=pl.Buffered(1)),
            row,
            pl.BlockSpec((1, d), lambda i: (0, 0)),
        ],
        out_specs=[row, row],
        out_shape=[jax.ShapeDtypeStruct((m, d), F32), jax.ShapeDtypeStruct((m, d), BF16)],
        compiler_params=_compiler_params(("parallel",)),
        name="out_proj",
    )(*a1_list, z, mkv, w_bf, h, g_next.reshape(1, d))


def _ffn_up_kernel(xn_ref, w_ref, o_ref):
    acc = _dot(xn_ref[...], w_ref[...])
    for c in range(o_ref.shape[1] // FFN_GU_CHUNK):
        gate = acc[:, 2 * c * FFN_GU_CHUNK:(2 * c + 1) * FFN_GU_CHUNK]
        up = acc[:, (2 * c + 1) * FFN_GU_CHUNK:(2 * c + 2) * FFN_GU_CHUNK]
        o_ref[:, c * FFN_GU_CHUNK:(c + 1) * FFN_GU_CHUNK] = (
            gate * jax.nn.sigmoid(gate) * up).astype(o_ref.dtype)


def _ffn_down_kernel(a_ref, w_ref, h_ref, g_ref, o_ref, *xn_ref, final_norm):
    y = h_ref[...] + _dot(a_ref[...], w_ref[...])
    yn = _rms_rows(y, g_ref[...])
    if final_norm:
        o_ref[...] = yn
    else:
        o_ref[...] = y
        xn_ref[0][...] = yn.astype(xn_ref[0].dtype)


def _ffn(h, xn, w_gate, w_up, wd_bf, gain, *, final_norm):
    m, d = h.shape
    dff = w_gate.shape[1]
    tm, tf = min(FFN_UP_TM, m), FFN_TF
    assert m % tm == 0 and dff % tf == 0 and tf % FFN_GU_CHUNK == 0
    n_chunks = dff // FFN_GU_CHUNK
    w_gu = jnp.stack([w_gate.reshape(d, n_chunks, FFN_GU_CHUNK),
                      w_up.reshape(d, n_chunks, FFN_GU_CHUNK)], axis=2).reshape(d, 2 * dff)
    act = pl.pallas_call(
        _ffn_up_kernel,
        grid=(m // tm, dff // tf),
        in_specs=[
            pl.BlockSpec((tm, d), lambda i, f: (i, 0)),
            pl.BlockSpec((d, 2 * tf), lambda i, f: (0, f)),
        ],
        out_specs=pl.BlockSpec((tm, tf), lambda i, f: (i, f)),
        out_shape=jax.ShapeDtypeStruct((m, dff), BF16),
        compiler_params=_compiler_params(("parallel", "arbitrary")),
        name="ffn_up",
    )(xn, w_gu.astype(BF16))

    tm = min(FFN_TM, m)
    kern = functools.partial(_ffn_down_kernel, final_norm=final_norm)
    row = pl.BlockSpec((tm, d), lambda i: (i, 0))
    h_shape = jax.ShapeDtypeStruct((m, d), F32)
    return pl.pallas_call(
        kern,
        grid=(m // tm,),
        in_specs=[
            pl.BlockSpec((tm, dff), lambda i: (i, 0)),
            pl.BlockSpec((dff, d), lambda i: (0, 0), pipeline_mode=pl.Buffered(1)),
            row,
            pl.BlockSpec((1, d), lambda i: (0, 0)),
        ],
        out_specs=row if final_norm else [row, row],
        out_shape=h_shape if final_norm else [h_shape, jax.ShapeDtypeStruct((m, d), BF16)],
        compiler_params=_compiler_params(("parallel",)),
        name="ffn_down",
    )(act, wd_bf, h, gain.reshape(1, d))


def _rope_tables(seq):
    inv = 1.0 / (ROPE_THETA ** (jnp.arange(0, HEAD_DIM, 2, dtype=F32) / HEAD_DIM))
    ang = jnp.arange(seq, dtype=F32)[:, None] * inv[None, :]
    cos, sin = jnp.cos(ang), jnp.sin(ang)
    return jnp.concatenate([cos, cos], axis=1), jnp.concatenate([-sin, sin], axis=1)


A_UNITS = {"q": 0, "k_cmp": 12, "k_slc": 14, "k_win": 16, "v_cmp": 18, "v_slc": 20,
           "v_win": 22, "mem_q": 24, "gates": 28}
B_UNITS = {"q": 0, "k": 3, "mem_q": 4, "v": 5}
B_TN = 2 * DIL_HEADS * HEAD_DIM
A_TN = 6 * HEAD_DIM
A_NPAD = 30 * HEAD_DIM
A_ROPE_BLOCKS = 3


def _layer_a_weight(w_in):
    wt = jnp.swapaxes(w_in, 0, 1)
    kv0 = A_Q

    def kv_rows(branch):
        return wt[kv0 + branch * N_KV_A * HEAD_DIM:kv0 + (branch + 1) * N_KV_A * HEAD_DIM]

    gate0 = A_Q + A_KV
    mem0 = gate0 + A_GATE
    w = jnp.concatenate([wt[:A_Q], kv_rows(0), kv_rows(2), kv_rows(4), kv_rows(1),
                         kv_rows(3), kv_rows(5), wt[mem0:mem0 + MEM_Q], wt[gate0:mem0]], axis=0)
    w = jnp.pad(w, ((0, A_NPAD - w.shape[0]), (0, 0)))
    scale = np.ones((1, A_NPAD), np.float32)
    scale[0, :A_Q] = SCALE * LOG2E
    scale[0, A_UNITS["mem_q"] * HEAD_DIM:A_UNITS["mem_q"] * HEAD_DIM + MEM_Q] = SCALE
    return w.astype(BF16), jnp.asarray(scale)


def _layer_a(h, hn, mem, cosf, sinf, p, *, batch, seq):
    w_in_bf, col_scale = _layer_a_weight(p["w_in"])
    z = _norm_proj(h if hn is None else hn, p["norm_attn"], w_in_bf, col_scale, cosf, sinf,
                   tn=A_TN, n_rope_blocks=A_ROPE_BLOCKS, seq=seq, w_rows=True)

    gw = N_KV_A * HEAD_DIM
    k_raw, v_raw = _to_class_order(
        z, [(A_UNITS["k_cmp"] * HEAD_DIM // gw, gw), (A_UNITS["v_cmp"] * HEAD_DIM // gw, gw)],
        CMP_STRIDE, batch=batch, seq=seq)
    kc = _compress(k_raw, p["cmp_pe_k"], p["cmp_w1_k"].astype(BF16), p["cmp_w2_k"].astype(BF16))
    vc = _compress(v_raw, p["cmp_pe_v"], p["cmp_w1_v"].astype(BF16), p["cmp_w2_v"].astype(BF16))
    gb = jnp.pad(p["gate_bias"], (0, LANES - A_GATE)).reshape(1, LANES)
    o_nsa = _nsa_attention(z, kc, vc, gb, batch=batch, seq=seq, units=A_UNITS)

    mkv = _mem_kv(mem, p["norm_mem"], p["w_mem_kv"])
    return _out_proj(o_nsa, z, A_UNITS["mem_q"] * HEAD_DIM // MEM_Q, mkv,
                     p["w_out"].astype(BF16), h, p["norm_ffn"], batch=batch)


def _mem_kv(mem, norm_mem, w_mem_kv):
    b, m, d = mem.shape
    ones = jnp.ones((1, w_mem_kv.shape[1]), F32)
    dummy = jnp.zeros((m, HEAD_DIM), F32)
    return _norm_proj(mem.reshape(b * m, d), norm_mem, w_mem_kv.astype(BF16), ones, dummy, dummy,
                      tn=MEM_Q, n_rope_blocks=0, seq=m, tm=m)


def kernel(x, mem, a_norm_attn, a_w_in, a_gate_bias, a_cmp_pe_k, a_cmp_w1_k, a_cmp_w2_k, a_cmp_pe_v, a_cmp_w1_v, a_cmp_w2_v, a_norm_mem, a_w_mem_kv, a_w_out, a_norm_ffn, a_w_gate, a_w_up, a_w_down, kv_norm, w_kv_shared, b_norm_attn, b_w_in, b_norm_mem, b_w_mem_kv, b_w_out, b_norm_ffn, b_w_gate, b_w_up, b_w_down, final_norm):
    batch, seq, d = x.shape
    n_a = a_w_in.shape[0]
    n_b = b_w_in.shape[0]
    cosf, sinf = _rope_tables(seq)
    h = x.reshape(batch * seq, d)
    unit_gain = jnp.ones((d,), F32)
    hn = None

    for l in range(n_a):
        p = {"norm_attn": a_norm_attn[l], "w_in": a_w_in[l], "gate_bias": a_gate_bias[l],
             "cmp_pe_k": a_cmp_pe_k[l], "cmp_w1_k": a_cmp_w1_k[l], "cmp_w2_k": a_cmp_w2_k[l],
             "cmp_pe_v": a_cmp_pe_v[l], "cmp_w1_v": a_cmp_w1_v[l], "cmp_w2_v": a_cmp_w2_v[l],
             "norm_mem": a_norm_mem[l], "w_mem_kv": a_w_mem_kv[l], "w_out": a_w_out[l],
             "norm_ffn": a_norm_ffn[l]}
        h, xn = _layer_a(h, hn, mem, cosf, sinf, p, batch=batch, seq=seq)
        last = (l == n_a - 1) and n_b == 0
        ffn_w = (a_w_gate[l], a_w_up[l], a_w_down[l].astype(BF16))
        if last:
            h = _ffn(h, xn, *ffn_w, final_norm, final_norm=True)
        else:
            next_gain = a_norm_attn[l + 1] if l + 1 < n_a else unit_gain
            h, hn = _ffn(h, xn, *ffn_w, next_gain, final_norm=False)

    if n_b > 0:
        assert n_b == 1, "the shared K/V projection is fused into the single mixer-B layer"
        n_kv_half = w_kv_shared.shape[1] // 2
        for l in range(n_b):
            w_q = b_norm_attn[l][:, None] * b_w_in[l]
            w_kv = kv_norm[:, None] * w_kv_shared
            w_cat = jnp.concatenate([w_q[:, :B_Q], w_kv[:, :n_kv_half], w_q[:, B_Q:],
                                     w_kv[:, n_kv_half:]], axis=1).astype(BF16)
            b_scale = np.ones((1, w_cat.shape[1]), np.float32)
            b_scale[0, :B_Q] = SCALE * LOG2E
            b_scale[0, B_Q + n_kv_half:B_Q + n_kv_half + MEM_Q] = SCALE
            zb = _norm_proj(h if hn is None else hn, unit_gain, w_cat, jnp.asarray(b_scale),
                            cosf, sinf, tm=PROJ_TM if hn is None else PROJ_TM_BF16,
                            tn=B_TN, n_rope_blocks=(B_Q + n_kv_half) // B_TN, seq=seq)
            o_dil = _dilated_attention(zb, batch=batch, seq=seq, units=B_UNITS)
            mkv = _mem_kv(mem, b_norm_mem[l], b_w_mem_kv[l])
            h, xn = _out_proj(o_dil, zb, B_UNITS["mem_q"], mkv, b_w_out[l].astype(BF16), h,
                              b_norm_ffn[l], batch=batch)
            h = _ffn(h, xn, b_w_gate[l], b_w_up[l], b_w_down[l].astype(BF16), final_norm,
                     final_norm=True)

    return h.reshape(batch, seq, d)
```

```python
import functools
import math

import numpy as np
import jax
import jax.numpy as jnp
from jax import lax
from jax.experimental import pallas as pl
from jax.experimental.pallas import tpu as pltpu

F32 = jnp.float32
BF16 = jnp.bfloat16

HEAD_DIM = 128
N_HEADS_A = 12
N_KV_A = 2
HPG_A = N_HEADS_A // N_KV_A
CMP_LEN = 32
CMP_STRIDE = 16
CMP_HIDDEN = 256
SLC_BLK = 64
SLC_SHIFT = SLC_BLK.bit_length() - 1
N_SEL = 16
WIN_A = 512
DIL_CONFIGS = ((128, 1), (512, 4), (2048, 16))
N_DIL_GROUPS = len(DIL_CONFIGS)
DIL_HEADS = 4
N_MEM_HEADS = 4
ROPE_THETA = 10000.0
EPS = 1e-6
NEG_INF = -1e30
TINY = 1e-30
SCALE = HEAD_DIM ** -0.5
LOG2E = math.log2(math.e)

A_Q = N_HEADS_A * HEAD_DIM
A_KV = 6 * N_KV_A * HEAD_DIM
A_GATE = 3 * N_HEADS_A
MEM_Q = N_MEM_HEADS * HEAD_DIM
B_Q = N_DIL_GROUPS * DIL_HEADS * HEAD_DIM

LANES = 128
SUBLANES = 8
VMEM_LIMIT_BYTES = 56 * 1024 * 1024

PROJ_TM = 1024
PROJ_TM_BF16 = 2048
FFN_UP_TM = 2048
FFN_TM = 512
FFN_TF = 512
FFN_GU_CHUNK = 256
OUT_TM = 512
NSA_TQ = 128
NSA_KC = 512
DIL_TQ = 256
DIL_DENSE_MAX = 4

NT_DIMS = (((1,), (1,)), ((), ()))
TN_DIMS = (((0,), (0,)), ((), ()))


def _compiler_params(semantics):
    return pltpu.CompilerParams(dimension_semantics=semantics,
                                vmem_limit_bytes=VMEM_LIMIT_BYTES)


def _rms_rows(x, g):
    ms = jnp.mean(x * x, axis=-1, keepdims=True)
    return x * lax.rsqrt(ms + EPS) * g


def _dot(a, b):
    return jnp.dot(a, b, preferred_element_type=F32)


def _dot_nt(a, b):
    return lax.dot_general(a, b, NT_DIMS, preferred_element_type=F32)


def _dot_tn(a, b):
    return lax.dot_general(a, b, TN_DIMS, preferred_element_type=F32)


def _norm_proj_kernel(x_ref, g_ref, w_ref, cs_ref, cos_ref, sin_ref, o_ref, *xn_ref,
                      n_rope_blocks, tn, w_rows):
    j = pl.program_id(1)
    if xn_ref:
        @pl.when(j == 0)
        def _():
            xn_ref[0][...] = _rms_rows(x_ref[...], g_ref[...]).astype(BF16)

        xn = xn_ref[0][...]
    else:
        xn = x_ref[...]
    w = w_ref[...]
    acc = (_dot_nt(xn, w) if w_rows else _dot(xn, w)) * cs_ref[...]

    if n_rope_blocks > 0:
        roped = j < n_rope_blocks
        c = jnp.where(roped, cos_ref[...], 1.0)
        s = jnp.where(roped, sin_ref[...], 0.0)
        for h in range(tn // HEAD_DIM):
            y = acc[:, h * HEAD_DIM:(h + 1) * HEAD_DIM]
            rot = pltpu.roll(y, HEAD_DIM // 2, 1)
            o_ref[:, h * HEAD_DIM:(h + 1) * HEAD_DIM] = (y * c + rot * s).astype(o_ref.dtype)
    else:
        o_ref[...] = acc.astype(o_ref.dtype)


def _norm_proj(x, g, w_bf, col_scale, cosf, sinf, *, tn, n_rope_blocks, seq, tm=PROJ_TM,
               w_rows=False):
    m, d = x.shape
    n = w_bf.shape[0] if w_rows else w_bf.shape[1]
    tm = min(tm, m)
    assert m % tm == 0 and n % tn == 0 and seq % tm == 0
    pos_blocks = seq // tm
    scratch = [] if x.dtype == BF16 else [pltpu.VMEM((tm, d), BF16)]
    kern = functools.partial(_norm_proj_kernel, n_rope_blocks=n_rope_blocks, tn=tn,
                             w_rows=w_rows)
    return pl.pallas_call(
        kern,
        grid=(m // tm, n // tn),
        in_specs=[
            pl.BlockSpec((tm, d), lambda i, j: (i, 0)),
            pl.BlockSpec((1, d), lambda i, j: (0, 0)),
            (pl.BlockSpec((tn, d), lambda i, j: (j, 0)) if w_rows
             else pl.BlockSpec((d, tn), lambda i, j: (0, j))),
            pl.BlockSpec((1, tn), lambda i, j: (0, j)),
            pl.BlockSpec((tm, HEAD_DIM), lambda i, j: (i % pos_blocks, 0)),
            pl.BlockSpec((tm, HEAD_DIM), lambda i, j: (i % pos_blocks, 0)),
        ],
        out_specs=pl.BlockSpec((tm, tn), lambda i, j: (i, j)),
        out_shape=jax.ShapeDtypeStruct((m, n), BF16),
        scratch_shapes=scratch,
        compiler_params=_compiler_params(("parallel", "arbitrary")),
        name="norm_proj",
    )(x, g.reshape(1, d), w_bf, col_scale, cosf, sinf)


def _class_perm(tm, r):
    dst = np.arange(tm)
    c, u = dst // (tm // r), dst % (tm // r)
    perm = np.zeros((tm, tm), np.float32)
    perm[dst, u * r + c] = 1.0
    return perm


def _to_class_kernel(p_ref, *refs, r):
    n = len(refs) // 2
    for x_ref, o_ref in zip(refs[:n], refs[n:]):
        y = _dot(p_ref[...], x_ref[...]).astype(o_ref.dtype)
        rows = y.shape[0] // r
        for c in range(r):
            o_ref[c] = y[c * rows:(c + 1) * rows, :]


def _to_class_order(x, slabs, r, *, batch, seq):
    tm = min(PROJ_TM, seq)
    nblk = seq // tm
    perm = jnp.asarray(_class_perm(tm, r), x.dtype)
    return pl.pallas_call(
        functools.partial(_to_class_kernel, r=r),
        grid=(batch * nblk,),
        in_specs=[pl.BlockSpec((tm, tm), lambda i: (0, 0))]
        + [pl.BlockSpec((tm, w), lambda i, cb=cb: (i, cb)) for cb, w in slabs],
        out_specs=[pl.BlockSpec((None, r, tm // r, w), lambda i: (i // nblk, 0, i % nblk, 0))
                   for _, w in slabs],
        out_shape=[jax.ShapeDtypeStruct((batch, r, seq // r, w), x.dtype) for _, w in slabs],
        compiler_params=_compiler_params(("parallel",)),
        name="to_class_order",
    )(perm, *([x] * len(slabs)))


def _from_class_kernel(pt_ref, x_ref, o_ref, *, r):
    x = jnp.concatenate([x_ref[c] for c in range(r)], axis=0)
    o_ref[...] = _dot(pt_ref[...], x).astype(o_ref.dtype)


def _from_class_order(xc, *, batch, seq):
    _, r, _, w = xc.shape
    tm = min(PROJ_TM, seq)
    nblk = seq // tm
    perm_t = jnp.asarray(_class_perm(tm, r).T, xc.dtype)
    return pl.pallas_call(
        functools.partial(_from_class_kernel, r=r),
        grid=(batch * nblk,),
        in_specs=[pl.BlockSpec((tm, tm), lambda i: (0, 0)),
                  pl.BlockSpec((None, r, tm // r, w), lambda i: (i // nblk, 0, i % nblk, 0))],
        out_specs=pl.BlockSpec((tm, w), lambda i: (i, 0)),
        out_shape=jax.ShapeDtypeStruct((batch * seq, w), xc.dtype),
        compiler_params=_compiler_params(("parallel",)),
        name="from_class_order",
    )(perm_t, xc)


def _compress_kernel(x_ref, pe_ref, w1_ref, w2_ref, o_ref):
    n_planes, n_rows, _ = x_ref.shape
    ylo = yhi = None
    for l in range(n_planes):
        x = x_ref[l].astype(F32)
        xlo = (x + pe_ref[l:l + 1, :]).astype(BF16)
        xhi = (x + pe_ref[n_planes + l:n_planes + l + 1, :]).astype(BF16)
        dlo = _dot(xlo, w1_ref[l * HEAD_DIM:(l + 1) * HEAD_DIM, :])
        dhi = _dot(xhi, w1_ref[(n_planes + l) * HEAD_DIM:(n_planes + l + 1) * HEAD_DIM, :])
        ylo = dlo if ylo is None else ylo + dlo
        yhi = dhi if yhi is None else yhi + dhi
    hid = ylo + pltpu.roll(yhi, n_rows - 1, 0)
    act = (hid * jax.nn.sigmoid(hid)).astype(BF16)
    o_ref[...] = _dot(act, w2_ref[...]).astype(o_ref.dtype)


def _compress(xc, pe, w1_bf, w2_bf):
    batch, planes, nrow, gd = xc.shape
    ng = gd // HEAD_DIM
    return pl.pallas_call(
        _compress_kernel,
        grid=(batch, ng),
        in_specs=[
            pl.BlockSpec((None, planes, nrow, HEAD_DIM), lambda b, g: (b, 0, 0, g)),
            pl.BlockSpec((CMP_LEN, HEAD_DIM), lambda b, g: (0, 0)),
            pl.BlockSpec((CMP_LEN * HEAD_DIM, CMP_HIDDEN), lambda b, g: (0, 0)),
            pl.BlockSpec((CMP_HIDDEN, HEAD_DIM), lambda b, g: (0, 0)),
        ],
        out_specs=pl.BlockSpec((None, None, nrow, HEAD_DIM), lambda b, g: (b, g, 0, 0)),
        out_shape=jax.ShapeDtypeStruct((batch, ng, nrow, HEAD_DIM), BF16),
        compiler_params=_compiler_params(("parallel", "arbitrary")),
        name="nsa_compress",
    )(xc, pe, w1_bf, w2_bf)


def _block_ranks(score, jrow):
    n_blk = score.shape[0]
    groups = n_blk // SUBLANES
    blocks = [score[SUBLANES * r:SUBLANES * (r + 1), :] for r in range(groups)]
    rows = [jrow[SUBLANES * r:SUBLANES * (r + 1), :] for r in range(groups)]
    ranks = [jnp.zeros(blocks[0].shape, F32) for _ in range(groups)]
    for j in range(n_blk):
        rj = score[j:j + 1, :]
        for r in range(groups):
            if r > j // SUBLANES:
                ahead = rj >= blocks[r]
            elif r < j // SUBLANES:
                ahead = rj > blocks[r]
            else:
                ahead = (rj > blocks[r]) | ((rj == blocks[r]) & (rows[r] > j))
            ranks[r] = ranks[r] + jnp.where(ahead, 1.0, 0.0)
    return jnp.concatenate(ranks, axis=0)


def _nsa_kernel(q_ref, kc_ref, vc_ref, ks_ref, e_ref, vs_ref, kw_ref, vw_ref, gz_ref, gb_ref,
                o_ref, qa_ref, sa_ref, sb_ref, sw_ref, ow_ref, m_ref, l_ref, acc_ref, *, tq, seq):
    hq = HPG_A
    groups = range(N_KV_A)
    n_cmp_rows = kc_ref.shape[1]
    n_slc = seq // SLC_BLK
    qi = pl.program_id(1)
    s0 = qi * tq
    t_row = s0 + lax.broadcasted_iota(jnp.int32, (1, tq), 1)

    def lanes(h):
        return slice(h * tq, (h + 1) * tq)

    def gcols(g):
        return slice(g * HEAD_DIM, (g + 1) * HEAD_DIM)

    def tile_heads(x):
        return jnp.concatenate([x] * hq, axis=1)

    wlen = WIN_A + tq
    ws = pl.multiple_of(jnp.maximum(s0 - WIN_A, 0), LANES)
    c_end = lax.broadcasted_iota(jnp.int32, (n_cmp_rows, 1), 0) * CMP_STRIDE + (CMP_LEN - 1)
    cbias = tile_heads(jnp.where(c_end <= t_row, 0.0, NEG_INF))
    any_cmp = tile_heads(t_row >= CMP_LEN - 1)
    jrow = lax.broadcasted_iota(jnp.int32, (n_slc, 1), 0)
    ccol = lax.broadcasted_iota(jnp.int32, (1, n_cmp_rows), 1)
    lo = (SLC_BLK // CMP_STRIDE) * jrow - (CMP_LEN // CMP_STRIDE - 1)
    hi = (SLC_BLK // CMP_STRIDE) * jrow + (SLC_BLK // CMP_STRIDE - 1)
    mmap = jnp.where((ccol >= lo) & (ccol <= hi), 1.0, 0.0).astype(BF16)
    cur = t_row >> SLC_SHIFT
    forced = (jrow == 0) | (jrow == cur) | (jrow == cur - 1)
    assert n_slc <= LANES and tq == LANES

    q6, sc = [], []
    for g in groups:
        for h in range(hq):
            head = g * hq + h
            qa_ref[g, lanes(h), 0:HEAD_DIM] = q_ref[:, head * HEAD_DIM:(head + 1) * HEAD_DIM]
        q6.append(qa_ref[g, :, 0:HEAD_DIM])
        sc.append(_dot_nt(kc_ref[g], q6[g]) + cbias)
    for g in groups:
        sw_ref[g] = _dot_nt(kw_ref[pl.ds(ws, wlen), gcols(g)], q6[g])

    o_cmp, score = [], []
    for g in groups:
        ec = jnp.exp2(sc[g] - jnp.max(sc[g], axis=0, keepdims=True))
        den = jnp.maximum(jnp.sum(ec, axis=0, keepdims=True), TINY)
        pc = ec * jnp.where(any_cmp, 1.0 / den, 0.0)
        o_cmp.append(_dot_tn(vc_ref[g], pc.astype(BF16)))
        psum = pc[:, lanes(0)]
        for h in range(1, hq):
            psum = psum + pc[:, lanes(h)]
        p1 = psum.astype(BF16)
        r1 = psum - p1.astype(F32)
        p2 = r1.astype(BF16)
        p3 = (r1 - p2.astype(F32)).astype(BF16)
        imp = _dot(mmap, p1) + _dot(mmap, p2) + _dot(mmap, p3)
        score.append(jnp.where(forced, 1e9, jnp.where(jrow <= cur, imp, -1e9)))

    for g in groups:
        rank = _block_ranks(score[g], jrow)
        sel_bias = jnp.where((rank < min(N_SEL, n_slc)) & (jrow <= cur), 0.0, NEG_INF)
        bias_q = jnp.transpose(jnp.concatenate(
            [sel_bias, jnp.zeros((LANES - n_slc, tq), F32)], axis=0)).astype(BF16)
        for h in range(hq):
            qa_ref[g, lanes(h), HEAD_DIM:HEAD_DIM + LANES] = bias_q

    m_ref[...] = jnp.full(m_ref.shape, NEG_INF, F32)
    l_ref[...] = jnp.zeros(l_ref.shape, F32)
    acc_ref[...] = jnp.zeros(acc_ref.shape, F32)

    def scores(g, c, s_ref):
        k0 = pl.multiple_of(c * NSA_KC, NSA_KC)
        k_aug = jnp.concatenate([ks_ref[pl.ds(k0, NSA_KC), gcols(g)],
                                 e_ref[pl.ds(k0, NSA_KC), :]], axis=1)
        s_ref[g] = _dot_nt(k_aug, qa_ref[g])

    def softmax_pv(g, c, s_ref):
        s = s_ref[g]
        m_old = m_ref[g]
        m_new = jnp.maximum(m_old, jnp.max(s, axis=0, keepdims=True))
        alpha = jnp.exp2(m_old - m_new)
        p = jnp.exp2(s - m_new)
        l_ref[g] = alpha * l_ref[g] + jnp.sum(p, axis=0, keepdims=True)
        k0 = pl.multiple_of(c * NSA_KC, NSA_KC)
        pv = _dot_tn(vs_ref[pl.ds(k0, NSA_KC), gcols(g)], p.astype(BF16))
        acc_ref[g] = alpha * acc_ref[g] + pv
        m_ref[g] = m_new

    last = s0 // NSA_KC
    for g in groups:
        scores(g, 0, sa_ref)

    dist = t_row - (ws + lax.broadcasted_iota(jnp.int32, (wlen, 1), 0))
    wbias = tile_heads(jnp.where((dist >= 0) & (dist < WIN_A), 0.0, NEG_INF))
    for g in groups:
        sw = sw_ref[g] + wbias
        ew = jnp.exp2(sw - jnp.max(sw, axis=0, keepdims=True))
        ow_ref[g] = (_dot_tn(vw_ref[pl.ds(ws, wlen), gcols(g)], ew.astype(BF16))
                     * (1.0 / jnp.sum(ew, axis=0, keepdims=True)))

    def chunk_pair(i, carry):
        for g in groups:
            scores(g, 2 * i + 1, sb_ref)
            softmax_pv(g, 2 * i, sa_ref)
        for g in groups:
            scores(g, 2 * i + 2, sa_ref)
            softmax_pv(g, 2 * i + 1, sb_ref)
        return carry

    lax.fori_loop(0, last // 2, chunk_pair, 0)

    def last_chunk(g, s_ref):
        diag = pl.multiple_of(s0 - last * NSA_KC, LANES)
        krow = lax.broadcasted_iota(jnp.int32, (tq, 1), 0)
        lane = lax.broadcasted_iota(jnp.int32, (1, tq), 1)
        causal = tile_heads(jnp.where(krow <= lane, 0.0, NEG_INF))
        s_ref[g, pl.ds(diag, tq), :] = s_ref[g, pl.ds(diag, tq), :] + causal
        softmax_pv(g, last, s_ref)

    @pl.when(last % 2 == 0)
    def _():
        for g in groups:
            last_chunk(g, sa_ref)

    @pl.when(last % 2 == 1)
    def _():
        for g in groups:
            scores(g, last, sb_ref)
            softmax_pv(g, last - 1, sa_ref)
        for g in groups:
            last_chunk(g, sb_ref)

    gates_t = jnp.transpose(jax.nn.sigmoid(gz_ref[...].astype(F32) + gb_ref[...]))
    for g in groups:
        o_slc = acc_ref[g] * (1.0 / l_ref[g])
        for h in range(hq):
            head = g * hq + h
            o_h = (gates_t[3 * head:3 * head + 1, :] * o_cmp[g][:, lanes(h)]
                   + gates_t[3 * head + 1:3 * head + 2, :] * o_slc[:, lanes(h)]
                   + gates_t[3 * head + 2:3 * head + 3, :] * ow_ref[g, :, lanes(h)])
            o_ref[:, head * HEAD_DIM:(head + 1) * HEAD_DIM] = jnp.transpose(o_h).astype(o_ref.dtype)


def _nsa_attention(z, kc, vc, gate_bias, *, batch, seq, units):
    tq = NSA_TQ
    nq = seq // tq
    n_cmp_rows = kc.shape[2]
    ng = N_KV_A
    hl = HPG_A * tq
    gw = ng * HEAD_DIM
    for name in ("k_slc", "v_slc", "k_win", "v_win"):
        assert units[name] % ng == 0
    kern = functools.partial(_nsa_kernel, tq=tq, seq=seq)

    def slab(name):
        return pl.BlockSpec((seq, gw), lambda b, i: (b, units[name] // ng))

    cmp_spec = pl.BlockSpec((None, ng, n_cmp_rows, HEAD_DIM), lambda b, i: (b, 0, 0, 0))

    onehot = jnp.asarray(np.arange(seq)[:, None] // SLC_BLK == np.arange(LANES)[None, :], BF16)
    return pl.pallas_call(
        kern,
        grid=(batch, nq),
        in_specs=[
            pl.BlockSpec((tq, A_Q), lambda b, i: (b * nq + i, 0)),
            cmp_spec,
            cmp_spec,
            slab("k_slc"),
            pl.BlockSpec((seq, LANES), lambda b, i: (0, 0)),
            slab("v_slc"),
            slab("k_win"),
            slab("v_win"),
            pl.BlockSpec((tq, LANES), lambda b, i: (b * nq + i, units["gates"])),
            pl.BlockSpec((1, LANES), lambda b, i: (0, 0)),
        ],
        out_specs=pl.BlockSpec((tq, A_Q), lambda b, i: (b * nq + i, 0)),
        out_shape=jax.ShapeDtypeStruct((batch * seq, A_Q), BF16),
        scratch_shapes=[
            pltpu.VMEM((ng, hl, HEAD_DIM + LANES), BF16),
            pltpu.VMEM((ng, NSA_KC, hl), F32),
            pltpu.VMEM((ng, NSA_KC, hl), F32),
            pltpu.VMEM((ng, WIN_A + tq, hl), F32),
            pltpu.VMEM((ng, HEAD_DIM, hl), F32),
            pltpu.VMEM((ng, 1, hl), F32),
            pltpu.VMEM((ng, 1, hl), F32),
            pltpu.VMEM((ng, HEAD_DIM, hl), F32),
        ],
        compiler_params=_compiler_params(("parallel", "arbitrary")),
        name="nsa_attention",
    )(z, kc, vc, z, onehot, z, z, z, z, gate_bias)


def _band_attn_kernel(*refs, tu, parts):
    q_refs = refs[:len(parts)]
    k_ref, v_ref, o_ref, lse_ref = refs[len(parts):]
    u0 = pl.program_id(2) * tu
    n_seq = k_ref.shape[0]
    windows = []
    for lk, span, stride in parts:
        ks = pl.multiple_of(jnp.clip(u0 - span, 0, n_seq - lk), LANES)
        dist = (u0 + lax.broadcasted_iota(jnp.int32, (tu, 1), 0)
                - (ks + lax.broadcasted_iota(jnp.int32, (1, lk), 1)))
        keep = (dist >= 0) & (dist <= span)
        if stride > 1:
            keep = keep & ((dist & (stride - 1)) == 0)
        windows.append((ks, lk, jnp.where(keep, 0.0, NEG_INF)))
    lane = lax.broadcasted_iota(jnp.int32, (1, LANES), 1)
    lse_tile = jnp.zeros((tu, LANES), F32)

    def head_cols(h):
        return slice(h * HEAD_DIM, (h + 1) * HEAD_DIM)

    scores = [[_dot_nt(q_ref[:, head_cols(h)], k_ref[pl.ds(ks, lk), head_cols(h)]) + bias
               for q_ref, (ks, lk, bias) in zip(q_refs, windows)]
              for h in range(DIL_HEADS)]
    probs, inv_l = [], []
    for h, s_parts in enumerate(scores):
        s = jnp.concatenate(s_parts, axis=1)
        m = jnp.max(s, axis=-1, keepdims=True)
        p = jnp.exp2(s - m)
        l = jnp.sum(p, axis=-1, keepdims=True)
        probs.append(p.astype(BF16))
        inv_l.append(1.0 / l)
        lse_tile = jnp.where(lane == h, m + jnp.log2(l), lse_tile)
    for h in range(DIL_HEADS):
        o, col = None, 0
        for ks, lk, _ in windows:
            pv = _dot(probs[h][:, col:col + lk], v_ref[pl.ds(ks, lk), head_cols(h)])
            o = pv if o is None else o + pv
            col += lk
        o_ref[:, head_cols(h)] = (o * inv_l[h]).astype(o_ref.dtype)
    lse_ref[...] = lse_tile


def _merge_dilation_calls(refs):
    calls = list(zip(refs[0::2], refs[1::2]))
    merged = []
    for h in range(DIL_HEADS):
        cols = slice(h * HEAD_DIM, (h + 1) * HEAD_DIM)
        shape = (calls[0][0].shape[0], HEAD_DIM)
        lses = [jnp.broadcast_to(l_ref[:, h:h + 1], shape) for _, l_ref in calls]
        top = functools.reduce(jnp.maximum, lses)
        ws = [jnp.exp2(lse - top) for lse in lses]
        num = sum(w * o_ref[:, cols].astype(F32) for w, (o_ref, _) in zip(ws, calls))
        merged.append((num * (1.0 / sum(ws))).astype(BF16))
    return jnp.concatenate(merged, axis=1)


def _dilated_attention(zb, *, batch, seq, units):
    n = zb.shape[1]
    width = DIL_HEADS * HEAD_DIM
    tu = DIL_TQ

    def band_call(r, q_units, parts, operands, k_unit, v_unit):
        n_seq = seq // r

        def z_spec(rows, unit, whole):
            return pl.BlockSpec((None, None, rows, width),
                                lambda b, c, i: (b, c, 0 if whole else i, unit))

        def out_spec(cols):
            return pl.BlockSpec((None, None, tu, cols), lambda b, c, i: (b, c, i, 0))

        return pl.pallas_call(
            functools.partial(_band_attn_kernel, tu=tu, parts=parts),
            grid=(batch, r, n_seq // tu),
            in_specs=[z_spec(tu, u, False) for u in q_units]
            + [z_spec(n_seq, k_unit, True), z_spec(n_seq, v_unit, True)],
            out_specs=[out_spec(width), out_spec(LANES)],
            out_shape=[jax.ShapeDtypeStruct((batch, r, n_seq, width), BF16),
                       jax.ShapeDtypeStruct((batch, r, n_seq, LANES), F32)],
            compiler_params=_compiler_params(("parallel", "parallel", "arbitrary")),
            name="dilated_attention",
        )(*operands)

    def window(n_seq, span):
        assert n_seq % tu == 0 and span % LANES == 0 and tu % LANES == 0
        return min(tu + span, n_seq)

    dense = [(gi, w, r) for gi, (w, r) in enumerate(DIL_CONFIGS) if r <= DIL_DENSE_MAX]
    zv = zb.reshape(batch, 1, seq, n)
    o_d, lse_d = band_call(1, [units["q"] + gi for gi, _, _ in dense],
                           tuple((window(seq, w), w, r) for _, w, r in dense),
                           [zv] * (len(dense) + 2), units["k"], units["v"])
    results = [o_d.reshape(batch * seq, width), lse_d.reshape(batch * seq, LANES)]
    for gi, (w, r) in enumerate(DIL_CONFIGS):
        if r <= DIL_DENSE_MAX:
            continue
        slabs = [(units["q"] + gi, width), (units["k"], width), (units["v"], width)]
        o_c, lse_c = band_call(r, [0], ((window(seq // r, w // r), w // r, 1),),
                               _to_class_order(zb, slabs, r, batch=batch, seq=seq), 0, 0)
        results += [_from_class_order(o_c, batch=batch, seq=seq),
                    jnp.transpose(lse_c, (0, 2, 1, 3)).reshape(batch * seq, LANES)]

    return results


def _memory_attention(q_ref, kv_ref):
    def cols(h, base=0):
        return slice(base + h * HEAD_DIM, base + (h + 1) * HEAD_DIM)

    heads = range(N_MEM_HEADS)
    scores = [_dot_nt(q_ref[:, cols(h)], kv_ref[:, cols(h)]) for h in heads]
    probs = []
    for s in scores:
        e = jnp.exp(s - jnp.max(s, axis=-1, keepdims=True))
        probs.append((e / jnp.sum(e, axis=-1, keepdims=True)).astype(BF16))
    return jnp.concatenate([_dot(probs[h], kv_ref[:, cols(h, MEM_Q)]).astype(BF16)
                            for h in heads], axis=1)


def _out_proj_kernel(*refs):
    a1_refs = refs[:-7]
    qm_ref, mkv_ref, w_ref, h_ref, g_ref, o_ref, xn_ref = refs[-7:]
    a1 = a1_refs[0][...] if len(a1_refs) == 1 else _merge_dilation_calls(a1_refs)
    a = jnp.concatenate([a1, _memory_attention(qm_ref, mkv_ref)], axis=1)
    y = h_ref[...] + _dot(a, w_ref[...])
    o_ref[...] = y
    xn_ref[...] = _rms_rows(y, g_ref[...]).astype(xn_ref.dtype)


def _out_proj(a1, z, q_unit, mkv, w_bf, h, g_next, *, batch):
    m, d = h.shape
    tm = min(OUT_TM, m)
    a1_list = list(a1) if isinstance(a1, (list, tuple)) else [a1]
    k1 = a1_list[0].shape[1]
    assert w_bf.shape[0] == k1 + MEM_Q and (m // batch) % tm == 0
    blocks_per_batch = m // batch // tm
    n_mem = mkv.shape[0] // batch
    row = pl.BlockSpec((tm, d), lambda i: (i, 0))
    return pl.pallas_call(
        _out_proj_kernel,
        grid=(m // tm,),
        in_specs=[pl.BlockSpec((tm, x.shape[1]), lambda i: (i, 0)) for x in a1_list] + [
            pl.BlockSpec((tm, MEM_Q), lambda i: (i, q_unit)),
            pl.BlockSpec((n_mem, 2 * MEM_Q), lambda i: (i // blocks_per_batch, 0)),
            pl.BlockSpec((k1 + MEM_Q, d), lambda i: (0, 0), pipeline_mode=pl.Buffered(1)),
            row,
            pl.BlockSpec((1, d), lambda i: (0, 0)),
        ],
        out_specs=[row, row],
        out_shape=[jax.ShapeDtypeStruct((m, d), F32), jax.ShapeDtypeStruct((m, d), BF16)],
        compiler_params=_compiler_params(("parallel",)),
        name="out_proj",
    )(*a1_list, z, mkv, w_bf, h, g_next.reshape(1, d))


def _ffn_up_kernel(xn_ref, *refs):
    o_ref = refs[-1]
    acc = _dot(xn_ref[...], jnp.concatenate([w_ref[...] for w_ref in refs[:-1]], axis=1))
    for c in range(o_ref.shape[1] // FFN_GU_CHUNK):
        gate = acc[:, 2 * c * FFN_GU_CHUNK:(2 * c + 1) * FFN_GU_CHUNK]
        up = acc[:, (2 * c + 1) * FFN_GU_CHUNK:(2 * c + 2) * FFN_GU_CHUNK]
        o_ref[:, c * FFN_GU_CHUNK:(c + 1) * FFN_GU_CHUNK] = (
            gate * jax.nn.sigmoid(gate) * up).astype(o_ref.dtype)


def _ffn_down_kernel(a_ref, w_ref, h_ref, g_ref, o_ref, *xn_ref, final_norm):
    y = h_ref[...] + _dot(a_ref[...], w_ref[...])
    yn = _rms_rows(y, g_ref[...])
    if final_norm:
        o_ref[...] = yn
    else:
        o_ref[...] = y
        xn_ref[0][...] = yn.astype(xn_ref[0].dtype)


def _ffn(h, xn, wg_bf, wu_bf, wd_bf, gain, *, final_norm):
    m, d = h.shape
    dff = wg_bf.shape[1]
    tm, tf = min(FFN_UP_TM, m), FFN_TF
    assert m % tm == 0 and dff % tf == 0 and tf % FFN_GU_CHUNK == 0
    per_step = tf // FFN_GU_CHUNK

    def slab(c):
        return pl.BlockSpec((d, FFN_GU_CHUNK), lambda i, f: (0, f * per_step + c))

    act = pl.pallas_call(
        _ffn_up_kernel,
        grid=(m // tm, dff // tf),
        in_specs=[pl.BlockSpec((tm, d), lambda i, f: (i, 0))]
        + [slab(c) for c in range(per_step) for _ in (wg_bf, wu_bf)],
        out_specs=pl.BlockSpec((tm, tf), lambda i, f: (i, f)),
        out_shape=jax.ShapeDtypeStruct((m, dff), BF16),
        compiler_params=_compiler_params(("parallel", "arbitrary")),
        name="ffn_up",
    )(xn, *[w for _ in range(per_step) for w in (wg_bf, wu_bf)])

    tm = min(FFN_TM, m)
    kern = functools.partial(_ffn_down_kernel, final_norm=final_norm)
    row = pl.BlockSpec((tm, d), lambda i: (i, 0))
    h_shape = jax.ShapeDtypeStruct((m, d), F32)
    return pl.pallas_call(
        kern,
        grid=(m // tm,),
        in_specs=[
            pl.BlockSpec((tm, dff), lambda i: (i, 0)),
            pl.BlockSpec((dff, d), lambda i: (0, 0), pipeline_mode=pl.Buffered(1)),
            row,
            pl.BlockSpec((1, d), lambda i: (0, 0)),
        ],
        out_specs=row if final_norm else [row, row],
        out_shape=h_shape if final_norm else [h_shape, jax.ShapeDtypeStruct((m, d), BF16)],
        compiler_params=_compiler_params(("parallel",)),
        name="ffn_down",
    )(act, wd_bf, h, gain.reshape(1, d))


def _rope_tables(seq):
    inv = 1.0 / (ROPE_THETA ** (jnp.arange(0, HEAD_DIM, 2, dtype=F32) / HEAD_DIM))
    ang = jnp.arange(seq, dtype=F32)[:, None] * inv[None, :]
    cos, sin = jnp.cos(ang), jnp.sin(ang)
    return jnp.concatenate([cos, cos], axis=1), jnp.concatenate([-sin, sin], axis=1)


A_UNITS = {"q": 0, "k_cmp": 12, "k_slc": 14, "k_win": 16, "v_cmp": 18, "v_slc": 20,
           "v_win": 22, "mem_q": 24, "gates": 28}
B_UNITS = {"q": 0, "k": 3, "mem_q": 4, "v": 5}
B_TN = 2 * DIL_HEADS * HEAD_DIM
A_TN = 6 * HEAD_DIM
A_NPAD = 30 * HEAD_DIM
A_ROPE_BLOCKS = 3


def _layer_a_weight(w_in):
    wt = jnp.swapaxes(w_in, 0, 1)
    kv0 = A_Q

    def kv_rows(branch):
        return wt[kv0 + branch * N_KV_A * HEAD_DIM:kv0 + (branch + 1) * N_KV_A * HEAD_DIM]

    gate0 = A_Q + A_KV
    mem0 = gate0 + A_GATE
    w = jnp.concatenate([wt[:A_Q], kv_rows(0), kv_rows(2), kv_rows(4), kv_rows(1),
                         kv_rows(3), kv_rows(5), wt[mem0:mem0 + MEM_Q], wt[gate0:mem0]], axis=0)
    w = jnp.pad(w, ((0, A_NPAD - w.shape[0]), (0, 0)))
    scale = np.ones((1, A_NPAD), np.float32)
    scale[0, :A_Q] = SCALE * LOG2E
    scale[0, A_UNITS["mem_q"] * HEAD_DIM:A_UNITS["mem_q"] * HEAD_DIM + MEM_Q] = SCALE
    return w.astype(BF16), jnp.asarray(scale)


def _layer_a(h, hn, mem, cosf, sinf, p, *, batch, seq):
    w_in_bf, col_scale = _layer_a_weight(p["w_in"])
    z = _norm_proj(h if hn is None else hn, p["norm_attn"], w_in_bf, col_scale, cosf, sinf,
                   tn=A_TN, n_rope_blocks=A_ROPE_BLOCKS, seq=seq, w_rows=True)

    gw = N_KV_A * HEAD_DIM
    k_raw, v_raw = _to_class_order(
        z, [(A_UNITS["k_cmp"] * HEAD_DIM // gw, gw), (A_UNITS["v_cmp"] * HEAD_DIM // gw, gw)],
        CMP_STRIDE, batch=batch, seq=seq)
    kc = _compress(k_raw, p["cmp_pe_k"], p["cmp_w1_k"].astype(BF16), p["cmp_w2_k"].astype(BF16))
    vc = _compress(v_raw, p["cmp_pe_v"], p["cmp_w1_v"].astype(BF16), p["cmp_w2_v"].astype(BF16))
    gb = jnp.pad(p["gate_bias"], (0, LANES - A_GATE)).reshape(1, LANES)
    o_nsa = _nsa_attention(z, kc, vc, gb, batch=batch, seq=seq, units=A_UNITS)

    mkv = _mem_kv(mem, p["norm_mem"], p["w_mem_kv"])
    return _out_proj(o_nsa, z, A_UNITS["mem_q"] * HEAD_DIM // MEM_Q, mkv,
                     p["w_out"].astype(BF16), h, p["norm_ffn"], batch=batch)


def _mem_kv(mem, norm_mem, w_mem_kv):
    b, m, d = mem.shape
    ones = jnp.ones((1, w_mem_kv.shape[1]), F32)
    dummy = jnp.zeros((m, HEAD_DIM), F32)
    return _norm_proj(mem.reshape(b * m, d), norm_mem, w_mem_kv.astype(BF16), ones, dummy, dummy,
                      tn=MEM_Q, n_rope_blocks=0, seq=m, tm=m)


def kernel(x, mem, a_norm_attn, a_w_in, a_gate_bias, a_cmp_pe_k, a_cmp_w1_k, a_cmp_w2_k, a_cmp_pe_v, a_cmp_w1_v, a_cmp_w2_v, a_norm_mem, a_w_mem_kv, a_w_out, a_norm_ffn, a_w_gate, a_w_up, a_w_down, kv_norm, w_kv_shared, b_norm_attn, b_w_in, b_norm_mem, b_w_mem_kv, b_w_out, b_norm_ffn, b_w_gate, b_w_up, b_w_down, final_norm):
    batch, seq, d = x.shape
    n_a = a_w_in.shape[0]
    n_b = b_w_in.shape[0]
    cosf, sinf = _rope_tables(seq)
    h = x.reshape(batch * seq, d)
    unit_gain = jnp.ones((d,), F32)
    hn = None

    for l in range(n_a):
        p = {"norm_attn": a_norm_attn[l], "w_in": a_w_in[l], "gate_bias": a_gate_bias[l],
             "cmp_pe_k": a_cmp_pe_k[l], "cmp_w1_k": a_cmp_w1_k[l], "cmp_w2_k": a_cmp_w2_k[l],
             "cmp_pe_v": a_cmp_pe_v[l], "cmp_w1_v": a_cmp_w1_v[l], "cmp_w2_v": a_cmp_w2_v[l],
             "norm_mem": a_norm_mem[l], "w_mem_kv": a_w_mem_kv[l], "w_out": a_w_out[l],
             "norm_ffn": a_norm_ffn[l]}
        h, xn = _layer_a(h, hn, mem, cosf, sinf, p, batch=batch, seq=seq)
        last = (l == n_a - 1) and n_b == 0
        ffn_w = (a_w_gate[l].astype(BF16), a_w_up[l].astype(BF16), a_w_down[l].astype(BF16))
        if last:
            h = _ffn(h, xn, *ffn_w, final_norm, final_norm=True)
        else:
            next_gain = a_norm_attn[l + 1] if l + 1 < n_a else unit_gain
            h, hn = _ffn(h, xn, *ffn_w, next_gain, final_norm=False)

    if n_b > 0:
        assert n_b == 1, "the shared K/V projection is fused into the single mixer-B layer"
        n_kv_half = w_kv_shared.shape[1] // 2
        for l in range(n_b):
            w_q = b_norm_attn[l][:, None] * b_w_in[l]
            w_kv = kv_norm[:, None] * w_kv_shared
            w_cat = jnp.concatenate([w_q[:, :B_Q], w_kv[:, :n_kv_half], w_q[:, B_Q:],
                                     w_kv[:, n_kv_half:]], axis=1).astype(BF16)
            b_scale = np.ones((1, w_cat.shape[1]), np.float32)
            b_scale[0, :B_Q] = SCALE * LOG2E
            b_scale[0, B_Q + n_kv_half:B_Q + n_kv_half + MEM_Q] = SCALE
            zb = _norm_proj(h if hn is None else hn, unit_gain, w_cat, jnp.asarray(b_scale),
                            cosf, sinf, tm=PROJ_TM if hn is None else PROJ_TM_BF16,
                            tn=B_TN, n_rope_blocks=(B_Q + n_kv_half) // B_TN, seq=seq)
            o_dil = _dilated_attention(zb, batch=batch, seq=seq, units=B_UNITS)
            mkv = _mem_kv(mem, b_norm_mem[l], b_w_mem_kv[l])
            h, xn = _out_proj(o_dil, zb, B_UNITS["mem_q"], mkv, b_w_out[l].astype(BF16), h,
                              b_norm_ffn[l], batch=batch)
            h = _ffn(h, xn, b_w_gate[l].astype(BF16), b_w_up[l].astype(BF16),
                     b_w_down[l].astype(BF16), final_norm, final_norm=True)

    return h.reshape(batch, seq, d)
```
